```python
import math
import numpy as np
import jax
import jax.numpy as jnp
from jax import lax

D_MODEL = 1024
BATCH = 4
SEQ = 4096
DEPTH = 2

HEAD_DIM = 64
N_MIXERS = 4
GROUP_WIDTH = D_MODEL // N_MIXERS
N_HEADS = GROUP_WIDTH // HEAD_DIM
ROPE_THETA = 10000.0
EPS = 1e-6
Q_BLOCK = 128
NEG_INF = -1e30

NSA_CMP_LEN = 32
NSA_CMP_STRIDE = 16
NSA_SLC_LEN = 64
NSA_TOP_N = 16
NSA_WINDOW = 512
NSA_FORCED_LOCAL = 2

DIFF_HALF = HEAD_DIM // 2

MLSTM_CHUNK = 64
MLSTM_CONV = 4

DILATED_PATTERNS = ((128, 1), (512, 4), (2048, 16))
DILATED_PAD = 2048

D_FF = ((8 * D_MODEL + 3 * 256 - 1) // (3 * 256)) * 256

IN_SPLITS = (
    GROUP_WIDTH,
    HEAD_DIM, HEAD_DIM,
    HEAD_DIM, HEAD_DIM,
    HEAD_DIM, HEAD_DIM,
    3 * N_HEADS,
    GROUP_WIDTH, GROUP_WIDTH, GROUP_WIDTH,
    GROUP_WIDTH, GROUP_WIDTH,
    N_HEADS, N_HEADS, GROUP_WIDTH,
    GROUP_WIDTH, GROUP_WIDTH, GROUP_WIDTH,
)
D_IN = sum(IN_SPLITS)

kernel_name = 'hybrid_nsa_diff_mlstm_dilated_trunk'


def rmsnorm(x, g):
    xf = x.astype(jnp.float32)
    y = xf * lax.rsqrt(jnp.mean(xf * xf, axis=-1, keepdims=True) + EPS)
    return (y * g.astype(jnp.float32)).astype(x.dtype)


def rope_tables(seq, dim):
    inv = 1.0 / (ROPE_THETA ** (jnp.arange(0, dim, 2, dtype=jnp.float32) / dim))
    ang = jnp.arange(seq, dtype=jnp.float32)[:, None] * inv[None, :]
    return jnp.cos(ang), jnp.sin(ang)


def apply_rope(x, cos, sin):
    shape = (cos.shape[0],) + (1,) * (x.ndim - 3) + (cos.shape[1],)
    c, s = cos.reshape(shape), sin.reshape(shape)
    xf = x.astype(jnp.float32)
    half = x.shape[-1] // 2
    x1, x2 = xf[..., :half], xf[..., half:]
    return jnp.concatenate([x1 * c - x2 * s, x1 * s + x2 * c], axis=-1).astype(x.dtype)


def masked_softmax(s, mask, return_lse=False):
    s = jnp.where(mask, s.astype(jnp.float32), NEG_INF)
    m = jnp.max(s, axis=-1, keepdims=True)
    e = jnp.where(mask, jnp.exp(s - m), 0.0)
    z = jnp.sum(e, axis=-1, keepdims=True)
    z_safe = jnp.where(z > 0, z, 1.0)
    p = e / z_safe
    if return_lse:
        return p, (m + jnp.log(z_safe))[..., 0]
    return p


def sweep_query_blocks(fn, seq):
    out = lax.map(fn, jnp.arange(seq // Q_BLOCK) * Q_BLOCK)
    out = jnp.moveaxis(out, 0, 1)
    return out.reshape((out.shape[0], seq) + out.shape[3:])


def nsa_mixer(q, k_cmp, v_cmp, k_slc, v_slc, k_win, v_win, gate_pre, cmp_pos, cmp_w, cos, sin):
    bsz, seq = q.shape[0], q.shape[1]
    scale = HEAD_DIM ** -0.5
    t_pos = jnp.arange(seq)

    n_cmp = (seq - NSA_CMP_LEN) // NSA_CMP_STRIDE + 1
    blk_idx = np.arange(n_cmp)[:, None] * NSA_CMP_STRIDE + np.arange(NSA_CMP_LEN)[None, :]

    def compress(t, pos_emb, w):
        blocks = t[:, blk_idx] + pos_emb
        return blocks.reshape(bsz, n_cmp, NSA_CMP_LEN * HEAD_DIM) @ w

    kc = compress(k_cmp, cmp_pos[0], cmp_w[0])
    vc = compress(v_cmp, cmp_pos[1], cmp_w[1])
    cmp_end = jnp.arange(n_cmp) * NSA_CMP_STRIDE + NSA_CMP_LEN - 1
    cmp_mask = cmp_end[None, :] <= t_pos[:, None]
    p_cmp = masked_softmax(jnp.einsum('bshd,bnd->bhsn', q, kc) * scale, cmp_mask)
    o_cmp = jnp.einsum('bhsn,bnd->bshd', p_cmp.astype(vc.dtype), vc)

    n_slc = seq // NSA_SLC_LEN
    top_n = min(NSA_TOP_N, n_slc)
    ratio_s, ratio_c = NSA_SLC_LEN // NSA_CMP_STRIDE, NSA_CMP_LEN // NSA_CMP_STRIDE
    jj = np.arange(n_slc)[:, None, None]
    src = ratio_s * jj - np.arange(ratio_s)[None, :, None] - np.arange(ratio_c)[None, None, :]
    ok = (src >= 0) & (src < n_cmp)
    cmp_to_slc = np.zeros((n_cmp, n_slc), np.float32)
    np.add.at(cmp_to_slc, (np.where(ok, src, 0), np.broadcast_to(jj, src.shape)), ok.astype(np.float32))
    importance = jnp.einsum('bhsn,nj->bsj', p_cmp, jnp.asarray(cmp_to_slc))
    blk = jnp.arange(n_slc)[None, :]
    cur = (t_pos // NSA_SLC_LEN)[:, None]
    forced = (blk == 0) | ((blk <= cur) & (blk > cur - NSA_FORCED_LOCAL))
    score = jnp.where(blk > cur, -1.0e6, jnp.where(forced, 1.0e6, importance))
    _, sel_idx = lax.top_k(score, top_n)
    sel_valid = sel_idx <= cur[None]

    q_rot = apply_rope(q, cos, sin)
    ks = apply_rope(k_slc, cos, sin).reshape(bsz, n_slc, NSA_SLC_LEN, HEAD_DIM)
    vs = v_slc.reshape(bsz, n_slc, NSA_SLC_LEN, HEAD_DIM)
    pad = ((0, 0), (NSA_WINDOW, 0), (0, 0))
    kw = jnp.pad(apply_rope(k_win, cos, sin), pad)
    vw = jnp.pad(v_win, pad)
    b_idx = jnp.arange(bsz)[:, None, None]
    n_sel_keys = top_n * NSA_SLC_LEN

    def block(s0):
        qb = lax.dynamic_slice_in_dim(q_rot, s0, Q_BLOCK, axis=1)
        tq = s0 + jnp.arange(Q_BLOCK)
        idx = lax.dynamic_slice_in_dim(sel_idx, s0, Q_BLOCK, axis=1)
        valid = lax.dynamic_slice_in_dim(sel_valid, s0, Q_BLOCK, axis=1)
        kg = ks[b_idx, idx].reshape(bsz, Q_BLOCK, n_sel_keys, HEAD_DIM)
        vg = vs[b_idx, idx].reshape(bsz, Q_BLOCK, n_sel_keys, HEAD_DIM)
        kpos = (idx[..., None] * NSA_SLC_LEN + jnp.arange(NSA_SLC_LEN)).reshape(bsz, Q_BLOCK, n_sel_keys)
        kvalid = jnp.repeat(valid, NSA_SLC_LEN, axis=-1) & (kpos <= tq[None, :, None])
        p = masked_softmax(jnp.einsum('bqhd,bqkd->bhqk', qb, kg) * scale, kvalid[:, None])
        o_slc = jnp.einsum('bhqk,bqkd->bqhd', p.astype(vg.dtype), vg)
        kwb = lax.dynamic_slice_in_dim(kw, s0, NSA_WINDOW + Q_BLOCK, axis=1)
        vwb = lax.dynamic_slice_in_dim(vw, s0, NSA_WINDOW + Q_BLOCK, axis=1)
        wpos = s0 - NSA_WINDOW + jnp.arange(NSA_WINDOW + Q_BLOCK)
        wmask = (wpos[None, :] <= tq[:, None]) & (wpos[None, :] > tq[:, None] - NSA_WINDOW) & (wpos[None, :] >= 0)
        p = masked_softmax(jnp.einsum('bqhd,bkd->bhqk', qb, kwb) * scale, wmask)
        o_win = jnp.einsum('bhqk,bkd->bqhd', p.astype(vwb.dtype), vwb)
        return jnp.stack([o_slc, o_win], axis=2)

    o_sw = sweep_query_blocks(block, seq)
    g = jax.nn.sigmoid(gate_pre.astype(jnp.float32)).reshape(bsz, seq, 3, N_HEADS, 1).astype(q.dtype)
    o = g[:, :, 0] * o_cmp + g[:, :, 1] * o_sw[:, :, 0] + g[:, :, 2] * o_sw[:, :, 1]
    return o.reshape(bsz, seq, GROUP_WIDTH)


def diff_mixer(q, k, v, lam_vecs, sub_g, lam_init, cos, sin):
    bsz, seq = q.shape[0], q.shape[1]
    q = apply_rope(q, cos, sin)
    k = apply_rope(k, cos, sin)
    lv = lam_vecs.astype(jnp.float32)
    lam = jnp.exp(jnp.sum(lv[0] * lv[1])) - jnp.exp(jnp.sum(lv[2] * lv[3])) + lam_init
    scale = DIFF_HALF ** -0.5
    kpos = jnp.arange(seq)

    def block(s0):
        qb = lax.dynamic_slice_in_dim(q, s0, Q_BLOCK, axis=1)
        tq = s0 + jnp.arange(Q_BLOCK)
        s = jnp.einsum('bqhcd,bkhcd->bhcqk', qb, k) * scale
        p = masked_softmax(s, kpos[None, :] <= tq[:, None])
        a = p[:, :, 0] - lam * p[:, :, 1]
        return jnp.einsum('bhqk,bkhd->bqhd', a.astype(v.dtype), v)

    o = sweep_query_blocks(block, seq)
    o = rmsnorm(o, sub_g) * (1.0 - lam_init)
    return o.reshape(bsz, seq, GROUP_WIDTH)


def mlstm_mixer(u, v, i_pre, f_pre, o_pre, conv_w, conv_b, wq, wk, gate_b, head_g):
    bsz, seq = u.shape[0], u.shape[1]
    f32 = jnp.float32
    uc = lax.conv_general_dilated(u, conv_w[:, None, :], window_strides=(1,),
                                  padding=[(MLSTM_CONV - 1, 0)],
                                  dimension_numbers=('NWC', 'WIO', 'NWC'),
                                  feature_group_count=GROUP_WIDTH) + conv_b
    uc = jax.nn.silu(uc).reshape(bsz, seq, N_HEADS, HEAD_DIM)
    q = jnp.einsum('bshd,hde->bhse', uc, wq).astype(f32)
    k = jnp.einsum('bshd,hde->bhse', uc, wk).astype(f32) * (HEAD_DIM ** -0.5)
    vv = v.reshape(bsz, seq, N_HEADS, HEAD_DIM).transpose(0, 2, 1, 3).astype(f32)
    gb = gate_b.astype(f32)
    ig = (i_pre.astype(f32) + gb[0]).transpose(0, 2, 1)
    lf = jax.nn.log_sigmoid(f_pre.astype(f32) + gb[1]).transpose(0, 2, 1)

    nc, L = seq // MLSTM_CHUNK, MLSTM_CHUNK
    q = q.reshape(bsz, N_HEADS, nc, L, HEAD_DIM)
    k = k.reshape(bsz, N_HEADS, nc, L, HEAD_DIM)
    vv = vv.reshape(bsz, N_HEADS, nc, L, HEAD_DIM)
    ig = ig.reshape(bsz, N_HEADS, nc, L)
    b = jnp.cumsum(lf.reshape(bsz, N_HEADS, nc, L), axis=-1)
    causal = jnp.tril(jnp.ones((L, L), dtype=bool))
    dmat = jnp.where(causal, b[..., :, None] - b[..., None, :] + ig[..., None, :], NEG_INF)

    a = b[..., -1]
    g_end = a[..., None] - b + ig
    m_loc = jnp.max(g_end, axis=-1)
    w_end = jnp.exp(g_end - m_loc[..., None])
    c_loc = jnp.einsum('bhcl,bhclv,bhclk->bhcvk', w_end, vv, k)
    n_loc = jnp.einsum('bhcl,bhclk->bhck', w_end, k)

    def step(carry, xs):
        c_st, n_st, m_st = carry
        a_c, m_l, c_l, n_l = xs
        m_new = jnp.maximum(a_c + m_st, m_l)
        decay = jnp.exp(a_c + m_st - m_new)
        fresh = jnp.exp(m_l - m_new)
        c_new = decay[..., None, None] * c_st + fresh[..., None, None] * c_l
        n_new = decay[..., None] * n_st + fresh[..., None] * n_l
        return (c_new, n_new, m_new), (c_st, n_st, m_st)

    init = (jnp.zeros((bsz, N_HEADS, HEAD_DIM, HEAD_DIM), f32),
            jnp.zeros((bsz, N_HEADS, HEAD_DIM), f32),
            jnp.zeros((bsz, N_HEADS), f32))
    xs = (jnp.moveaxis(a, 2, 0), jnp.moveaxis(m_loc, 2, 0),
          jnp.moveaxis(c_loc, 2, 0), jnp.moveaxis(n_loc, 2, 0))
    _, (c_in, n_in, m_in) = lax.scan(step, init, xs)
    c_in = jnp.moveaxis(c_in, 0, 2)
    n_in = jnp.moveaxis(n_in, 0, 2)
    m_in = jnp.moveaxis(m_in, 0, 2)

    inter = b + m_in[..., None]
    m_t = jnp.maximum(inter, jnp.max(dmat, axis=-1))
    e_inter = jnp.exp(inter - m_t)
    s_qk = jnp.einsum('bhctd,bhcsd->bhcts', q, k) * jnp.exp(dmat - m_t[..., None])
    num = (e_inter[..., None] * jnp.einsum('bhcvk,bhctk->bhctv', c_in, q)
           + jnp.einsum('bhcts,bhcsv->bhctv', s_qk, vv))
    den = e_inter * jnp.einsum('bhck,bhctk->bhct', n_in, q) + jnp.sum(s_qk, axis=-1)
    h = num / jnp.maximum(jnp.abs(den), jnp.exp(-m_t))[..., None]
    h = h.reshape(bsz, N_HEADS, seq, HEAD_DIM).transpose(0, 2, 1, 3)
    h = rmsnorm(h, head_g.reshape(N_HEADS, HEAD_DIM))
    h = h * jax.nn.sigmoid(o_pre.astype(f32)).reshape(bsz, seq, N_HEADS, HEAD_DIM)
    return h.reshape(bsz, seq, GROUP_WIDTH).astype(u.dtype)


def dilated_mixer(q, k, v, cos, sin):
    bsz, seq = q.shape[0], q.shape[1]
    q = apply_rope(q, cos, sin)
    k = apply_rope(k, cos, sin)
    pad = ((0, 0), (DILATED_PAD, 0), (0, 0), (0, 0))
    kp, vp = jnp.pad(k, pad), jnp.pad(v, pad)
    scale = HEAD_DIM ** -0.5

    def block(s0):
        qb = lax.dynamic_slice_in_dim(q, s0, Q_BLOCK, axis=1)
        tq = s0 + jnp.arange(Q_BLOCK)
        kseg = lax.dynamic_slice_in_dim(kp, s0, DILATED_PAD + Q_BLOCK, axis=1)
        vseg = lax.dynamic_slice_in_dim(vp, s0, DILATED_PAD + Q_BLOCK, axis=1)
        outs, lses = [], []
        for window, dil in DILATED_PATTERNS:
            n_keys = window // dil + 1
            rel = np.arange(Q_BLOCK)[:, None] + DILATED_PAD - np.arange(n_keys)[None, :] * dil
            kg, vg = kseg[:, rel], vseg[:, rel]
            kpos = tq[:, None] - jnp.arange(n_keys)[None, :] * dil
            s = jnp.einsum('bqhd,bqjhd->bhqj', qb, kg) * scale
            p, lse = masked_softmax(s, kpos >= 0, return_lse=True)
            outs.append(jnp.einsum('bhqj,bqjhd->bqhd', p.astype(vg.dtype), vg))
            lses.append(jnp.transpose(lse, (0, 2, 1)))
        alpha = jax.nn.softmax(jnp.stack(lses, axis=0), axis=0)
        return jnp.einsum('gbqh,gbqhd->bqhd', alpha.astype(q.dtype), jnp.stack(outs, axis=0))

    o = sweep_query_blocks(block, seq)
    return o.reshape(bsz, seq, GROUP_WIDTH)


def setup_inputs(seed: int = 0) -> dict:
    key = jax.random.key(seed)
    k = jax.random.split(key, 20)
    f32 = jnp.float32

    def normal(kk, shape, scale):
        return jax.random.normal(kk, shape, f32) * scale

    def gain(kk, shape):
        return 1.0 + 0.02 * jax.random.normal(kk, shape, f32)

    forget_bias = jnp.linspace(3.0, 6.0, N_HEADS, dtype=f32)
    mlstm_gate_b = jnp.stack([normal(k[10], (DEPTH, N_HEADS), 0.1),
                              forget_bias + normal(k[11], (DEPTH, N_HEADS), 0.1)], axis=1)
    return {
        'x': normal(k[0], (BATCH, SEQ, D_MODEL), 1.0),
        'norm_mix': gain(k[1], (DEPTH, D_MODEL)),
        'w_in': normal(k[2], (DEPTH, D_MODEL, D_IN), D_MODEL ** -0.5),
        'nsa_cmp_pos': normal(k[3], (DEPTH, 2, NSA_CMP_LEN, HEAD_DIM), 0.1),
        'nsa_cmp_w': normal(k[4], (DEPTH, 2, NSA_CMP_LEN * HEAD_DIM, HEAD_DIM), (NSA_CMP_LEN * HEAD_DIM) ** -0.5),
        'diff_lambda': normal(k[5], (DEPTH, 4, DIFF_HALF), 0.1),
        'diff_norm': gain(k[6], (DEPTH, HEAD_DIM)),
        'mlstm_conv_w': normal(k[7], (DEPTH, MLSTM_CONV, GROUP_WIDTH), MLSTM_CONV ** -0.5),
        'mlstm_conv_b': normal(k[8], (DEPTH, GROUP_WIDTH), 0.02),
        'mlstm_wq': normal(k[9], (DEPTH, N_HEADS, HEAD_DIM, HEAD_DIM), HEAD_DIM ** -0.5),
        'mlstm_wk': normal(k[12], (DEPTH, N_HEADS, HEAD_DIM, HEAD_DIM), HEAD_DIM ** -0.5),
        'mlstm_gate_b': mlstm_gate_b,
        'mlstm_norm': gain(k[13], (DEPTH, GROUP_WIDTH)),
        'w_out': normal(k[14], (DEPTH, D_MODEL, D_MODEL), D_MODEL ** -0.5),
        'norm_ffn': gain(k[15], (DEPTH, D_MODEL)),
        'w_gate': normal(k[16], (DEPTH, D_MODEL, D_FF), D_MODEL ** -0.5),
        'w_up': normal(k[17], (DEPTH, D_MODEL, D_FF), D_MODEL ** -0.5),
        'w_down': normal(k[18], (DEPTH, D_FF, D_MODEL), D_FF ** -0.5),
        'norm_final': gain(k[19], (D_MODEL,)),
    }


def reference(x, norm_mix, w_in, nsa_cmp_pos, nsa_cmp_w, diff_lambda, diff_norm,
              mlstm_conv_w, mlstm_conv_b, mlstm_wq, mlstm_wk, mlstm_gate_b, mlstm_norm,
              w_out, norm_ffn, w_gate, w_up, w_down, norm_final):
    bsz, seq, _ = x.shape
    cos64, sin64 = rope_tables(seq, HEAD_DIM)
    cos32, sin32 = rope_tables(seq, DIFF_HALF)
    offsets = [int(o) for o in np.cumsum(IN_SPLITS)[:-1]]

    def heads(t):
        return t.reshape(bsz, seq, N_HEADS, HEAD_DIM)

    for layer in range(DEPTH):
        h = rmsnorm(x, norm_mix[layer])
        z = h @ w_in[layer]
        (a_q, a_kc, a_vc, a_ks, a_vs, a_kw, a_vw, a_g,
         b_q, b_k, b_v, c_u, c_v, c_i, c_f, c_o, d_q, d_k, d_v) = jnp.split(z, offsets, axis=-1)
        o_a = nsa_mixer(heads(a_q), a_kc, a_vc, a_ks, a_vs, a_kw, a_vw, a_g,
                        nsa_cmp_pos[layer], nsa_cmp_w[layer], cos64, sin64)
        lam_init = 0.8 - 0.6 * math.exp(-0.3 * layer)
        o_b = diff_mixer(b_q.reshape(bsz, seq, N_HEADS, 2, DIFF_HALF),
                         b_k.reshape(bsz, seq, N_HEADS, 2, DIFF_HALF), heads(b_v),
                         diff_lambda[layer], diff_norm[layer], lam_init, cos32, sin32)
        o_c = mlstm_mixer(c_u, c_v, c_i, c_f, c_o, mlstm_conv_w[layer], mlstm_conv_b[layer],
                          mlstm_wq[layer], mlstm_wk[layer], mlstm_gate_b[layer], mlstm_norm[layer])
        o_d = dilated_mixer(heads(d_q), heads(d_k), heads(d_v), cos64, sin64)
        x = x + jnp.concatenate([o_a, o_b, o_c, o_d], axis=-1) @ w_out[layer]
        h = rmsnorm(x, norm_ffn[layer])
        x = x + (jax.nn.silu(h @ w_gate[layer]) * (h @ w_up[layer])) @ w_down[layer]
    return rmsnorm(x, norm_final)
```

```python
import functools
import math

import numpy as np
import jax
import jax.numpy as jnp
from jax import lax
from jax.experimental import pallas as pl
from jax.experimental.pallas import tpu as pltpu

F32 = jnp.float32
BF16 = jnp.bfloat16

D_MODEL = 1024
HEAD_DIM = 64
N_HEADS = 4
GROUP_WIDTH = N_HEADS * HEAD_DIM
ROPE_THETA = 10000.0
EPS = 1e-6
NEG = -1e30

CMP_LEN = 32
CMP_STRIDE = 16
SLC_LEN = 64
TOP_N = 16
NSA_WINDOW = 512
FORCED_LOCAL = 2
DIFF_HALF = HEAD_DIM // 2
MLSTM_CHUNK = 64
MLSTM_CONV = 4
DILATED_PATTERNS = ((128, 1), (512, 4), (2048, 16))
D_FF = ((8 * D_MODEL + 3 * 256 - 1) // (3 * 256)) * 256

LANES = 128
VMEM_LIMIT = 56 * 1024 * 1024

IN_SPLITS = (
    GROUP_WIDTH, HEAD_DIM, HEAD_DIM, HEAD_DIM, HEAD_DIM, HEAD_DIM, HEAD_DIM, 3 * N_HEADS,
    GROUP_WIDTH, GROUP_WIDTH, GROUP_WIDTH,
    GROUP_WIDTH, GROUP_WIDTH, N_HEADS, N_HEADS, GROUP_WIDTH,
    GROUP_WIDTH, GROUP_WIDTH, GROUP_WIDTH,
)
(A_Q, A_KC, A_VC, A_KS, A_VS, A_KW, A_VW, A_G, B_Q, B_K, B_V,
 C_U, C_V, C_I, C_F, C_O, D_Q, D_K, D_V) = range(19)
SM_AG, SM_CI, SM_CF = 0, 12, 16


def _cparams(sem):
    return pltpu.CompilerParams(dimension_semantics=sem, vmem_limit_bytes=VMEM_LIMIT)


def _iota(shape, dim):
    return lax.broadcasted_iota(jnp.int32, shape, dim)


def _dot(a, b):
    return jnp.dot(a, b, preferred_element_type=F32)


def _dot_nt(a, b):
    return lax.dot_general(a, b, (((1,), (1,)), ((), ())), preferred_element_type=F32)


def _pack_w_in(w):
    offs = np.concatenate([[0], np.cumsum(IN_SPLITS)])
    col = lambda i: w[:, int(offs[i]):int(offs[i + 1])]
    z64 = jnp.zeros((w.shape[0], HEAD_DIM), w.dtype)
    smalls = jnp.concatenate(
        [col(A_G), col(C_I), col(C_F), jnp.zeros((w.shape[0], LANES - 20), w.dtype)], axis=1)
    parts = [col(A_Q), col(A_KC), col(A_VC), col(A_KS), z64, col(A_VS), z64, col(A_KW), z64,
             col(A_VW), z64, smalls, col(B_Q), col(B_K), col(B_V), col(C_U), col(C_V), col(C_O),
             col(D_Q), col(D_K), col(D_V)]
    return jnp.concatenate(parts, axis=1).astype(BF16)


def _rope_tables(seq, dim):
    inv = 1.0 / (ROPE_THETA ** (jnp.arange(0, dim, 2, dtype=F32) / dim))
    ang = jnp.arange(seq, dtype=F32)[:, None] * inv[None, :]
    return jnp.cos(ang), jnp.sin(ang)


def _lane_tables(seq):
    c64, s64 = _rope_tables(seq, HEAD_DIM)
    c32, s32 = _rope_tables(seq, DIFF_HALF)
    t64c = jnp.concatenate([c64, c64] * (LANES // HEAD_DIM), axis=1)
    t64s = jnp.concatenate([-s64, s64] * (LANES // HEAD_DIM), axis=1)
    t32c = jnp.concatenate([c32, c32] * (LANES // DIFF_HALF), axis=1)
    t32s = jnp.concatenate([-s32, s32] * (LANES // DIFF_HALF), axis=1)
    pos = np.arange(seq)[:, None]
    lane = np.arange(LANES)[None, :]
    onehot = ((lane >= HEAD_DIM) & ((pos // SLC_LEN) % HEAD_DIM == lane - HEAD_DIM)
              ).astype(np.float32)
    return t64c, t64s, t32c, t32s, jnp.asarray(onehot)


def _swap_halves(x, group):
    width = x.shape[-1]
    half = group // 2
    lane = _iota(x.shape, 1) & (group - 1)
    up = pltpu.roll(x, width - half, axis=1)
    down = pltpu.roll(x, half, axis=1)
    return jnp.where(lane < half, up, down)


def _rope(x, cos_t, sin_t, group):
    reps = x.shape[-1] // LANES
    if reps > 1:
        cos_t = jnp.concatenate([cos_t] * reps, axis=1)
        sin_t = jnp.concatenate([sin_t] * reps, axis=1)
    return x * cos_t + _swap_halves(x, group) * sin_t


def _in_proj_kernel(x_ref, g_ref, w_ref, c64_ref, s64_ref, c32_ref, s32_ref, oh_ref,
                    aq_ref, aqr_ref, akvc_ref, aks_ref, avs_ref, akw_ref, avw_ref, sm_ref,
                    bq_ref, bk_ref, bv_ref, cu_ref, cv_ref, co_ref, dq_ref, dk_ref, dv_ref):
    x = x_ref[...]
    h = x * lax.rsqrt(jnp.mean(x * x, axis=-1, keepdims=True) + EPS)
    h = (h * g_ref[...]).astype(BF16)
    c64, s64 = c64_ref[...], s64_ref[...]
    c32, s32 = c32_ref[...], s32_ref[...]

    def mm(c0, c1):
        return _dot(h, w_ref[:, c0 * LANES:c1 * LANES])

    zq = mm(0, 2) * (HEAD_DIM ** -0.5)
    aq_ref[...] = zq.astype(BF16)
    aqr_ref[...] = _rope(zq, c64, s64, HEAD_DIM).astype(BF16)
    akvc_ref[...] = mm(2, 3).astype(BF16)
    aks_ref[...] = (_rope(mm(3, 4), c64, s64, HEAD_DIM) + oh_ref[...]).astype(BF16)
    avs_ref[...] = mm(4, 5).astype(BF16)
    akw_ref[...] = _rope(mm(5, 6), c64, s64, HEAD_DIM).astype(BF16)
    avw_ref[...] = mm(6, 7).astype(BF16)
    sm_ref[...] = mm(7, 8)
    bq_ref[...] = (_rope(mm(8, 10), c32, s32, DIFF_HALF) * (DIFF_HALF ** -0.5)).astype(BF16)
    bk_ref[...] = _rope(mm(10, 12), c32, s32, DIFF_HALF).astype(BF16)
    bv_ref[...] = mm(12, 14).astype(BF16)
    cu_ref[...] = mm(14, 16)
    cv_ref[...] = mm(16, 18).astype(BF16)
    co_ref[...] = mm(18, 20)
    dq_ref[...] = (_rope(mm(20, 22), c64, s64, HEAD_DIM) * (HEAD_DIM ** -0.5)).astype(BF16)
    dk_ref[...] = _rope(mm(22, 24), c64, s64, HEAD_DIM).astype(BF16)
    dv_ref[...] = mm(24, 26).astype(BF16)


_IN_PROJ_OUTS = (
    (256, BF16), (256, BF16), (128, BF16), (128, BF16), (128, BF16), (128, BF16), (128, BF16),
    (128, F32), (256, BF16), (256, BF16), (256, BF16), (256, F32), (256, BF16), (256, F32),
    (256, BF16), (256, BF16), (256, BF16))


def _in_proj(x2, gain, w_packed, tables, seq, tm):
    n = x2.shape[0]
    nblk_s = seq // tm
    row = lambda i: (i, 0)
    tab = lambda i: (i % nblk_s, 0)
    const = lambda i: (0, 0)
    in_specs = [pl.BlockSpec((tm, D_MODEL), row),
                pl.BlockSpec((1, D_MODEL), const),
                pl.BlockSpec(w_packed.shape, const)]
    in_specs += [pl.BlockSpec((tm, LANES), tab)] * 5
    out_specs = [pl.BlockSpec((tm, w), row) for w, _ in _IN_PROJ_OUTS]
    out_shape = [jax.ShapeDtypeStruct((n, w), dt) for w, dt in _IN_PROJ_OUTS]
    return pl.pallas_call(
        _in_proj_kernel, grid=(n // tm,), in_specs=in_specs, out_specs=out_specs,
        out_shape=out_shape, compiler_params=_cparams(("parallel",)), name="in_proj",
    )(x2, gain.reshape(1, D_MODEL), w_packed, *tables)


def _pack_cmp_w(cmp_w):
    w = cmp_w.reshape(2, 2, CMP_STRIDE, HEAD_DIM, HEAD_DIM)
    zeros = jnp.zeros_like(w[0])
    top = jnp.concatenate([w[0], zeros], axis=-1)
    bot = jnp.concatenate([zeros, w[1]], axis=-1)
    full = jnp.concatenate([top, bot], axis=-2)
    return full.reshape(2, CMP_STRIDE * LANES, LANES).astype(BF16)


def _compress_kernel(r_ref, w_ref, pos_ref, kc_ref, vc_ref):
    r = r_ref[0]
    first = _dot(r, w_ref[0])
    second = _dot(r, w_ref[1])
    nrow = first.shape[0]
    pos = pos_ref[...].astype(BF16)
    half = pos.shape[1] // 2
    const = _dot(pos[:, :half], w_ref[0]) + _dot(pos[:, half:], w_ref[1])
    out = first + pltpu.roll(second, nrow - 1, axis=0) + const[0:1, :]
    lane = _iota(out.shape, 1)
    kc_ref[0] = jnp.where(lane < HEAD_DIM, out, 0.0).astype(BF16)
    vc_ref[0] = pltpu.roll(jnp.where(lane >= HEAD_DIM, out, 0.0), HEAD_DIM, axis=1).astype(BF16)


def _compress(a_kvc, w_packed, cmp_pos):
    bsz, seq, _ = a_kvc.shape
    nrow = seq // CMP_STRIDE
    r = a_kvc.reshape(bsz, nrow, CMP_STRIDE * LANES)
    pos = jnp.concatenate([cmp_pos[0], cmp_pos[1]], axis=-1)
    pos = jnp.broadcast_to(pos.reshape(1, CMP_LEN * LANES), (8, CMP_LEN * LANES))
    spec = pl.BlockSpec((1, nrow, LANES), lambda b: (b, 0, 0))
    return pl.pallas_call(
        _compress_kernel, grid=(bsz,),
        in_specs=[pl.BlockSpec((1, nrow, CMP_STRIDE * LANES), lambda b: (b, 0, 0)),
                  pl.BlockSpec(w_packed.shape, lambda b: (0, 0, 0)),
                  pl.BlockSpec(pos.shape, lambda b: (0, 0))],
        out_specs=[spec, spec],
        out_shape=[jax.ShapeDtypeStruct((bsz, nrow, LANES), BF16)] * 2,
        compiler_params=_cparams(("parallel",)), name="nsa_compress",
    )(r, w_packed, pos)


def _cmp_to_slc(seq):
    n_cmp = (seq - CMP_LEN) // CMP_STRIDE + 1
    n_slc = seq // SLC_LEN
    ratio_s, ratio_c = SLC_LEN // CMP_STRIDE, CMP_LEN // CMP_STRIDE
    jj = np.arange(n_slc)[:, None, None]
    src = ratio_s * jj - np.arange(ratio_s)[None, :, None] - np.arange(ratio_c)[None, None, :]
    ok = (src >= 0) & (src < n_cmp)
    m = np.zeros((seq // CMP_STRIDE, n_slc), np.float32)
    np.add.at(m, (np.where(ok, src, 0), np.broadcast_to(jj, src.shape)), ok.astype(np.float32))
    return jnp.asarray(m)


def _softmax_step(s, m_ref, l_ref, acc_ref, v):
    m_old = m_ref[...]
    m_new = jnp.maximum(m_old, jnp.max(s, axis=-1, keepdims=True))
    alpha = jnp.exp(m_old - m_new)
    p = jnp.exp(s - m_new)
    l_ref[...] = alpha * l_ref[...] + jnp.sum(p, axis=-1, keepdims=True)
    acc_ref[...] = alpha * acc_ref[...] + _dot(p.astype(BF16), v)
    m_ref[...] = m_new


def _nsa_kernel(q_ref, qr_ref, kc_ref, vc_ref, ks_ref, vs_ref, kw_ref, vw_ref, sm_ref, c2s_ref,
                o_ref, m_ref, l_ref, acc_ref, *, tq, top_n):
    qi = pl.program_id(1)
    s0 = qi * tq
    rows = N_HEADS * tq
    n_slc = c2s_ref.shape[1]
    t_row = s0 + (_iota((rows, 1), 0) & (tq - 1))

    q4 = q_ref[0].reshape(rows, HEAD_DIM)
    sc = _dot_nt(q4, kc_ref[0][:, :HEAD_DIM])
    cmask = (_iota((1, sc.shape[1]), 1) * CMP_STRIDE + (CMP_LEN - 1)) <= t_row
    sc = jnp.where(cmask, sc, NEG)
    e = jnp.where(cmask, jnp.exp(sc - jnp.max(sc, axis=-1, keepdims=True)), 0.0)
    z = jnp.sum(e, axis=-1, keepdims=True)
    p_cmp = e / jnp.where(z > 0, z, 1.0)
    o_cmp = _dot(p_cmp.astype(BF16), vc_ref[0])
    imp4 = jnp.dot(p_cmp, c2s_ref[...], preferred_element_type=F32,
                   precision=lax.Precision.HIGHEST)
    imp = imp4[0:tq]
    for h in range(1, N_HEADS):
        imp = imp + imp4[h * tq:(h + 1) * tq]

    blk = _iota((1, n_slc), 1)
    cur = (s0 + _iota((tq, 1), 0)) >> 6
    forced = (blk == 0) | ((blk <= cur) & (blk > cur - FORCED_LOCAL))
    score = jnp.where(blk > cur, -1.0e6, jnp.where(forced, 1.0e6, imp))
    rank = jnp.zeros((tq, n_slc), F32)
    for i in range(n_slc):
        s_i = score[:, i:i + 1]
        ahead = (s_i > score) | ((s_i == score) & (blk > i))
        rank = rank + jnp.where(ahead, 1.0, 0.0)
    sel = (rank < top_n) & (blk <= cur)
    bias = jnp.where(sel, 0.0, NEG)
    if n_slc < HEAD_DIM:
        bias = jnp.concatenate([bias, jnp.zeros((tq, HEAD_DIM - n_slc), F32)], axis=1)
    bias4 = jnp.concatenate([bias] * N_HEADS, axis=0)
    qr4 = qr_ref[0].reshape(rows, HEAD_DIM).astype(F32)
    qsel = jnp.concatenate([qr4, bias4], axis=1).astype(BF16)

    def attend(lo, hi, k_ref, v_ref, window):
        m_ref[...] = jnp.full(m_ref.shape, NEG, F32)
        l_ref[...] = jnp.zeros(l_ref.shape, F32)
        acc_ref[...] = jnp.zeros(acc_ref.shape, F32)

        def body(j, carry):
            k0 = pl.multiple_of(j * tq, tq)
            s = _dot_nt(qsel, k_ref[0, pl.ds(k0, tq), :])
            kpos = k0 + _iota((1, tq), 1)
            ok = kpos <= t_row
            if window:
                ok = ok & (kpos > t_row - NSA_WINDOW)
            s = jnp.where(ok, s, NEG)
            _softmax_step(s, m_ref, l_ref, acc_ref, v_ref[0, pl.ds(k0, tq), :])
            return carry

        lax.fori_loop(lo, hi, body, 0)
        return acc_ref[...] / l_ref[...]

    o_slc = attend(0, qi + 1, ks_ref, vs_ref, False)
    o_win = attend(jnp.maximum(qi - NSA_WINDOW // tq, 0), qi + 1, kw_ref, vw_ref, True)

    g = jax.nn.sigmoid(sm_ref[0][:, SM_AG:SM_AG + 3 * N_HEADS])

    def gate(branch):
        return jnp.concatenate(
            [g[:, branch * N_HEADS + h:branch * N_HEADS + h + 1] for h in range(N_HEADS)], axis=0)

    o = gate(0) * o_cmp + gate(1) * o_slc + gate(2) * o_win
    o_ref[0] = o[:, :HEAD_DIM].reshape(N_HEADS, tq, HEAD_DIM).astype(o_ref.dtype)


def _nsa(q_hm, qr_hm, kc, vc, ks, vs, kw, vw, smalls, tq=128):
    bsz, _, seq, _ = q_hm.shape
    n_slc = seq // SLC_LEN
    c2s = _cmp_to_slc(seq)
    rows = N_HEADS * tq
    qspec = pl.BlockSpec((1, N_HEADS, tq, HEAD_DIM), lambda b, i: (b, 0, i, 0))
    full = lambda a: pl.BlockSpec((1,) + a.shape[1:], lambda b, i: (b, 0, 0))
    kern = functools.partial(_nsa_kernel, tq=tq, top_n=min(TOP_N, n_slc))
    return pl.pallas_call(
        kern, grid=(bsz, seq // tq),
        in_specs=[qspec, qspec, full(kc), full(vc), full(ks), full(vs), full(kw), full(vw),
                  pl.BlockSpec((1, tq, LANES), lambda b, i: (b, i, 0)),
                  pl.BlockSpec(c2s.shape, lambda b, i: (0, 0))],
        out_specs=qspec,
        out_shape=jax.ShapeDtypeStruct(q_hm.shape, BF16),
        scratch_shapes=[pltpu.VMEM((rows, 1), F32), pltpu.VMEM((rows, 1), F32),
                        pltpu.VMEM((rows, LANES), F32)],
        compiler_params=_cparams(("parallel", "arbitrary")), name="nsa_attention",
    )(q_hm, qr_hm, kc, vc, ks, vs, kw, vw, smalls, c2s)


def _diff_kernel(q_ref, kt_ref, v_ref, lam_ref, g_ref, o_ref, m_ref, l_ref, acc_ref,
                 *, tq, lam_init):
    qi = pl.program_id(2)
    s0 = qi * tq
    q = q_ref[0, 0]
    lane = _iota(q.shape, 1)
    zero = jnp.zeros_like(q)
    q2 = jnp.concatenate([jnp.where(lane < DIFF_HALF, q, zero),
                          jnp.where(lane >= DIFF_HALF, q, zero)], axis=0)
    t_row = s0 + (_iota((2 * tq, 1), 0) & (tq - 1))

    m_ref[...] = jnp.full(m_ref.shape, NEG, F32)
    l_ref[...] = jnp.zeros(l_ref.shape, F32)
    acc_ref[...] = jnp.zeros(acc_ref.shape, F32)

    def body(j, carry):
        k0 = pl.multiple_of(j * tq, tq)
        s = _dot(q2, kt_ref[0, :, pl.ds(k0, tq)])
        kpos = k0 + _iota((1, tq), 1)
        s = jnp.where(kpos <= t_row, s, NEG)
        _softmax_step(s, m_ref, l_ref, acc_ref, v_ref[0, 0, pl.ds(k0, tq), :])
        return carry

    lax.fori_loop(0, qi + 1, body, 0)

    lv = lam_ref[...]
    lam = (jnp.exp(jnp.sum(lv[0:1] * lv[1:2], axis=-1, keepdims=True))
           - jnp.exp(jnp.sum(lv[2:3] * lv[3:4], axis=-1, keepdims=True)) + lam_init)
    o_all = acc_ref[...] / l_ref[...]
    o = o_all[:tq] - lam * o_all[tq:]
    y = o * lax.rsqrt(jnp.mean(o * o, axis=-1, keepdims=True) + EPS)
    o_ref[0, 0] = ((y * g_ref[...]) * (1.0 - lam_init)).astype(o_ref.dtype)


def _diff(q_hm, kt, v_hm, lam_vecs, sub_g, lam_init, tq=256):
    bsz, _, seq, _ = q_hm.shape
    tq = min(tq, seq)
    hspec = pl.BlockSpec((1, 1, tq, HEAD_DIM), lambda b, h, i: (b, h, i, 0))
    kern = functools.partial(_diff_kernel, tq=tq, lam_init=lam_init)
    return pl.pallas_call(
        kern, grid=(bsz, N_HEADS, seq // tq),
        in_specs=[hspec,
                  pl.BlockSpec((1, HEAD_DIM, seq), lambda b, h, i: (b, h, 0)),
                  pl.BlockSpec((1, 1, seq, HEAD_DIM), lambda b, h, i: (b, h, 0, 0)),
                  pl.BlockSpec(lam_vecs.shape, lambda b, h, i: (0, 0)),
                  pl.BlockSpec((1, HEAD_DIM), lambda b, h, i: (0, 0))],
        out_specs=hspec,
        out_shape=jax.ShapeDtypeStruct(q_hm.shape, BF16),
        scratch_shapes=[pltpu.VMEM((2 * tq, 1), F32), pltpu.VMEM((2 * tq, 1), F32),
                        pltpu.VMEM((2 * tq, HEAD_DIM), F32)],
        compiler_params=_cparams(("parallel", "parallel", "arbitrary")), name="diff_attention",
    )(q_hm, kt, v_hm, lam_vecs, sub_g.reshape(1, HEAD_DIM))


def _dilated_bias(tq):
    max_back = max(w for w, _ in DILATED_PATTERNS) // tq
    classes = [0, 1, 2, 3, max_back]
    i = np.arange(tq)[:, None]
    j = np.arange(tq)[None, :]
    out = []
    for d in classes:
        delta = d * tq + i - j
        cnt = np.zeros((tq, tq), np.float64)
        for w, dil in DILATED_PATTERNS:
            cnt += (delta >= 0) & (delta <= w) & (delta % dil == 0)
        out.append(np.where(cnt > 0, np.log(np.maximum(cnt, 1.0)), NEG))
    return jnp.asarray(np.stack(out).astype(np.float32)), max_back


def _dilated_kernel(q_ref, kt_ref, v_ref, bias_ref, o_ref, m_ref, l_ref, acc_ref, *, tq, max_back):
    qi = pl.program_id(2)
    q = q_ref[0, 0]
    m_ref[...] = jnp.full(m_ref.shape, NEG, F32)
    l_ref[...] = jnp.zeros(l_ref.shape, F32)
    acc_ref[...] = jnp.zeros(acc_ref.shape, F32)

    def body(j, carry):
        k0 = pl.multiple_of(j * tq, tq)
        d = qi - j
        cls = jnp.where(d < 3, d, jnp.where(d == max_back, 4, 3))
        s = _dot(q, kt_ref[0, :, pl.ds(k0, tq)]) + bias_ref[cls]
        _softmax_step(s, m_ref, l_ref, acc_ref, v_ref[0, 0, pl.ds(k0, tq), :])
        return carry

    lax.fori_loop(jnp.maximum(qi - max_back, 0), qi + 1, body, 0)
    o_ref[0, 0] = (acc_ref[...] / l_ref[...]).astype(o_ref.dtype)


def _dilated(q_hm, kt, v_hm, tq=256):
    bsz, _, seq, _ = q_hm.shape
    tq = min(tq, seq)
    bias, max_back = _dilated_bias(tq)
    hspec = pl.BlockSpec((1, 1, tq, HEAD_DIM), lambda b, h, i: (b, h, i, 0))
    kern = functools.partial(_dilated_kernel, tq=tq, max_back=max_back)
    return pl.pallas_call(
        kern, grid=(bsz, N_HEADS, seq // tq),
        in_specs=[hspec,
                  pl.BlockSpec((1, HEAD_DIM, seq), lambda b, h, i: (b, h, 0)),
                  pl.BlockSpec((1, 1, seq, HEAD_DIM), lambda b, h, i: (b, h, 0, 0)),
                  pl.BlockSpec(bias.shape, lambda b, h, i: (0, 0, 0))],
        out_specs=hspec,
        out_shape=jax.ShapeDtypeStruct(q_hm.shape, BF16),
        scratch_shapes=[pltpu.VMEM((tq, 1), F32), pltpu.VMEM((tq, 1), F32),
                        pltpu.VMEM((tq, HEAD_DIM), F32)],
        compiler_params=_cparams(("parallel", "parallel", "arbitrary")), name="dilated_attention",
    )(q_hm, kt, v_hm, bias)


def _mlstm_kernel(u_ref, up_ref, v_ref, sm_ref, o_ref, cw_ref, cb_ref, wq_ref, wk_ref,
                  gb_ref, hg_ref, out_ref, c_st, n_st, m_st):
    ci = pl.program_id(1)
    L = MLSTM_CHUNK

    @pl.when(ci == 0)
    def _():
        c_st[...] = jnp.zeros(c_st.shape, F32)
        n_st[...] = jnp.zeros(n_st.shape, F32)
        m_st[...] = jnp.zeros(m_st.shape, F32)

    tail = jnp.where(ci > 0, up_ref[0], 0.0)
    ext = jnp.concatenate([tail, u_ref[0]], axis=0)
    cw = cw_ref[...]
    uc = cb_ref[...] + cw[MLSTM_CONV - 1:MLSTM_CONV] * ext[8:]
    for j in range(MLSTM_CONV - 1):
        shifted = pltpu.roll(ext, MLSTM_CONV - 1 - j, axis=0)[8:]
        uc = uc + cw[j:j + 1] * shifted
    uc = (uc * jax.nn.sigmoid(uc)).astype(BF16)

    sm = sm_ref[0]
    gb = gb_ref[...]
    tri = (_iota((L, L), 1) <= _iota((L, L), 0))
    eye = _iota((L, L), 1) == _iota((L, L), 0)
    eye_bf = jnp.where(eye, 1.0, 0.0).astype(BF16)
    ig_all = sm[:, SM_CI:SM_CI + N_HEADS] + gb[0:1]
    lf_all = jax.nn.log_sigmoid(sm[:, SM_CF:SM_CF + N_HEADS] + gb[1:2])
    b_all = jnp.dot(tri.astype(F32), lf_all, preferred_element_type=F32,
                    precision=lax.Precision.HIGHEST)
    hg = hg_ref[...]
    og = jax.nn.sigmoid(o_ref[0])

    outs = []
    for h in range(N_HEADS):
        hs = slice(h * HEAD_DIM, (h + 1) * HEAD_DIM)
        uch = uc[:, hs]
        q = _dot(uch, wq_ref[h])
        k = _dot(uch, wk_ref[h]) * (HEAD_DIM ** -0.5)
        qb, kb = q.astype(BF16), k.astype(BF16)
        vv = v_ref[0][:, hs]
        b_col = b_all[:, h:h + 1]
        ig_col = ig_all[:, h:h + 1]
        src_col = ig_col - b_col
        src_row = jnp.sum(jnp.where(eye, src_col, 0.0), axis=0, keepdims=True)
        dmat = jnp.where(tri, b_col + src_row, NEG)
        a = b_col[L - 1:L]
        g_end = a + src_col
        m_loc = jnp.max(g_end, axis=0, keepdims=True)
        w_end = jnp.exp(g_end - m_loc)
        kw = k * w_end
        vt = _dot_nt(eye_bf, vv).astype(BF16)
        c_loc = _dot(vt, kw.astype(BF16))
        n_loc = jnp.sum(kw, axis=0, keepdims=True)

        c_in, n_in, m_in = c_st[h], n_st[h], m_st[h]
        inter = b_col + m_in
        m_t = jnp.maximum(inter, jnp.max(dmat, axis=-1, keepdims=True))
        e_inter = jnp.exp(inter - m_t)
        s_qk = _dot_nt(qb, kb) * jnp.exp(dmat - m_t)
        num = e_inter * _dot_nt(qb, c_in.astype(BF16)) + _dot(s_qk.astype(BF16), vv)
        den = (e_inter * jnp.sum(q * n_in, axis=-1, keepdims=True)
               + jnp.sum(s_qk, axis=-1, keepdims=True))
        hh = num / jnp.maximum(jnp.abs(den), jnp.exp(-m_t))
        hh = hh * lax.rsqrt(jnp.mean(hh * hh, axis=-1, keepdims=True) + EPS) * hg[:, hs]
        outs.append(hh * og[:, hs])

        m_new = jnp.maximum(a + m_in, m_loc)
        decay = jnp.exp(a + m_in - m_new)
        fresh = jnp.exp(m_loc - m_new)
        c_st[h] = decay * c_in + fresh * c_loc
        n_st[h] = decay * n_in + fresh * n_loc
        m_st[h] = m_new

    out_ref[0] = jnp.concatenate(outs, axis=1).astype(out_ref.dtype)


def _mlstm(u, v, smalls, o_pre, conv_w, conv_b, wq, wk, gate_b, head_g):
    bsz, seq, _ = u.shape
    L = MLSTM_CHUNK
    row = pl.BlockSpec((1, L, GROUP_WIDTH), lambda b, c: (b, c, 0))
    const2 = lambda a: pl.BlockSpec(a.shape, lambda b, c: (0,) * a.ndim)
    conv_b = conv_b.reshape(1, GROUP_WIDTH)
    head_g = head_g.reshape(1, GROUP_WIDTH)
    wq, wk = wq.astype(BF16), wk.astype(BF16)
    return pl.pallas_call(
        _mlstm_kernel, grid=(bsz, seq // L),
        in_specs=[row,
                  pl.BlockSpec((1, 8, GROUP_WIDTH), lambda b, c: (b, jnp.maximum(c * (L // 8) - 1, 0), 0)),
                  row,
                  pl.BlockSpec((1, L, LANES), lambda b, c: (b, c, 0)),
                  row, const2(conv_w), const2(conv_b), const2(wq), const2(wk), const2(gate_b),
                  const2(head_g)],
        out_specs=row,
        out_shape=jax.ShapeDtypeStruct((bsz, seq, GROUP_WIDTH), BF16),
        scratch_shapes=[pltpu.VMEM((N_HEADS, HEAD_DIM, HEAD_DIM), F32),
                        pltpu.VMEM((N_HEADS, 1, HEAD_DIM), F32),
                        pltpu.VMEM((N_HEADS, 1, 1), F32)],
        compiler_params=_cparams(("parallel", "arbitrary")), name="mlstm",
    )(u, u, v, smalls, o_pre, conv_w, conv_b, wq, wk, gate_b, head_g)


def _out_ffn_kernel(x_ref, oa_ref, ob_ref, oc_ref, od_ref, wo_ref, g_ref, wg_ref, wu_ref, wd_ref,
                    gf_ref, y_ref, *, final, ff_chunk):
    mixed = jnp.concatenate([oa_ref[...], ob_ref[...], oc_ref[...], od_ref[...]], axis=1)
    x = x_ref[...] + _dot(mixed, wo_ref[...])
    h = x * lax.rsqrt(jnp.mean(x * x, axis=-1, keepdims=True) + EPS)
    h = (h * g_ref[...]).astype(BF16)
    ffn = None
    for c0 in range(0, D_FF, ff_chunk):
        gate = _dot(h, wg_ref[:, c0:c0 + ff_chunk])
        up = _dot(h, wu_ref[:, c0:c0 + ff_chunk])
        act = (gate * jax.nn.sigmoid(gate) * up).astype(BF16)
        part = _dot(act, wd_ref[c0:c0 + ff_chunk, :])
        ffn = part if ffn is None else ffn + part
    y = x + ffn
    if final:
        y = y * lax.rsqrt(jnp.mean(y * y, axis=-1, keepdims=True) + EPS) * gf_ref[...]
    y_ref[...] = y


def _out_ffn(x2, o_a, o_b, o_c, o_d, w_out, gain, w_gate, w_up, w_down, gain_final, final, tm=256):
    n = x2.shape[0]
    row = lambda w: pl.BlockSpec((tm, w), lambda i: (i, 0))
    const = lambda a: pl.BlockSpec(a.shape, lambda i: (0, 0))
    gain = gain.reshape(1, D_MODEL)
    gain_final = gain_final.reshape(1, D_MODEL)
    kern = functools.partial(_out_ffn_kernel, final=final, ff_chunk=256)
    return pl.pallas_call(
        kern, grid=(n // tm,),
        in_specs=[row(D_MODEL)] + [row(GROUP_WIDTH)] * 4
                 + [const(w_out), const(gain), const(w_gate), const(w_up), const(w_down),
                    const(gain_final)],
        out_specs=row(D_MODEL),
        out_shape=jax.ShapeDtypeStruct((n, D_MODEL), F32),
        compiler_params=_cparams(("parallel",)), name="out_ffn",
    )(x2, o_a, o_b, o_c, o_d, w_out, gain, w_gate, w_up, w_down, gain_final)


def _head_major(t, bsz, seq):
    return t.reshape(bsz, seq, N_HEADS, HEAD_DIM).transpose(0, 2, 1, 3)


def _from_head_major(t, bsz, seq):
    return t.transpose(0, 2, 1, 3).reshape(bsz * seq, GROUP_WIDTH)


def _channel_major(t, bsz, seq):
    return t.reshape(bsz, seq, GROUP_WIDTH).transpose(0, 2, 1)


def kernel(x, norm_mix, w_in, nsa_cmp_pos, nsa_cmp_w, diff_lambda, diff_norm, mlstm_conv_w,
           mlstm_conv_b, mlstm_wq, mlstm_wk, mlstm_gate_b, mlstm_norm, w_out, norm_ffn, w_gate,
           w_up, w_down, norm_final):
    bsz, seq, _ = x.shape
    depth = w_in.shape[0]
    tables = _lane_tables(seq)
    x2 = x.reshape(bsz * seq, D_MODEL)
    r3 = lambda t: t.reshape(bsz, seq, t.shape[-1])

    for layer in range(depth):
        (a_q, a_qr, a_kvc, a_ks, a_vs, a_kw, a_vw, smalls, b_q, b_k, b_v, c_u, c_v, c_o,
         d_q, d_k, d_v) = _in_proj(x2, norm_mix[layer], _pack_w_in(w_in[layer]), tables, seq,
                                   tm=min(256, seq))
        smalls3 = r3(smalls)

        kc, vc = _compress(r3(a_kvc), _pack_cmp_w(nsa_cmp_w[layer]), nsa_cmp_pos[layer])
        o_a = _nsa(_head_major(a_q, bsz, seq), _head_major(a_qr, bsz, seq), kc, vc,
                   r3(a_ks), r3(a_vs), r3(a_kw), r3(a_vw), smalls3)
        o_a = _from_head_major(o_a, bsz, seq)

        lam_init = 0.8 - 0.6 * math.exp(-0.3 * layer)
        o_b = _diff(_head_major(b_q, bsz, seq), _channel_major(b_k, bsz, seq),
                    _head_major(b_v, bsz, seq), diff_lambda[layer], diff_norm[layer], lam_init)
        o_b = _from_head_major(o_b, bsz, seq)

        o_c = _mlstm(r3(c_u), r3(c_v), smalls3, r3(c_o),
                     mlstm_conv_w[layer], mlstm_conv_b[layer], mlstm_wq[layer], mlstm_wk[layer],
                     mlstm_gate_b[layer], mlstm_norm[layer])
        o_c = o_c.reshape(bsz * seq, GROUP_WIDTH)

        o_d = _dilated(_head_major(d_q, bsz, seq), _channel_major(d_k, bsz, seq),
                       _head_major(d_v, bsz, seq))
        o_d = _from_head_major(o_d, bsz, seq)

        x2 = _out_ffn(x2, o_a, o_b, o_c, o_d, w_out[layer].astype(BF16), norm_ffn[layer],
                      w_gate[layer].astype(BF16), w_up[layer].astype(BF16),
                      w_down[layer].astype(BF16), norm_final, final=(layer == depth - 1))
    return x2.reshape(bsz, seq, D_MODEL)
```

```python
import functools
import math

import numpy as np
import jax
import jax.numpy as jnp
from jax import lax
from jax.experimental import pallas as pl
from jax.experimental.pallas import tpu as pltpu

F32 = jnp.float32
BF16 = jnp.bfloat16

D_MODEL = 1024
HEAD_DIM = 64
N_HEADS = 4
GROUP_WIDTH = N_HEADS * HEAD_DIM
ROPE_THETA = 10000.0
EPS = 1e-6
NEG = -1e30
LOG2E = math.log2(math.e)

CMP_LEN = 32
CMP_STRIDE = 16
SLC_LEN = 64
TOP_N = 16
NSA_WINDOW = 512
FORCED_LOCAL = 2
DIFF_HALF = HEAD_DIM // 2
MLSTM_CHUNK = 64
MLSTM_CONV = 4
DILATED_PATTERNS = ((128, 1), (512, 4), (2048, 16))
D_FF = ((8 * D_MODEL + 3 * 256 - 1) // (3 * 256)) * 256

LANES = 128
BF16_SUBLANES = 16
VMEM_LIMIT = 56 * 1024 * 1024
ACC_ROWS = HEAD_DIM + BF16_SUBLANES

IN_SPLITS = (
    GROUP_WIDTH, HEAD_DIM, HEAD_DIM, HEAD_DIM, HEAD_DIM, HEAD_DIM, HEAD_DIM, 3 * N_HEADS,
    GROUP_WIDTH, GROUP_WIDTH, GROUP_WIDTH,
    GROUP_WIDTH, GROUP_WIDTH, N_HEADS, N_HEADS, GROUP_WIDTH,
    GROUP_WIDTH, GROUP_WIDTH, GROUP_WIDTH,
)
(A_Q, A_KC, A_VC, A_KS, A_VS, A_KW, A_VW, A_G, B_Q, B_K, B_V,
 C_U, C_V, C_I, C_F, C_O, D_Q, D_K, D_V) = range(19)
SM_AG, SM_CI, SM_CF = 0, 12, 16


def _cparams(sem):
    return pltpu.CompilerParams(dimension_semantics=sem, vmem_limit_bytes=VMEM_LIMIT)


def _iota(shape, dim):
    return lax.broadcasted_iota(jnp.int32, shape, dim)


def _dot(a, b):
    return jnp.dot(a, b, preferred_element_type=F32)


def _dot_nt(a, b):
    return lax.dot_general(a, b, (((1,), (1,)), ((), ())), preferred_element_type=F32)


def _pack_w_in(w):
    offs = np.concatenate([[0], np.cumsum(IN_SPLITS)])
    col = lambda i: w[:, int(offs[i]):int(offs[i + 1])]
    z64 = jnp.zeros((w.shape[0], HEAD_DIM), w.dtype)
    smalls = jnp.concatenate(
        [col(A_G), col(C_I), col(C_F), jnp.zeros((w.shape[0], LANES - 20), w.dtype)], axis=1)
    parts = [col(A_Q), col(A_KC), col(A_VC), col(A_KS), z64, col(A_VS), z64, col(A_KW), z64,
             col(A_VW), z64, smalls, col(B_Q), col(B_K), col(B_V), col(C_U), col(C_V), col(C_O),
             col(D_Q), col(D_K), col(D_V)]
    return jnp.concatenate(parts, axis=1).astype(BF16)


def _rope_tables(seq, dim):
    inv = 1.0 / (ROPE_THETA ** (jnp.arange(0, dim, 2, dtype=F32) / dim))
    ang = jnp.arange(seq, dtype=F32)[:, None] * inv[None, :]
    return jnp.cos(ang), jnp.sin(ang)


def _lane_tables(seq):
    c64, s64 = _rope_tables(seq, HEAD_DIM)
    c32, s32 = _rope_tables(seq, DIFF_HALF)
    t64c = jnp.concatenate([c64, c64] * (LANES // HEAD_DIM), axis=1)
    t64s = jnp.concatenate([-s64, s64] * (LANES // HEAD_DIM), axis=1)
    t32c = jnp.concatenate([c32, c32] * (LANES // DIFF_HALF), axis=1)
    t32s = jnp.concatenate([-s32, s32] * (LANES // DIFF_HALF), axis=1)
    pos = np.arange(seq)[:, None]
    lane = np.arange(LANES)[None, :]
    onehot = ((lane >= HEAD_DIM) & ((pos // SLC_LEN) % HEAD_DIM == lane - HEAD_DIM)
              ).astype(np.float32)
    return t64c, t64s, t32c, t32s, jnp.asarray(onehot)


def _swap_halves(x, group):
    width = x.shape[-1]
    half = group // 2
    lane = _iota(x.shape, 1) & (group - 1)
    up = pltpu.roll(x, width - half, axis=1)
    down = pltpu.roll(x, half, axis=1)
    return jnp.where(lane < half, up, down)


def _rope(x, cos_t, sin_t, group):
    reps = x.shape[-1] // LANES
    if reps > 1:
        cos_t = jnp.concatenate([cos_t] * reps, axis=1)
        sin_t = jnp.concatenate([sin_t] * reps, axis=1)
    return x * cos_t + _swap_halves(x, group) * sin_t


def _in_proj_kernel(x_ref, g_ref, w_ref, c64_ref, s64_ref, c32_ref, s32_ref, oh_ref,
                    aq_ref, aqr_ref, akvc_ref, aks_ref, avs_ref, akw_ref, avw_ref, sm_ref,
                    bq_ref, bk_ref, bv_ref, cu_ref, cv_ref, co_ref, dq_ref, dk_ref, dv_ref):
    x = x_ref[...]
    h = x * lax.rsqrt(jnp.mean(x * x, axis=-1, keepdims=True) + EPS)
    h = (h * g_ref[...]).astype(BF16)
    c64, s64 = c64_ref[...], s64_ref[...]
    c32, s32 = c32_ref[...], s32_ref[...]

    def mm(c0, c1):
        return _dot(h, w_ref[:, c0 * LANES:c1 * LANES])

    zq = mm(0, 2) * (HEAD_DIM ** -0.5 * LOG2E)
    aq_ref[...] = zq.astype(BF16)
    aqr_ref[...] = _rope(zq, c64, s64, HEAD_DIM).astype(BF16)
    akvc_ref[...] = mm(2, 3).astype(BF16)
    aks_ref[...] = (_rope(mm(3, 4), c64, s64, HEAD_DIM) + oh_ref[...]).astype(BF16)
    avs_ref[...] = mm(4, 5).astype(BF16)
    akw_ref[...] = _rope(mm(5, 6), c64, s64, HEAD_DIM).astype(BF16)
    avw_ref[...] = mm(6, 7).astype(BF16)
    sm_ref[...] = mm(7, 8)
    bq_ref[...] = (_rope(mm(8, 10), c32, s32, DIFF_HALF) * (DIFF_HALF ** -0.5 * LOG2E)).astype(BF16)
    bk_ref[...] = _rope(mm(10, 12), c32, s32, DIFF_HALF).astype(BF16)
    bv_ref[...] = mm(12, 14).astype(BF16)
    cu_ref[...] = mm(14, 16)
    cv_ref[...] = mm(16, 18).astype(BF16)
    co_ref[...] = mm(18, 20)
    dq_ref[...] = (_rope(mm(20, 22), c64, s64, HEAD_DIM) * (HEAD_DIM ** -0.5 * LOG2E)).astype(BF16)
    dk_ref[...] = _rope(mm(22, 24), c64, s64, HEAD_DIM).astype(BF16)
    dv_ref[...] = mm(24, 26).astype(BF16)


_IN_PROJ_OUTS = (
    (256, BF16), (256, BF16), (128, BF16), (128, BF16), (128, BF16), (128, BF16), (128, BF16),
    (128, F32), (256, BF16), (256, BF16), (256, BF16), (256, F32), (256, BF16), (256, F32),
    (256, BF16), (256, BF16), (256, BF16))


def _in_proj(x2, gain, w_packed, tables, seq, tm):
    n = x2.shape[0]
    nblk_s = seq // tm
    row = lambda i: (i, 0)
    tab = lambda i: (i % nblk_s, 0)
    const = lambda i: (0, 0)
    in_specs = [pl.BlockSpec((tm, D_MODEL), row),
                pl.BlockSpec((1, D_MODEL), const),
                pl.BlockSpec(w_packed.shape, const)]
    in_specs += [pl.BlockSpec((tm, LANES), tab)] * 5
    out_specs = [pl.BlockSpec((tm, w), row) for w, _ in _IN_PROJ_OUTS]
    out_shape = [jax.ShapeDtypeStruct((n, w), dt) for w, dt in _IN_PROJ_OUTS]
    return pl.pallas_call(
        _in_proj_kernel, grid=(n // tm,), in_specs=in_specs, out_specs=out_specs,
        out_shape=out_shape, compiler_params=_cparams(("parallel",)), name="in_proj",
    )(x2, gain.reshape(1, D_MODEL), w_packed, *tables)


def _pack_cmp_w(cmp_w):
    w = cmp_w.reshape(2, 2, CMP_STRIDE, HEAD_DIM, HEAD_DIM)
    zeros = jnp.zeros_like(w[0])
    top = jnp.concatenate([w[0], zeros], axis=-1)
    bot = jnp.concatenate([zeros, w[1]], axis=-1)
    full = jnp.concatenate([top, bot], axis=-2)
    return full.reshape(2, CMP_STRIDE * LANES, LANES).astype(BF16)


def _compress_kernel(r_ref, w_ref, pos_ref, kc_ref, kvct_ref):
    r = r_ref[0]
    first = _dot(r, w_ref[0])
    second = _dot(r, w_ref[1])
    nrow = first.shape[0]
    pos = pos_ref[...].astype(BF16)
    half = pos.shape[1] // 2
    const = _dot(pos[:, :half], w_ref[0]) + _dot(pos[:, half:], w_ref[1])
    out = first + pltpu.roll(second, nrow - 1, axis=0) + const[0:1, :]
    lane = _iota(out.shape, 1)
    kc_ref[0] = jnp.where(lane < HEAD_DIM, out, 0.0).astype(BF16)
    kvct_ref[0] = out.T.astype(BF16)


def _compress(a_kvc, w_packed, cmp_pos):
    bsz, seq, _ = a_kvc.shape
    nrow = seq // CMP_STRIDE
    r = a_kvc.reshape(bsz, nrow, CMP_STRIDE * LANES)
    pos = jnp.concatenate([cmp_pos[0], cmp_pos[1]], axis=-1)
    pos = jnp.broadcast_to(pos.reshape(1, CMP_LEN * LANES), (8, CMP_LEN * LANES))
    return pl.pallas_call(
        _compress_kernel, grid=(bsz,),
        in_specs=[pl.BlockSpec((1, nrow, CMP_STRIDE * LANES), lambda b: (b, 0, 0)),
                  pl.BlockSpec(w_packed.shape, lambda b: (0, 0, 0)),
                  pl.BlockSpec(pos.shape, lambda b: (0, 0))],
        out_specs=[pl.BlockSpec((1, nrow, LANES), lambda b: (b, 0, 0)),
                   pl.BlockSpec((1, LANES, nrow), lambda b: (b, 0, 0))],
        out_shape=[jax.ShapeDtypeStruct((bsz, nrow, LANES), BF16),
                   jax.ShapeDtypeStruct((bsz, LANES, nrow), BF16)],
        compiler_params=_cparams(("parallel",)), name="nsa_compress",
    )(r, w_packed, pos)


def _flash_init(m_ref, acc_ref):
    m_ref[...] = jnp.full(m_ref.shape, NEG, F32)
    acc_ref[...] = jnp.zeros(acc_ref.shape, F32)


def _flash_step(s, values, m_ref, acc_ref):
    m_old = m_ref[...]
    m_new = jnp.maximum(m_old, jnp.max(s, axis=0, keepdims=True))
    alpha = jnp.exp2(m_old - m_new)
    p = jnp.exp2(s - m_new).astype(BF16)
    acc_ref[...] = alpha * acc_ref[...] + _dot(values, p)
    m_ref[...] = m_new


def _with_ones(vt):
    return jnp.concatenate([vt, jnp.ones((BF16_SUBLANES, vt.shape[1]), BF16)], axis=0)


def _cmp_to_slc_t(seq):
    n_cmp = (seq - CMP_LEN) // CMP_STRIDE + 1
    n_slc = seq // SLC_LEN
    ratio_s, ratio_c = SLC_LEN // CMP_STRIDE, CMP_LEN // CMP_STRIDE
    jj = np.arange(n_slc)[:, None, None]
    src = ratio_s * jj - np.arange(ratio_s)[None, :, None] - np.arange(ratio_c)[None, None, :]
    ok = (src >= 0) & (src < n_cmp)
    m = np.zeros((seq // CMP_STRIDE, n_slc), np.float32)
    np.add.at(m, (np.where(ok, src, 0), np.broadcast_to(jj, src.shape)), ok.astype(np.float32))
    return jnp.asarray(m.T)


def _nsa_kernel(qt_ref, qrt_ref, kc_ref, kvct_ref, ks_ref, vst_ref, kw_ref, vwt_ref, sm_ref,
                c2st_ref, o_ref, m_ref, acc_ref, *, tq, top_n):
    qi = pl.program_id(1)
    s0 = qi * tq
    rows = N_HEADS * tq
    n_slc = c2st_ref.shape[0]
    lane_t = s0 + (_iota((1, rows), 1) & (tq - 1))

    def heads_on_lanes(ref):
        x = ref[0]
        return jnp.concatenate([x[h * HEAD_DIM:(h + 1) * HEAD_DIM, :] for h in range(N_HEADS)],
                               axis=1)

    q4 = jnp.concatenate([heads_on_lanes(qt_ref), jnp.zeros((HEAD_DIM, rows), BF16)], axis=0)
    sc = _dot(kc_ref[0], q4)
    cmask = (_iota((sc.shape[0], 1), 0) * CMP_STRIDE + (CMP_LEN - 1)) <= lane_t
    sc = jnp.where(cmask, sc, NEG)
    e = jnp.where(cmask, jnp.exp2(sc - jnp.max(sc, axis=0, keepdims=True)), 0.0)
    z = jnp.sum(e, axis=0, keepdims=True)
    p_cmp = e * (1.0 / jnp.where(z > 0, z, 1.0))
    o_cmp = _dot(kvct_ref[0][HEAD_DIM:2 * HEAD_DIM, :], p_cmp.astype(BF16))
    imp4 = jnp.dot(c2st_ref[...], p_cmp, preferred_element_type=F32,
                   precision=lax.Precision.HIGHEST)
    imp = imp4[:, 0:tq]
    for h in range(1, N_HEADS):
        imp = imp + imp4[:, h * tq:(h + 1) * tq]

    blk = _iota((n_slc, 1), 0)
    cur = (s0 + _iota((1, tq), 1)) >> 6
    forced = (blk == 0) | ((blk <= cur) & (blk > cur - FORCED_LOCAL))
    score = jnp.where(blk > cur, -1.0e6, jnp.where(forced, 1.0e6, imp))
    rank = jnp.zeros((n_slc, tq), F32)
    for i in range(n_slc):
        s_i = score[i:i + 1, :]
        rank = rank + jnp.where(blk > i, jnp.where(s_i >= score, 1.0, 0.0),
                                jnp.where(s_i > score, 1.0, 0.0))
    sel = (rank < top_n) & (blk <= cur)
    bias = jnp.where(sel, 0.0, NEG)
    if n_slc < HEAD_DIM:
        bias = jnp.concatenate([bias, jnp.zeros((HEAD_DIM - n_slc, tq), F32)], axis=0)
    bias4 = jnp.concatenate([bias] * N_HEADS, axis=1).astype(BF16)
    qsel = jnp.concatenate([heads_on_lanes(qrt_ref), bias4], axis=0)

    def attend(k_ref, vt_ref, lo, window):
        _flash_init(m_ref, acc_ref)

        def tile(j, diagonal):
            k0 = pl.multiple_of(j * tq, tq)
            s = _dot(k_ref[0, pl.ds(k0, tq), :], qsel)
            kpos = k0 + _iota((tq, 1), 0)
            ok = None
            if diagonal:
                ok = kpos <= lane_t
            if window:
                low = kpos > lane_t - NSA_WINDOW
                ok = low if ok is None else ok & low
            if ok is not None:
                s = jnp.where(ok, s, NEG)
            _flash_step(s, _with_ones(vt_ref[0, :, pl.ds(k0, tq)]), m_ref, acc_ref)

        def body(j, carry):
            tile(j, False)
            return carry

        lax.fori_loop(lo, qi, body, 0)
        tile(qi, True)
        acc = acc_ref[...]
        return acc[0:HEAD_DIM] * (1.0 / acc[HEAD_DIM:HEAD_DIM + 1])

    o_slc = attend(ks_ref, vst_ref, 0, False)
    o_win = attend(kw_ref, vwt_ref, jnp.maximum(qi - NSA_WINDOW // tq, 0), True)

    g = jax.nn.sigmoid(sm_ref[0].T[0:BF16_SUBLANES, :])

    def gate(branch):
        return jnp.concatenate(
            [g[branch * N_HEADS + h:branch * N_HEADS + h + 1, :] for h in range(N_HEADS)], axis=1)

    o = gate(0) * o_cmp + gate(1) * o_slc + gate(2) * o_win
    for h in range(N_HEADS):
        o_ref[0, h * HEAD_DIM:(h + 1) * HEAD_DIM, :] = o[:, h * tq:(h + 1) * tq].astype(o_ref.dtype)


def _nsa(qt, qrt, kc, kvct, ks, vst, kw, vwt, smalls, tq=256):
    bsz, _, seq = qt.shape
    tq = min(tq, seq)
    n_slc = seq // SLC_LEN
    c2st = _cmp_to_slc_t(seq)
    rows = N_HEADS * tq
    qspec = pl.BlockSpec((1, GROUP_WIDTH, tq), lambda b, i: (b, 0, i))
    full = lambda a: pl.BlockSpec((1,) + a.shape[1:], lambda b, i: (b, 0, 0))
    vspec = pl.BlockSpec((1, HEAD_DIM, seq), lambda b, i: (b, 0, 0))
    kern = functools.partial(_nsa_kernel, tq=tq, top_n=min(TOP_N, n_slc))
    return pl.pallas_call(
        kern, grid=(bsz, seq // tq),
        in_specs=[qspec, qspec, full(kc), full(kvct), full(ks), vspec, full(kw), vspec,
                  pl.BlockSpec((1, tq, LANES), lambda b, i: (b, i, 0)),
                  pl.BlockSpec(c2st.shape, lambda b, i: (0, 0))],
        out_specs=qspec,
        out_shape=jax.ShapeDtypeStruct(qt.shape, BF16),
        scratch_shapes=[pltpu.VMEM((1, rows), F32), pltpu.VMEM((ACC_ROWS, rows), F32)],
        compiler_params=_cparams(("parallel", "arbitrary")), name="nsa_attention",
    )(qt, qrt, kc, kvct, ks, vst, kw, vwt, smalls, c2st)


def _diff_kernel(qt_ref, k_ref, vt_ref, lam_ref, g_ref, o_ref, m_ref, acc_ref, *, tq, lam_init):
    qi = pl.program_id(2)
    s0 = qi * tq
    rows = 4 * tq
    qt = qt_ref[0]
    row = _iota((2 * HEAD_DIM, 1), 0)
    zero = jnp.zeros_like(qt)
    qmat = jnp.concatenate(
        [jnp.where((row >= DIFF_HALF * c) & (row < DIFF_HALF * (c + 1)), qt, zero)
         for c in range(4)], axis=1)
    lane_t = s0 + (_iota((1, rows), 1) & (tq - 1))
    ones = jnp.ones((BF16_SUBLANES, tq), BF16)

    def tile(j, diagonal):
        k0 = pl.multiple_of(j * tq, tq)
        s = _dot(k_ref[0, pl.ds(k0, tq), :], qmat)
        if diagonal:
            s = jnp.where(k0 + _iota((tq, 1), 0) <= lane_t, s, NEG)
        vt = vt_ref[0, :, pl.ds(k0, tq)]
        values = jnp.concatenate([vt[0:HEAD_DIM], ones, vt[HEAD_DIM:], ones], axis=0)
        _flash_step(s, values, m_ref, acc_ref)

    _flash_init(m_ref, acc_ref)

    def body(j, carry):
        tile(j, False)
        return carry

    lax.fori_loop(0, qi, body, 0)
    tile(qi, True)

    lv = lam_ref[...]
    lam = (jnp.exp(jnp.sum(lv[0:1] * lv[1:2], axis=-1, keepdims=True))
           - jnp.exp(jnp.sum(lv[2:3] * lv[3:4], axis=-1, keepdims=True)) + lam_init)
    acc = acc_ref[...]
    for hh in range(2):
        a = acc[ACC_ROWS * hh:ACC_ROWS * (hh + 1), 2 * tq * hh:2 * tq * (hh + 1)]
        o_all = a[0:HEAD_DIM] * (1.0 / a[HEAD_DIM:HEAD_DIM + 1])
        o = o_all[:, :tq] - lam * o_all[:, tq:]
        y = o * lax.rsqrt(jnp.mean(o * o, axis=0, keepdims=True) + EPS)
        o_ref[0, HEAD_DIM * hh:HEAD_DIM * (hh + 1), :] = (
            (y * g_ref[...]) * (1.0 - lam_init)).astype(o_ref.dtype)


def _diff(qt, k, vt, lam_vecs, sub_g, lam_init, tq=256):
    bsz, _, seq = qt.shape
    tq = min(tq, seq)
    pair = pl.BlockSpec((1, 2 * HEAD_DIM, tq), lambda b, p, i: (b, p, i))
    kern = functools.partial(_diff_kernel, tq=tq, lam_init=lam_init)
    return pl.pallas_call(
        kern, grid=(bsz, N_HEADS // 2, seq // tq),
        in_specs=[pair,
                  pl.BlockSpec((1, seq, 2 * HEAD_DIM), lambda b, p, i: (b, 0, p)),
                  pl.BlockSpec((1, 2 * HEAD_DIM, seq), lambda b, p, i: (b, p, 0)),
                  pl.BlockSpec(lam_vecs.shape, lambda b, p, i: (0, 0)),
                  pl.BlockSpec((HEAD_DIM, 1), lambda b, p, i: (0, 0))],
        out_specs=pair,
        out_shape=jax.ShapeDtypeStruct(qt.shape, BF16),
        scratch_shapes=[pltpu.VMEM((1, 4 * tq), F32), pltpu.VMEM((2 * ACC_ROWS, 4 * tq), F32)],
        compiler_params=_cparams(("parallel", "parallel", "arbitrary")), name="diff_attention",
    )(qt, k, vt, lam_vecs, sub_g.reshape(HEAD_DIM, 1))


def _dilated_bias(tq):
    max_back = max(w for w, _ in DILATED_PATTERNS) // tq
    classes = [0, 1, 2, 3, max_back]
    q = np.arange(tq)[None, :]
    k = np.arange(tq)[:, None]
    out = []
    for d in classes:
        delta = d * tq + q - k
        cnt = np.zeros((tq, tq), np.float64)
        for w, dil in DILATED_PATTERNS:
            cnt += (delta >= 0) & (delta <= w) & (delta % dil == 0)
        tab = np.where(cnt > 0, np.log2(np.maximum(cnt, 1.0)), NEG)
        out.append(np.concatenate([tab, tab], axis=1))
    return jnp.asarray(np.stack(out).astype(np.float32)), max_back


def _dilated_kernel(qt_ref, k_ref, vt_ref, bias_ref, o_ref, m_ref, acc_ref, *, tq, max_back):
    qi = pl.program_id(2)
    qt = qt_ref[0]
    row = _iota((2 * HEAD_DIM, 1), 0)
    zero = jnp.zeros_like(qt)
    qmat = jnp.concatenate([jnp.where(row < HEAD_DIM, qt, zero),
                            jnp.where(row >= HEAD_DIM, qt, zero)], axis=1)
    ones = jnp.ones((BF16_SUBLANES, tq), BF16)
    _flash_init(m_ref, acc_ref)

    def body(j, carry):
        k0 = pl.multiple_of(j * tq, tq)
        d = qi - j
        cls = jnp.where(d < 3, d, jnp.where(d == max_back, 4, 3))
        s = _dot(k_ref[0, pl.ds(k0, tq), :], qmat) + bias_ref[cls]
        vt = vt_ref[0, :, pl.ds(k0, tq)]
        values = jnp.concatenate([vt[0:HEAD_DIM], ones, vt[HEAD_DIM:], ones], axis=0)
        _flash_step(s, values, m_ref, acc_ref)
        return carry

    lax.fori_loop(jnp.maximum(qi - max_back, 0), qi + 1, body, 0)
    acc = acc_ref[...]
    for hh in range(2):
        a = acc[ACC_ROWS * hh:ACC_ROWS * (hh + 1), tq * hh:tq * (hh + 1)]
        o_ref[0, HEAD_DIM * hh:HEAD_DIM * (hh + 1), :] = (
            a[0:HEAD_DIM] * (1.0 / a[HEAD_DIM:HEAD_DIM + 1])).astype(o_ref.dtype)


def _dilated(qt, k, vt, tq=256):
    bsz, _, seq = qt.shape
    tq = min(tq, seq)
    bias, max_back = _dilated_bias(tq)
    pair = pl.BlockSpec((1, 2 * HEAD_DIM, tq), lambda b, p, i: (b, p, i))
    kern = functools.partial(_dilated_kernel, tq=tq, max_back=max_back)
    return pl.pallas_call(
        kern, grid=(bsz, N_HEADS // 2, seq // tq),
        in_specs=[pair,
                  pl.BlockSpec((1, seq, 2 * HEAD_DIM), lambda b, p, i: (b, 0, p)),
                  pl.BlockSpec((1, 2 * HEAD_DIM, seq), lambda b, p, i: (b, p, 0)),
                  pl.BlockSpec(bias.shape, lambda b, p, i: (0, 0, 0))],
        out_specs=pair,
        out_shape=jax.ShapeDtypeStruct(qt.shape, BF16),
        scratch_shapes=[pltpu.VMEM((1, 2 * tq), F32), pltpu.VMEM((2 * ACC_ROWS, 2 * tq), F32)],
        compiler_params=_cparams(("parallel", "parallel", "arbitrary")), name="dilated_attention",
    )(qt, k, vt, bias)


def _mlstm_kernel(u_ref, up_ref, v_ref, sm_ref, o_ref, cw_ref, cb_ref, wq_ref, wk_ref,
                  gb_ref, hg_ref, out_ref, c_st, n_st, m_st):
    ci = pl.program_id(1)
    L = MLSTM_CHUNK

    @pl.when(ci == 0)
    def _():
        c_st[...] = jnp.zeros(c_st.shape, F32)
        n_st[...] = jnp.zeros(n_st.shape, F32)
        m_st[...] = jnp.zeros(m_st.shape, F32)

    tail = jnp.where(ci > 0, up_ref[0], 0.0)
    ext = jnp.concatenate([tail, u_ref[0]], axis=0)
    cw = cw_ref[...]
    uc = cb_ref[...] + cw[MLSTM_CONV - 1:MLSTM_CONV] * ext[8:]
    for j in range(MLSTM_CONV - 1):
        shifted = pltpu.roll(ext, MLSTM_CONV - 1 - j, axis=0)[8:]
        uc = uc + cw[j:j + 1] * shifted
    uc = (uc * jax.nn.sigmoid(uc)).astype(BF16)

    sm = sm_ref[0]
    gb = gb_ref[...]
    tri = (_iota((L, L), 1) <= _iota((L, L), 0))
    eye = _iota((L, L), 1) == _iota((L, L), 0)
    eye_bf = jnp.where(eye, 1.0, 0.0).astype(BF16)
    ig_all = sm[:, SM_CI:SM_CI + N_HEADS] + gb[0:1]
    lf_all = jax.nn.log_sigmoid(sm[:, SM_CF:SM_CF + N_HEADS] + gb[1:2])
    b_all = jnp.dot(tri.astype(F32), lf_all, preferred_element_type=F32,
                    precision=lax.Precision.HIGHEST)
    hg = hg_ref[...]
    og = jax.nn.sigmoid(o_ref[0])

    outs = []
    for h in range(N_HEADS):
        hs = slice(h * HEAD_DIM, (h + 1) * HEAD_DIM)
        uch = uc[:, hs]
        q = _dot(uch, wq_ref[h])
        k = _dot(uch, wk_ref[h]) * (HEAD_DIM ** -0.5)
        qb, kb = q.astype(BF16), k.astype(BF16)
        vv = v_ref[0][:, hs]
        b_col = b_all[:, h:h + 1]
        ig_col = ig_all[:, h:h + 1]
        src_col = ig_col - b_col
        src_row = jnp.sum(jnp.where(eye, src_col, 0.0), axis=0, keepdims=True)
        dmat = jnp.where(tri, b_col + src_row, NEG)
        a = b_col[L - 1:L]
        g_end = a + src_col
        m_loc = jnp.max(g_end, axis=0, keepdims=True)
        w_end = jnp.exp(g_end - m_loc)
        kw = k * w_end
        vt = _dot_nt(eye_bf, vv).astype(BF16)
        c_loc = _dot(vt, kw.astype(BF16))
        n_loc = jnp.sum(kw, axis=0, keepdims=True)

        c_in, n_in, m_in = c_st[h], n_st[h], m_st[h]
        inter = b_col + m_in
        m_t = jnp.maximum(inter, jnp.max(dmat, axis=-1, keepdims=True))
        e_inter = jnp.exp(inter - m_t)
        s_qk = _dot_nt(qb, kb) * jnp.exp(dmat - m_t)
        num = e_inter * _dot_nt(qb, c_in.astype(BF16)) + _dot(s_qk.astype(BF16), vv)
        den = (e_inter * jnp.sum(q * n_in, axis=-1, keepdims=True)
               + jnp.sum(s_qk, axis=-1, keepdims=True))
        hh = num / jnp.maximum(jnp.abs(den), jnp.exp(-m_t))
        hh = hh * lax.rsqrt(jnp.mean(hh * hh, axis=-1, keepdims=True) + EPS) * hg[:, hs]
        outs.append(hh * og[:, hs])

        m_new = jnp.maximum(a + m_in, m_loc)
        decay = jnp.exp(a + m_in - m_new)
        fresh = jnp.exp(m_loc - m_new)
        c_st[h] = decay * c_in + fresh * c_loc
        n_st[h] = decay * n_in + fresh * n_loc
        m_st[h] = m_new

    out_ref[0] = jnp.concatenate(outs, axis=1).astype(out_ref.dtype)


def _mlstm(u, v, smalls, o_pre, conv_w, conv_b, wq, wk, gate_b, head_g):
    bsz, seq, _ = u.shape
    L = MLSTM_CHUNK
    row = pl.BlockSpec((1, L, GROUP_WIDTH), lambda b, c: (b, c, 0))
    const2 = lambda a: pl.BlockSpec(a.shape, lambda b, c: (0,) * a.ndim)
    conv_b = conv_b.reshape(1, GROUP_WIDTH)
    head_g = head_g.reshape(1, GROUP_WIDTH)
    wq, wk = wq.astype(BF16), wk.astype(BF16)
    return pl.pallas_call(
        _mlstm_kernel, grid=(bsz, seq // L),
        in_specs=[row,
                  pl.BlockSpec((1, 8, GROUP_WIDTH), lambda b, c: (b, jnp.maximum(c * (L // 8) - 1, 0), 0)),
                  row,
                  pl.BlockSpec((1, L, LANES), lambda b, c: (b, c, 0)),
                  row, const2(conv_w), const2(conv_b), const2(wq), const2(wk), const2(gate_b),
                  const2(head_g)],
        out_specs=row,
        out_shape=jax.ShapeDtypeStruct((bsz, seq, GROUP_WIDTH), BF16),
        scratch_shapes=[pltpu.VMEM((N_HEADS, HEAD_DIM, HEAD_DIM), F32),
                        pltpu.VMEM((N_HEADS, 1, HEAD_DIM), F32),
                        pltpu.VMEM((N_HEADS, 1, 1), F32)],
        compiler_params=_cparams(("parallel", "arbitrary")), name="mlstm",
    )(u, u, v, smalls, o_pre, conv_w, conv_b, wq, wk, gate_b, head_g)


def _out_ffn_kernel(x_ref, oa_ref, ob_ref, oc_ref, od_ref, wo_ref, g_ref, wg_ref, wu_ref, wd_ref,
                    gf_ref, y_ref, *, final, ff_chunk):
    mixed = jnp.concatenate([oa_ref[...], ob_ref[...], oc_ref[...], od_ref[...]], axis=1)
    x = x_ref[...] + _dot(mixed, wo_ref[...])
    h = x * lax.rsqrt(jnp.mean(x * x, axis=-1, keepdims=True) + EPS)
    h = (h * g_ref[...]).astype(BF16)
    ffn = None
    for c0 in range(0, D_FF, ff_chunk):
        gate = _dot(h, wg_ref[:, c0:c0 + ff_chunk])
        up = _dot(h, wu_ref[:, c0:c0 + ff_chunk])
        act = (gate * jax.nn.sigmoid(gate) * up).astype(BF16)
        part = _dot(act, wd_ref[c0:c0 + ff_chunk, :])
        ffn = part if ffn is None else ffn + part
    y = x + ffn
    if final:
        y = y * lax.rsqrt(jnp.mean(y * y, axis=-1, keepdims=True) + EPS) * gf_ref[...]
    y_ref[...] = y


def _out_ffn(x2, o_a, o_b, o_c, o_d, w_out, gain, w_gate, w_up, w_down, gain_final, final, tm=256):
    n = x2.shape[0]
    row = lambda w: pl.BlockSpec((tm, w), lambda i: (i, 0))
    const = lambda a: pl.BlockSpec(a.shape, lambda i: (0, 0))
    gain = gain.reshape(1, D_MODEL)
    gain_final = gain_final.reshape(1, D_MODEL)
    kern = functools.partial(_out_ffn_kernel, final=final, ff_chunk=256)
    return pl.pallas_call(
        kern, grid=(n // tm,),
        in_specs=[row(D_MODEL)] + [row(GROUP_WIDTH)] * 4
                 + [const(w_out), const(gain), const(w_gate), const(w_up), const(w_down),
                    const(gain_final)],
        out_specs=row(D_MODEL),
        out_shape=jax.ShapeDtypeStruct((n, D_MODEL), F32),
        compiler_params=_cparams(("parallel",)), name="out_ffn",
    )(x2, o_a, o_b, o_c, o_d, w_out, gain, w_gate, w_up, w_down, gain_final)


def _channel_major(t, bsz, seq):
    return t.reshape(bsz, seq, t.shape[-1]).transpose(0, 2, 1)


def _row_major(t, bsz, seq):
    return t.transpose(0, 2, 1).reshape(bsz * seq, t.shape[1])


def kernel(x, norm_mix, w_in, nsa_cmp_pos, nsa_cmp_w, diff_lambda, diff_norm, mlstm_conv_w,
           mlstm_conv_b, mlstm_wq, mlstm_wk, mlstm_gate_b, mlstm_norm, w_out, norm_ffn, w_gate,
           w_up, w_down, norm_final):
    bsz, seq, _ = x.shape
    depth = w_in.shape[0]
    tables = _lane_tables(seq)
    x2 = x.reshape(bsz * seq, D_MODEL)
    r3 = lambda t: t.reshape(bsz, seq, t.shape[-1])
    cm = lambda t: _channel_major(t, bsz, seq)

    for layer in range(depth):
        (a_q, a_qr, a_kvc, a_ks, a_vs, a_kw, a_vw, smalls, b_q, b_k, b_v, c_u, c_v, c_o,
         d_q, d_k, d_v) = _in_proj(x2, norm_mix[layer], _pack_w_in(w_in[layer]), tables, seq,
                                   tm=min(256, seq))
        smalls3 = r3(smalls)

        kc, kvct = _compress(r3(a_kvc), _pack_cmp_w(nsa_cmp_w[layer]), nsa_cmp_pos[layer])
        o_a = _nsa(cm(a_q), cm(a_qr), kc, kvct, r3(a_ks), cm(a_vs), r3(a_kw), cm(a_vw), smalls3)

        lam_init = 0.8 - 0.6 * math.exp(-0.3 * layer)
        o_b = _diff(cm(b_q), r3(b_k), cm(b_v), diff_lambda[layer], diff_norm[layer], lam_init)

        o_c = _mlstm(r3(c_u), r3(c_v), smalls3, r3(c_o),
                     mlstm_conv_w[layer], mlstm_conv_b[layer], mlstm_wq[layer], mlstm_wk[layer],
                     mlstm_gate_b[layer], mlstm_norm[layer])

        o_d = _dilated(cm(d_q), r3(d_k), cm(d_v))

        x2 = _out_ffn(x2, _row_major(o_a, bsz, seq), _row_major(o_b, bsz, seq),
                      o_c.reshape(bsz * seq, GROUP_WIDTH), _row_major(o_d, bsz, seq),
                      w_out[layer].astype(BF16), norm_ffn[layer],
                      w_gate[layer].astype(BF16), w_up[layer].astype(BF16),
                      w_down[layer].astype(BF16), norm_final, final=(layer == depth - 1))
    return x2.reshape(bsz, seq, D_MODEL)
```

```python
import functools
import math

import numpy as np
import jax
import jax.numpy as jnp
from jax import lax
from jax.experimental import pallas as pl
from jax.experimental.pallas import tpu as pltpu

F32 = jnp.float32
BF16 = jnp.bfloat16

D_MODEL = 1024
HEAD_DIM = 64
N_HEADS = 4
GROUP_WIDTH = N_HEADS * HEAD_DIM
ROPE_THETA = 10000.0
EPS = 1e-6
NEG = -1e30
LOG2E = math.log2(math.e)

CMP_LEN = 32
CMP_STRIDE = 16
SLC_LEN = 64
TOP_N = 16
NSA_WINDOW = 512
FORCED_LOCAL = 2
DIFF_HALF = HEAD_DIM // 2
MLSTM_CHUNK = 64
MLSTM_CONV = 4
DILATED_PATTERNS = ((128, 1), (512, 4), (2048, 16))
D_FF = ((8 * D_MODEL + 3 * 256 - 1) // (3 * 256)) * 256

LANES = 128
BF16_SUBLANES = 16
VMEM_LIMIT = 56 * 1024 * 1024
ACC_ROWS = HEAD_DIM + BF16_SUBLANES

IN_SPLITS = (
    GROUP_WIDTH, HEAD_DIM, HEAD_DIM, HEAD_DIM, HEAD_DIM, HEAD_DIM, HEAD_DIM, 3 * N_HEADS,
    GROUP_WIDTH, GROUP_WIDTH, GROUP_WIDTH,
    GROUP_WIDTH, GROUP_WIDTH, N_HEADS, N_HEADS, GROUP_WIDTH,
    GROUP_WIDTH, GROUP_WIDTH, GROUP_WIDTH,
)
(A_Q, A_KC, A_VC, A_KS, A_VS, A_KW, A_VW, A_G, B_Q, B_K, B_V,
 C_U, C_V, C_I, C_F, C_O, D_Q, D_K, D_V) = range(19)
SM_AG, SM_CI, SM_CF = 0, 12, 16


def _cparams(sem):
    return pltpu.CompilerParams(dimension_semantics=sem, vmem_limit_bytes=VMEM_LIMIT)


def _iota(shape, dim):
    return lax.broadcasted_iota(jnp.int32, shape, dim)


def _dot(a, b):
    return jnp.dot(a, b, preferred_element_type=F32)


def _dot_nt(a, b):
    return lax.dot_general(a, b, (((1,), (1,)), ((), ())), preferred_element_type=F32)


def _pack_w_in(w):
    offs = np.concatenate([[0], np.cumsum(IN_SPLITS)])
    col = lambda i: w[:, int(offs[i]):int(offs[i + 1])]
    z64 = jnp.zeros((w.shape[0], HEAD_DIM), w.dtype)
    smalls = jnp.concatenate(
        [col(A_G), col(C_I), col(C_F), jnp.zeros((w.shape[0], LANES - 20), w.dtype)], axis=1)
    parts = [col(A_Q), col(A_KC), col(A_VC), col(A_KS), z64, col(A_VS), z64, col(A_KW), z64,
             col(A_VW), z64, smalls, col(B_Q), col(B_K), col(B_V), col(C_U), col(C_V), col(C_O),
             col(D_Q), col(D_K), col(D_V)]
    return jnp.concatenate(parts, axis=1).astype(BF16)


def _rope_tables(seq, dim):
    inv = 1.0 / (ROPE_THETA ** (jnp.arange(0, dim, 2, dtype=F32) / dim))
    ang = jnp.arange(seq, dtype=F32)[:, None] * inv[None, :]
    return jnp.cos(ang), jnp.sin(ang)


def _lane_tables(seq):
    c64, s64 = _rope_tables(seq, HEAD_DIM)
    c32, s32 = _rope_tables(seq, DIFF_HALF)
    t64c = jnp.concatenate([c64, c64] * (LANES // HEAD_DIM), axis=1)
    t64s = jnp.concatenate([-s64, s64] * (LANES // HEAD_DIM), axis=1)
    t32c = jnp.concatenate([c32, c32] * (LANES // DIFF_HALF), axis=1)
    t32s = jnp.concatenate([-s32, s32] * (LANES // DIFF_HALF), axis=1)
    pos = np.arange(seq)[:, None]
    lane = np.arange(LANES)[None, :]
    onehot = ((lane >= HEAD_DIM) & ((pos // SLC_LEN) % HEAD_DIM == lane - HEAD_DIM)
              ).astype(np.float32)
    return t64c, t64s, t32c, t32s, jnp.asarray(onehot)


def _swap_halves(x, group):
    width = x.shape[-1]
    half = group // 2
    lane = _iota(x.shape, 1) & (group - 1)
    up = pltpu.roll(x, width - half, axis=1)
    down = pltpu.roll(x, half, axis=1)
    return jnp.where(lane < half, up, down)


def _rope(x, cos_t, sin_t, group):
    reps = x.shape[-1] // LANES
    if reps > 1:
        cos_t = jnp.concatenate([cos_t] * reps, axis=1)
        sin_t = jnp.concatenate([sin_t] * reps, axis=1)
    return x * cos_t + _swap_halves(x, group) * sin_t


def _in_proj_kernel(x_ref, g_ref, w_ref, c64_ref, s64_ref, c32_ref, s32_ref, oh_ref,
                    aq_ref, aqr_ref, akvc_ref, aks_ref, avs_ref, akw_ref, avw_ref, sm_ref,
                    bq_ref, bk_ref, bv_ref, cu_ref, cv_ref, co_ref, dq_ref, dk_ref, dv_ref):
    x = x_ref[...]
    h = x * lax.rsqrt(jnp.mean(x * x, axis=-1, keepdims=True) + EPS)
    h = (h * g_ref[...]).astype(BF16)
    c64, s64 = c64_ref[...], s64_ref[...]
    c32, s32 = c32_ref[...], s32_ref[...]

    def mm(c0, c1):
        return _dot(h, w_ref[:, c0 * LANES:c1 * LANES])

    zq = mm(0, 2) * (HEAD_DIM ** -0.5 * LOG2E)
    aq_ref[...] = zq.astype(BF16)
    aqr_ref[...] = _rope(zq, c64, s64, HEAD_DIM).astype(BF16)
    akvc_ref[...] = mm(2, 3).astype(BF16)
    aks_ref[...] = (_rope(mm(3, 4), c64, s64, HEAD_DIM) + oh_ref[...]).astype(BF16)
    avs_ref[...] = mm(4, 5).astype(BF16)
    akw_ref[...] = _rope(mm(5, 6), c64, s64, HEAD_DIM).astype(BF16)
    avw_ref[...] = mm(6, 7).astype(BF16)
    sm_ref[...] = mm(7, 8)
    bq_ref[...] = (_rope(mm(8, 10), c32, s32, DIFF_HALF) * (DIFF_HALF ** -0.5 * LOG2E)).astype(BF16)
    bk_ref[...] = _rope(mm(10, 12), c32, s32, DIFF_HALF).astype(BF16)
    bv_ref[...] = mm(12, 14).astype(BF16)
    cu_ref[...] = mm(14, 16)
    cv_ref[...] = mm(16, 18).astype(BF16)
    co_ref[...] = mm(18, 20)
    dq_ref[...] = (_rope(mm(20, 22), c64, s64, HEAD_DIM) * (HEAD_DIM ** -0.5 * LOG2E)).astype(BF16)
    dk_ref[...] = _rope(mm(22, 24), c64, s64, HEAD_DIM).astype(BF16)
    dv_ref[...] = mm(24, 26).astype(BF16)


_IN_PROJ_OUTS = (
    (256, BF16), (256, BF16), (128, BF16), (128, BF16), (128, BF16), (128, BF16), (128, BF16),
    (128, F32), (256, BF16), (256, BF16), (256, BF16), (256, F32), (256, BF16), (256, F32),
    (256, BF16), (256, BF16), (256, BF16))


def _in_proj(x2, gain, w_packed, tables, seq, tm):
    n = x2.shape[0]
    nblk_s = seq // tm
    row = lambda i: (i, 0)
    tab = lambda i: (i % nblk_s, 0)
    const = lambda i: (0, 0)
    in_specs = [pl.BlockSpec((tm, D_MODEL), row),
                pl.BlockSpec((1, D_MODEL), const),
                pl.BlockSpec(w_packed.shape, const)]
    in_specs += [pl.BlockSpec((tm, LANES), tab)] * 5
    out_specs = [pl.BlockSpec((tm, w), row) for w, _ in _IN_PROJ_OUTS]
    out_shape = [jax.ShapeDtypeStruct((n, w), dt) for w, dt in _IN_PROJ_OUTS]
    return pl.pallas_call(
        _in_proj_kernel, grid=(n // tm,), in_specs=in_specs, out_specs=out_specs,
        out_shape=out_shape, compiler_params=_cparams(("parallel",)), name="in_proj",
    )(x2, gain.reshape(1, D_MODEL), w_packed, *tables)


def _pack_cmp_w(cmp_w):
    w = cmp_w.reshape(2, 2, CMP_STRIDE, HEAD_DIM, HEAD_DIM)
    zeros = jnp.zeros_like(w[0])
    top = jnp.concatenate([w[0], zeros], axis=-1)
    bot = jnp.concatenate([zeros, w[1]], axis=-1)
    full = jnp.concatenate([top, bot], axis=-2)
    return full.reshape(2, CMP_STRIDE * LANES, LANES).astype(BF16)


def _compress_kernel(r_ref, w_ref, pos_ref, kc_ref, kvct_ref):
    r = r_ref[0]
    first = _dot(r, w_ref[0])
    second = _dot(r, w_ref[1])
    nrow = first.shape[0]
    pos = pos_ref[...].astype(BF16)
    half = pos.shape[1] // 2
    const = _dot(pos[:, :half], w_ref[0]) + _dot(pos[:, half:], w_ref[1])
    out = first + pltpu.roll(second, nrow - 1, axis=0) + const[0:1, :]
    lane = _iota(out.shape, 1)
    kc_ref[0] = jnp.where(lane < HEAD_DIM, out, 0.0).astype(BF16)
    kvct_ref[0] = out.T.astype(BF16)


def _compress(a_kvc, w_packed, cmp_pos):
    bsz, seq, _ = a_kvc.shape
    nrow = seq // CMP_STRIDE
    r = a_kvc.reshape(bsz, nrow, CMP_STRIDE * LANES)
    pos = jnp.concatenate([cmp_pos[0], cmp_pos[1]], axis=-1)
    pos = jnp.broadcast_to(pos.reshape(1, CMP_LEN * LANES), (8, CMP_LEN * LANES))
    return pl.pallas_call(
        _compress_kernel, grid=(bsz,),
        in_specs=[pl.BlockSpec((1, nrow, CMP_STRIDE * LANES), lambda b: (b, 0, 0)),
                  pl.BlockSpec(w_packed.shape, lambda b: (0, 0, 0)),
                  pl.BlockSpec(pos.shape, lambda b: (0, 0))],
        out_specs=[pl.BlockSpec((1, nrow, LANES), lambda b: (b, 0, 0)),
                   pl.BlockSpec((1, LANES, nrow), lambda b: (b, 0, 0))],
        out_shape=[jax.ShapeDtypeStruct((bsz, nrow, LANES), BF16),
                   jax.ShapeDtypeStruct((bsz, LANES, nrow), BF16)],
        compiler_params=_cparams(("parallel",)), name="nsa_compress",
    )(r, w_packed, pos)


def _flash_init(m_ref, acc_ref):
    m_ref[...] = jnp.full(m_ref.shape, NEG, F32)
    acc_ref[...] = jnp.zeros(acc_ref.shape, F32)


def _flash_step(s, values, m_ref, acc_ref):
    m_old = m_ref[...]
    m_new = jnp.maximum(m_old, jnp.max(s, axis=0, keepdims=True))
    alpha = jnp.exp2(m_old - m_new)
    p = jnp.exp2(s - m_new).astype(BF16)
    acc_ref[...] = alpha * acc_ref[...] + _dot(values, p)
    m_ref[...] = m_new


def _pipelined_tiles(lo, hi, scores, consume, sa_ref, sb_ref):
    n = hi - lo
    sa_ref[...] = scores(lo)

    def body(i, carry):
        j = lo + 2 * i
        sb_ref[...] = scores(j + 1)
        consume(sa_ref[...], j, False)
        sa_ref[...] = scores(j + 2)
        consume(sb_ref[...], j + 1, False)
        return carry

    lax.fori_loop(0, n // 2, body, 0)

    @pl.when(n % 2 == 0)
    def _():
        consume(sa_ref[...], hi, True)

    @pl.when(n % 2 == 1)
    def _():
        sb_ref[...] = scores(hi)
        consume(sa_ref[...], hi - 1, False)
        consume(sb_ref[...], hi, True)


def _with_ones(vt):
    return jnp.concatenate([vt, jnp.ones((BF16_SUBLANES, vt.shape[1]), BF16)], axis=0)


def _cmp_to_slc_t(seq):
    n_cmp = (seq - CMP_LEN) // CMP_STRIDE + 1
    n_slc = seq // SLC_LEN
    ratio_s, ratio_c = SLC_LEN // CMP_STRIDE, CMP_LEN // CMP_STRIDE
    jj = np.arange(n_slc)[:, None, None]
    src = ratio_s * jj - np.arange(ratio_s)[None, :, None] - np.arange(ratio_c)[None, None, :]
    ok = (src >= 0) & (src < n_cmp)
    m = np.zeros((seq // CMP_STRIDE, n_slc), np.float32)
    np.add.at(m, (np.where(ok, src, 0), np.broadcast_to(jj, src.shape)), ok.astype(np.float32))
    return jnp.asarray(m.T)


def _nsa_kernel(qt_ref, qrt_ref, kc_ref, kvct_ref, ks_ref, vst_ref, kw_ref, vwt_ref, sm_ref,
                c2st_ref, o_ref, m_ref, acc_ref, sa_ref, sb_ref, *, tq, top_n):
    qi = pl.program_id(1)
    s0 = qi * tq
    rows = N_HEADS * tq
    n_slc = c2st_ref.shape[0]
    lane_t = s0 + (_iota((1, rows), 1) & (tq - 1))

    def heads_on_lanes(ref):
        x = ref[0]
        return jnp.concatenate([x[h * HEAD_DIM:(h + 1) * HEAD_DIM, :] for h in range(N_HEADS)],
                               axis=1)

    q4 = jnp.concatenate([heads_on_lanes(qt_ref), jnp.zeros((HEAD_DIM, rows), BF16)], axis=0)
    sc = _dot(kc_ref[0], q4)
    cmask = (_iota((sc.shape[0], 1), 0) * CMP_STRIDE + (CMP_LEN - 1)) <= lane_t
    sc = jnp.where(cmask, sc, NEG)
    e = jnp.where(cmask, jnp.exp2(sc - jnp.max(sc, axis=0, keepdims=True)), 0.0)
    z = jnp.sum(e, axis=0, keepdims=True)
    p_cmp = e * (1.0 / jnp.where(z > 0, z, 1.0))
    o_cmp = _dot(kvct_ref[0][HEAD_DIM:2 * HEAD_DIM, :], p_cmp.astype(BF16))
    imp4 = jnp.dot(c2st_ref[...], p_cmp, preferred_element_type=F32,
                   precision=lax.Precision.HIGHEST)
    imp = imp4[:, 0:tq]
    for h in range(1, N_HEADS):
        imp = imp + imp4[:, h * tq:(h + 1) * tq]

    blk = _iota((n_slc, 1), 0)
    cur = (s0 + _iota((1, tq), 1)) >> 6
    forced = (blk == 0) | ((blk <= cur) & (blk > cur - FORCED_LOCAL))
    score = jnp.where(blk > cur, -1.0e6, jnp.where(forced, 1.0e6, imp))
    rank = jnp.zeros((n_slc, tq), F32)
    for i in range(n_slc):
        s_i = score[i:i + 1, :]
        rank = rank + jnp.where(blk > i, jnp.where(s_i >= score, 1.0, 0.0),
                                jnp.where(s_i > score, 1.0, 0.0))
    sel = (rank < top_n) & (blk <= cur)
    bias = jnp.where(sel, 0.0, NEG)
    if n_slc < HEAD_DIM:
        bias = jnp.concatenate([bias, jnp.zeros((HEAD_DIM - n_slc, tq), F32)], axis=0)
    bias4 = jnp.concatenate([bias] * N_HEADS, axis=1).astype(BF16)
    qsel = jnp.concatenate([heads_on_lanes(qrt_ref), bias4], axis=0)

    def attend(k_ref, vt_ref, lo, window):
        _flash_init(m_ref, acc_ref)

        def scores(j):
            k0 = pl.multiple_of(j * tq, tq)
            return _dot(k_ref[0, pl.ds(k0, tq), :], qsel)

        def consume(s, j, diagonal):
            k0 = pl.multiple_of(j * tq, tq)
            kpos = k0 + _iota((tq, 1), 0)
            if diagonal:
                s = jnp.where(kpos <= lane_t, s, NEG)
            if window:
                s = jnp.where(kpos > lane_t - NSA_WINDOW, s, NEG)
            _flash_step(s, _with_ones(vt_ref[0, :, pl.ds(k0, tq)]), m_ref, acc_ref)

        _pipelined_tiles(lo, qi, scores, consume, sa_ref, sb_ref)
        acc = acc_ref[...]
        return acc[0:HEAD_DIM] * (1.0 / acc[HEAD_DIM:HEAD_DIM + 1])

    o_slc = attend(ks_ref, vst_ref, 0, False)
    o_win = attend(kw_ref, vwt_ref, jnp.maximum(qi - NSA_WINDOW // tq, 0), True)

    g = jax.nn.sigmoid(sm_ref[0].T[0:BF16_SUBLANES, :])

    def gate(branch):
        return jnp.concatenate(
            [g[branch * N_HEADS + h:branch * N_HEADS + h + 1, :] for h in range(N_HEADS)], axis=1)

    o = gate(0) * o_cmp + gate(1) * o_slc + gate(2) * o_win
    for h in range(N_HEADS):
        o_ref[0, h * HEAD_DIM:(h + 1) * HEAD_DIM, :] = o[:, h * tq:(h + 1) * tq].astype(o_ref.dtype)


def _nsa(qt, qrt, kc, kvct, ks, vst, kw, vwt, smalls, tq=256):
    bsz, _, seq = qt.shape
    tq = min(tq, seq)
    n_slc = seq // SLC_LEN
    c2st = _cmp_to_slc_t(seq)
    rows = N_HEADS * tq
    qspec = pl.BlockSpec((1, GROUP_WIDTH, tq), lambda b, i: (b, 0, i))
    full = lambda a: pl.BlockSpec((1,) + a.shape[1:], lambda b, i: (b, 0, 0))
    vspec = pl.BlockSpec((1, HEAD_DIM, seq), lambda b, i: (b, 0, 0))
    kern = functools.partial(_nsa_kernel, tq=tq, top_n=min(TOP_N, n_slc))
    return pl.pallas_call(
        kern, grid=(bsz, seq // tq),
        in_specs=[qspec, qspec, full(kc), full(kvct), full(ks), vspec, full(kw), vspec,
                  pl.BlockSpec((1, tq, LANES), lambda b, i: (b, i, 0)),
                  pl.BlockSpec(c2st.shape, lambda b, i: (0, 0))],
        out_specs=qspec,
        out_shape=jax.ShapeDtypeStruct(qt.shape, BF16),
        scratch_shapes=[pltpu.VMEM((1, rows), F32), pltpu.VMEM((ACC_ROWS, rows), F32),
                        pltpu.VMEM((tq, rows), F32), pltpu.VMEM((tq, rows), F32)],
        compiler_params=_cparams(("parallel", "arbitrary")), name="nsa_attention",
    )(qt, qrt, kc, kvct, ks, vst, kw, vwt, smalls, c2st)


def _diff_kernel(qt_ref, k_ref, vt_ref, lam_ref, g_ref, o_ref, m_ref, acc_ref, sa_ref, sb_ref,
                 *, tq, lam_init):
    qi = pl.program_id(2)
    s0 = qi * tq
    rows = 4 * tq
    qt = qt_ref[0]
    row = _iota((2 * HEAD_DIM, 1), 0)
    zero = jnp.zeros_like(qt)
    qmat = jnp.concatenate(
        [jnp.where((row >= DIFF_HALF * c) & (row < DIFF_HALF * (c + 1)), qt, zero)
         for c in range(4)], axis=1)
    lane_t = s0 + (_iota((1, rows), 1) & (tq - 1))
    ones = jnp.ones((BF16_SUBLANES, tq), BF16)

    def scores(j):
        k0 = pl.multiple_of(j * tq, tq)
        return _dot(k_ref[0, pl.ds(k0, tq), :], qmat)

    def consume(s, j, diagonal):
        k0 = pl.multiple_of(j * tq, tq)
        if diagonal:
            s = jnp.where(k0 + _iota((tq, 1), 0) <= lane_t, s, NEG)
        vt = vt_ref[0, :, pl.ds(k0, tq)]
        values = jnp.concatenate([vt[0:HEAD_DIM], ones, vt[HEAD_DIM:], ones], axis=0)
        _flash_step(s, values, m_ref, acc_ref)

    _flash_init(m_ref, acc_ref)
    _pipelined_tiles(0, qi, scores, consume, sa_ref, sb_ref)

    lv = lam_ref[...]
    lam = (jnp.exp(jnp.sum(lv[0:1] * lv[1:2], axis=-1, keepdims=True))
           - jnp.exp(jnp.sum(lv[2:3] * lv[3:4], axis=-1, keepdims=True)) + lam_init)
    acc = acc_ref[...]
    for hh in range(2):
        a = acc[ACC_ROWS * hh:ACC_ROWS * (hh + 1), 2 * tq * hh:2 * tq * (hh + 1)]
        o_all = a[0:HEAD_DIM] * (1.0 / a[HEAD_DIM:HEAD_DIM + 1])
        o = o_all[:, :tq] - lam * o_all[:, tq:]
        y = o * lax.rsqrt(jnp.mean(o * o, axis=0, keepdims=True) + EPS)
        o_ref[0, HEAD_DIM * hh:HEAD_DIM * (hh + 1), :] = (
            (y * g_ref[...]) * (1.0 - lam_init)).astype(o_ref.dtype)


def _diff(qt, k, vt, lam_vecs, sub_g, lam_init, tq=256):
    bsz, _, seq = qt.shape
    tq = min(tq, seq)
    pair = pl.BlockSpec((1, 2 * HEAD_DIM, tq), lambda b, p, i: (b, p, i))
    kern = functools.partial(_diff_kernel, tq=tq, lam_init=lam_init)
    return pl.pallas_call(
        kern, grid=(bsz, N_HEADS // 2, seq // tq),
        in_specs=[pair,
                  pl.BlockSpec((1, seq, 2 * HEAD_DIM), lambda b, p, i: (b, 0, p)),
                  pl.BlockSpec((1, 2 * HEAD_DIM, seq), lambda b, p, i: (b, p, 0)),
                  pl.BlockSpec(lam_vecs.shape, lambda b, p, i: (0, 0)),
                  pl.BlockSpec((HEAD_DIM, 1), lambda b, p, i: (0, 0))],
        out_specs=pair,
        out_shape=jax.ShapeDtypeStruct(qt.shape, BF16),
        scratch_shapes=[pltpu.VMEM((1, 4 * tq), F32), pltpu.VMEM((2 * ACC_ROWS, 4 * tq), F32),
                        pltpu.VMEM((tq, 4 * tq), F32), pltpu.VMEM((tq, 4 * tq), F32)],
        compiler_params=_cparams(("parallel", "parallel", "arbitrary")), name="diff_attention",
    )(qt, k, vt, lam_vecs, sub_g.reshape(HEAD_DIM, 1))


def _dilated_bias(tq):
    max_back = max(w for w, _ in DILATED_PATTERNS) // tq
    classes = [0, 1, 2, 3, max_back]
    q = np.arange(tq)[None, :]
    k = np.arange(tq)[:, None]
    out = []
    for d in classes:
        delta = d * tq + q - k
        cnt = np.zeros((tq, tq), np.float64)
        for w, dil in DILATED_PATTERNS:
            cnt += (delta >= 0) & (delta <= w) & (delta % dil == 0)
        tab = np.where(cnt > 0, np.log2(np.maximum(cnt, 1.0)), NEG)
        out.append(np.concatenate([tab, tab], axis=1))
    return jnp.asarray(np.stack(out).astype(np.float32)), max_back


def _dilated_kernel(qt_ref, k_ref, vt_ref, bias_ref, o_ref, m_ref, acc_ref, sa_ref, sb_ref,
                    *, tq, max_back):
    qi = pl.program_id(2)
    qt = qt_ref[0]
    row = _iota((2 * HEAD_DIM, 1), 0)
    zero = jnp.zeros_like(qt)
    qmat = jnp.concatenate([jnp.where(row < HEAD_DIM, qt, zero),
                            jnp.where(row >= HEAD_DIM, qt, zero)], axis=1)
    ones = jnp.ones((BF16_SUBLANES, tq), BF16)
    _flash_init(m_ref, acc_ref)

    def scores(j):
        k0 = pl.multiple_of(j * tq, tq)
        return _dot(k_ref[0, pl.ds(k0, tq), :], qmat)

    def consume(s, j, diagonal):
        del diagonal
        k0 = pl.multiple_of(j * tq, tq)
        d = qi - j
        cls = jnp.where(d < 3, d, jnp.where(d == max_back, 4, 3))
        vt = vt_ref[0, :, pl.ds(k0, tq)]
        values = jnp.concatenate([vt[0:HEAD_DIM], ones, vt[HEAD_DIM:], ones], axis=0)
        _flash_step(s + bias_ref[cls], values, m_ref, acc_ref)

    _pipelined_tiles(jnp.maximum(qi - max_back, 0), qi, scores, consume, sa_ref, sb_ref)
    acc = acc_ref[...]
    for hh in range(2):
        a = acc[ACC_ROWS * hh:ACC_ROWS * (hh + 1), tq * hh:tq * (hh + 1)]
        o_ref[0, HEAD_DIM * hh:HEAD_DIM * (hh + 1), :] = (
            a[0:HEAD_DIM] * (1.0 / a[HEAD_DIM:HEAD_DIM + 1])).astype(o_ref.dtype)


def _dilated(qt, k, vt, tq=256):
    bsz, _, seq = qt.shape
    tq = min(tq, seq)
    bias, max_back = _dilated_bias(tq)
    pair = pl.BlockSpec((1, 2 * HEAD_DIM, tq), lambda b, p, i: (b, p, i))
    kern = functools.partial(_dilated_kernel, tq=tq, max_back=max_back)
    return pl.pallas_call(
        kern, grid=(bsz, N_HEADS // 2, seq // tq),
        in_specs=[pair,
                  pl.BlockSpec((1, seq, 2 * HEAD_DIM), lambda b, p, i: (b, 0, p)),
                  pl.BlockSpec((1, 2 * HEAD_DIM, seq), lambda b, p, i: (b, p, 0)),
                  pl.BlockSpec(bias.shape, lambda b, p, i: (0, 0, 0))],
        out_specs=pair,
        out_shape=jax.ShapeDtypeStruct(qt.shape, BF16),
        scratch_shapes=[pltpu.VMEM((1, 2 * tq), F32), pltpu.VMEM((2 * ACC_ROWS, 2 * tq), F32),
                        pltpu.VMEM((tq, 2 * tq), F32), pltpu.VMEM((tq, 2 * tq), F32)],
        compiler_params=_cparams(("parallel", "parallel", "arbitrary")), name="dilated_attention",
    )(qt, k, vt, bias)


def _mlstm_kernel(u_ref, up_ref, v_ref, sm_ref, o_ref, cw_ref, cb_ref, wq_ref, wk_ref,
                  gb_ref, hg_ref, out_ref, c_st, n_st, m_st, *, chunks):
    ci = pl.program_id(1)
    L = MLSTM_CHUNK
    rows = chunks * L

    @pl.when(ci == 0)
    def _():
        c_st[...] = jnp.zeros(c_st.shape, F32)
        n_st[...] = jnp.zeros(n_st.shape, F32)
        m_st[...] = jnp.zeros(m_st.shape, F32)

    tail = jnp.where(ci > 0, up_ref[0], 0.0)
    ext = jnp.concatenate([tail, u_ref[0]], axis=0)
    cw = cw_ref[...]
    uc = cb_ref[...] + cw[MLSTM_CONV - 1:MLSTM_CONV] * ext[8:]
    for j in range(MLSTM_CONV - 1):
        shifted = pltpu.roll(ext, MLSTM_CONV - 1 - j, axis=0)[8:]
        uc = uc + cw[j:j + 1] * shifted
    uc = (uc * jax.nn.sigmoid(uc)).astype(BF16)

    sm = sm_ref[0]
    gb = gb_ref[...]
    tri = (_iota((L, L), 1) <= _iota((L, L), 0))
    eye = _iota((L, L), 1) == _iota((L, L), 0)
    eye_bf = jnp.where(eye, 1.0, 0.0).astype(BF16)
    tri_blk = ((_iota((rows, rows), 1) <= _iota((rows, rows), 0))
               & ((_iota((rows, rows), 1) >> 6) == (_iota((rows, rows), 0) >> 6)))
    ig_all = sm[:, SM_CI:SM_CI + N_HEADS] + gb[0:1]
    lf_all = jax.nn.log_sigmoid(sm[:, SM_CF:SM_CF + N_HEADS] + gb[1:2])
    b_all = jnp.dot(jnp.where(tri_blk, 1.0, 0.0), lf_all, preferred_element_type=F32,
                    precision=lax.Precision.HIGHEST)
    hg = hg_ref[...]
    og = jax.nn.sigmoid(o_ref[0])
    v_all = v_ref[0]

    head_outs = []
    for h in range(N_HEADS):
        hs = slice(h * HEAD_DIM, (h + 1) * HEAD_DIM)
        q_all = _dot(uc[:, hs], wq_ref[h])
        k_all = _dot(uc[:, hs], wk_ref[h]) * (HEAD_DIM ** -0.5)
        c_in, n_in, m_in = c_st[h], n_st[h], m_st[h]
        outs = []
        for g in range(chunks):
            rs = slice(g * L, (g + 1) * L)
            q, k = q_all[rs], k_all[rs]
            qb, kb = q.astype(BF16), k.astype(BF16)
            vv = v_all[rs, hs]
            b_col = b_all[rs, h:h + 1]
            src_col = ig_all[rs, h:h + 1] - b_col
            src_row = jnp.sum(jnp.where(eye, src_col, 0.0), axis=0, keepdims=True)
            dmat = jnp.where(tri, b_col + src_row, NEG)
            a = b_col[L - 1:L]
            g_end = a + src_col
            m_loc = jnp.max(g_end, axis=0, keepdims=True)
            w_end = jnp.exp(g_end - m_loc)
            kw = k * w_end
            vt = _dot_nt(eye_bf, vv).astype(BF16)
            c_loc = _dot(vt, kw.astype(BF16))
            n_loc = jnp.sum(kw, axis=0, keepdims=True)

            inter = b_col + m_in
            m_t = jnp.maximum(inter, jnp.max(dmat, axis=-1, keepdims=True))
            e_inter = jnp.exp(inter - m_t)
            s_qk = _dot_nt(qb, kb) * jnp.exp(dmat - m_t)
            num = e_inter * _dot_nt(qb, c_in.astype(BF16)) + _dot(s_qk.astype(BF16), vv)
            den = (e_inter * jnp.sum(q * n_in, axis=-1, keepdims=True)
                   + jnp.sum(s_qk, axis=-1, keepdims=True))
            hh = num / jnp.maximum(jnp.abs(den), jnp.exp(-m_t))
            hh = hh * lax.rsqrt(jnp.mean(hh * hh, axis=-1, keepdims=True) + EPS) * hg[:, hs]
            outs.append(hh * og[rs, hs])

            m_new = jnp.maximum(a + m_in, m_loc)
            decay = jnp.exp(a + m_in - m_new)
            fresh = jnp.exp(m_loc - m_new)
            c_in = decay * c_in + fresh * c_loc
            n_in = decay * n_in + fresh * n_loc
            m_in = m_new
        c_st[h], n_st[h], m_st[h] = c_in, n_in, m_in
        head_outs.append(jnp.concatenate(outs, axis=0))

    out_ref[0] = jnp.concatenate(head_outs, axis=1).astype(out_ref.dtype)


def _mlstm(u, v, smalls, o_pre, conv_w, conv_b, wq, wk, gate_b, head_g, chunks=4):
    bsz, seq, _ = u.shape
    L = MLSTM_CHUNK * chunks
    row = pl.BlockSpec((1, L, GROUP_WIDTH), lambda b, c: (b, c, 0))
    const2 = lambda a: pl.BlockSpec(a.shape, lambda b, c: (0,) * a.ndim)
    conv_b = conv_b.reshape(1, GROUP_WIDTH)
    head_g = head_g.reshape(1, GROUP_WIDTH)
    wq, wk = wq.astype(BF16), wk.astype(BF16)
    return pl.pallas_call(
        functools.partial(_mlstm_kernel, chunks=chunks), grid=(bsz, seq // L),
        in_specs=[row,
                  pl.BlockSpec((1, 8, GROUP_WIDTH), lambda b, c: (b, jnp.maximum(c * (L // 8) - 1, 0), 0)),
                  row,
                  pl.BlockSpec((1, L, LANES), lambda b, c: (b, c, 0)),
                  row, const2(conv_w), const2(conv_b), const2(wq), const2(wk), const2(gate_b),
                  const2(head_g)],
        out_specs=row,
        out_shape=jax.ShapeDtypeStruct((bsz, seq, GROUP_WIDTH), BF16),
        scratch_shapes=[pltpu.VMEM((N_HEADS, HEAD_DIM, HEAD_DIM), F32),
                        pltpu.VMEM((N_HEADS, 1, HEAD_DIM), F32),
                        pltpu.VMEM((N_HEADS, 1, 1), F32)],
        compiler_params=_cparams(("parallel", "arbitrary")), name="mlstm",
    )(u, u, v, smalls, o_pre, conv_w, conv_b, wq, wk, gate_b, head_g)


def _out_ffn_kernel(x_ref, oa_ref, ob_ref, oc_ref, od_ref, wo_ref, g_ref, wg_ref, wu_ref, wd_ref,
                    gf_ref, y_ref, *, final, ff_chunk):
    mixed = jnp.concatenate([oa_ref[...], ob_ref[...], oc_ref[...], od_ref[...]], axis=1)
    x = x_ref[...] + _dot(mixed, wo_ref[...])
    h = x * lax.rsqrt(jnp.mean(x * x, axis=-1, keepdims=True) + EPS)
    h = (h * g_ref[...]).astype(BF16)
    ffn = None
    for c0 in range(0, D_FF, ff_chunk):
        gate = _dot(h, wg_ref[:, c0:c0 + ff_chunk])
        up = _dot(h, wu_ref[:, c0:c0 + ff_chunk])
        act = (gate * jax.nn.sigmoid(gate) * up).astype(BF16)
        part = _dot(act, wd_ref[c0:c0 + ff_chunk, :])
        ffn = part if ffn is None else ffn + part
    y = x + ffn
    if final:
        y = y * lax.rsqrt(jnp.mean(y * y, axis=-1, keepdims=True) + EPS) * gf_ref[...]
    y_ref[...] = y


def _out_ffn(x2, o_a, o_b, o_c, o_d, w_out, gain, w_gate, w_up, w_down, gain_final, final, tm=256):
    n = x2.shape[0]
    row = lambda w: pl.BlockSpec((tm, w), lambda i: (i, 0))
    const = lambda a: pl.BlockSpec(a.shape, lambda i: (0, 0))
    gain = gain.reshape(1, D_MODEL)
    gain_final = gain_final.reshape(1, D_MODEL)
    kern = functools.partial(_out_ffn_kernel, final=final, ff_chunk=256)
    return pl.pallas_call(
        kern, grid=(n // tm,),
        in_specs=[row(D_MODEL)] + [row(GROUP_WIDTH)] * 4
                 + [const(w_out), const(gain), const(w_gate), const(w_up), const(w_down),
                    const(gain_final)],
        out_specs=row(D_MODEL),
        out_shape=jax.ShapeDtypeStruct((n, D_MODEL), F32),
        compiler_params=_cparams(("parallel",)), name="out_ffn",
    )(x2, o_a, o_b, o_c, o_d, w_out, gain, w_gate, w_up, w_down, gain_final)


def _channel_major(t, bsz, seq):
    return t.reshape(bsz, seq, t.shape[-1]).transpose(0, 2, 1)


def _row_major(t, bsz, seq):
    return t.transpose(0, 2, 1).reshape(bsz * seq, t.shape[1])


def kernel(x, norm_mix, w_in, nsa_cmp_pos, nsa_cmp_w, diff_lambda, diff_norm, mlstm_conv_w,
           mlstm_conv_b, mlstm_wq, mlstm_wk, mlstm_gate_b, mlstm_norm, w_out, norm_ffn, w_gate,
           w_up, w_down, norm_final):
    bsz, seq, _ = x.shape
    depth = w_in.shape[0]
    tables = _lane_tables(seq)
    x2 = x.reshape(bsz * seq, D_MODEL)
    r3 = lambda t: t.reshape(bsz, seq, t.shape[-1])
    cm = lambda t: _channel_major(t, bsz, seq)

    for layer in range(depth):
        (a_q, a_qr, a_kvc, a_ks, a_vs, a_kw, a_vw, smalls, b_q, b_k, b_v, c_u, c_v, c_o,
         d_q, d_k, d_v) = _in_proj(x2, norm_mix[layer], _pack_w_in(w_in[layer]), tables, seq,
                                   tm=min(256, seq))
        smalls3 = r3(smalls)

        kc, kvct = _compress(r3(a_kvc), _pack_cmp_w(nsa_cmp_w[layer]), nsa_cmp_pos[layer])
        o_a = _nsa(cm(a_q), cm(a_qr), kc, kvct, r3(a_ks), cm(a_vs), r3(a_kw), cm(a_vw), smalls3)

        lam_init = 0.8 - 0.6 * math.exp(-0.3 * layer)
        o_b = _diff(cm(b_q), r3(b_k), cm(b_v), diff_lambda[layer], diff_norm[layer], lam_init)

        o_c = _mlstm(r3(c_u), r3(c_v), smalls3, r3(c_o),
                     mlstm_conv_w[layer], mlstm_conv_b[layer], mlstm_wq[layer], mlstm_wk[layer],
                     mlstm_gate_b[layer], mlstm_norm[layer])

        o_d = _dilated(cm(d_q), r3(d_k), cm(d_v))

        x2 = _out_ffn(x2, _row_major(o_a, bsz, seq), _row_major(o_b, bsz, seq),
                      o_c.reshape(bsz * seq, GROUP_WIDTH), _row_major(o_d, bsz, seq),
                      w_out[layer].astype(BF16), norm_ffn[layer],
                      w_gate[layer].astype(BF16), w_up[layer].astype(BF16),
                      w_down[layer].astype(BF16), norm_final, final=(layer == depth - 1))
    return x2.reshape(bsz, seq, D_MODEL)
```

```python
import functools
import math

import numpy as np
import jax
import jax.numpy as jnp
from jax import lax
from jax.experimental import pallas as pl
from jax.experimental.pallas import tpu as pltpu

F32 = jnp.float32
BF16 = jnp.bfloat16

D_MODEL = 1024
HEAD_DIM = 64
N_HEADS = 4
GROUP_WIDTH = N_HEADS * HEAD_DIM
ROPE_THETA = 10000.0
EPS = 1e-6
NEG = -1e30
LOG2E = math.log2(math.e)

CMP_LEN = 32
CMP_STRIDE = 16
SLC_LEN = 64
TOP_N = 16
NSA_WINDOW = 512
FORCED_LOCAL = 2
DIFF_HALF = HEAD_DIM // 2
MLSTM_CHUNK = 64
MLSTM_CONV = 4
DILATED_PATTERNS = ((128, 1), (512, 4), (2048, 16))
D_FF = ((8 * D_MODEL + 3 * 256 - 1) // (3 * 256)) * 256

LANES = 128
BF16_SUBLANES = 16
VMEM_LIMIT = 56 * 1024 * 1024
ACC_ROWS = HEAD_DIM + BF16_SUBLANES

IN_SPLITS = (
    GROUP_WIDTH, HEAD_DIM, HEAD_DIM, HEAD_DIM, HEAD_DIM, HEAD_DIM, HEAD_DIM, 3 * N_HEADS,
    GROUP_WIDTH, GROUP_WIDTH, GROUP_WIDTH,
    GROUP_WIDTH, GROUP_WIDTH, N_HEADS, N_HEADS, GROUP_WIDTH,
    GROUP_WIDTH, GROUP_WIDTH, GROUP_WIDTH,
)
(A_Q, A_KC, A_VC, A_KS, A_VS, A_KW, A_VW, A_G, B_Q, B_K, B_V,
 C_U, C_V, C_I, C_F, C_O, D_Q, D_K, D_V) = range(19)
SM_AG, SM_CI, SM_CF = 0, 12, 16


def _cparams(sem):
    return pltpu.CompilerParams(dimension_semantics=sem, vmem_limit_bytes=VMEM_LIMIT)


def _iota(shape, dim):
    return lax.broadcasted_iota(jnp.int32, shape, dim)


def _dot(a, b):
    return jnp.dot(a, b, preferred_element_type=F32)


def _dot_nt(a, b):
    return lax.dot_general(a, b, (((1,), (1,)), ((), ())), preferred_element_type=F32)


def _pack_w_in(w):
    offs = np.concatenate([[0], np.cumsum(IN_SPLITS)])
    col = lambda i: w[:, int(offs[i]):int(offs[i + 1])]
    z64 = jnp.zeros((w.shape[0], HEAD_DIM), w.dtype)
    smalls = jnp.concatenate(
        [col(A_G), col(C_I), col(C_F), jnp.zeros((w.shape[0], LANES - 20), w.dtype)], axis=1)
    parts = [col(A_Q), col(A_KC), col(A_VC), col(A_KS), z64, col(A_VS), z64, col(A_KW), z64,
             col(A_VW), z64, smalls, col(B_Q), col(B_K), col(B_V), col(C_U), col(C_V), col(C_O),
             col(D_Q), col(D_K), col(D_V)]
    return jnp.concatenate(parts, axis=1).astype(BF16)


def _rope_tables(seq, dim):
    inv = 1.0 / (ROPE_THETA ** (jnp.arange(0, dim, 2, dtype=F32) / dim))
    ang = jnp.arange(seq, dtype=F32)[:, None] * inv[None, :]
    return jnp.cos(ang), jnp.sin(ang)


def _lane_tables(seq):
    c64, s64 = _rope_tables(seq, HEAD_DIM)
    c32, s32 = _rope_tables(seq, DIFF_HALF)
    t64c = jnp.concatenate([c64, c64] * (LANES // HEAD_DIM), axis=1)
    t64s = jnp.concatenate([-s64, s64] * (LANES // HEAD_DIM), axis=1)
    t32c = jnp.concatenate([c32, c32] * (LANES // DIFF_HALF), axis=1)
    t32s = jnp.concatenate([-s32, s32] * (LANES // DIFF_HALF), axis=1)
    pos = np.arange(seq)[:, None]
    lane = np.arange(LANES)[None, :]
    onehot = ((lane >= HEAD_DIM) & ((pos // SLC_LEN) % HEAD_DIM == lane - HEAD_DIM)
              ).astype(np.float32)
    return t64c, t64s, t32c, t32s, jnp.asarray(onehot)


def _swap_halves(x, group):
    width = x.shape[-1]
    half = group // 2
    lane = _iota(x.shape, 1) & (group - 1)
    up = pltpu.roll(x, width - half, axis=1)
    down = pltpu.roll(x, half, axis=1)
    return jnp.where(lane < half, up, down)


def _rope(x, cos_t, sin_t, group):
    reps = x.shape[-1] // LANES
    if reps > 1:
        cos_t = jnp.concatenate([cos_t] * reps, axis=1)
        sin_t = jnp.concatenate([sin_t] * reps, axis=1)
    return x * cos_t + _swap_halves(x, group) * sin_t


def _in_proj_kernel(x_ref, g_ref, w_ref, c64_ref, s64_ref, c32_ref, s32_ref, oh_ref,
                    aq_ref, aqr_ref, akvc_ref, aks_ref, avs_ref, akw_ref, avw_ref, sm_ref,
                    bq_ref, bk_ref, bv_ref, cu_ref, cv_ref, co_ref, dq_ref, dk_ref, dv_ref):
    x = x_ref[...]
    h = x * lax.rsqrt(jnp.mean(x * x, axis=-1, keepdims=True) + EPS)
    h = (h * g_ref[...]).astype(BF16)
    c64, s64 = c64_ref[...], s64_ref[...]
    c32, s32 = c32_ref[...], s32_ref[...]

    def mm(c0, c1):
        return _dot(h, w_ref[:, c0 * LANES:c1 * LANES])

    def channel_major(ref, z, channels=None):
        zt = z.T
        ref[0] = (zt if channels is None else zt[0:channels]).astype(ref.dtype)

    zq = mm(0, 2) * (HEAD_DIM ** -0.5 * LOG2E)
    channel_major(aq_ref, zq)
    channel_major(aqr_ref, _rope(zq, c64, s64, HEAD_DIM))
    akvc_ref[...] = mm(2, 3).astype(BF16)
    aks_ref[...] = (_rope(mm(3, 4), c64, s64, HEAD_DIM) + oh_ref[...]).astype(BF16)
    channel_major(avs_ref, mm(4, 5), HEAD_DIM)
    akw_ref[...] = _rope(mm(5, 6), c64, s64, HEAD_DIM).astype(BF16)
    channel_major(avw_ref, mm(6, 7), HEAD_DIM)
    sm_ref[...] = mm(7, 8)
    channel_major(bq_ref, _rope(mm(8, 10), c32, s32, DIFF_HALF) * (DIFF_HALF ** -0.5 * LOG2E))
    bk_ref[...] = _rope(mm(10, 12), c32, s32, DIFF_HALF).astype(BF16)
    channel_major(bv_ref, mm(12, 14))
    cu_ref[...] = mm(14, 16)
    cv_ref[...] = mm(16, 18).astype(BF16)
    co_ref[...] = mm(18, 20)
    channel_major(dq_ref, _rope(mm(20, 22), c64, s64, HEAD_DIM) * (HEAD_DIM ** -0.5 * LOG2E))
    dk_ref[...] = _rope(mm(22, 24), c64, s64, HEAD_DIM).astype(BF16)
    channel_major(dv_ref, mm(24, 26))


_IN_PROJ_OUTS = (
    (256, BF16, True), (256, BF16, True), (128, BF16, False), (128, BF16, False),
    (HEAD_DIM, BF16, True), (128, BF16, False), (HEAD_DIM, BF16, True), (128, F32, False),
    (256, BF16, True), (256, BF16, False), (256, BF16, True), (256, F32, False),
    (256, BF16, False), (256, F32, False), (256, BF16, True), (256, BF16, False),
    (256, BF16, True))


def _in_proj(x2, gain, w_packed, tables, seq, tm):
    n = x2.shape[0]
    nblk_s = seq // tm
    row = lambda i: (i, 0)
    tab = lambda i: (i % nblk_s, 0)
    const = lambda i: (0, 0)
    in_specs = [pl.BlockSpec((tm, D_MODEL), row),
                pl.BlockSpec((1, D_MODEL), const),
                pl.BlockSpec(w_packed.shape, const)]
    in_specs += [pl.BlockSpec((tm, LANES), tab)] * 5
    out_specs = [pl.BlockSpec((1, w, tm), lambda i: (i // nblk_s, 0, i % nblk_s)) if cmaj
                 else pl.BlockSpec((tm, w), row) for w, _, cmaj in _IN_PROJ_OUTS]
    out_shape = [jax.ShapeDtypeStruct((n // seq, w, seq) if cmaj else (n, w), dt)
                 for w, dt, cmaj in _IN_PROJ_OUTS]
    return pl.pallas_call(
        _in_proj_kernel, grid=(n // tm,), in_specs=in_specs, out_specs=out_specs,
        out_shape=out_shape, compiler_params=_cparams(("parallel",)), name="in_proj",
    )(x2, gain.reshape(1, D_MODEL), w_packed, *tables)


def _pack_cmp_w(cmp_w):
    w = cmp_w.reshape(2, 2, CMP_STRIDE, HEAD_DIM, HEAD_DIM)
    zeros = jnp.zeros_like(w[0])
    top = jnp.concatenate([w[0], zeros], axis=-1)
    bot = jnp.concatenate([zeros, w[1]], axis=-1)
    full = jnp.concatenate([top, bot], axis=-2)
    return full.reshape(2, CMP_STRIDE * LANES, LANES).astype(BF16)


def _compress_kernel(r_ref, w_ref, pos_ref, kc_ref, kvct_ref):
    r = r_ref[0]
    first = _dot(r, w_ref[0])
    second = _dot(r, w_ref[1])
    nrow = first.shape[0]
    pos = pos_ref[...].astype(BF16)
    half = pos.shape[1] // 2
    const = _dot(pos[:, :half], w_ref[0]) + _dot(pos[:, half:], w_ref[1])
    out = first + pltpu.roll(second, nrow - 1, axis=0) + const[0:1, :]
    lane = _iota(out.shape, 1)
    kc_ref[0] = jnp.where(lane < HEAD_DIM, out, 0.0).astype(BF16)
    kvct_ref[0] = out.T.astype(BF16)


def _compress(a_kvc, w_packed, cmp_pos):
    bsz, seq, _ = a_kvc.shape
    nrow = seq // CMP_STRIDE
    r = a_kvc.reshape(bsz, nrow, CMP_STRIDE * LANES)
    pos = jnp.concatenate([cmp_pos[0], cmp_pos[1]], axis=-1)
    pos = jnp.broadcast_to(pos.reshape(1, CMP_LEN * LANES), (8, CMP_LEN * LANES))
    return pl.pallas_call(
        _compress_kernel, grid=(bsz,),
        in_specs=[pl.BlockSpec((1, nrow, CMP_STRIDE * LANES), lambda b: (b, 0, 0)),
                  pl.BlockSpec(w_packed.shape, lambda b: (0, 0, 0)),
                  pl.BlockSpec(pos.shape, lambda b: (0, 0))],
        out_specs=[pl.BlockSpec((1, nrow, LANES), lambda b: (b, 0, 0)),
                   pl.BlockSpec((1, LANES, nrow), lambda b: (b, 0, 0))],
        out_shape=[jax.ShapeDtypeStruct((bsz, nrow, LANES), BF16),
                   jax.ShapeDtypeStruct((bsz, LANES, nrow), BF16)],
        compiler_params=_cparams(("parallel",)), name="nsa_compress",
    )(r, w_packed, pos)


def _flash_init(m_ref, acc_ref):
    m_ref[...] = jnp.full(m_ref.shape, NEG, F32)
    acc_ref[...] = jnp.zeros(acc_ref.shape, F32)


def _flash_step(s, values, m_ref, acc_ref):
    m_old = m_ref[...]
    m_new = jnp.maximum(m_old, jnp.max(s, axis=0, keepdims=True))
    alpha = jnp.exp2(m_old - m_new)
    p = jnp.exp2(s - m_new).astype(BF16)
    acc_ref[...] = alpha * acc_ref[...] + _dot(values, p)
    m_ref[...] = m_new


def _pipelined_tiles(lo, hi, scores, consume, sa_ref, sb_ref):
    n = hi - lo
    sa_ref[...] = scores(lo)

    def body(i, carry):
        j = lo + 2 * i
        sb_ref[...] = scores(j + 1)
        consume(sa_ref[...], j, False)
        sa_ref[...] = scores(j + 2)
        consume(sb_ref[...], j + 1, False)
        return carry

    lax.fori_loop(0, n // 2, body, 0)

    @pl.when(n % 2 == 0)
    def _():
        consume(sa_ref[...], hi, True)

    @pl.when(n % 2 == 1)
    def _():
        sb_ref[...] = scores(hi)
        consume(sa_ref[...], hi - 1, False)
        consume(sb_ref[...], hi, True)


def _with_ones(vt):
    return jnp.concatenate([vt, jnp.ones((BF16_SUBLANES, vt.shape[1]), BF16)], axis=0)


def _cmp_to_slc_t(seq):
    n_cmp = (seq - CMP_LEN) // CMP_STRIDE + 1
    n_slc = seq // SLC_LEN
    ratio_s, ratio_c = SLC_LEN // CMP_STRIDE, CMP_LEN // CMP_STRIDE
    jj = np.arange(n_slc)[:, None, None]
    src = ratio_s * jj - np.arange(ratio_s)[None, :, None] - np.arange(ratio_c)[None, None, :]
    ok = (src >= 0) & (src < n_cmp)
    m = np.zeros((seq // CMP_STRIDE, n_slc), np.float32)
    np.add.at(m, (np.where(ok, src, 0), np.broadcast_to(jj, src.shape)), ok.astype(np.float32))
    return jnp.asarray(m.T)


def _nsa_kernel(qt_ref, qrt_ref, kc_ref, kvct_ref, ks_ref, vst_ref, kw_ref, vwt_ref, sm_ref,
                c2st_ref, o_ref, m_ref, acc_ref, sa_ref, sb_ref, *, tq, top_n):
    qi = pl.program_id(1)
    s0 = qi * tq
    rows = N_HEADS * tq
    n_slc = c2st_ref.shape[0]
    lane_t = s0 + (_iota((1, rows), 1) & (tq - 1))

    def heads_on_lanes(ref):
        x = ref[0]
        return jnp.concatenate([x[h * HEAD_DIM:(h + 1) * HEAD_DIM, :] for h in range(N_HEADS)],
                               axis=1)

    q4 = jnp.concatenate([heads_on_lanes(qt_ref), jnp.zeros((HEAD_DIM, rows), BF16)], axis=0)
    sc = _dot(kc_ref[0], q4)
    cmask = (_iota((sc.shape[0], 1), 0) * CMP_STRIDE + (CMP_LEN - 1)) <= lane_t
    sc = jnp.where(cmask, sc, NEG)
    e = jnp.where(cmask, jnp.exp2(sc - jnp.max(sc, axis=0, keepdims=True)), 0.0)
    z = jnp.sum(e, axis=0, keepdims=True)
    p_cmp = e * (1.0 / jnp.where(z > 0, z, 1.0))
    o_cmp = _dot(kvct_ref[0][HEAD_DIM:2 * HEAD_DIM, :], p_cmp.astype(BF16))
    imp4 = jnp.dot(c2st_ref[...], p_cmp, preferred_element_type=F32,
                   precision=lax.Precision.HIGHEST)
    imp = imp4[:, 0:tq]
    for h in range(1, N_HEADS):
        imp = imp + imp4[:, h * tq:(h + 1) * tq]

    blk = _iota((n_slc, 1), 0)
    cur = (s0 + _iota((1, tq), 1)) >> 6
    forced = (blk == 0) | ((blk <= cur) & (blk > cur - FORCED_LOCAL))
    score = jnp.where(blk > cur, -1.0e6, jnp.where(forced, 1.0e6, imp))
    rank = jnp.zeros((n_slc, tq), F32)
    for i in range(n_slc):
        s_i = score[i:i + 1, :]
        rank = rank + jnp.where(blk > i, jnp.where(s_i >= score, 1.0, 0.0),
                                jnp.where(s_i > score, 1.0, 0.0))
    sel = (rank < top_n) & (blk <= cur)
    bias = jnp.where(sel, 0.0, NEG)
    if n_slc < HEAD_DIM:
        bias = jnp.concatenate([bias, jnp.zeros((HEAD_DIM - n_slc, tq), F32)], axis=0)
    bias4 = jnp.concatenate([bias] * N_HEADS, axis=1).astype(BF16)
    qsel = jnp.concatenate([heads_on_lanes(qrt_ref), bias4], axis=0)

    def attend(k_ref, vt_ref, lo, window):
        _flash_init(m_ref, acc_ref)

        def scores(j):
            k0 = pl.multiple_of(j * tq, tq)
            return _dot(k_ref[0, pl.ds(k0, tq), :], qsel)

        def consume(s, j, diagonal):
            k0 = pl.multiple_of(j * tq, tq)
            kpos = k0 + _iota((tq, 1), 0)
            if diagonal:
                s = jnp.where(kpos <= lane_t, s, NEG)
            if window:
                s = jnp.where(kpos > lane_t - NSA_WINDOW, s, NEG)
            _flash_step(s, _with_ones(vt_ref[0, :, pl.ds(k0, tq)]), m_ref, acc_ref)

        _pipelined_tiles(lo, qi, scores, consume, sa_ref, sb_ref)
        acc = acc_ref[...]
        return acc[0:HEAD_DIM] * (1.0 / acc[HEAD_DIM:HEAD_DIM + 1])

    o_slc = attend(ks_ref, vst_ref, 0, False)
    o_win = attend(kw_ref, vwt_ref, jnp.maximum(qi - NSA_WINDOW // tq, 0), True)

    g = jax.nn.sigmoid(sm_ref[0].T[0:BF16_SUBLANES, :])

    def gate(branch):
        return jnp.concatenate(
            [g[branch * N_HEADS + h:branch * N_HEADS + h + 1, :] for h in range(N_HEADS)], axis=1)

    o = gate(0) * o_cmp + gate(1) * o_slc + gate(2) * o_win
    for h in range(N_HEADS):
        o_ref[0, h * HEAD_DIM:(h + 1) * HEAD_DIM, :] = o[:, h * tq:(h + 1) * tq].astype(o_ref.dtype)


def _nsa(qt, qrt, kc, kvct, ks, vst, kw, vwt, smalls, tq=256):
    bsz, _, seq = qt.shape
    tq = min(tq, seq)
    n_slc = seq // SLC_LEN
    c2st = _cmp_to_slc_t(seq)
    rows = N_HEADS * tq
    qspec = pl.BlockSpec((1, GROUP_WIDTH, tq), lambda b, i: (b, 0, i))
    full = lambda a: pl.BlockSpec((1,) + a.shape[1:], lambda b, i: (b, 0, 0))
    vspec = pl.BlockSpec((1, HEAD_DIM, seq), lambda b, i: (b, 0, 0))
    kern = functools.partial(_nsa_kernel, tq=tq, top_n=min(TOP_N, n_slc))
    return pl.pallas_call(
        kern, grid=(bsz, seq // tq),
        in_specs=[qspec, qspec, full(kc), full(kvct), full(ks), vspec, full(kw), vspec,
                  pl.BlockSpec((1, tq, LANES), lambda b, i: (b, i, 0)),
                  pl.BlockSpec(c2st.shape, lambda b, i: (0, 0))],
        out_specs=qspec,
        out_shape=jax.ShapeDtypeStruct(qt.shape, BF16),
        scratch_shapes=[pltpu.VMEM((1, rows), F32), pltpu.VMEM((ACC_ROWS, rows), F32),
                        pltpu.VMEM((tq, rows), F32), pltpu.VMEM((tq, rows), F32)],
        compiler_params=_cparams(("parallel", "arbitrary")), name="nsa_attention",
    )(qt, qrt, kc, kvct, ks, vst, kw, vwt, smalls, c2st)


def _diff_kernel(qt_ref, k_ref, vt_ref, lam_ref, g_ref, o_ref, m_ref, acc_ref, sa_ref, sb_ref,
                 *, tq, lam_init):
    qi = pl.program_id(2)
    s0 = qi * tq
    rows = 4 * tq
    qt = qt_ref[0]
    row = _iota((2 * HEAD_DIM, 1), 0)
    zero = jnp.zeros_like(qt)
    qmat = jnp.concatenate(
        [jnp.where((row >= DIFF_HALF * c) & (row < DIFF_HALF * (c + 1)), qt, zero)
         for c in range(4)], axis=1)
    lane_t = s0 + (_iota((1, rows), 1) & (tq - 1))
    ones = jnp.ones((BF16_SUBLANES, tq), BF16)

    def scores(j):
        k0 = pl.multiple_of(j * tq, tq)
        return _dot(k_ref[0, pl.ds(k0, tq), :], qmat)

    def consume(s, j, diagonal):
        k0 = pl.multiple_of(j * tq, tq)
        if diagonal:
            s = jnp.where(k0 + _iota((tq, 1), 0) <= lane_t, s, NEG)
        vt = vt_ref[0, :, pl.ds(k0, tq)]
        values = jnp.concatenate([vt[0:HEAD_DIM], ones, vt[HEAD_DIM:], ones], axis=0)
        _flash_step(s, values, m_ref, acc_ref)

    _flash_init(m_ref, acc_ref)
    _pipelined_tiles(0, qi, scores, consume, sa_ref, sb_ref)

    lv = lam_ref[...]
    lam = (jnp.exp(jnp.sum(lv[0:1] * lv[1:2], axis=-1, keepdims=True))
           - jnp.exp(jnp.sum(lv[2:3] * lv[3:4], axis=-1, keepdims=True)) + lam_init)
    acc = acc_ref[...]
    for hh in range(2):
        a = acc[ACC_ROWS * hh:ACC_ROWS * (hh + 1), 2 * tq * hh:2 * tq * (hh + 1)]
        o_all = a[0:HEAD_DIM] * (1.0 / a[HEAD_DIM:HEAD_DIM + 1])
        o = o_all[:, :tq] - lam * o_all[:, tq:]
        y = o * lax.rsqrt(jnp.mean(o * o, axis=0, keepdims=True) + EPS)
        o_ref[0, HEAD_DIM * hh:HEAD_DIM * (hh + 1), :] = (
            (y * g_ref[...]) * (1.0 - lam_init)).astype(o_ref.dtype)


def _diff(qt, k, vt, lam_vecs, sub_g, lam_init, tq=256):
    bsz, _, seq = qt.shape
    tq = min(tq, seq)
    pair = pl.BlockSpec((1, 2 * HEAD_DIM, tq), lambda b, p, i: (b, p, i))
    kern = functools.partial(_diff_kernel, tq=tq, lam_init=lam_init)
    return pl.pallas_call(
        kern, grid=(bsz, N_HEADS // 2, seq // tq),
        in_specs=[pair,
                  pl.BlockSpec((1, seq, 2 * HEAD_DIM), lambda b, p, i: (b, 0, p)),
                  pl.BlockSpec((1, 2 * HEAD_DIM, seq), lambda b, p, i: (b, p, 0)),
                  pl.BlockSpec(lam_vecs.shape, lambda b, p, i: (0, 0)),
                  pl.BlockSpec((HEAD_DIM, 1), lambda b, p, i: (0, 0))],
        out_specs=pair,
        out_shape=jax.ShapeDtypeStruct(qt.shape, BF16),
        scratch_shapes=[pltpu.VMEM((1, 4 * tq), F32), pltpu.VMEM((2 * ACC_ROWS, 4 * tq), F32),
                        pltpu.VMEM((tq, 4 * tq), F32), pltpu.VMEM((tq, 4 * tq), F32)],
        compiler_params=_cparams(("parallel", "parallel", "arbitrary")), name="diff_attention",
    )(qt, k, vt, lam_vecs, sub_g.reshape(HEAD_DIM, 1))


def _dilated_bias(tq):
    max_back = max(w for w, _ in DILATED_PATTERNS) // tq
    classes = [0, 1, 2, 3, max_back]
    q = np.arange(tq)[None, :]
    k = np.arange(tq)[:, None]
    out = []
    for d in classes:
        delta = d * tq + q - k
        cnt = np.zeros((tq, tq), np.float64)
        for w, dil in DILATED_PATTERNS:
            cnt += (delta >= 0) & (delta <= w) & (delta % dil == 0)
        tab = np.where(cnt > 0, np.log2(np.maximum(cnt, 1.0)), NEG)
        out.append(np.concatenate([tab, tab], axis=1))
    return jnp.asarray(np.stack(out).astype(np.float32)), max_back


def _dilated_kernel(qt_ref, k_ref, vt_ref, bias_ref, o_ref, m_ref, acc_ref, sa_ref, sb_ref,
                    *, tq, max_back):
    qi = pl.program_id(2)
    qt = qt_ref[0]
    row = _iota((2 * HEAD_DIM, 1), 0)
    zero = jnp.zeros_like(qt)
    qmat = jnp.concatenate([jnp.where(row < HEAD_DIM, qt, zero),
                            jnp.where(row >= HEAD_DIM, qt, zero)], axis=1)
    ones = jnp.ones((BF16_SUBLANES, tq), BF16)
    _flash_init(m_ref, acc_ref)

    def scores(j):
        k0 = pl.multiple_of(j * tq, tq)
        return _dot(k_ref[0, pl.ds(k0, tq), :], qmat)

    def consume(s, j, diagonal):
        del diagonal
        k0 = pl.multiple_of(j * tq, tq)
        d = qi - j
        cls = jnp.where(d < 3, d, jnp.where(d == max_back, 4, 3))
        vt = vt_ref[0, :, pl.ds(k0, tq)]
        values = jnp.concatenate([vt[0:HEAD_DIM], ones, vt[HEAD_DIM:], ones], axis=0)
        _flash_step(s + bias_ref[cls], values, m_ref, acc_ref)

    _pipelined_tiles(jnp.maximum(qi - max_back, 0), qi, scores, consume, sa_ref, sb_ref)
    acc = acc_ref[...]
    for hh in range(2):
        a = acc[ACC_ROWS * hh:ACC_ROWS * (hh + 1), tq * hh:tq * (hh + 1)]
        o_ref[0, HEAD_DIM * hh:HEAD_DIM * (hh + 1), :] = (
            a[0:HEAD_DIM] * (1.0 / a[HEAD_DIM:HEAD_DIM + 1])).astype(o_ref.dtype)


def _dilated(qt, k, vt, tq=256):
    bsz, _, seq = qt.shape
    tq = min(tq, seq)
    bias, max_back = _dilated_bias(tq)
    pair = pl.BlockSpec((1, 2 * HEAD_DIM, tq), lambda b, p, i: (b, p, i))
    kern = functools.partial(_dilated_kernel, tq=tq, max_back=max_back)
    return pl.pallas_call(
        kern, grid=(bsz, N_HEADS // 2, seq // tq),
        in_specs=[pair,
                  pl.BlockSpec((1, seq, 2 * HEAD_DIM), lambda b, p, i: (b, 0, p)),
                  pl.BlockSpec((1, 2 * HEAD_DIM, seq), lambda b, p, i: (b, p, 0)),
                  pl.BlockSpec(bias.shape, lambda b, p, i: (0, 0, 0))],
        out_specs=pair,
        out_shape=jax.ShapeDtypeStruct(qt.shape, BF16),
        scratch_shapes=[pltpu.VMEM((1, 2 * tq), F32), pltpu.VMEM((2 * ACC_ROWS, 2 * tq), F32),
                        pltpu.VMEM((tq, 2 * tq), F32), pltpu.VMEM((tq, 2 * tq), F32)],
        compiler_params=_cparams(("parallel", "parallel", "arbitrary")), name="dilated_attention",
    )(qt, k, vt, bias)


def _mlstm_kernel(u_ref, up_ref, v_ref, sm_ref, o_ref, cw_ref, cb_ref, wqt_ref, wk_ref,
                  gb_ref, gcol_ref, hg_ref, out_ref, c_st, m_st):
    ci = pl.program_id(1)
    rows = u_ref.shape[1]

    @pl.when(ci == 0)
    def _():
        c_st[...] = jnp.zeros(c_st.shape, F32)
        m_st[...] = jnp.zeros(m_st.shape, F32)

    tail = jnp.where(ci > 0, up_ref[0], 0.0)
    ext = jnp.concatenate([tail, u_ref[0]], axis=0)
    cw = cw_ref[...]
    uc = cb_ref[...] + cw[MLSTM_CONV - 1:MLSTM_CONV] * ext[8:]
    for j in range(MLSTM_CONV - 1):
        shifted = pltpu.roll(ext, MLSTM_CONV - 1 - j, axis=0)[8:]
        uc = uc + cw[j:j + 1] * shifted
    uc = uc * jax.nn.sigmoid(uc)

    qt_all = _dot(wqt_ref[...], uc.T.astype(BF16))
    k_all = _dot(uc.astype(BF16), wk_ref[...]) * (HEAD_DIM ** -0.5)
    vt_all = v_ref[0].astype(F32).T.astype(BF16)
    ogt = jax.nn.sigmoid(o_ref[0]).T
    sm = sm_ref[0]
    smt = sm.T
    gb = gb_ref[...]
    gcol = gcol_ref[...]
    upper = _iota((rows, rows), 0) <= _iota((rows, rows), 1)
    lower = _iota((rows, rows), 1) <= _iota((rows, rows), 0)
    ig_rows = smt[8:16] + gcol[:, 0:1]
    lf_rows = jax.nn.log_sigmoid(smt[16:24] + gcol[:, 1:2])
    b_rows = jnp.dot(lf_rows, jnp.where(upper, 1.0, 0.0), preferred_element_type=F32,
                     precision=lax.Precision.HIGHEST)
    ig_cols = sm[:, SM_CI:SM_CI + N_HEADS] + gb[0:1]
    lf_cols = jax.nn.log_sigmoid(sm[:, SM_CF:SM_CF + N_HEADS] + gb[1:2])
    b_cols = jnp.dot(jnp.where(lower, 1.0, 0.0), lf_cols, preferred_element_type=F32,
                     precision=lax.Precision.HIGHEST)
    ones = jnp.ones((BF16_SUBLANES, rows), BF16)

    for h in range(N_HEADS):
        hs = slice(h * HEAD_DIM, (h + 1) * HEAD_DIM)
        b_row = b_rows[h:h + 1]
        src_row = ig_rows[N_HEADS + h:N_HEADS + h + 1] - b_row
        src_col = ig_cols[:, h:h + 1] - b_cols[:, h:h + 1]
        dmat = jnp.where(upper, b_row + src_col, NEG)
        a = b_row[:, rows - 1:rows]
        g_end = a + src_row
        m_loc = jnp.max(g_end, axis=-1, keepdims=True)
        w_end = jnp.exp(g_end - m_loc)

        state = c_st[h]
        m_in = m_st[h]
        inter = b_row + m_in
        m_t = jnp.maximum(inter, jnp.max(dmat, axis=0, keepdims=True))
        e_inter = jnp.exp(inter - m_t)
        qt = qt_all[hs].astype(BF16)
        kb = k_all[:, hs].astype(BF16)
        values = jnp.concatenate([vt_all[hs], ones], axis=0)
        p = (_dot(kb, qt) * jnp.exp(dmat - m_t)).astype(BF16)
        from_state = _dot(state.astype(BF16), qt)
        from_chunk = _dot(values, p)
        num = e_inter * from_state[0:HEAD_DIM] + from_chunk[0:HEAD_DIM]
        den = e_inter * from_state[HEAD_DIM:HEAD_DIM + 1] + from_chunk[HEAD_DIM:HEAD_DIM + 1]
        hh = num * (1.0 / jnp.maximum(jnp.abs(den), jnp.exp(-m_t)))
        hh = hh * lax.rsqrt(jnp.mean(hh * hh, axis=0, keepdims=True) + EPS) * hg_ref[hs, :]
        out_ref[0, hs, :] = (hh * ogt[hs]).astype(out_ref.dtype)

        m_new = jnp.maximum(a + m_in, m_loc)
        decay = jnp.exp(a + m_in - m_new)
        fresh = jnp.exp(m_loc - m_new)
        local = _dot((values.astype(F32) * w_end).astype(BF16), kb)
        c_st[h] = decay * state + fresh * local
        m_st[h] = m_new


def _mlstm(u, v, smalls, o_pre, conv_w, conv_b, wq, wk, gate_b, head_g, rows=256):
    bsz, seq, _ = u.shape
    rows = min(rows, seq)
    row = pl.BlockSpec((1, rows, GROUP_WIDTH), lambda b, c: (b, c, 0))
    const2 = lambda a: pl.BlockSpec(a.shape, lambda b, c: (0,) * a.ndim)
    conv_b = conv_b.reshape(1, GROUP_WIDTH)
    head_g = head_g.reshape(GROUP_WIDTH, 1)
    eye = jnp.eye(N_HEADS, dtype=wq.dtype)
    wqt = jnp.einsum('hde,hg->hegd', wq, eye).reshape(GROUP_WIDTH, GROUP_WIDTH).astype(BF16)
    wkb = jnp.einsum('hde,hg->hdge', wk, eye).reshape(GROUP_WIDTH, GROUP_WIDTH).astype(BF16)
    zeros4 = jnp.zeros((N_HEADS,), gate_b.dtype)
    gcol = jnp.stack([jnp.concatenate([zeros4, gate_b[0]]), jnp.concatenate([gate_b[1], zeros4])],
                     axis=1)
    return pl.pallas_call(
        _mlstm_kernel, grid=(bsz, seq // rows),
        in_specs=[row,
                  pl.BlockSpec((1, 8, GROUP_WIDTH),
                               lambda b, c: (b, jnp.maximum(c * (rows // 8) - 1, 0), 0)),
                  row,
                  pl.BlockSpec((1, rows, LANES), lambda b, c: (b, c, 0)),
                  row, const2(conv_w), const2(conv_b), const2(wqt), const2(wkb), const2(gate_b),
                  const2(gcol), const2(head_g)],
        out_specs=pl.BlockSpec((1, GROUP_WIDTH, rows), lambda b, c: (b, 0, c)),
        out_shape=jax.ShapeDtypeStruct((bsz, GROUP_WIDTH, seq), BF16),
        scratch_shapes=[pltpu.VMEM((N_HEADS, ACC_ROWS, HEAD_DIM), F32),
                        pltpu.VMEM((N_HEADS, 1, 1), F32)],
        compiler_params=_cparams(("parallel", "arbitrary")), name="mlstm",
    )(u, u, v, smalls, o_pre, conv_w, conv_b, wqt, wkb, gate_b, gcol, head_g)


def _out_ffn_kernel(x_ref, oa_ref, ob_ref, oc_ref, od_ref, wo_ref, g_ref, wg_ref, wu_ref, wd_ref,
                    gf_ref, y_ref, *, final, ff_chunk):
    mixed_t = jnp.concatenate([oa_ref[0], ob_ref[0], oc_ref[0], od_ref[0]], axis=0)
    x = x_ref[...] + lax.dot_general(mixed_t, wo_ref[...], (((0,), (0,)), ((), ())),
                                     preferred_element_type=F32)
    h = x * lax.rsqrt(jnp.mean(x * x, axis=-1, keepdims=True) + EPS)
    h = (h * g_ref[...]).astype(BF16)
    ffn = None
    for c0 in range(0, D_FF, ff_chunk):
        gate = _dot(h, wg_ref[:, c0:c0 + ff_chunk])
        up = _dot(h, wu_ref[:, c0:c0 + ff_chunk])
        act = (gate * jax.nn.sigmoid(gate) * up).astype(BF16)
        part = _dot(act, wd_ref[c0:c0 + ff_chunk, :])
        ffn = part if ffn is None else ffn + part
    y = x + ffn
    if final:
        y = y * lax.rsqrt(jnp.mean(y * y, axis=-1, keepdims=True) + EPS) * gf_ref[...]
    y_ref[...] = y


def _out_ffn(x2, o_a, o_b, o_c, o_d, w_out, gain, w_gate, w_up, w_down, gain_final, final, tm=256):
    n = x2.shape[0]
    seq = o_a.shape[2]
    tm = min(tm, seq)
    nblk_s = seq // tm
    row = lambda w: pl.BlockSpec((tm, w), lambda i: (i, 0))
    mixer = pl.BlockSpec((1, GROUP_WIDTH, tm), lambda i: (i // nblk_s, 0, i % nblk_s))
    const = lambda a: pl.BlockSpec(a.shape, lambda i: (0, 0))
    gain = gain.reshape(1, D_MODEL)
    gain_final = gain_final.reshape(1, D_MODEL)
    kern = functools.partial(_out_ffn_kernel, final=final, ff_chunk=256)
    return pl.pallas_call(
        kern, grid=(n // tm,),
        in_specs=[row(D_MODEL)] + [mixer] * 4
                 + [const(w_out), const(gain), const(w_gate), const(w_up), const(w_down),
                    const(gain_final)],
        out_specs=row(D_MODEL),
        out_shape=jax.ShapeDtypeStruct((n, D_MODEL), F32),
        compiler_params=_cparams(("parallel",)), name="out_ffn",
    )(x2, o_a, o_b, o_c, o_d, w_out, gain, w_gate, w_up, w_down, gain_final)


def kernel(x, norm_mix, w_in, nsa_cmp_pos, nsa_cmp_w, diff_lambda, diff_norm, mlstm_conv_w,
           mlstm_conv_b, mlstm_wq, mlstm_wk, mlstm_gate_b, mlstm_norm, w_out, norm_ffn, w_gate,
           w_up, w_down, norm_final):
    bsz, seq, _ = x.shape
    depth = w_in.shape[0]
    tables = _lane_tables(seq)
    x2 = x.reshape(bsz * seq, D_MODEL)
    r3 = lambda t: t.reshape(bsz, seq, t.shape[-1])

    for layer in range(depth):
        (a_q, a_qr, a_kvc, a_ks, a_vs, a_kw, a_vw, smalls, b_q, b_k, b_v, c_u, c_v, c_o,
         d_q, d_k, d_v) = _in_proj(x2, norm_mix[layer], _pack_w_in(w_in[layer]), tables, seq,
                                   tm=min(256, seq))
        smalls3 = r3(smalls)

        kc, kvct = _compress(r3(a_kvc), _pack_cmp_w(nsa_cmp_w[layer]), nsa_cmp_pos[layer])
        o_a = _nsa(a_q, a_qr, kc, kvct, r3(a_ks), a_vs, r3(a_kw), a_vw, smalls3)

        lam_init = 0.8 - 0.6 * math.exp(-0.3 * layer)
        o_b = _diff(b_q, r3(b_k), b_v, diff_lambda[layer], diff_norm[layer], lam_init)

        o_c = _mlstm(r3(c_u), r3(c_v), smalls3, r3(c_o),
                     mlstm_conv_w[layer], mlstm_conv_b[layer], mlstm_wq[layer], mlstm_wk[layer],
                     mlstm_gate_b[layer], mlstm_norm[layer])

        o_d = _dilated(d_q, r3(d_k), d_v)

        x2 = _out_ffn(x2, o_a, o_b, o_c, o_d, w_out[layer].astype(BF16), norm_ffn[layer],
                      w_gate[layer].astype(BF16), w_up[layer].astype(BF16),
                      w_down[layer].astype(BF16), norm_final, final=(layer == depth - 1))
    return x2.reshape(bsz, seq, D_MODEL)
```

```python
import functools
import math

import numpy as np
import jax
import jax.numpy as jnp
from jax import lax
from jax.experimental import pallas as pl
from jax.experimental.pallas import tpu as pltpu

F32 = jnp.float32
BF16 = jnp.bfloat16

D_MODEL = 1024
HEAD_DIM = 64
N_HEADS = 4
GROUP_WIDTH = N_HEADS * HEAD_DIM
ROPE_THETA = 10000.0
EPS = 1e-6
NEG = -1e30
LOG2E = math.log2(math.e)

CMP_LEN = 32
CMP_STRIDE = 16
SLC_LEN = 64
TOP_N = 16
NSA_WINDOW = 512
FORCED_LOCAL = 2
DIFF_HALF = HEAD_DIM // 2
MLSTM_CHUNK = 64
MLSTM_CONV = 4
DILATED_PATTERNS = ((128, 1), (512, 4), (2048, 16))
D_FF = ((8 * D_MODEL + 3 * 256 - 1) // (3 * 256)) * 256

LANES = 128
BF16_SUBLANES = 16
VMEM_LIMIT = 56 * 1024 * 1024
ACC_ROWS = HEAD_DIM + BF16_SUBLANES

IN_SPLITS = (
    GROUP_WIDTH, HEAD_DIM, HEAD_DIM, HEAD_DIM, HEAD_DIM, HEAD_DIM, HEAD_DIM, 3 * N_HEADS,
    GROUP_WIDTH, GROUP_WIDTH, GROUP_WIDTH,
    GROUP_WIDTH, GROUP_WIDTH, N_HEADS, N_HEADS, GROUP_WIDTH,
    GROUP_WIDTH, GROUP_WIDTH, GROUP_WIDTH,
)
(A_Q, A_KC, A_VC, A_KS, A_VS, A_KW, A_VW, A_G, B_Q, B_K, B_V,
 C_U, C_V, C_I, C_F, C_O, D_Q, D_K, D_V) = range(19)
SM_AG, SM_CI, SM_CF = 0, 12, 16


def _cparams(sem):
    return pltpu.CompilerParams(dimension_semantics=sem, vmem_limit_bytes=VMEM_LIMIT)


def _iota(shape, dim):
    return lax.broadcasted_iota(jnp.int32, shape, dim)


def _dot(a, b):
    return jnp.dot(a, b, preferred_element_type=F32)


def _dot_nt(a, b):
    return lax.dot_general(a, b, (((1,), (1,)), ((), ())), preferred_element_type=F32)


PACKED_CHUNKS = 26
D_IN_PADDED = 3072


def _column_plan():
    offs = np.concatenate([[0], np.cumsum(IN_SPLITS)])
    order = [A_Q, A_KC, A_VC, A_KS, None, A_VS, None, A_KW, None, A_VW, None,
             A_G, C_I, C_F, ('pad', LANES - 20),
             B_Q, B_K, B_V, C_U, C_V, C_O, D_Q, D_K, D_V]
    src = []
    for item in order:
        if item is None:
            src += [-1] * HEAD_DIM
        elif isinstance(item, tuple):
            src += [-1] * item[1]
        else:
            src += list(range(int(offs[item]), int(offs[item + 1])))
    src = np.asarray(src)
    assert src.size == PACKED_CHUNKS * LANES
    terms, mats = [], []
    for j in range(PACKED_CHUNKS):
        cols = src[j * LANES:(j + 1) * LANES]
        todo = cols >= 0
        while todo.any():
            start = (cols[todo].min() // LANES) * LANES
            take = todo & (cols < start + 2 * LANES)
            sel = np.zeros((2 * LANES, LANES), np.float32)
            sel[cols[take] - start, np.nonzero(take)[0]] = 1.0
            terms.append((j, int(start)))
            mats.append(sel)
            todo &= ~take
    return tuple(terms), np.stack(mats)


def _rope_tables(seq, dim):
    inv = 1.0 / (ROPE_THETA ** (jnp.arange(0, dim, 2, dtype=F32) / dim))
    ang = jnp.arange(seq, dtype=F32)[:, None] * inv[None, :]
    return jnp.cos(ang), jnp.sin(ang)


def _lane_tables(seq):
    c64, s64 = _rope_tables(seq, HEAD_DIM)
    c32, s32 = _rope_tables(seq, DIFF_HALF)
    t64c = jnp.concatenate([c64, c64] * (LANES // HEAD_DIM), axis=1)
    t64s = jnp.concatenate([-s64, s64] * (LANES // HEAD_DIM), axis=1)
    t32c = jnp.concatenate([c32, c32] * (LANES // DIFF_HALF), axis=1)
    t32s = jnp.concatenate([-s32, s32] * (LANES // DIFF_HALF), axis=1)
    pos = np.arange(seq)[:, None]
    lane = np.arange(LANES)[None, :]
    onehot = ((lane >= HEAD_DIM) & ((pos // SLC_LEN) % HEAD_DIM == lane - HEAD_DIM)
              ).astype(np.float32)
    return t64c, t64s, t32c, t32s, jnp.asarray(onehot)


def _swap_halves(x, group):
    width = x.shape[-1]
    half = group // 2
    lane = _iota(x.shape, 1) & (group - 1)
    up = pltpu.roll(x, width - half, axis=1)
    down = pltpu.roll(x, half, axis=1)
    return jnp.where(lane < half, up, down)


def _rope(x, cos_t, sin_t, group):
    reps = x.shape[-1] // LANES
    if reps > 1:
        cos_t = jnp.concatenate([cos_t] * reps, axis=1)
        sin_t = jnp.concatenate([sin_t] * reps, axis=1)
    return x * cos_t + _swap_halves(x, group) * sin_t


def _in_proj_kernel(x_ref, g_ref, wraw_ref, sel_ref, c64_ref, s64_ref, c32_ref, s32_ref, oh_ref,
                    aq_ref, aqr_ref, akvc_ref, aks_ref, avs_ref, akw_ref, avw_ref, sm_ref,
                    bq_ref, bk_ref, bv_ref, cu_ref, cv_ref, co_ref, dq_ref, dk_ref, dv_ref,
                    w_ref, *, terms):
    @pl.when(pl.program_id(0) == 0)
    def _():
        for j in range(PACKED_CHUNKS):
            chunk = jnp.zeros((D_MODEL, LANES), F32)
            for t, (dst, start) in enumerate(terms):
                if dst == j:
                    chunk = chunk + _dot(wraw_ref[:, start:start + 2 * LANES], sel_ref[t])
            w_ref[:, j * LANES:(j + 1) * LANES] = chunk.astype(BF16)

    x = x_ref[...]
    h = x * lax.rsqrt(jnp.mean(x * x, axis=-1, keepdims=True) + EPS)
    h = (h * g_ref[...]).astype(BF16)
    c64, s64 = c64_ref[...], s64_ref[...]
    c32, s32 = c32_ref[...], s32_ref[...]

    def mm(c0, c1):
        return _dot(h, w_ref[:, c0 * LANES:c1 * LANES])

    def channel_major(ref, z, channels=None):
        zt = z.T
        ref[0] = (zt if channels is None else zt[0:channels]).astype(ref.dtype)

    zq = mm(0, 2) * (HEAD_DIM ** -0.5 * LOG2E)
    channel_major(aq_ref, zq)
    channel_major(aqr_ref, _rope(zq, c64, s64, HEAD_DIM))
    akvc_ref[...] = mm(2, 3).astype(BF16)
    aks_ref[...] = (_rope(mm(3, 4), c64, s64, HEAD_DIM) + oh_ref[...]).astype(BF16)
    channel_major(avs_ref, mm(4, 5), HEAD_DIM)
    akw_ref[...] = _rope(mm(5, 6), c64, s64, HEAD_DIM).astype(BF16)
    channel_major(avw_ref, mm(6, 7), HEAD_DIM)
    sm_ref[...] = mm(7, 8)
    channel_major(bq_ref, _rope(mm(8, 10), c32, s32, DIFF_HALF) * (DIFF_HALF ** -0.5 * LOG2E))
    bk_ref[...] = _rope(mm(10, 12), c32, s32, DIFF_HALF).astype(BF16)
    channel_major(bv_ref, mm(12, 14))
    cu_ref[...] = mm(14, 16)
    cv_ref[...] = mm(16, 18).astype(BF16)
    co_ref[...] = mm(18, 20)
    channel_major(dq_ref, _rope(mm(20, 22), c64, s64, HEAD_DIM) * (HEAD_DIM ** -0.5 * LOG2E))
    dk_ref[...] = _rope(mm(22, 24), c64, s64, HEAD_DIM).astype(BF16)
    channel_major(dv_ref, mm(24, 26))


_IN_PROJ_OUTS = (
    (256, BF16, True), (256, BF16, True), (128, BF16, False), (128, BF16, False),
    (HEAD_DIM, BF16, True), (128, BF16, False), (HEAD_DIM, BF16, True), (128, F32, False),
    (256, BF16, True), (256, BF16, False), (256, BF16, True), (256, F32, False),
    (256, BF16, False), (256, F32, False), (256, BF16, True), (256, BF16, False),
    (256, BF16, True))


def _in_proj(x2, gain, w, tables, seq, tm):
    n = x2.shape[0]
    nblk_s = seq // tm
    terms, select = _column_plan()
    select = jnp.asarray(select, BF16)
    w_raw = jnp.pad(w.astype(BF16), ((0, 0), (0, D_IN_PADDED - w.shape[1])))
    row = lambda i: (i, 0)
    tab = lambda i: (i % nblk_s, 0)
    const = lambda i: (0, 0)
    in_specs = [pl.BlockSpec((tm, D_MODEL), row),
                pl.BlockSpec((1, D_MODEL), const),
                pl.BlockSpec(w_raw.shape, const),
                pl.BlockSpec(select.shape, lambda i: (0, 0, 0))]
    in_specs += [pl.BlockSpec((tm, LANES), tab)] * 5
    out_specs = [pl.BlockSpec((1, w, tm), lambda i: (i // nblk_s, 0, i % nblk_s)) if cmaj
                 else pl.BlockSpec((tm, w), row) for w, _, cmaj in _IN_PROJ_OUTS]
    out_shape = [jax.ShapeDtypeStruct((n // seq, w, seq) if cmaj else (n, w), dt)
                 for w, dt, cmaj in _IN_PROJ_OUTS]
    return pl.pallas_call(
        functools.partial(_in_proj_kernel, terms=terms), grid=(n // tm,), in_specs=in_specs,
        out_specs=out_specs, out_shape=out_shape,
        scratch_shapes=[pltpu.VMEM((D_MODEL, PACKED_CHUNKS * LANES), BF16)],
        compiler_params=_cparams(("arbitrary",)), name="in_proj",
    )(x2, gain.reshape(1, D_MODEL), w_raw, select, *tables)


def _compress_kernel(r_ref, w_ref, pos_ref, kc_ref, kvct_ref):
    wk = w_ref[0].reshape(CMP_LEN, HEAD_DIM, HEAD_DIM)
    wv = w_ref[1].reshape(CMP_LEN, HEAD_DIM, HEAD_DIM)
    zeros = jnp.zeros_like(wk)
    full = jnp.concatenate([jnp.concatenate([wk, zeros], axis=2),
                            jnp.concatenate([zeros, wv], axis=2)], axis=1)
    w_first = full[0:CMP_STRIDE].reshape(CMP_STRIDE * LANES, LANES).astype(BF16)
    w_second = full[CMP_STRIDE:].reshape(CMP_STRIDE * LANES, LANES).astype(BF16)

    r = r_ref[0]
    first = _dot(r, w_first)
    second = _dot(r, w_second)
    nrow = first.shape[0]
    pos = pos_ref[...].astype(BF16)
    half = pos.shape[1] // 2
    const = _dot(pos[:, :half], w_first) + _dot(pos[:, half:], w_second)
    out = first + pltpu.roll(second, nrow - 1, axis=0) + const[0:1, :]
    lane = _iota(out.shape, 1)
    kc_ref[0] = jnp.where(lane < HEAD_DIM, out, 0.0).astype(BF16)
    kvct_ref[0] = out.T.astype(BF16)


def _compress(a_kvc, cmp_w, cmp_pos):
    bsz, seq, _ = a_kvc.shape
    nrow = seq // CMP_STRIDE
    r = a_kvc.reshape(bsz, nrow, CMP_STRIDE * LANES)
    pos = jnp.concatenate([cmp_pos[0], cmp_pos[1]], axis=-1)
    pos = jnp.broadcast_to(pos.reshape(1, CMP_LEN * LANES), (8, CMP_LEN * LANES))
    return pl.pallas_call(
        _compress_kernel, grid=(bsz,),
        in_specs=[pl.BlockSpec((1, nrow, CMP_STRIDE * LANES), lambda b: (b, 0, 0)),
                  pl.BlockSpec(cmp_w.shape, lambda b: (0, 0, 0)),
                  pl.BlockSpec(pos.shape, lambda b: (0, 0))],
        out_specs=[pl.BlockSpec((1, nrow, LANES), lambda b: (b, 0, 0)),
                   pl.BlockSpec((1, LANES, nrow), lambda b: (b, 0, 0))],
        out_shape=[jax.ShapeDtypeStruct((bsz, nrow, LANES), BF16),
                   jax.ShapeDtypeStruct((bsz, LANES, nrow), BF16)],
        compiler_params=_cparams(("parallel",)), name="nsa_compress",
    )(r, cmp_w, pos)


def _flash_init(m_ref, acc_ref):
    m_ref[...] = jnp.full(m_ref.shape, NEG, F32)
    acc_ref[...] = jnp.zeros(acc_ref.shape, F32)


def _flash_step(s, values, m_ref, acc_ref):
    m_old = m_ref[...]
    m_new = jnp.maximum(m_old, jnp.max(s, axis=0, keepdims=True))
    alpha = jnp.exp2(m_old - m_new)
    p = jnp.exp2(s - m_new).astype(BF16)
    acc_ref[...] = alpha * acc_ref[...] + _dot(values, p)
    m_ref[...] = m_new


def _pipelined_tiles(lo, hi, scores, consume, sa_ref, sb_ref):
    n = hi - lo
    sa_ref[...] = scores(lo)

    def body(i, carry):
        j = lo + 2 * i
        sb_ref[...] = scores(j + 1)
        consume(sa_ref[...], j, False)
        sa_ref[...] = scores(j + 2)
        consume(sb_ref[...], j + 1, False)
        return carry

    lax.fori_loop(0, n // 2, body, 0)

    @pl.when(n % 2 == 0)
    def _():
        consume(sa_ref[...], hi, True)

    @pl.when(n % 2 == 1)
    def _():
        sb_ref[...] = scores(hi)
        consume(sa_ref[...], hi - 1, False)
        consume(sb_ref[...], hi, True)


def _with_ones(vt):
    return jnp.concatenate([vt, jnp.ones((BF16_SUBLANES, vt.shape[1]), BF16)], axis=0)


def _cmp_to_slc_t(seq):
    n_cmp = (seq - CMP_LEN) // CMP_STRIDE + 1
    n_slc = seq // SLC_LEN
    ratio_s, ratio_c = SLC_LEN // CMP_STRIDE, CMP_LEN // CMP_STRIDE
    jj = np.arange(n_slc)[:, None, None]
    src = ratio_s * jj - np.arange(ratio_s)[None, :, None] - np.arange(ratio_c)[None, None, :]
    ok = (src >= 0) & (src < n_cmp)
    m = np.zeros((seq // CMP_STRIDE, n_slc), np.float32)
    np.add.at(m, (np.where(ok, src, 0), np.broadcast_to(jj, src.shape)), ok.astype(np.float32))
    return jnp.asarray(m.T)


def _nsa_kernel(qt_ref, qrt_ref, kc_ref, kvct_ref, ks_ref, vst_ref, kw_ref, vwt_ref, sm_ref,
                c2st_ref, o_ref, m_ref, acc_ref, sa_ref, sb_ref, *, tq, top_n):
    qi = pl.program_id(1)
    s0 = qi * tq
    rows = N_HEADS * tq
    n_slc = c2st_ref.shape[0]
    lane_t = s0 + (_iota((1, rows), 1) & (tq - 1))

    def heads_on_lanes(ref):
        x = ref[0]
        return jnp.concatenate([x[h * HEAD_DIM:(h + 1) * HEAD_DIM, :] for h in range(N_HEADS)],
                               axis=1)

    q4 = jnp.concatenate([heads_on_lanes(qt_ref), jnp.zeros((HEAD_DIM, rows), BF16)], axis=0)
    sc = _dot(kc_ref[0], q4)
    cmask = (_iota((sc.shape[0], 1), 0) * CMP_STRIDE + (CMP_LEN - 1)) <= lane_t
    sc = jnp.where(cmask, sc, NEG)
    e = jnp.where(cmask, jnp.exp2(sc - jnp.max(sc, axis=0, keepdims=True)), 0.0)
    z = jnp.sum(e, axis=0, keepdims=True)
    p_cmp = e * (1.0 / jnp.where(z > 0, z, 1.0))
    o_cmp = _dot(kvct_ref[0][HEAD_DIM:2 * HEAD_DIM, :], p_cmp.astype(BF16))
    imp4 = jnp.dot(c2st_ref[...], p_cmp, preferred_element_type=F32,
                   precision=lax.Precision.HIGHEST)
    imp = imp4[:, 0:tq]
    for h in range(1, N_HEADS):
        imp = imp + imp4[:, h * tq:(h + 1) * tq]

    blk = _iota((n_slc, 1), 0)
    cur = (s0 + _iota((1, tq), 1)) >> 6
    forced = (blk == 0) | ((blk <= cur) & (blk > cur - FORCED_LOCAL))
    score = jnp.where(blk > cur, -1.0e6, jnp.where(forced, 1.0e6, imp))
    rank = jnp.zeros((n_slc, tq), F32)
    for i in range(n_slc):
        s_i = score[i:i + 1, :]
        rank = rank + jnp.where(blk > i, jnp.where(s_i >= score, 1.0, 0.0),
                                jnp.where(s_i > score, 1.0, 0.0))
    sel = (rank < top_n) & (blk <= cur)
    bias = jnp.where(sel, 0.0, NEG)
    if n_slc < HEAD_DIM:
        bias = jnp.concatenate([bias, jnp.zeros((HEAD_DIM - n_slc, tq), F32)], axis=0)
    bias4 = jnp.concatenate([bias] * N_HEADS, axis=1).astype(BF16)
    qsel = jnp.concatenate([heads_on_lanes(qrt_ref), bias4], axis=0)

    def attend(k_ref, vt_ref, lo, window):
        _flash_init(m_ref, acc_ref)

        def scores(j):
            k0 = pl.multiple_of(j * tq, tq)
            return _dot(k_ref[0, pl.ds(k0, tq), :], qsel)

        def consume(s, j, diagonal):
            k0 = pl.multiple_of(j * tq, tq)
            kpos = k0 + _iota((tq, 1), 0)
            if diagonal:
                s = jnp.where(kpos <= lane_t, s, NEG)
            if window:
                s = jnp.where(kpos > lane_t - NSA_WINDOW, s, NEG)
            _flash_step(s, _with_ones(vt_ref[0, :, pl.ds(k0, tq)]), m_ref, acc_ref)

        _pipelined_tiles(lo, qi, scores, consume, sa_ref, sb_ref)
        acc = acc_ref[...]
        return acc[0:HEAD_DIM] * (1.0 / acc[HEAD_DIM:HEAD_DIM + 1])

    o_slc = attend(ks_ref, vst_ref, 0, False)
    o_win = attend(kw_ref, vwt_ref, jnp.maximum(qi - NSA_WINDOW // tq, 0), True)

    g = jax.nn.sigmoid(sm_ref[0].T[0:BF16_SUBLANES, :])

    def gate(branch):
        return jnp.concatenate(
            [g[branch * N_HEADS + h:branch * N_HEADS + h + 1, :] for h in range(N_HEADS)], axis=1)

    o = gate(0) * o_cmp + gate(1) * o_slc + gate(2) * o_win
    for h in range(N_HEADS):
        o_ref[0, h * HEAD_DIM:(h + 1) * HEAD_DIM, :] = o[:, h * tq:(h + 1) * tq].astype(o_ref.dtype)


def _nsa(qt, qrt, kc, kvct, ks, vst, kw, vwt, smalls, tq=256):
    bsz, _, seq = qt.shape
    tq = min(tq, seq)
    n_slc = seq // SLC_LEN
    c2st = _cmp_to_slc_t(seq)
    rows = N_HEADS * tq
    qspec = pl.BlockSpec((1, GROUP_WIDTH, tq), lambda b, i: (b, 0, i))
    full = lambda a: pl.BlockSpec((1,) + a.shape[1:], lambda b, i: (b, 0, 0))
    vspec = pl.BlockSpec((1, HEAD_DIM, seq), lambda b, i: (b, 0, 0))
    kern = functools.partial(_nsa_kernel, tq=tq, top_n=min(TOP_N, n_slc))
    return pl.pallas_call(
        kern, grid=(bsz, seq // tq),
        in_specs=[qspec, qspec, full(kc), full(kvct), full(ks), vspec, full(kw), vspec,
                  pl.BlockSpec((1, tq, LANES), lambda b, i: (b, i, 0)),
                  pl.BlockSpec(c2st.shape, lambda b, i: (0, 0))],
        out_specs=qspec,
        out_shape=jax.ShapeDtypeStruct(qt.shape, BF16),
        scratch_shapes=[pltpu.VMEM((1, rows), F32), pltpu.VMEM((ACC_ROWS, rows), F32),
                        pltpu.VMEM((tq, rows), F32), pltpu.VMEM((tq, rows), F32)],
        compiler_params=_cparams(("parallel", "arbitrary")), name="nsa_attention",
    )(qt, qrt, kc, kvct, ks, vst, kw, vwt, smalls, c2st)


def _diff_kernel(qt_ref, k_ref, vt_ref, lam_ref, g_ref, o_ref, m_ref, acc_ref, sa_ref, sb_ref,
                 *, tq, lam_init):
    qi = pl.program_id(2)
    s0 = qi * tq
    rows = 4 * tq
    qt = qt_ref[0]
    row = _iota((2 * HEAD_DIM, 1), 0)
    zero = jnp.zeros_like(qt)
    qmat = jnp.concatenate(
        [jnp.where((row >= DIFF_HALF * c) & (row < DIFF_HALF * (c + 1)), qt, zero)
         for c in range(4)], axis=1)
    lane_t = s0 + (_iota((1, rows), 1) & (tq - 1))
    ones = jnp.ones((BF16_SUBLANES, tq), BF16)

    def scores(j):
        k0 = pl.multiple_of(j * tq, tq)
        return _dot(k_ref[0, pl.ds(k0, tq), :], qmat)

    def consume(s, j, diagonal):
        k0 = pl.multiple_of(j * tq, tq)
        if diagonal:
            s = jnp.where(k0 + _iota((tq, 1), 0) <= lane_t, s, NEG)
        vt = vt_ref[0, :, pl.ds(k0, tq)]
        values = jnp.concatenate([vt[0:HEAD_DIM], ones, vt[HEAD_DIM:], ones], axis=0)
        _flash_step(s, values, m_ref, acc_ref)

    _flash_init(m_ref, acc_ref)
    _pipelined_tiles(0, qi, scores, consume, sa_ref, sb_ref)

    lv = lam_ref[...]
    lam = (jnp.exp(jnp.sum(lv[0:1] * lv[1:2], axis=-1, keepdims=True))
           - jnp.exp(jnp.sum(lv[2:3] * lv[3:4], axis=-1, keepdims=True)) + lam_init)
    acc = acc_ref[...]
    for hh in range(2):
        a = acc[ACC_ROWS * hh:ACC_ROWS * (hh + 1), 2 * tq * hh:2 * tq * (hh + 1)]
        o_all = a[0:HEAD_DIM] * (1.0 / a[HEAD_DIM:HEAD_DIM + 1])
        o = o_all[:, :tq] - lam * o_all[:, tq:]
        y = o * lax.rsqrt(jnp.mean(o * o, axis=0, keepdims=True) + EPS)
        o_ref[0, HEAD_DIM * hh:HEAD_DIM * (hh + 1), :] = (
            (y * g_ref[...]) * (1.0 - lam_init)).astype(o_ref.dtype)


def _diff(qt, k, vt, lam_vecs, sub_g, lam_init, tq=256):
    bsz, _, seq = qt.shape
    tq = min(tq, seq)
    pair = pl.BlockSpec((1, 2 * HEAD_DIM, tq), lambda b, p, i: (b, p, i))
    kern = functools.partial(_diff_kernel, tq=tq, lam_init=lam_init)
    return pl.pallas_call(
        kern, grid=(bsz, N_HEADS // 2, seq // tq),
        in_specs=[pair,
                  pl.BlockSpec((1, seq, 2 * HEAD_DIM), lambda b, p, i: (b, 0, p)),
                  pl.BlockSpec((1, 2 * HEAD_DIM, seq), lambda b, p, i: (b, p, 0)),
                  pl.BlockSpec(lam_vecs.shape, lambda b, p, i: (0, 0)),
                  pl.BlockSpec((HEAD_DIM, 1), lambda b, p, i: (0, 0))],
        out_specs=pair,
        out_shape=jax.ShapeDtypeStruct(qt.shape, BF16),
        scratch_shapes=[pltpu.VMEM((1, 4 * tq), F32), pltpu.VMEM((2 * ACC_ROWS, 4 * tq), F32),
                        pltpu.VMEM((tq, 4 * tq), F32), pltpu.VMEM((tq, 4 * tq), F32)],
        compiler_params=_cparams(("parallel", "parallel", "arbitrary")), name="diff_attention",
    )(qt, k, vt, lam_vecs, sub_g.reshape(HEAD_DIM, 1))


def _dilated_bias(tq):
    max_back = max(w for w, _ in DILATED_PATTERNS) // tq
    classes = [0, 1, 2, 3, max_back]
    q = np.arange(tq)[None, :]
    k = np.arange(tq)[:, None]
    out = []
    for d in classes:
        delta = d * tq + q - k
        cnt = np.zeros((tq, tq), np.float64)
        for w, dil in DILATED_PATTERNS:
            cnt += (delta >= 0) & (delta <= w) & (delta % dil == 0)
        tab = np.where(cnt > 0, np.log2(np.maximum(cnt, 1.0)), NEG)
        out.append(np.concatenate([tab, tab], axis=1))
    return jnp.asarray(np.stack(out).astype(np.float32)), max_back


def _dilated_kernel(qt_ref, k_ref, vt_ref, bias_ref, o_ref, m_ref, acc_ref, sa_ref, sb_ref,
                    *, tq, max_back):
    qi = pl.program_id(2)
    qt = qt_ref[0]
    row = _iota((2 * HEAD_DIM, 1), 0)
    zero = jnp.zeros_like(qt)
    qmat = jnp.concatenate([jnp.where(row < HEAD_DIM, qt, zero),
                            jnp.where(row >= HEAD_DIM, qt, zero)], axis=1)
    ones = jnp.ones((BF16_SUBLANES, tq), BF16)
    _flash_init(m_ref, acc_ref)

    def scores(j):
        k0 = pl.multiple_of(j * tq, tq)
        return _dot(k_ref[0, pl.ds(k0, tq), :], qmat)

    def consume(s, j, diagonal):
        del diagonal
        k0 = pl.multiple_of(j * tq, tq)
        d = qi - j
        cls = jnp.where(d < 3, d, jnp.where(d == max_back, 4, 3))
        vt = vt_ref[0, :, pl.ds(k0, tq)]
        values = jnp.concatenate([vt[0:HEAD_DIM], ones, vt[HEAD_DIM:], ones], axis=0)
        _flash_step(s + bias_ref[cls], values, m_ref, acc_ref)

    _pipelined_tiles(jnp.maximum(qi - max_back, 0), qi, scores, consume, sa_ref, sb_ref)
    acc = acc_ref[...]
    for hh in range(2):
        a = acc[ACC_ROWS * hh:ACC_ROWS * (hh + 1), tq * hh:tq * (hh + 1)]
        o_ref[0, HEAD_DIM * hh:HEAD_DIM * (hh + 1), :] = (
            a[0:HEAD_DIM] * (1.0 / a[HEAD_DIM:HEAD_DIM + 1])).astype(o_ref.dtype)


def _dilated(qt, k, vt, tq=256):
    bsz, _, seq = qt.shape
    tq = min(tq, seq)
    bias, max_back = _dilated_bias(tq)
    pair = pl.BlockSpec((1, 2 * HEAD_DIM, tq), lambda b, p, i: (b, p, i))
    kern = functools.partial(_dilated_kernel, tq=tq, max_back=max_back)
    return pl.pallas_call(
        kern, grid=(bsz, N_HEADS // 2, seq // tq),
        in_specs=[pair,
                  pl.BlockSpec((1, seq, 2 * HEAD_DIM), lambda b, p, i: (b, 0, p)),
                  pl.BlockSpec((1, 2 * HEAD_DIM, seq), lambda b, p, i: (b, p, 0)),
                  pl.BlockSpec(bias.shape, lambda b, p, i: (0, 0, 0))],
        out_specs=pair,
        out_shape=jax.ShapeDtypeStruct(qt.shape, BF16),
        scratch_shapes=[pltpu.VMEM((1, 2 * tq), F32), pltpu.VMEM((2 * ACC_ROWS, 2 * tq), F32),
                        pltpu.VMEM((tq, 2 * tq), F32), pltpu.VMEM((tq, 2 * tq), F32)],
        compiler_params=_cparams(("parallel", "parallel", "arbitrary")), name="dilated_attention",
    )(qt, k, vt, bias)


def _mlstm_kernel(u_ref, up_ref, v_ref, sm_ref, o_ref, cw_ref, cb_ref, wqt_ref, wk_ref,
                  gb_ref, gcol_ref, hg_ref, out_ref, c_st, m_st):
    ci = pl.program_id(1)
    rows = u_ref.shape[1]

    @pl.when(ci == 0)
    def _():
        c_st[...] = jnp.zeros(c_st.shape, F32)
        m_st[...] = jnp.zeros(m_st.shape, F32)

    tail = jnp.where(ci > 0, up_ref[0], 0.0)
    ext = jnp.concatenate([tail, u_ref[0]], axis=0)
    cw = cw_ref[...]
    uc = cb_ref[...] + cw[MLSTM_CONV - 1:MLSTM_CONV] * ext[8:]
    for j in range(MLSTM_CONV - 1):
        shifted = pltpu.roll(ext, MLSTM_CONV - 1 - j, axis=0)[8:]
        uc = uc + cw[j:j + 1] * shifted
    uc = uc * jax.nn.sigmoid(uc)

    qt_all = _dot(wqt_ref[...], uc.T.astype(BF16))
    k_all = _dot(uc.astype(BF16), wk_ref[...]) * (HEAD_DIM ** -0.5)
    vt_all = v_ref[0].astype(F32).T.astype(BF16)
    ogt = jax.nn.sigmoid(o_ref[0]).T
    sm = sm_ref[0]
    smt = sm.T
    gb = gb_ref[...]
    gcol = gcol_ref[...]
    upper = _iota((rows, rows), 0) <= _iota((rows, rows), 1)
    lower = _iota((rows, rows), 1) <= _iota((rows, rows), 0)
    ig_rows = smt[8:16] + gcol[:, 0:1]
    lf_rows = jax.nn.log_sigmoid(smt[16:24] + gcol[:, 1:2])
    b_rows = jnp.dot(lf_rows, jnp.where(upper, 1.0, 0.0), preferred_element_type=F32,
                     precision=lax.Precision.HIGHEST)
    ig_cols = sm[:, SM_CI:SM_CI + N_HEADS] + gb[0:1]
    lf_cols = jax.nn.log_sigmoid(sm[:, SM_CF:SM_CF + N_HEADS] + gb[1:2])
    b_cols = jnp.dot(jnp.where(lower, 1.0, 0.0), lf_cols, preferred_element_type=F32,
                     precision=lax.Precision.HIGHEST)
    ones = jnp.ones((BF16_SUBLANES, rows), BF16)

    for h in range(N_HEADS):
        hs = slice(h * HEAD_DIM, (h + 1) * HEAD_DIM)
        b_row = b_rows[h:h + 1]
        src_row = ig_rows[N_HEADS + h:N_HEADS + h + 1] - b_row
        src_col = ig_cols[:, h:h + 1] - b_cols[:, h:h + 1]
        dmat = jnp.where(upper, b_row + src_col, NEG)
        a = b_row[:, rows - 1:rows]
        g_end = a + src_row
        m_loc = jnp.max(g_end, axis=-1, keepdims=True)
        w_end = jnp.exp(g_end - m_loc)

        state = c_st[h]
        m_in = m_st[h]
        inter = b_row + m_in
        m_t = jnp.maximum(inter, jnp.max(dmat, axis=0, keepdims=True))
        e_inter = jnp.exp(inter - m_t)
        qt = qt_all[hs].astype(BF16)
        kb = k_all[:, hs].astype(BF16)
        values = jnp.concatenate([vt_all[hs], ones], axis=0)
        p = (_dot(kb, qt) * jnp.exp(dmat - m_t)).astype(BF16)
        from_state = _dot(state.astype(BF16), qt)
        from_chunk = _dot(values, p)
        num = e_inter * from_state[0:HEAD_DIM] + from_chunk[0:HEAD_DIM]
        den = e_inter * from_state[HEAD_DIM:HEAD_DIM + 1] + from_chunk[HEAD_DIM:HEAD_DIM + 1]
        hh = num * (1.0 / jnp.maximum(jnp.abs(den), jnp.exp(-m_t)))
        hh = hh * lax.rsqrt(jnp.mean(hh * hh, axis=0, keepdims=True) + EPS) * hg_ref[hs, :]
        out_ref[0, hs, :] = (hh * ogt[hs]).astype(out_ref.dtype)

        m_new = jnp.maximum(a + m_in, m_loc)
        decay = jnp.exp(a + m_in - m_new)
        fresh = jnp.exp(m_loc - m_new)
        local = _dot((values.astype(F32) * w_end).astype(BF16), kb)
        c_st[h] = decay * state + fresh * local
        m_st[h] = m_new


def _mlstm(u, v, smalls, o_pre, conv_w, conv_b, wq, wk, gate_b, head_g, rows=256):
    bsz, seq, _ = u.shape
    rows = min(rows, seq)
    row = pl.BlockSpec((1, rows, GROUP_WIDTH), lambda b, c: (b, c, 0))
    const2 = lambda a: pl.BlockSpec(a.shape, lambda b, c: (0,) * a.ndim)
    conv_b = conv_b.reshape(1, GROUP_WIDTH)
    head_g = head_g.reshape(GROUP_WIDTH, 1)
    eye = jnp.eye(N_HEADS, dtype=wq.dtype)
    wqt = jnp.einsum('hde,hg->hegd', wq, eye).reshape(GROUP_WIDTH, GROUP_WIDTH).astype(BF16)
    wkb = jnp.einsum('hde,hg->hdge', wk, eye).reshape(GROUP_WIDTH, GROUP_WIDTH).astype(BF16)
    zeros4 = jnp.zeros((N_HEADS,), gate_b.dtype)
    gcol = jnp.stack([jnp.concatenate([zeros4, gate_b[0]]), jnp.concatenate([gate_b[1], zeros4])],
                     axis=1)
    return pl.pallas_call(
        _mlstm_kernel, grid=(bsz, seq // rows),
        in_specs=[row,
                  pl.BlockSpec((1, 8, GROUP_WIDTH),
                               lambda b, c: (b, jnp.maximum(c * (rows // 8) - 1, 0), 0)),
                  row,
                  pl.BlockSpec((1, rows, LANES), lambda b, c: (b, c, 0)),
                  row, const2(conv_w), const2(conv_b), const2(wqt), const2(wkb), const2(gate_b),
                  const2(gcol), const2(head_g)],
        out_specs=pl.BlockSpec((1, GROUP_WIDTH, rows), lambda b, c: (b, 0, c)),
        out_shape=jax.ShapeDtypeStruct((bsz, GROUP_WIDTH, seq), BF16),
        scratch_shapes=[pltpu.VMEM((N_HEADS, ACC_ROWS, HEAD_DIM), F32),
                        pltpu.VMEM((N_HEADS, 1, 1), F32)],
        compiler_params=_cparams(("parallel", "arbitrary")), name="mlstm",
    )(u, u, v, smalls, o_pre, conv_w, conv_b, wqt, wkb, gate_b, gcol, head_g)


def _out_ffn_kernel(x_ref, oa_ref, ob_ref, oc_ref, od_ref, wo_ref, g_ref, wg_ref, wu_ref, wd_ref,
                    gf_ref, y_ref, *, final, ff_chunk):
    mixed_t = jnp.concatenate([oa_ref[0], ob_ref[0], oc_ref[0], od_ref[0]], axis=0)
    x = x_ref[...] + lax.dot_general(mixed_t, wo_ref[...], (((0,), (0,)), ((), ())),
                                     preferred_element_type=F32)
    h = x * lax.rsqrt(jnp.mean(x * x, axis=-1, keepdims=True) + EPS)
    h = (h * g_ref[...]).astype(BF16)
    ffn = None
    for c0 in range(0, D_FF, ff_chunk):
        gate = _dot(h, wg_ref[:, c0:c0 + ff_chunk])
        up = _dot(h, wu_ref[:, c0:c0 + ff_chunk])
        act = (gate * jax.nn.sigmoid(gate) * up).astype(BF16)
        part = _dot(act, wd_ref[c0:c0 + ff_chunk, :])
        ffn = part if ffn is None else ffn + part
    y = x + ffn
    if final:
        y = y * lax.rsqrt(jnp.mean(y * y, axis=-1, keepdims=True) + EPS) * gf_ref[...]
    y_ref[...] = y


def _out_ffn(x2, o_a, o_b, o_c, o_d, w_out, gain, w_gate, w_up, w_down, gain_final, final, tm=256):
    n = x2.shape[0]
    seq = o_a.shape[2]
    tm = min(tm, seq)
    nblk_s = seq // tm
    row = lambda w: pl.BlockSpec((tm, w), lambda i: (i, 0))
    mixer = pl.BlockSpec((1, GROUP_WIDTH, tm), lambda i: (i // nblk_s, 0, i % nblk_s))
    const = lambda a: pl.BlockSpec(a.shape, lambda i: (0, 0))
    gain = gain.reshape(1, D_MODEL)
    gain_final = gain_final.reshape(1, D_MODEL)
    kern = functools.partial(_out_ffn_kernel, final=final, ff_chunk=256)
    return pl.pallas_call(
        kern, grid=(n // tm,),
        in_specs=[row(D_MODEL)] + [mixer] * 4
                 + [const(w_out), const(gain), const(w_gate), const(w_up), const(w_down),
                    const(gain_final)],
        out_specs=row(D_MODEL),
        out_shape=jax.ShapeDtypeStruct((n, D_MODEL), F32),
        compiler_params=_cparams(("parallel",)), name="out_ffn",
    )(x2, o_a, o_b, o_c, o_d, w_out, gain, w_gate, w_up, w_down, gain_final)


def kernel(x, norm_mix, w_in, nsa_cmp_pos, nsa_cmp_w, diff_lambda, diff_norm, mlstm_conv_w,
           mlstm_conv_b, mlstm_wq, mlstm_wk, mlstm_gate_b, mlstm_norm, w_out, norm_ffn, w_gate,
           w_up, w_down, norm_final):
    bsz, seq, _ = x.shape
    depth = w_in.shape[0]
    tables = _lane_tables(seq)
    x2 = x.reshape(bsz * seq, D_MODEL)
    r3 = lambda t: t.reshape(bsz, seq, t.shape[-1])

    for layer in range(depth):
        (a_q, a_qr, a_kvc, a_ks, a_vs, a_kw, a_vw, smalls, b_q, b_k, b_v, c_u, c_v, c_o,
         d_q, d_k, d_v) = _in_proj(x2, norm_mix[layer], w_in[layer], tables, seq,
                                   tm=min(256, seq))
        smalls3 = r3(smalls)

        kc, kvct = _compress(r3(a_kvc), nsa_cmp_w[layer], nsa_cmp_pos[layer])
        o_a = _nsa(a_q, a_qr, kc, kvct, r3(a_ks), a_vs, r3(a_kw), a_vw, smalls3)

        lam_init = 0.8 - 0.6 * math.exp(-0.3 * layer)
        o_b = _diff(b_q, r3(b_k), b_v, diff_lambda[layer], diff_norm[layer], lam_init)

        o_c = _mlstm(r3(c_u), r3(c_v), smalls3, r3(c_o),
                     mlstm_conv_w[layer], mlstm_conv_b[layer], mlstm_wq[layer], mlstm_wk[layer],
                     mlstm_gate_b[layer], mlstm_norm[layer])

        o_d = _dilated(d_q, r3(d_k), d_v)

        x2 = _out_ffn(x2, o_a, o_b, o_c, o_d, w_out[layer].astype(BF16), norm_ffn[layer],
                      w_gate[layer].astype(BF16), w_up[layer].astype(BF16),
                      w_down[layer].astype(BF16), norm_final, final=(layer == depth - 1))
    return x2.reshape(bsz, seq, D_MODEL)
```

```python
import functools
import math

import numpy as np
import jax
import jax.numpy as jnp
from jax import lax
from jax.experimental import pallas as pl
from jax.experimental.pallas import tpu as pltpu

F32 = jnp.float32
BF16 = jnp.bfloat16

D_MODEL = 1024
HEAD_DIM = 64
N_HEADS = 4
GROUP_WIDTH = N_HEADS * HEAD_DIM
ROPE_THETA = 10000.0
EPS = 1e-6
NEG = -1e30
LOG2E = math.log2(math.e)

CMP_LEN = 32
CMP_STRIDE = 16
SLC_LEN = 64
TOP_N = 16
NSA_WINDOW = 512
FORCED_LOCAL = 2
DIFF_HALF = HEAD_DIM // 2
MLSTM_CHUNK = 64
MLSTM_CONV = 4
DILATED_PATTERNS = ((128, 1), (512, 4), (2048, 16))
D_FF = ((8 * D_MODEL + 3 * 256 - 1) // (3 * 256)) * 256

LANES = 128
BF16_SUBLANES = 16
VMEM_LIMIT = 56 * 1024 * 1024
ACC_ROWS = HEAD_DIM + BF16_SUBLANES

IN_SPLITS = (
    GROUP_WIDTH, HEAD_DIM, HEAD_DIM, HEAD_DIM, HEAD_DIM, HEAD_DIM, HEAD_DIM, 3 * N_HEADS,
    GROUP_WIDTH, GROUP_WIDTH, GROUP_WIDTH,
    GROUP_WIDTH, GROUP_WIDTH, N_HEADS, N_HEADS, GROUP_WIDTH,
    GROUP_WIDTH, GROUP_WIDTH, GROUP_WIDTH,
)
(A_Q, A_KC, A_VC, A_KS, A_VS, A_KW, A_VW, A_G, B_Q, B_K, B_V,
 C_U, C_V, C_I, C_F, C_O, D_Q, D_K, D_V) = range(19)
SM_AG, SM_CI, SM_CF = 0, 12, 16


def _cparams(sem):
    return pltpu.CompilerParams(dimension_semantics=sem, vmem_limit_bytes=VMEM_LIMIT)


def _iota(shape, dim):
    return lax.broadcasted_iota(jnp.int32, shape, dim)


def _dot(a, b):
    return jnp.dot(a, b, preferred_element_type=F32)


def _dot_nt(a, b):
    return lax.dot_general(a, b, (((1,), (1,)), ((), ())), preferred_element_type=F32)


PACKED_CHUNKS = 26
D_IN_PADDED = 3072


def _column_plan():
    offs = np.concatenate([[0], np.cumsum(IN_SPLITS)])
    order = [A_Q, A_KC, A_VC, A_KS, None, A_VS, None, A_KW, None, A_VW, None,
             A_G, C_I, C_F, ('pad', LANES - 20),
             B_Q, B_K, B_V, C_U, C_V, C_O, D_Q, D_K, D_V]
    src = []
    for item in order:
        if item is None:
            src += [-1] * HEAD_DIM
        elif isinstance(item, tuple):
            src += [-1] * item[1]
        else:
            src += list(range(int(offs[item]), int(offs[item + 1])))
    src = np.asarray(src)
    assert src.size == PACKED_CHUNKS * LANES
    terms, mats = [], []
    for j in range(PACKED_CHUNKS):
        cols = src[j * LANES:(j + 1) * LANES]
        todo = cols >= 0
        while todo.any():
            start = (cols[todo].min() // LANES) * LANES
            take = todo & (cols < start + 2 * LANES)
            sel = np.zeros((2 * LANES, LANES), np.float32)
            sel[cols[take] - start, np.nonzero(take)[0]] = 1.0
            terms.append((j, int(start)))
            mats.append(sel)
            todo &= ~take
    return tuple(terms), np.stack(mats)


def _rope_tables(seq, dim):
    inv = 1.0 / (ROPE_THETA ** (jnp.arange(0, dim, 2, dtype=F32) / dim))
    ang = jnp.arange(seq, dtype=F32)[:, None] * inv[None, :]
    return jnp.cos(ang), jnp.sin(ang)


def _lane_tables(seq):
    c64, s64 = _rope_tables(seq, HEAD_DIM)
    c32, s32 = _rope_tables(seq, DIFF_HALF)
    t64c = jnp.concatenate([c64, c64] * (LANES // HEAD_DIM), axis=1)
    t64s = jnp.concatenate([-s64, s64] * (LANES // HEAD_DIM), axis=1)
    t32c = jnp.concatenate([c32, c32] * (LANES // DIFF_HALF), axis=1)
    t32s = jnp.concatenate([-s32, s32] * (LANES // DIFF_HALF), axis=1)
    pos = np.arange(seq)[:, None]
    lane = np.arange(LANES)[None, :]
    onehot = ((lane >= HEAD_DIM) & ((pos // SLC_LEN) % HEAD_DIM == lane - HEAD_DIM)
              ).astype(np.float32)
    return t64c, t64s, t32c, t32s, jnp.asarray(onehot)


def _swap_halves(x, group):
    width = x.shape[-1]
    half = group // 2
    lane = _iota(x.shape, 1) & (group - 1)
    up = pltpu.roll(x, width - half, axis=1)
    down = pltpu.roll(x, half, axis=1)
    return jnp.where(lane < half, up, down)


def _rope(x, cos_t, sin_t, group):
    reps = x.shape[-1] // LANES
    if reps > 1:
        cos_t = jnp.concatenate([cos_t] * reps, axis=1)
        sin_t = jnp.concatenate([sin_t] * reps, axis=1)
    return x * cos_t + _swap_halves(x, group) * sin_t


def _in_proj_kernel(x_ref, g_ref, wraw_ref, sel_ref, c64_ref, s64_ref, c32_ref, s32_ref, oh_ref,
                    aq_ref, aqr_ref, akvc_ref, aks_ref, avs_ref, akw_ref, avw_ref, sm_ref,
                    bq_ref, bk_ref, bv_ref, cu_ref, cv_ref, co_ref, dq_ref, dk_ref, dv_ref,
                    w_ref, *, terms):
    @pl.when(pl.program_id(0) == 0)
    def _():
        for j in range(PACKED_CHUNKS):
            chunk = jnp.zeros((D_MODEL, LANES), F32)
            for t, (dst, start) in enumerate(terms):
                if dst == j:
                    chunk = chunk + _dot(wraw_ref[:, start:start + 2 * LANES], sel_ref[t])
            w_ref[:, j * LANES:(j + 1) * LANES] = chunk.astype(BF16)

    x = x_ref[...]
    h = x * lax.rsqrt(jnp.mean(x * x, axis=-1, keepdims=True) + EPS)
    h = (h * g_ref[...]).astype(BF16)
    c64, s64 = c64_ref[...], s64_ref[...]
    c32, s32 = c32_ref[...], s32_ref[...]

    def mm(c0, c1):
        return _dot(h, w_ref[:, c0 * LANES:c1 * LANES])

    def channel_major(ref, z, channels=None):
        zt = z.T
        ref[0] = (zt if channels is None else zt[0:channels]).astype(ref.dtype)

    zq = mm(0, 2) * (HEAD_DIM ** -0.5 * LOG2E)
    channel_major(aq_ref, zq)
    channel_major(aqr_ref, _rope(zq, c64, s64, HEAD_DIM))
    akvc_ref[...] = mm(2, 3).astype(BF16)
    aks_ref[...] = (_rope(mm(3, 4), c64, s64, HEAD_DIM) + oh_ref[...]).astype(BF16)
    channel_major(avs_ref, mm(4, 5), HEAD_DIM)
    akw_ref[...] = _rope(mm(5, 6), c64, s64, HEAD_DIM).astype(BF16)
    channel_major(avw_ref, mm(6, 7), HEAD_DIM)
    sm_ref[...] = mm(7, 8)
    channel_major(bq_ref, _rope(mm(8, 10), c32, s32, DIFF_HALF) * (DIFF_HALF ** -0.5 * LOG2E))
    bk_ref[...] = _rope(mm(10, 12), c32, s32, DIFF_HALF).astype(BF16)
    channel_major(bv_ref, mm(12, 14))
    cu_ref[...] = mm(14, 16)
    cv_ref[...] = mm(16, 18).astype(BF16)
    co_ref[...] = mm(18, 20)
    channel_major(dq_ref, _rope(mm(20, 22), c64, s64, HEAD_DIM) * (HEAD_DIM ** -0.5 * LOG2E))
    dk_ref[...] = _rope(mm(22, 24), c64, s64, HEAD_DIM).astype(BF16)
    channel_major(dv_ref, mm(24, 26))


_IN_PROJ_OUTS = (
    (256, BF16, True), (256, BF16, True), (128, BF16, False), (128, BF16, False),
    (HEAD_DIM, BF16, True), (128, BF16, False), (HEAD_DIM, BF16, True), (128, F32, False),
    (256, BF16, True), (256, BF16, False), (256, BF16, True), (256, F32, False),
    (256, BF16, False), (256, F32, False), (256, BF16, True), (256, BF16, False),
    (256, BF16, True))


def _in_proj(x2, gain, w, tables, seq, tm):
    n = x2.shape[0]
    nblk_s = seq // tm
    terms, select = _column_plan()
    select = jnp.asarray(select, BF16)
    w_raw = jnp.pad(w.astype(BF16), ((0, 0), (0, D_IN_PADDED - w.shape[1])))
    row = lambda i: (i, 0)
    tab = lambda i: (i % nblk_s, 0)
    const = lambda i: (0, 0)
    in_specs = [pl.BlockSpec((tm, D_MODEL), row),
                pl.BlockSpec((1, D_MODEL), const),
                pl.BlockSpec(w_raw.shape, const),
                pl.BlockSpec(select.shape, lambda i: (0, 0, 0))]
    in_specs += [pl.BlockSpec((tm, LANES), tab)] * 5
    out_specs = [pl.BlockSpec((1, w, tm), lambda i: (i // nblk_s, 0, i % nblk_s)) if cmaj
                 else pl.BlockSpec((tm, w), row) for w, _, cmaj in _IN_PROJ_OUTS]
    out_shape = [jax.ShapeDtypeStruct((n // seq, w, seq) if cmaj else (n, w), dt)
                 for w, dt, cmaj in _IN_PROJ_OUTS]
    return pl.pallas_call(
        functools.partial(_in_proj_kernel, terms=terms), grid=(n // tm,), in_specs=in_specs,
        out_specs=out_specs, out_shape=out_shape,
        scratch_shapes=[pltpu.VMEM((D_MODEL, PACKED_CHUNKS * LANES), BF16)],
        compiler_params=_cparams(("arbitrary",)), name="in_proj",
    )(x2, gain.reshape(1, D_MODEL), w_raw, select, *tables)


def _compress_kernel(r_ref, w_ref, pos_ref, kc_ref, kvct_ref):
    wk = w_ref[0].reshape(CMP_LEN, HEAD_DIM, HEAD_DIM)
    wv = w_ref[1].reshape(CMP_LEN, HEAD_DIM, HEAD_DIM)
    zeros = jnp.zeros_like(wk)
    full = jnp.concatenate([jnp.concatenate([wk, zeros], axis=2),
                            jnp.concatenate([zeros, wv], axis=2)], axis=1)
    w_first = full[0:CMP_STRIDE].reshape(CMP_STRIDE * LANES, LANES).astype(BF16)
    w_second = full[CMP_STRIDE:].reshape(CMP_STRIDE * LANES, LANES).astype(BF16)

    r = r_ref[0]
    first = _dot(r, w_first)
    second = _dot(r, w_second)
    nrow = first.shape[0]
    pos = pos_ref[...].astype(BF16)
    half = pos.shape[1] // 2
    const = _dot(pos[:, :half], w_first) + _dot(pos[:, half:], w_second)
    out = first + pltpu.roll(second, nrow - 1, axis=0) + const[0:1, :]
    lane = _iota(out.shape, 1)
    kc_ref[0] = jnp.where(lane < HEAD_DIM, out, 0.0).astype(BF16)
    kvct_ref[0] = out.T.astype(BF16)


def _compress(a_kvc, cmp_w, cmp_pos):
    bsz, seq, _ = a_kvc.shape
    nrow = seq // CMP_STRIDE
    r = a_kvc.reshape(bsz, nrow, CMP_STRIDE * LANES)
    pos = jnp.concatenate([cmp_pos[0], cmp_pos[1]], axis=-1)
    pos = jnp.broadcast_to(pos.reshape(1, CMP_LEN * LANES), (8, CMP_LEN * LANES))
    return pl.pallas_call(
        _compress_kernel, grid=(bsz,),
        in_specs=[pl.BlockSpec((1, nrow, CMP_STRIDE * LANES), lambda b: (b, 0, 0)),
                  pl.BlockSpec(cmp_w.shape, lambda b: (0, 0, 0)),
                  pl.BlockSpec(pos.shape, lambda b: (0, 0))],
        out_specs=[pl.BlockSpec((1, nrow, LANES), lambda b: (b, 0, 0)),
                   pl.BlockSpec((1, LANES, nrow), lambda b: (b, 0, 0))],
        out_shape=[jax.ShapeDtypeStruct((bsz, nrow, LANES), BF16),
                   jax.ShapeDtypeStruct((bsz, LANES, nrow), BF16)],
        compiler_params=_cparams(("parallel",)), name="nsa_compress",
    )(r, cmp_w, pos)


def _flash_init(m_ref, acc_ref):
    m_ref[...] = jnp.full(m_ref.shape, NEG, F32)
    acc_ref[...] = jnp.zeros(acc_ref.shape, F32)


def _flash_step(s, values, m_ref, acc_ref):
    m_old = m_ref[...]
    m_new = jnp.maximum(m_old, jnp.max(s, axis=0, keepdims=True))
    alpha = jnp.exp2(m_old - m_new)
    p = jnp.exp2(s - m_new).astype(BF16)
    width = acc_ref.shape[1]
    for g, vals in enumerate(values):
        rows = slice(g * ACC_ROWS, (g + 1) * ACC_ROWS)
        lanes = slice(g * width, (g + 1) * width)
        acc_ref[rows, :] = alpha[:, lanes] * acc_ref[rows, :] + _dot(vals, p[:, lanes])
    m_ref[...] = m_new


def _pipelined_tiles(lo, hi, scores, consume, sa_ref, sb_ref):
    n = hi - lo
    sa_ref[...] = scores(lo)

    def body(i, carry):
        j = lo + 2 * i
        sb_ref[...] = scores(j + 1)
        consume(sa_ref[...], j, False)
        sa_ref[...] = scores(j + 2)
        consume(sb_ref[...], j + 1, False)
        return carry

    lax.fori_loop(0, n // 2, body, 0)

    @pl.when(n % 2 == 0)
    def _():
        consume(sa_ref[...], hi, True)

    @pl.when(n % 2 == 1)
    def _():
        sb_ref[...] = scores(hi)
        consume(sa_ref[...], hi - 1, False)
        consume(sb_ref[...], hi, True)


def _with_ones(vt):
    return jnp.concatenate([vt, jnp.ones((BF16_SUBLANES, vt.shape[1]), BF16)], axis=0)


def _cmp_to_slc_t(seq):
    n_cmp = (seq - CMP_LEN) // CMP_STRIDE + 1
    n_slc = seq // SLC_LEN
    ratio_s, ratio_c = SLC_LEN // CMP_STRIDE, CMP_LEN // CMP_STRIDE
    jj = np.arange(n_slc)[:, None, None]
    src = ratio_s * jj - np.arange(ratio_s)[None, :, None] - np.arange(ratio_c)[None, None, :]
    ok = (src >= 0) & (src < n_cmp)
    m = np.zeros((seq // CMP_STRIDE, n_slc), np.float32)
    np.add.at(m, (np.where(ok, src, 0), np.broadcast_to(jj, src.shape)), ok.astype(np.float32))
    return jnp.asarray(m.T, BF16)


def _nsa_kernel(qt_ref, qrt_ref, kc_ref, kvct_ref, ks_ref, vst_ref, kw_ref, vwt_ref, sm_ref,
                c2st_ref, o_ref, m_ref, acc_ref, sa_ref, sb_ref, *, tq, top_n):
    qi = pl.program_id(1)
    s0 = qi * tq
    rows = N_HEADS * tq
    n_slc = c2st_ref.shape[0]
    lane_t = s0 + (_iota((1, rows), 1) & (tq - 1))

    def heads_on_lanes(ref):
        x = ref[0]
        return jnp.concatenate([x[h * HEAD_DIM:(h + 1) * HEAD_DIM, :] for h in range(N_HEADS)],
                               axis=1)

    q4 = jnp.concatenate([heads_on_lanes(qt_ref), jnp.zeros((HEAD_DIM, rows), BF16)], axis=0)
    sc = _dot(kc_ref[0], q4)
    cmask = (_iota((sc.shape[0], 1), 0) * CMP_STRIDE + (CMP_LEN - 1)) <= lane_t
    sc = jnp.where(cmask, sc, NEG)
    e = jnp.where(cmask, jnp.exp2(sc - jnp.max(sc, axis=0, keepdims=True)), 0.0)
    z = jnp.sum(e, axis=0, keepdims=True)
    p_cmp = e * (1.0 / jnp.where(z > 0, z, 1.0))
    o_cmp = _dot(kvct_ref[0][HEAD_DIM:2 * HEAD_DIM, :], p_cmp.astype(BF16))
    p_heads = p_cmp[:, 0:tq]
    for h in range(1, N_HEADS):
        p_heads = p_heads + p_cmp[:, h * tq:(h + 1) * tq]
    p_hi = p_heads.astype(BF16)
    p_lo = (p_heads - p_hi.astype(F32)).astype(BF16)
    c2st = c2st_ref[...]
    imp = _dot(c2st, p_hi) + _dot(c2st, p_lo)

    blk = _iota((n_slc, 1), 0)
    cur = (s0 + _iota((1, tq), 1)) >> 6
    forced = (blk == 0) | ((blk <= cur) & (blk > cur - FORCED_LOCAL))
    score = jnp.where(blk > cur, -1.0e6, jnp.where(forced, 1.0e6, imp))
    rank = jnp.zeros((n_slc, tq), F32)
    for i in range(n_slc):
        s_i = score[i:i + 1, :]
        rank = rank + jnp.where(blk > i, jnp.where(s_i >= score, 1.0, 0.0),
                                jnp.where(s_i > score, 1.0, 0.0))
    sel = (rank < top_n) & (blk <= cur)
    bias = jnp.where(sel, 0.0, NEG)
    if n_slc < HEAD_DIM:
        bias = jnp.concatenate([bias, jnp.zeros((HEAD_DIM - n_slc, tq), F32)], axis=0)
    bias4 = jnp.concatenate([bias] * N_HEADS, axis=1).astype(BF16)
    qsel = jnp.concatenate([heads_on_lanes(qrt_ref), bias4], axis=0)

    def attend(k_ref, vt_ref, lo, window):
        _flash_init(m_ref, acc_ref)

        def scores(j):
            k0 = pl.multiple_of(j * tq, tq)
            return _dot(k_ref[0, pl.ds(k0, tq), :], qsel)

        def consume(s, j, diagonal):
            k0 = pl.multiple_of(j * tq, tq)
            kpos = k0 + _iota((tq, 1), 0)
            if diagonal:
                s = jnp.where(kpos <= lane_t, s, NEG)
            if window:
                s = jnp.where(kpos > lane_t - NSA_WINDOW, s, NEG)
            _flash_step(s, [_with_ones(vt_ref[0, :, pl.ds(k0, tq)])], m_ref, acc_ref)

        _pipelined_tiles(lo, qi, scores, consume, sa_ref, sb_ref)
        acc = acc_ref[...]
        return acc[0:HEAD_DIM] * (1.0 / acc[HEAD_DIM:HEAD_DIM + 1])

    o_slc = attend(ks_ref, vst_ref, 0, False)
    o_win = attend(kw_ref, vwt_ref, jnp.maximum(qi - NSA_WINDOW // tq, 0), True)

    g = jax.nn.sigmoid(sm_ref[0].T[0:BF16_SUBLANES, :])

    def gate(branch):
        return jnp.concatenate(
            [g[branch * N_HEADS + h:branch * N_HEADS + h + 1, :] for h in range(N_HEADS)], axis=1)

    o = gate(0) * o_cmp + gate(1) * o_slc + gate(2) * o_win
    for h in range(N_HEADS):
        o_ref[0, h * HEAD_DIM:(h + 1) * HEAD_DIM, :] = o[:, h * tq:(h + 1) * tq].astype(o_ref.dtype)


def _nsa(qt, qrt, kc, kvct, ks, vst, kw, vwt, smalls, tq=256):
    bsz, _, seq = qt.shape
    tq = min(tq, seq)
    n_slc = seq // SLC_LEN
    c2st = _cmp_to_slc_t(seq)
    rows = N_HEADS * tq
    qspec = pl.BlockSpec((1, GROUP_WIDTH, tq), lambda b, i: (b, 0, i))
    full = lambda a: pl.BlockSpec((1,) + a.shape[1:], lambda b, i: (b, 0, 0))
    vspec = pl.BlockSpec((1, HEAD_DIM, seq), lambda b, i: (b, 0, 0))
    kern = functools.partial(_nsa_kernel, tq=tq, top_n=min(TOP_N, n_slc))
    return pl.pallas_call(
        kern, grid=(bsz, seq // tq),
        in_specs=[qspec, qspec, full(kc), full(kvct), full(ks), vspec, full(kw), vspec,
                  pl.BlockSpec((1, tq, LANES), lambda b, i: (b, i, 0)),
                  pl.BlockSpec(c2st.shape, lambda b, i: (0, 0))],
        out_specs=qspec,
        out_shape=jax.ShapeDtypeStruct(qt.shape, BF16),
        scratch_shapes=[pltpu.VMEM((1, rows), F32), pltpu.VMEM((ACC_ROWS, rows), F32),
                        pltpu.VMEM((tq, rows), F32), pltpu.VMEM((tq, rows), F32)],
        compiler_params=_cparams(("parallel", "arbitrary")), name="nsa_attention",
    )(qt, qrt, kc, kvct, ks, vst, kw, vwt, smalls, c2st)


def _diff_kernel(qt_ref, k_ref, vt_ref, lam_ref, g_ref, o_ref, m_ref, acc_ref, sa_ref, sb_ref,
                 *, tq, lam_init):
    qi = pl.program_id(2)
    s0 = qi * tq
    rows = 4 * tq
    qt = qt_ref[0]
    row = _iota((2 * HEAD_DIM, 1), 0)
    zero = jnp.zeros_like(qt)
    qmat = jnp.concatenate(
        [jnp.where((row >= DIFF_HALF * c) & (row < DIFF_HALF * (c + 1)), qt, zero)
         for c in range(4)], axis=1)
    lane_t = s0 + (_iota((1, rows), 1) & (tq - 1))

    def scores(j):
        k0 = pl.multiple_of(j * tq, tq)
        return _dot(k_ref[0, pl.ds(k0, tq), :], qmat)

    def consume(s, j, diagonal):
        k0 = pl.multiple_of(j * tq, tq)
        if diagonal:
            s = jnp.where(k0 + _iota((tq, 1), 0) <= lane_t, s, NEG)
        vt = vt_ref[0, :, pl.ds(k0, tq)]
        _flash_step(s, [_with_ones(vt[0:HEAD_DIM]), _with_ones(vt[HEAD_DIM:])], m_ref, acc_ref)

    _flash_init(m_ref, acc_ref)
    _pipelined_tiles(0, qi, scores, consume, sa_ref, sb_ref)

    lv = lam_ref[...]
    lam = (jnp.exp(jnp.sum(lv[0:1] * lv[1:2], axis=-1, keepdims=True))
           - jnp.exp(jnp.sum(lv[2:3] * lv[3:4], axis=-1, keepdims=True)) + lam_init)
    acc = acc_ref[...]
    for hh in range(2):
        a = acc[ACC_ROWS * hh:ACC_ROWS * (hh + 1)]
        o_all = a[0:HEAD_DIM] * (1.0 / a[HEAD_DIM:HEAD_DIM + 1])
        o = o_all[:, :tq] - lam * o_all[:, tq:]
        y = o * lax.rsqrt(jnp.mean(o * o, axis=0, keepdims=True) + EPS)
        o_ref[0, HEAD_DIM * hh:HEAD_DIM * (hh + 1), :] = (
            (y * g_ref[...]) * (1.0 - lam_init)).astype(o_ref.dtype)


def _diff(qt, k, vt, lam_vecs, sub_g, lam_init, tq=256):
    bsz, _, seq = qt.shape
    tq = min(tq, seq)
    pair = pl.BlockSpec((1, 2 * HEAD_DIM, tq), lambda b, p, i: (b, p, i))
    kern = functools.partial(_diff_kernel, tq=tq, lam_init=lam_init)
    return pl.pallas_call(
        kern, grid=(bsz, N_HEADS // 2, seq // tq),
        in_specs=[pair,
                  pl.BlockSpec((1, seq, 2 * HEAD_DIM), lambda b, p, i: (b, 0, p)),
                  pl.BlockSpec((1, 2 * HEAD_DIM, seq), lambda b, p, i: (b, p, 0)),
                  pl.BlockSpec(lam_vecs.shape, lambda b, p, i: (0, 0)),
                  pl.BlockSpec((HEAD_DIM, 1), lambda b, p, i: (0, 0))],
        out_specs=pair,
        out_shape=jax.ShapeDtypeStruct(qt.shape, BF16),
        scratch_shapes=[pltpu.VMEM((1, 4 * tq), F32), pltpu.VMEM((2 * ACC_ROWS, 2 * tq), F32),
                        pltpu.VMEM((tq, 4 * tq), F32), pltpu.VMEM((tq, 4 * tq), F32)],
        compiler_params=_cparams(("parallel", "parallel", "arbitrary")), name="diff_attention",
    )(qt, k, vt, lam_vecs, sub_g.reshape(HEAD_DIM, 1))


def _dilated_bias(tq):
    max_back = max(w for w, _ in DILATED_PATTERNS) // tq
    classes = [0, 1, 2, 3, max_back]
    q = np.arange(tq)[None, :]
    k = np.arange(tq)[:, None]
    out = []
    for d in classes:
        delta = d * tq + q - k
        cnt = np.zeros((tq, tq), np.float64)
        for w, dil in DILATED_PATTERNS:
            cnt += (delta >= 0) & (delta <= w) & (delta % dil == 0)
        tab = np.where(cnt > 0, np.log2(np.maximum(cnt, 1.0)), NEG)
        out.append(np.concatenate([tab, tab], axis=1))
    return jnp.asarray(np.stack(out).astype(np.float32)), max_back


def _dilated_kernel(qt_ref, k_ref, vt_ref, bias_ref, o_ref, m_ref, acc_ref, sa_ref, sb_ref,
                    *, tq, max_back):
    qi = pl.program_id(2)
    qt = qt_ref[0]
    row = _iota((2 * HEAD_DIM, 1), 0)
    zero = jnp.zeros_like(qt)
    qmat = jnp.concatenate([jnp.where(row < HEAD_DIM, qt, zero),
                            jnp.where(row >= HEAD_DIM, qt, zero)], axis=1)
    _flash_init(m_ref, acc_ref)

    def scores(j):
        k0 = pl.multiple_of(j * tq, tq)
        return _dot(k_ref[0, pl.ds(k0, tq), :], qmat)

    def consume(s, j, diagonal):
        del diagonal
        k0 = pl.multiple_of(j * tq, tq)
        d = qi - j
        cls = jnp.where(d < 3, d, jnp.where(d == max_back, 4, 3))
        vt = vt_ref[0, :, pl.ds(k0, tq)]
        _flash_step(s + bias_ref[cls], [_with_ones(vt[0:HEAD_DIM]), _with_ones(vt[HEAD_DIM:])],
                    m_ref, acc_ref)

    _pipelined_tiles(jnp.maximum(qi - max_back, 0), qi, scores, consume, sa_ref, sb_ref)
    acc = acc_ref[...]
    for hh in range(2):
        a = acc[ACC_ROWS * hh:ACC_ROWS * (hh + 1)]
        o_ref[0, HEAD_DIM * hh:HEAD_DIM * (hh + 1), :] = (
            a[0:HEAD_DIM] * (1.0 / a[HEAD_DIM:HEAD_DIM + 1])).astype(o_ref.dtype)


def _dilated(qt, k, vt, tq=256):
    bsz, _, seq = qt.shape
    tq = min(tq, seq)
    bias, max_back = _dilated_bias(tq)
    pair = pl.BlockSpec((1, 2 * HEAD_DIM, tq), lambda b, p, i: (b, p, i))
    kern = functools.partial(_dilated_kernel, tq=tq, max_back=max_back)
    return pl.pallas_call(
        kern, grid=(bsz, N_HEADS // 2, seq // tq),
        in_specs=[pair,
                  pl.BlockSpec((1, seq, 2 * HEAD_DIM), lambda b, p, i: (b, 0, p)),
                  pl.BlockSpec((1, 2 * HEAD_DIM, seq), lambda b, p, i: (b, p, 0)),
                  pl.BlockSpec(bias.shape, lambda b, p, i: (0, 0, 0))],
        out_specs=pair,
        out_shape=jax.ShapeDtypeStruct(qt.shape, BF16),
        scratch_shapes=[pltpu.VMEM((1, 2 * tq), F32), pltpu.VMEM((2 * ACC_ROWS, tq), F32),
                        pltpu.VMEM((tq, 2 * tq), F32), pltpu.VMEM((tq, 2 * tq), F32)],
        compiler_params=_cparams(("parallel", "parallel", "arbitrary")), name="dilated_attention",
    )(qt, k, vt, bias)


def _mlstm_kernel(u_ref, up_ref, v_ref, sm_ref, o_ref, cw_ref, cb_ref, wqt_ref, wk_ref,
                  gb_ref, gcol_ref, hg_ref, out_ref, c_st, m_st):
    ci = pl.program_id(1)
    rows = u_ref.shape[1]

    @pl.when(ci == 0)
    def _():
        c_st[...] = jnp.zeros(c_st.shape, F32)
        m_st[...] = jnp.zeros(m_st.shape, F32)

    tail = jnp.where(ci > 0, up_ref[0], 0.0)
    ext = jnp.concatenate([tail, u_ref[0]], axis=0)
    cw = cw_ref[...]
    uc = cb_ref[...] + cw[MLSTM_CONV - 1:MLSTM_CONV] * ext[8:]
    for j in range(MLSTM_CONV - 1):
        shifted = pltpu.roll(ext, MLSTM_CONV - 1 - j, axis=0)[8:]
        uc = uc + cw[j:j + 1] * shifted
    uc = uc * jax.nn.sigmoid(uc)

    qt_all = _dot(wqt_ref[...], uc.T.astype(BF16))
    k_all = _dot(uc.astype(BF16), wk_ref[...]) * (HEAD_DIM ** -0.5)
    vt_all = v_ref[0].astype(F32).T.astype(BF16)
    ogt = jax.nn.sigmoid(o_ref[0]).T
    sm = sm_ref[0]
    smt = sm.T
    gb = gb_ref[...]
    gcol = gcol_ref[...]
    upper = _iota((rows, rows), 0) <= _iota((rows, rows), 1)
    lower = _iota((rows, rows), 1) <= _iota((rows, rows), 0)
    ig_rows = smt[8:16] + gcol[:, 0:1]
    lf_rows = jax.nn.log_sigmoid(smt[16:24] + gcol[:, 1:2])
    b_rows = jnp.dot(lf_rows, jnp.where(upper, 1.0, 0.0), preferred_element_type=F32,
                     precision=lax.Precision.HIGHEST)
    ig_cols = sm[:, SM_CI:SM_CI + N_HEADS] + gb[0:1]
    lf_cols = jax.nn.log_sigmoid(sm[:, SM_CF:SM_CF + N_HEADS] + gb[1:2])
    b_cols = jnp.dot(jnp.where(lower, 1.0, 0.0), lf_cols, preferred_element_type=F32,
                     precision=lax.Precision.HIGHEST)
    ones = jnp.ones((BF16_SUBLANES, rows), BF16)

    for h in range(N_HEADS):
        hs = slice(h * HEAD_DIM, (h + 1) * HEAD_DIM)
        b_row = b_rows[h:h + 1]
        src_row = ig_rows[N_HEADS + h:N_HEADS + h + 1] - b_row
        src_col = ig_cols[:, h:h + 1] - b_cols[:, h:h + 1]
        dmat = jnp.where(upper, b_row + src_col, NEG)
        a = b_row[:, rows - 1:rows]
        g_end = a + src_row
        m_loc = jnp.max(g_end, axis=-1, keepdims=True)
        w_end = jnp.exp(g_end - m_loc)

        state = c_st[h]
        m_in = m_st[h]
        inter = b_row + m_in
        m_t = jnp.maximum(inter, jnp.max(dmat, axis=0, keepdims=True))
        e_inter = jnp.exp(inter - m_t)
        qt = qt_all[hs].astype(BF16)
        kb = k_all[:, hs].astype(BF16)
        values = jnp.concatenate([vt_all[hs], ones], axis=0)
        p = (_dot(kb, qt) * jnp.exp(dmat - m_t)).astype(BF16)
        from_state = _dot(state.astype(BF16), qt)
        from_chunk = _dot(values, p)
        num = e_inter * from_state[0:HEAD_DIM] + from_chunk[0:HEAD_DIM]
        den = e_inter * from_state[HEAD_DIM:HEAD_DIM + 1] + from_chunk[HEAD_DIM:HEAD_DIM + 1]
        hh = num * (1.0 / jnp.maximum(jnp.abs(den), jnp.exp(-m_t)))
        hh = hh * lax.rsqrt(jnp.mean(hh * hh, axis=0, keepdims=True) + EPS) * hg_ref[hs, :]
        out_ref[0, hs, :] = (hh * ogt[hs]).astype(out_ref.dtype)

        m_new = jnp.maximum(a + m_in, m_loc)
        decay = jnp.exp(a + m_in - m_new)
        fresh = jnp.exp(m_loc - m_new)
        local = _dot((values.astype(F32) * w_end).astype(BF16), kb)
        c_st[h] = decay * state + fresh * local
        m_st[h] = m_new


def _mlstm(u, v, smalls, o_pre, conv_w, conv_b, wq, wk, gate_b, head_g, rows=256):
    bsz, seq, _ = u.shape
    rows = min(rows, seq)
    row = pl.BlockSpec((1, rows, GROUP_WIDTH), lambda b, c: (b, c, 0))
    const2 = lambda a: pl.BlockSpec(a.shape, lambda b, c: (0,) * a.ndim)
    conv_b = conv_b.reshape(1, GROUP_WIDTH)
    head_g = head_g.reshape(GROUP_WIDTH, 1)
    eye = jnp.eye(N_HEADS, dtype=wq.dtype)
    wqt = jnp.einsum('hde,hg->hegd', wq, eye).reshape(GROUP_WIDTH, GROUP_WIDTH).astype(BF16)
    wkb = jnp.einsum('hde,hg->hdge', wk, eye).reshape(GROUP_WIDTH, GROUP_WIDTH).astype(BF16)
    zeros4 = jnp.zeros((N_HEADS,), gate_b.dtype)
    gcol = jnp.stack([jnp.concatenate([zeros4, gate_b[0]]), jnp.concatenate([gate_b[1], zeros4])],
                     axis=1)
    return pl.pallas_call(
        _mlstm_kernel, grid=(bsz, seq // rows),
        in_specs=[row,
                  pl.BlockSpec((1, 8, GROUP_WIDTH),
                               lambda b, c: (b, jnp.maximum(c * (rows // 8) - 1, 0), 0)),
                  row,
                  pl.BlockSpec((1, rows, LANES), lambda b, c: (b, c, 0)),
                  row, const2(conv_w), const2(conv_b), const2(wqt), const2(wkb), const2(gate_b),
                  const2(gcol), const2(head_g)],
        out_specs=pl.BlockSpec((1, GROUP_WIDTH, rows), lambda b, c: (b, 0, c)),
        out_shape=jax.ShapeDtypeStruct((bsz, GROUP_WIDTH, seq), BF16),
        scratch_shapes=[pltpu.VMEM((N_HEADS, ACC_ROWS, HEAD_DIM), F32),
                        pltpu.VMEM((N_HEADS, 1, 1), F32)],
        compiler_params=_cparams(("parallel", "arbitrary")), name="mlstm",
    )(u, u, v, smalls, o_pre, conv_w, conv_b, wqt, wkb, gate_b, gcol, head_g)


def _out_ffn_kernel(x_ref, oa_ref, ob_ref, oc_ref, od_ref, wo_ref, g_ref, wg_ref, wu_ref, wd_ref,
                    gf_ref, y_ref, *, final, ff_chunk):
    mixed_t = jnp.concatenate([oa_ref[0], ob_ref[0], oc_ref[0], od_ref[0]], axis=0)
    x = x_ref[...] + lax.dot_general(mixed_t, wo_ref[...], (((0,), (0,)), ((), ())),
                                     preferred_element_type=F32)
    h = x * lax.rsqrt(jnp.mean(x * x, axis=-1, keepdims=True) + EPS)
    h = (h * g_ref[...]).astype(BF16)
    ffn = None
    for c0 in range(0, D_FF, ff_chunk):
        gate = _dot(h, wg_ref[:, c0:c0 + ff_chunk])
        up = _dot(h, wu_ref[:, c0:c0 + ff_chunk])
        act = (gate * jax.nn.sigmoid(gate) * up).astype(BF16)
        part = _dot(act, wd_ref[c0:c0 + ff_chunk, :])
        ffn = part if ffn is None else ffn + part
    y = x + ffn
    if final:
        y = y * lax.rsqrt(jnp.mean(y * y, axis=-1, keepdims=True) + EPS) * gf_ref[...]
    y_ref[...] = y


def _out_ffn(x2, o_a, o_b, o_c, o_d, w_out, gain, w_gate, w_up, w_down, gain_final, final, tm=256):
    n = x2.shape[0]
    seq = o_a.shape[2]
    tm = min(tm, seq)
    nblk_s = seq // tm
    row = lambda w: pl.BlockSpec((tm, w), lambda i: (i, 0))
    mixer = pl.BlockSpec((1, GROUP_WIDTH, tm), lambda i: (i // nblk_s, 0, i % nblk_s))
    const = lambda a: pl.BlockSpec(a.shape, lambda i: (0, 0))
    gain = gain.reshape(1, D_MODEL)
    gain_final = gain_final.reshape(1, D_MODEL)
    kern = functools.partial(_out_ffn_kernel, final=final, ff_chunk=256)
    return pl.pallas_call(
        kern, grid=(n // tm,),
        in_specs=[row(D_MODEL)] + [mixer] * 4
                 + [const(w_out), const(gain), const(w_gate), const(w_up), const(w_down),
                    const(gain_final)],
        out_specs=row(D_MODEL),
        out_shape=jax.ShapeDtypeStruct((n, D_MODEL), F32),
        compiler_params=_cparams(("parallel",)), name="out_ffn",
    )(x2, o_a, o_b, o_c, o_d, w_out, gain, w_gate, w_up, w_down, gain_final)


def kernel(x, norm_mix, w_in, nsa_cmp_pos, nsa_cmp_w, diff_lambda, diff_norm, mlstm_conv_w,
           mlstm_conv_b, mlstm_wq, mlstm_wk, mlstm_gate_b, mlstm_norm, w_out, norm_ffn, w_gate,
           w_up, w_down, norm_final):
    bsz, seq, _ = x.shape
    depth = w_in.shape[0]
    tables = _lane_tables(seq)
    x2 = x.reshape(bsz * seq, D_MODEL)
    r3 = lambda t: t.reshape(bsz, seq, t.shape[-1])

    for layer in range(depth):
        (a_q, a_qr, a_kvc, a_ks, a_vs, a_kw, a_vw, smalls, b_q, b_k, b_v, c_u, c_v, c_o,
         d_q, d_k, d_v) = _in_proj(x2, norm_mix[layer], w_in[layer], tables, seq,
                                   tm=min(256, seq))
        smalls3 = r3(smalls)

        kc, kvct = _compress(r3(a_kvc), nsa_cmp_w[layer], nsa_cmp_pos[layer])
        o_a = _nsa(a_q, a_qr, kc, kvct, r3(a_ks), a_vs, r3(a_kw), a_vw, smalls3)

        lam_init = 0.8 - 0.6 * math.exp(-0.3 * layer)
        o_b = _diff(b_q, r3(b_k), b_v, diff_lambda[layer], diff_norm[layer], lam_init)

        o_c = _mlstm(r3(c_u), r3(c_v), smalls3, r3(c_o),
                     mlstm_conv_w[layer], mlstm_conv_b[layer], mlstm_wq[layer], mlstm_wk[layer],
                     mlstm_gate_b[layer], mlstm_norm[layer])

        o_d = _dilated(d_q, r3(d_k), d_v)

        x2 = _out_ffn(x2, o_a, o_b, o_c, o_d, w_out[layer].astype(BF16), norm_ffn[layer],
                      w_gate[layer].astype(BF16), w_up[layer].astype(BF16),
                      w_down[layer].astype(BF16), norm_final, final=(layer == depth - 1))
    return x2.reshape(bsz, seq, D_MODEL)
```

```python
import functools
import math

import numpy as np
import jax
import jax.numpy as jnp
from jax import lax
from jax.experimental import pallas as pl
from jax.experimental.pallas import tpu as pltpu

F32 = jnp.float32
BF16 = jnp.bfloat16

D_MODEL = 1024
HEAD_DIM = 64
N_HEADS = 4
GROUP_WIDTH = N_HEADS * HEAD_DIM
ROPE_THETA = 10000.0
EPS = 1e-6
NEG = -1e30
LOG2E = math.log2(math.e)

CMP_LEN = 32
CMP_STRIDE = 16
SLC_LEN = 64
TOP_N = 16
NSA_WINDOW = 512
FORCED_LOCAL = 2
DIFF_HALF = HEAD_DIM // 2
MLSTM_CHUNK = 64
MLSTM_CONV = 4
DILATED_PATTERNS = ((128, 1), (512, 4), (2048, 16))
D_FF = ((8 * D_MODEL + 3 * 256 - 1) // (3 * 256)) * 256

LANES = 128
BF16_SUBLANES = 16
VMEM_LIMIT = 56 * 1024 * 1024
ACC_ROWS = HEAD_DIM + BF16_SUBLANES

IN_SPLITS = (
    GROUP_WIDTH, HEAD_DIM, HEAD_DIM, HEAD_DIM, HEAD_DIM, HEAD_DIM, HEAD_DIM, 3 * N_HEADS,
    GROUP_WIDTH, GROUP_WIDTH, GROUP_WIDTH,
    GROUP_WIDTH, GROUP_WIDTH, N_HEADS, N_HEADS, GROUP_WIDTH,
    GROUP_WIDTH, GROUP_WIDTH, GROUP_WIDTH,
)
(A_Q, A_KC, A_VC, A_KS, A_VS, A_KW, A_VW, A_G, B_Q, B_K, B_V,
 C_U, C_V, C_I, C_F, C_O, D_Q, D_K, D_V) = range(19)
SM_AG, SM_CI, SM_CF = 0, 12, 16


def _cparams(sem):
    return pltpu.CompilerParams(dimension_semantics=sem, vmem_limit_bytes=VMEM_LIMIT)


def _iota(shape, dim):
    return lax.broadcasted_iota(jnp.int32, shape, dim)


def _dot(a, b):
    return jnp.dot(a, b, preferred_element_type=F32)


def _dot_nt(a, b):
    return lax.dot_general(a, b, (((1,), (1,)), ((), ())), preferred_element_type=F32)


PACKED_CHUNKS = 26
D_IN_PADDED = 3072


def _column_plan():
    offs = np.concatenate([[0], np.cumsum(IN_SPLITS)])
    order = [A_Q, A_KC, A_VC, A_KS, None, A_VS, None, A_KW, None, A_VW, None,
             A_G, C_I, C_F, ('pad', LANES - 20),
             B_Q, B_K, B_V, C_U, C_V, C_O, D_Q, D_K, D_V]
    src = []
    for item in order:
        if item is None:
            src += [-1] * HEAD_DIM
        elif isinstance(item, tuple):
            src += [-1] * item[1]
        else:
            src += list(range(int(offs[item]), int(offs[item + 1])))
    src = np.asarray(src)
    assert src.size == PACKED_CHUNKS * LANES
    terms, mats = [], []
    for j in range(PACKED_CHUNKS):
        cols = src[j * LANES:(j + 1) * LANES]
        todo = cols >= 0
        while todo.any():
            start = (cols[todo].min() // LANES) * LANES
            take = todo & (cols < start + 2 * LANES)
            sel = np.zeros((2 * LANES, LANES), np.float32)
            sel[cols[take] - start, np.nonzero(take)[0]] = 1.0
            terms.append((j, int(start)))
            mats.append(sel)
            todo &= ~take
    return tuple(terms), np.stack(mats)


def _rope_tables(seq, dim):
    inv = 1.0 / (ROPE_THETA ** (jnp.arange(0, dim, 2, dtype=F32) / dim))
    ang = jnp.arange(seq, dtype=F32)[:, None] * inv[None, :]
    return jnp.cos(ang), jnp.sin(ang)


def _lane_tables(seq):
    c64, s64 = _rope_tables(seq, HEAD_DIM)
    c32, s32 = _rope_tables(seq, DIFF_HALF)
    t64c = jnp.concatenate([c64, c64] * (LANES // HEAD_DIM), axis=1)
    t64s = jnp.concatenate([-s64, s64] * (LANES // HEAD_DIM), axis=1)
    t32c = jnp.concatenate([c32, c32] * (LANES // DIFF_HALF), axis=1)
    t32s = jnp.concatenate([-s32, s32] * (LANES // DIFF_HALF), axis=1)
    pos = np.arange(seq)[:, None]
    lane = np.arange(LANES)[None, :]
    onehot = ((lane >= HEAD_DIM) & ((pos // SLC_LEN) % HEAD_DIM == lane - HEAD_DIM)
              ).astype(np.float32)
    return t64c, t64s, t32c, t32s, jnp.asarray(onehot)


def _swap_halves(x, group):
    width = x.shape[-1]
    half = group // 2
    lane = _iota(x.shape, 1) & (group - 1)
    up = pltpu.roll(x, width - half, axis=1)
    down = pltpu.roll(x, half, axis=1)
    return jnp.where(lane < half, up, down)


def _rope(x, cos_t, sin_t, group):
    reps = x.shape[-1] // LANES
    if reps > 1:
        cos_t = jnp.concatenate([cos_t] * reps, axis=1)
        sin_t = jnp.concatenate([sin_t] * reps, axis=1)
    return x * cos_t + _swap_halves(x, group) * sin_t


def _in_proj_kernel(x_ref, g_ref, wraw_ref, sel_ref, c64_ref, s64_ref, c32_ref, s32_ref, oh_ref,
                    aq_ref, aqr_ref, akvc_ref, aks_ref, avs_ref, akw_ref, avw_ref, sm_ref,
                    bq_ref, bk_ref, bv_ref, cu_ref, cv_ref, co_ref, dq_ref, dk_ref, dv_ref,
                    w_ref, *, terms):
    @pl.when(pl.program_id(0) == 0)
    def _():
        for j in range(PACKED_CHUNKS):
            chunk = jnp.zeros((D_MODEL, LANES), F32)
            for t, (dst, start) in enumerate(terms):
                if dst == j:
                    chunk = chunk + _dot(wraw_ref[:, start:start + 2 * LANES], sel_ref[t])
            w_ref[:, j * LANES:(j + 1) * LANES] = chunk.astype(BF16)

    x = x_ref[...]
    h = x * lax.rsqrt(jnp.mean(x * x, axis=-1, keepdims=True) + EPS)
    h = (h * g_ref[...]).astype(BF16)
    c64, s64 = c64_ref[...], s64_ref[...]
    c32, s32 = c32_ref[...], s32_ref[...]

    def mm(c0, c1):
        return _dot(h, w_ref[:, c0 * LANES:c1 * LANES])

    def channel_major(ref, z, channels=None):
        zt = z.T
        ref[0] = (zt if channels is None else zt[0:channels]).astype(ref.dtype)

    zq = mm(0, 2) * (HEAD_DIM ** -0.5 * LOG2E)
    channel_major(aq_ref, zq)
    channel_major(aqr_ref, _rope(zq, c64, s64, HEAD_DIM))
    akvc_ref[...] = mm(2, 3).astype(BF16)
    aks_ref[...] = (_rope(mm(3, 4), c64, s64, HEAD_DIM) + oh_ref[...]).astype(BF16)
    channel_major(avs_ref, mm(4, 5), HEAD_DIM)
    akw_ref[...] = _rope(mm(5, 6), c64, s64, HEAD_DIM).astype(BF16)
    channel_major(avw_ref, mm(6, 7), HEAD_DIM)
    sm_ref[...] = mm(7, 8)
    channel_major(bq_ref, _rope(mm(8, 10), c32, s32, DIFF_HALF) * (DIFF_HALF ** -0.5 * LOG2E))
    bk_ref[...] = _rope(mm(10, 12), c32, s32, DIFF_HALF).astype(BF16)
    channel_major(bv_ref, mm(12, 14))
    cu_ref[...] = mm(14, 16)
    cv_ref[...] = mm(16, 18).astype(BF16)
    co_ref[...] = mm(18, 20)
    channel_major(dq_ref, _rope(mm(20, 22), c64, s64, HEAD_DIM) * (HEAD_DIM ** -0.5 * LOG2E))
    dk_ref[...] = _rope(mm(22, 24), c64, s64, HEAD_DIM).astype(BF16)
    channel_major(dv_ref, mm(24, 26))


_IN_PROJ_OUTS = (
    (256, BF16, True), (256, BF16, True), (128, BF16, False), (128, BF16, False),
    (HEAD_DIM, BF16, True), (128, BF16, False), (HEAD_DIM, BF16, True), (128, F32, False),
    (256, BF16, True), (256, BF16, False), (256, BF16, True), (256, F32, False),
    (256, BF16, False), (256, F32, False), (256, BF16, True), (256, BF16, False),
    (256, BF16, True))


def _in_proj(x2, gain, w, tables, seq, tm):
    n = x2.shape[0]
    nblk_s = seq // tm
    terms, select = _column_plan()
    select = jnp.asarray(select, BF16)
    w_raw = jnp.pad(w.astype(BF16), ((0, 0), (0, D_IN_PADDED - w.shape[1])))
    row = lambda i: (i, 0)
    tab = lambda i: (i % nblk_s, 0)
    const = lambda i: (0, 0)
    in_specs = [pl.BlockSpec((tm, D_MODEL), row),
                pl.BlockSpec((1, D_MODEL), const),
                pl.BlockSpec(w_raw.shape, const),
                pl.BlockSpec(select.shape, lambda i: (0, 0, 0))]
    in_specs += [pl.BlockSpec((tm, LANES), tab)] * 5
    out_specs = [pl.BlockSpec((1, w, tm), lambda i: (i // nblk_s, 0, i % nblk_s)) if cmaj
                 else pl.BlockSpec((tm, w), row) for w, _, cmaj in _IN_PROJ_OUTS]
    out_shape = [jax.ShapeDtypeStruct((n // seq, w, seq) if cmaj else (n, w), dt)
                 for w, dt, cmaj in _IN_PROJ_OUTS]
    return pl.pallas_call(
        functools.partial(_in_proj_kernel, terms=terms), grid=(n // tm,), in_specs=in_specs,
        out_specs=out_specs, out_shape=out_shape,
        scratch_shapes=[pltpu.VMEM((D_MODEL, PACKED_CHUNKS * LANES), BF16)],
        compiler_params=_cparams(("arbitrary",)), name="in_proj",
    )(x2, gain.reshape(1, D_MODEL), w_raw, select, *tables)


def _compress_kernel(r_ref, w_ref, pos_ref, kc_ref, kvct_ref):
    wk = w_ref[0].reshape(CMP_LEN, HEAD_DIM, HEAD_DIM)
    wv = w_ref[1].reshape(CMP_LEN, HEAD_DIM, HEAD_DIM)
    zeros = jnp.zeros_like(wk)
    full = jnp.concatenate([jnp.concatenate([wk, zeros], axis=2),
                            jnp.concatenate([zeros, wv], axis=2)], axis=1)
    w_first = full[0:CMP_STRIDE].reshape(CMP_STRIDE * LANES, LANES).astype(BF16)
    w_second = full[CMP_STRIDE:].reshape(CMP_STRIDE * LANES, LANES).astype(BF16)

    r = r_ref[0]
    first = _dot(r, w_first)
    second = _dot(r, w_second)
    nrow = first.shape[0]
    pos = pos_ref[...].astype(BF16)
    half = pos.shape[1] // 2
    const = _dot(pos[:, :half], w_first) + _dot(pos[:, half:], w_second)
    out = first + pltpu.roll(second, nrow - 1, axis=0) + const[0:1, :]
    lane = _iota(out.shape, 1)
    kc_ref[0] = jnp.where(lane < HEAD_DIM, out, 0.0).astype(BF16)
    kvct_ref[0] = out.T.astype(BF16)


def _compress(a_kvc, cmp_w, cmp_pos):
    bsz, seq, _ = a_kvc.shape
    nrow = seq // CMP_STRIDE
    r = a_kvc.reshape(bsz, nrow, CMP_STRIDE * LANES)
    pos = jnp.concatenate([cmp_pos[0], cmp_pos[1]], axis=-1)
    pos = jnp.broadcast_to(pos.reshape(1, CMP_LEN * LANES), (8, CMP_LEN * LANES))
    return pl.pallas_call(
        _compress_kernel, grid=(bsz,),
        in_specs=[pl.BlockSpec((1, nrow, CMP_STRIDE * LANES), lambda b: (b, 0, 0)),
                  pl.BlockSpec(cmp_w.shape, lambda b: (0, 0, 0)),
                  pl.BlockSpec(pos.shape, lambda b: (0, 0))],
        out_specs=[pl.BlockSpec((1, nrow, LANES), lambda b: (b, 0, 0)),
                   pl.BlockSpec((1, LANES, nrow), lambda b: (b, 0, 0))],
        out_shape=[jax.ShapeDtypeStruct((bsz, nrow, LANES), BF16),
                   jax.ShapeDtypeStruct((bsz, LANES, nrow), BF16)],
        compiler_params=_cparams(("parallel",)), name="nsa_compress",
    )(r, cmp_w, pos)


def _flash_init(m_ref, acc_ref):
    m_ref[...] = jnp.full(m_ref.shape, NEG, F32)
    acc_ref[...] = jnp.zeros(acc_ref.shape, F32)


def _flash_step(s, values, m_ref, acc_ref):
    m_old = m_ref[...]
    m_new = jnp.maximum(m_old, jnp.max(s, axis=0, keepdims=True))
    alpha = jnp.exp2(m_old - m_new)
    p = jnp.exp2(s - m_new).astype(BF16)
    width = acc_ref.shape[1]
    for g, vals in enumerate(values):
        rows = slice(g * ACC_ROWS, (g + 1) * ACC_ROWS)
        lanes = slice(g * width, (g + 1) * width)
        acc_ref[rows, :] = alpha[:, lanes] * acc_ref[rows, :] + _dot(vals, p[:, lanes])
    m_ref[...] = m_new


def _pipelined_tiles(lo, hi, scores, consume, sa_ref, sb_ref):
    n = hi - lo
    sa_ref[...] = scores(lo)

    def body(i, carry):
        j = lo + 2 * i
        sb_ref[...] = scores(j + 1)
        consume(sa_ref[...], j, False)
        sa_ref[...] = scores(j + 2)
        consume(sb_ref[...], j + 1, False)
        return carry

    lax.fori_loop(0, n // 2, body, 0)

    @pl.when(n % 2 == 0)
    def _():
        consume(sa_ref[...], hi, True)

    @pl.when(n % 2 == 1)
    def _():
        sb_ref[...] = scores(hi)
        consume(sa_ref[...], hi - 1, False)
        consume(sb_ref[...], hi, True)


def _with_ones(vt):
    return jnp.concatenate([vt, jnp.ones((BF16_SUBLANES, vt.shape[1]), BF16)], axis=0)


def _cmp_to_slc_t(seq):
    n_cmp = (seq - CMP_LEN) // CMP_STRIDE + 1
    n_slc = seq // SLC_LEN
    ratio_s, ratio_c = SLC_LEN // CMP_STRIDE, CMP_LEN // CMP_STRIDE
    jj = np.arange(n_slc)[:, None, None]
    src = ratio_s * jj - np.arange(ratio_s)[None, :, None] - np.arange(ratio_c)[None, None, :]
    ok = (src >= 0) & (src < n_cmp)
    m = np.zeros((seq // CMP_STRIDE, n_slc), np.float32)
    np.add.at(m, (np.where(ok, src, 0), np.broadcast_to(jj, src.shape)), ok.astype(np.float32))
    return jnp.asarray(m.T, BF16)


def _nsa_kernel(qt_ref, qrt_ref, kc_ref, kvct_ref, ks_ref, vst_ref, kw_ref, vwt_ref, sm_ref,
                c2st_ref, o_ref, m_ref, acc_ref, sa_ref, sb_ref, *, tq, top_n):
    qi = pl.program_id(1)
    s0 = qi * tq
    rows = N_HEADS * tq
    n_slc = c2st_ref.shape[0]
    lane_t = s0 + (_iota((1, rows), 1) & (tq - 1))

    def heads_on_lanes(ref):
        x = ref[0]
        return jnp.concatenate([x[h * HEAD_DIM:(h + 1) * HEAD_DIM, :] for h in range(N_HEADS)],
                               axis=1)

    q4 = jnp.concatenate([heads_on_lanes(qt_ref), jnp.zeros((HEAD_DIM, rows), BF16)], axis=0)
    sc = _dot(kc_ref[0], q4)
    cmask = (_iota((sc.shape[0], 1), 0) * CMP_STRIDE + (CMP_LEN - 1)) <= lane_t
    sc = jnp.where(cmask, sc, NEG)
    e = jnp.where(cmask, jnp.exp2(sc - jnp.max(sc, axis=0, keepdims=True)), 0.0)
    z = jnp.sum(e, axis=0, keepdims=True)
    p_cmp = e * (1.0 / jnp.where(z > 0, z, 1.0))
    o_cmp = _dot(kvct_ref[0][HEAD_DIM:2 * HEAD_DIM, :], p_cmp.astype(BF16))
    p_heads = p_cmp[:, 0:tq]
    for h in range(1, N_HEADS):
        p_heads = p_heads + p_cmp[:, h * tq:(h + 1) * tq]
    p_hi = p_heads.astype(BF16)
    p_lo = (p_heads - p_hi.astype(F32)).astype(BF16)
    c2st = c2st_ref[...]
    imp = _dot(c2st, p_hi) + _dot(c2st, p_lo)

    blk = _iota((n_slc, 1), 0)
    cur = (s0 + _iota((1, tq), 1)) >> 6
    forced = (blk == 0) | ((blk <= cur) & (blk > cur - FORCED_LOCAL))
    score = jnp.where(blk > cur, -1.0e6, jnp.where(forced, 1.0e6, imp))
    rank = jnp.zeros((n_slc, tq), F32)
    for i in range(n_slc):
        s_i = score[i:i + 1, :]
        rank = rank + jnp.where(blk > i, jnp.where(s_i >= score, 1.0, 0.0),
                                jnp.where(s_i > score, 1.0, 0.0))
    sel = (rank < top_n) & (blk <= cur)
    bias = jnp.where(sel, 0.0, NEG)
    if n_slc < HEAD_DIM:
        bias = jnp.concatenate([bias, jnp.zeros((HEAD_DIM - n_slc, tq), F32)], axis=0)
    bias4 = jnp.concatenate([bias] * N_HEADS, axis=1).astype(BF16)
    qsel = jnp.concatenate([heads_on_lanes(qrt_ref), bias4], axis=0)

    def attend(k_ref, vt_ref, lo, window):
        _flash_init(m_ref, acc_ref)

        def scores(j):
            k0 = pl.multiple_of(j * tq, tq)
            return _dot(k_ref[0, pl.ds(k0, tq), :], qsel)

        def consume(s, j, diagonal):
            k0 = pl.multiple_of(j * tq, tq)
            kpos = k0 + _iota((tq, 1), 0)
            if diagonal:
                s = jnp.where(kpos <= lane_t, s, NEG)
            if window:
                s = jnp.where(kpos > lane_t - NSA_WINDOW, s, NEG)
            _flash_step(s, [_with_ones(vt_ref[0, :, pl.ds(k0, tq)])], m_ref, acc_ref)

        _pipelined_tiles(lo, qi, scores, consume, sa_ref, sb_ref)
        acc = acc_ref[...]
        return acc[0:HEAD_DIM] * (1.0 / acc[HEAD_DIM:HEAD_DIM + 1])

    o_slc = attend(ks_ref, vst_ref, 0, False)
    o_win = attend(kw_ref, vwt_ref, jnp.maximum(qi - NSA_WINDOW // tq, 0), True)

    g = jax.nn.sigmoid(sm_ref[0].T[0:BF16_SUBLANES, :])

    def gate(branch):
        return jnp.concatenate(
            [g[branch * N_HEADS + h:branch * N_HEADS + h + 1, :] for h in range(N_HEADS)], axis=1)

    o = gate(0) * o_cmp + gate(1) * o_slc + gate(2) * o_win
    for h in range(N_HEADS):
        o_ref[0, h * HEAD_DIM:(h + 1) * HEAD_DIM, :] = o[:, h * tq:(h + 1) * tq].astype(o_ref.dtype)


def _nsa(qt, qrt, kc, kvct, ks, vst, kw, vwt, smalls, tq=256):
    bsz, _, seq = qt.shape
    tq = min(tq, seq)
    n_slc = seq // SLC_LEN
    c2st = _cmp_to_slc_t(seq)
    rows = N_HEADS * tq
    qspec = pl.BlockSpec((1, GROUP_WIDTH, tq), lambda b, i: (b, 0, i))
    full = lambda a: pl.BlockSpec((1,) + a.shape[1:], lambda b, i: (b, 0, 0))
    vspec = pl.BlockSpec((1, HEAD_DIM, seq), lambda b, i: (b, 0, 0))
    kern = functools.partial(_nsa_kernel, tq=tq, top_n=min(TOP_N, n_slc))
    return pl.pallas_call(
        kern, grid=(bsz, seq // tq),
        in_specs=[qspec, qspec, full(kc), full(kvct), full(ks), vspec, full(kw), vspec,
                  pl.BlockSpec((1, tq, LANES), lambda b, i: (b, i, 0)),
                  pl.BlockSpec(c2st.shape, lambda b, i: (0, 0))],
        out_specs=qspec,
        out_shape=jax.ShapeDtypeStruct(qt.shape, BF16),
        scratch_shapes=[pltpu.VMEM((1, rows), F32), pltpu.VMEM((ACC_ROWS, rows), F32),
                        pltpu.VMEM((tq, rows), F32), pltpu.VMEM((tq, rows), F32)],
        compiler_params=_cparams(("parallel", "arbitrary")), name="nsa_attention",
    )(qt, qrt, kc, kvct, ks, vst, kw, vwt, smalls, c2st)


def _diff_kernel(qt_ref, k_ref, vt_ref, lam_ref, g_ref, o_ref, m_ref, acc_ref, sa_ref, sb_ref,
                 *, tq, lam_init):
    qi = pl.program_id(2)
    s0 = qi * tq
    rows = 4 * tq
    qt = qt_ref[0]
    row = _iota((2 * HEAD_DIM, 1), 0)
    zero = jnp.zeros_like(qt)
    qmat = jnp.concatenate(
        [jnp.where((row >= DIFF_HALF * c) & (row < DIFF_HALF * (c + 1)), qt, zero)
         for c in range(4)], axis=1)
    lane_t = s0 + (_iota((1, rows), 1) & (tq - 1))

    def scores(j):
        k0 = pl.multiple_of(j * tq, tq)
        return _dot(k_ref[0, pl.ds(k0, tq), :], qmat)

    def consume(s, j, diagonal):
        k0 = pl.multiple_of(j * tq, tq)
        if diagonal:
            s = jnp.where(k0 + _iota((tq, 1), 0) <= lane_t, s, NEG)
        vt = vt_ref[0, :, pl.ds(k0, tq)]
        _flash_step(s, [_with_ones(vt[0:HEAD_DIM]), _with_ones(vt[HEAD_DIM:])], m_ref, acc_ref)

    _flash_init(m_ref, acc_ref)
    _pipelined_tiles(0, qi, scores, consume, sa_ref, sb_ref)

    lv = lam_ref[...]
    lam = (jnp.exp(jnp.sum(lv[0:1] * lv[1:2], axis=-1, keepdims=True))
           - jnp.exp(jnp.sum(lv[2:3] * lv[3:4], axis=-1, keepdims=True)) + lam_init)
    acc = acc_ref[...]
    for hh in range(2):
        a = acc[ACC_ROWS * hh:ACC_ROWS * (hh + 1)]
        o_all = a[0:HEAD_DIM] * (1.0 / a[HEAD_DIM:HEAD_DIM + 1])
        o = o_all[:, :tq] - lam * o_all[:, tq:]
        y = o * lax.rsqrt(jnp.mean(o * o, axis=0, keepdims=True) + EPS)
        o_ref[0, HEAD_DIM * hh:HEAD_DIM * (hh + 1), :] = (
            (y * g_ref[...]) * (1.0 - lam_init)).astype(o_ref.dtype)


def _diff(qt, k, vt, lam_vecs, sub_g, lam_init, tq=256):
    bsz, _, seq = qt.shape
    tq = min(tq, seq)
    pair = pl.BlockSpec((1, 2 * HEAD_DIM, tq), lambda b, p, i: (b, p, i))
    kern = functools.partial(_diff_kernel, tq=tq, lam_init=lam_init)
    return pl.pallas_call(
        kern, grid=(bsz, N_HEADS // 2, seq // tq),
        in_specs=[pair,
                  pl.BlockSpec((1, seq, 2 * HEAD_DIM), lambda b, p, i: (b, 0, p)),
                  pl.BlockSpec((1, 2 * HEAD_DIM, seq), lambda b, p, i: (b, p, 0)),
                  pl.BlockSpec(lam_vecs.shape, lambda b, p, i: (0, 0)),
                  pl.BlockSpec((HEAD_DIM, 1), lambda b, p, i: (0, 0))],
        out_specs=pair,
        out_shape=jax.ShapeDtypeStruct(qt.shape, BF16),
        scratch_shapes=[pltpu.VMEM((1, 4 * tq), F32), pltpu.VMEM((2 * ACC_ROWS, 2 * tq), F32),
                        pltpu.VMEM((tq, 4 * tq), F32), pltpu.VMEM((tq, 4 * tq), F32)],
        compiler_params=_cparams(("parallel", "parallel", "arbitrary")), name="diff_attention",
    )(qt, k, vt, lam_vecs, sub_g.reshape(HEAD_DIM, 1))


def _dilated_bias(tq):
    max_back = max(w for w, _ in DILATED_PATTERNS) // tq
    classes = [0, 1, 2, 3, max_back]
    q = np.arange(tq)[None, :]
    k = np.arange(tq)[:, None]
    out = []
    for d in classes:
        delta = d * tq + q - k
        cnt = np.zeros((tq, tq), np.float64)
        for w, dil in DILATED_PATTERNS:
            cnt += (delta >= 0) & (delta <= w) & (delta % dil == 0)
        tab = np.where(cnt > 0, np.log2(np.maximum(cnt, 1.0)), NEG)
        out.append(np.concatenate([tab, tab], axis=1))
    return jnp.asarray(np.stack(out).astype(np.float32)), max_back


def _dilated_kernel(qt_ref, k_ref, vt_ref, bias_ref, o_ref, m_ref, acc_ref, sa_ref, sb_ref,
                    *, tq, max_back):
    qi = pl.program_id(2)
    qt = qt_ref[0]
    row = _iota((2 * HEAD_DIM, 1), 0)
    zero = jnp.zeros_like(qt)
    qmat = jnp.concatenate([jnp.where(row < HEAD_DIM, qt, zero),
                            jnp.where(row >= HEAD_DIM, qt, zero)], axis=1)
    _flash_init(m_ref, acc_ref)

    def scores(j):
        k0 = pl.multiple_of(j * tq, tq)
        return _dot(k_ref[0, pl.ds(k0, tq), :], qmat)

    def consume(s, j, diagonal):
        del diagonal
        k0 = pl.multiple_of(j * tq, tq)
        d = qi - j
        cls = jnp.where(d < 3, d, jnp.where(d == max_back, 4, 3))
        vt = vt_ref[0, :, pl.ds(k0, tq)]
        _flash_step(s + bias_ref[cls], [_with_ones(vt[0:HEAD_DIM]), _with_ones(vt[HEAD_DIM:])],
                    m_ref, acc_ref)

    _pipelined_tiles(jnp.maximum(qi - max_back, 0), qi, scores, consume, sa_ref, sb_ref)
    acc = acc_ref[...]
    for hh in range(2):
        a = acc[ACC_ROWS * hh:ACC_ROWS * (hh + 1)]
        o_ref[0, HEAD_DIM * hh:HEAD_DIM * (hh + 1), :] = (
            a[0:HEAD_DIM] * (1.0 / a[HEAD_DIM:HEAD_DIM + 1])).astype(o_ref.dtype)


def _dilated(qt, k, vt, tq=256):
    bsz, _, seq = qt.shape
    tq = min(tq, seq)
    bias, max_back = _dilated_bias(tq)
    pair = pl.BlockSpec((1, 2 * HEAD_DIM, tq), lambda b, p, i: (b, p, i))
    kern = functools.partial(_dilated_kernel, tq=tq, max_back=max_back)
    return pl.pallas_call(
        kern, grid=(bsz, N_HEADS // 2, seq // tq),
        in_specs=[pair,
                  pl.BlockSpec((1, seq, 2 * HEAD_DIM), lambda b, p, i: (b, 0, p)),
                  pl.BlockSpec((1, 2 * HEAD_DIM, seq), lambda b, p, i: (b, p, 0)),
                  pl.BlockSpec(bias.shape, lambda b, p, i: (0, 0, 0))],
        out_specs=pair,
        out_shape=jax.ShapeDtypeStruct(qt.shape, BF16),
        scratch_shapes=[pltpu.VMEM((1, 2 * tq), F32), pltpu.VMEM((2 * ACC_ROWS, tq), F32),
                        pltpu.VMEM((tq, 2 * tq), F32), pltpu.VMEM((tq, 2 * tq), F32)],
        compiler_params=_cparams(("parallel", "parallel", "arbitrary")), name="dilated_attention",
    )(qt, k, vt, bias)


def _mlstm_kernel(u_ref, up_ref, v_ref, sm_ref, o_ref, cw_ref, cb_ref, wqt_ref, wk_ref,
                  gb_ref, gcol_ref, hg_ref, out_ref, c_st, m_st):
    ci = pl.program_id(1)
    rows = u_ref.shape[1]

    @pl.when(ci == 0)
    def _():
        c_st[...] = jnp.zeros(c_st.shape, F32)
        m_st[...] = jnp.zeros(m_st.shape, F32)

    tail = jnp.where(ci > 0, up_ref[0], 0.0)
    ext = jnp.concatenate([tail, u_ref[0]], axis=0)
    cw = cw_ref[...]
    uc = cb_ref[...] + cw[MLSTM_CONV - 1:MLSTM_CONV] * ext[8:]
    for j in range(MLSTM_CONV - 1):
        shifted = pltpu.roll(ext, MLSTM_CONV - 1 - j, axis=0)[8:]
        uc = uc + cw[j:j + 1] * shifted
    uc = uc * jax.nn.sigmoid(uc)

    qt_all = _dot(wqt_ref[...], uc.T.astype(BF16))
    k_all = _dot(uc.astype(BF16), wk_ref[...]) * (HEAD_DIM ** -0.5)
    vt_all = v_ref[0].astype(F32).T.astype(BF16)
    ogt = jax.nn.sigmoid(o_ref[0]).T
    sm = sm_ref[0]
    smt = sm.T
    gb = gb_ref[...]
    gcol = gcol_ref[...]
    upper = _iota((rows, rows), 0) <= _iota((rows, rows), 1)
    lower = _iota((rows, rows), 1) <= _iota((rows, rows), 0)
    ig_rows = smt[8:16] + gcol[:, 0:1]
    lf_rows = jax.nn.log_sigmoid(smt[16:24] + gcol[:, 1:2])
    b_rows = jnp.dot(lf_rows, jnp.where(upper, 1.0, 0.0), preferred_element_type=F32,
                     precision=lax.Precision.HIGHEST)
    ig_cols = sm[:, SM_CI:SM_CI + N_HEADS] + gb[0:1]
    lf_cols = jax.nn.log_sigmoid(sm[:, SM_CF:SM_CF + N_HEADS] + gb[1:2])
    b_cols = jnp.dot(jnp.where(lower, 1.0, 0.0), lf_cols, preferred_element_type=F32,
                     precision=lax.Precision.HIGHEST)
    ones = jnp.ones((BF16_SUBLANES, rows), BF16)

    for h in range(N_HEADS):
        hs = slice(h * HEAD_DIM, (h + 1) * HEAD_DIM)
        b_row = b_rows[h:h + 1]
        src_row = ig_rows[N_HEADS + h:N_HEADS + h + 1] - b_row
        src_col = ig_cols[:, h:h + 1] - b_cols[:, h:h + 1]
        dmat = jnp.where(upper, b_row + src_col, NEG)
        a = b_row[:, rows - 1:rows]
        g_end = a + src_row
        m_loc = jnp.max(g_end, axis=-1, keepdims=True)
        w_end = jnp.exp(g_end - m_loc)

        state = c_st[h]
        m_in = m_st[h]
        inter = b_row + m_in
        m_t = jnp.maximum(inter, jnp.max(dmat, axis=0, keepdims=True))
        e_inter = jnp.exp(inter - m_t)
        qt = qt_all[hs].astype(BF16)
        kb = k_all[:, hs].astype(BF16)
        values = jnp.concatenate([vt_all[hs], ones], axis=0)
        p = (_dot(kb, qt) * jnp.exp(dmat - m_t)).astype(BF16)
        from_state = _dot(state.astype(BF16), qt)
        from_chunk = _dot(values, p)
        num = e_inter * from_state[0:HEAD_DIM] + from_chunk[0:HEAD_DIM]
        den = e_inter * from_state[HEAD_DIM:HEAD_DIM + 1] + from_chunk[HEAD_DIM:HEAD_DIM + 1]
        hh = num * (1.0 / jnp.maximum(jnp.abs(den), jnp.exp(-m_t)))
        hh = hh * lax.rsqrt(jnp.mean(hh * hh, axis=0, keepdims=True) + EPS) * hg_ref[hs, :]
        out_ref[0, hs, :] = (hh * ogt[hs]).astype(out_ref.dtype)

        m_new = jnp.maximum(a + m_in, m_loc)
        decay = jnp.exp(a + m_in - m_new)
        fresh = jnp.exp(m_loc - m_new)
        local = _dot((values.astype(F32) * w_end).astype(BF16), kb)
        c_st[h] = decay * state + fresh * local
        m_st[h] = m_new


def _mlstm(u, v, smalls, o_pre, conv_w, conv_b, wq, wk, gate_b, head_g, rows=256):
    bsz, seq, _ = u.shape
    rows = min(rows, seq)
    row = pl.BlockSpec((1, rows, GROUP_WIDTH), lambda b, c: (b, c, 0))
    const2 = lambda a: pl.BlockSpec(a.shape, lambda b, c: (0,) * a.ndim)
    conv_b = conv_b.reshape(1, GROUP_WIDTH)
    head_g = head_g.reshape(GROUP_WIDTH, 1)
    eye = jnp.eye(N_HEADS, dtype=wq.dtype)
    wqt = jnp.einsum('hde,hg->hegd', wq, eye).reshape(GROUP_WIDTH, GROUP_WIDTH).astype(BF16)
    wkb = jnp.einsum('hde,hg->hdge', wk, eye).reshape(GROUP_WIDTH, GROUP_WIDTH).astype(BF16)
    zeros4 = jnp.zeros((N_HEADS,), gate_b.dtype)
    gcol = jnp.stack([jnp.concatenate([zeros4, gate_b[0]]), jnp.concatenate([gate_b[1], zeros4])],
                     axis=1)
    return pl.pallas_call(
        _mlstm_kernel, grid=(bsz, seq // rows),
        in_specs=[row,
                  pl.BlockSpec((1, 8, GROUP_WIDTH),
                               lambda b, c: (b, jnp.maximum(c * (rows // 8) - 1, 0), 0)),
                  row,
                  pl.BlockSpec((1, rows, LANES), lambda b, c: (b, c, 0)),
                  row, const2(conv_w), const2(conv_b), const2(wqt), const2(wkb), const2(gate_b),
                  const2(gcol), const2(head_g)],
        out_specs=pl.BlockSpec((1, GROUP_WIDTH, rows), lambda b, c: (b, 0, c)),
        out_shape=jax.ShapeDtypeStruct((bsz, GROUP_WIDTH, seq), BF16),
        scratch_shapes=[pltpu.VMEM((N_HEADS, ACC_ROWS, HEAD_DIM), F32),
                        pltpu.VMEM((N_HEADS, 1, 1), F32)],
        compiler_params=_cparams(("parallel", "arbitrary")), name="mlstm",
    )(u, u, v, smalls, o_pre, conv_w, conv_b, wqt, wkb, gate_b, gcol, head_g)


def _out_ffn_kernel(x_ref, oa_ref, ob_ref, oc_ref, od_ref, wo_ref, g_ref, wg_ref, wu_ref, wd_ref,
                    gf_ref, y_ref, *, final, ff_chunk):
    mixed_t = jnp.concatenate([oa_ref[0], ob_ref[0], oc_ref[0], od_ref[0]], axis=0)
    x = x_ref[...] + lax.dot_general(mixed_t, wo_ref[...], (((0,), (0,)), ((), ())),
                                     preferred_element_type=F32)
    h = x * lax.rsqrt(jnp.mean(x * x, axis=-1, keepdims=True) + EPS)
    h = (h * g_ref[...]).astype(BF16)
    ffn = None
    for c0 in range(0, D_FF, ff_chunk):
        gate = _dot(h, wg_ref[:, c0:c0 + ff_chunk])
        up = _dot(h, wu_ref[:, c0:c0 + ff_chunk])
        act = (gate * jax.nn.sigmoid(gate) * up).astype(BF16)
        part = _dot(act, wd_ref[c0:c0 + ff_chunk, :])
        ffn = part if ffn is None else ffn + part
    y = x + ffn
    if final:
        y = y * lax.rsqrt(jnp.mean(y * y, axis=-1, keepdims=True) + EPS) * gf_ref[...]
    y_ref[...] = y


def _out_ffn(x2, o_a, o_b, o_c, o_d, w_out, gain, w_gate, w_up, w_down, gain_final, final, tm=512):
    n = x2.shape[0]
    seq = o_a.shape[2]
    tm = min(tm, seq)
    nblk_s = seq // tm
    row = lambda w: pl.BlockSpec((tm, w), lambda i: (i, 0))
    mixer = pl.BlockSpec((1, GROUP_WIDTH, tm), lambda i: (i // nblk_s, 0, i % nblk_s))
    const = lambda a: pl.BlockSpec(a.shape, lambda i: (0, 0), pipeline_mode=pl.Buffered(1))
    gain = gain.reshape(1, D_MODEL)
    gain_final = gain_final.reshape(1, D_MODEL)
    kern = functools.partial(_out_ffn_kernel, final=final, ff_chunk=256)
    return pl.pallas_call(
        kern, grid=(n // tm,),
        in_specs=[row(D_MODEL)] + [mixer] * 4
                 + [const(w_out), const(gain), const(w_gate), const(w_up), const(w_down),
                    const(gain_final)],
        out_specs=row(D_MODEL),
        out_shape=jax.ShapeDtypeStruct((n, D_MODEL), F32),
        compiler_params=_cparams(("parallel",)), name="out_ffn",
    )(x2, o_a, o_b, o_c, o_d, w_out, gain, w_gate, w_up, w_down, gain_final)


def kernel(x, norm_mix, w_in, nsa_cmp_pos, nsa_cmp_w, diff_lambda, diff_norm, mlstm_conv_w,
           mlstm_conv_b, mlstm_wq, mlstm_wk, mlstm_gate_b, mlstm_norm, w_out, norm_ffn, w_gate,
           w_up, w_down, norm_final):
    bsz, seq, _ = x.shape
    depth = w_in.shape[0]
    tables = _lane_tables(seq)
    x2 = x.reshape(bsz * seq, D_MODEL)
    r3 = lambda t: t.reshape(bsz, seq, t.shape[-1])

    for layer in range(depth):
        (a_q, a_qr, a_kvc, a_ks, a_vs, a_kw, a_vw, smalls, b_q, b_k, b_v, c_u, c_v, c_o,
         d_q, d_k, d_v) = _in_proj(x2, norm_mix[layer], w_in[layer], tables, seq,
                                   tm=min(512, seq))
        smalls3 = r3(smalls)

        kc, kvct = _compress(r3(a_kvc), nsa_cmp_w[layer], nsa_cmp_pos[layer])
        o_a = _nsa(a_q, a_qr, kc, kvct, r3(a_ks), a_vs, r3(a_kw), a_vw, smalls3)

        lam_init = 0.8 - 0.6 * math.exp(-0.3 * layer)
        o_b = _diff(b_q, r3(b_k), b_v, diff_lambda[layer], diff_norm[layer], lam_init)

        o_c = _mlstm(r3(c_u), r3(c_v), smalls3, r3(c_o),
                     mlstm_conv_w[layer], mlstm_conv_b[layer], mlstm_wq[layer], mlstm_wk[layer],
                     mlstm_gate_b[layer], mlstm_norm[layer])

        o_d = _dilated(d_q, r3(d_k), d_v)

        x2 = _out_ffn(x2, o_a, o_b, o_c, o_d, w_out[layer].astype(BF16), norm_ffn[layer],
                      w_gate[layer].astype(BF16), w_up[layer].astype(BF16),
                      w_down[layer].astype(BF16), norm_final, final=(layer == depth - 1))
    return x2.reshape(bsz, seq, D_MODEL)
```

```python
import functools
import math

import numpy as np
import jax
import jax.numpy as jnp
from jax import lax
from jax.experimental import pallas as pl
from jax.experimental.pallas import tpu as pltpu

F32 = jnp.float32
BF16 = jnp.bfloat16

D_MODEL = 1024
HEAD_DIM = 64
N_HEADS = 4
GROUP_WIDTH = N_HEADS * HEAD_DIM
ROPE_THETA = 10000.0
EPS = 1e-6
NEG = -1e30
LOG2E = math.log2(math.e)

CMP_LEN = 32
CMP_STRIDE = 16
SLC_LEN = 64
TOP_N = 16
NSA_WINDOW = 512
FORCED_LOCAL = 2
DIFF_HALF = HEAD_DIM // 2
MLSTM_CHUNK = 64
MLSTM_CONV = 4
DILATED_PATTERNS = ((128, 1), (512, 4), (2048, 16))
D_FF = ((8 * D_MODEL + 3 * 256 - 1) // (3 * 256)) * 256

LANES = 128
BF16_SUBLANES = 16
VMEM_LIMIT = 56 * 1024 * 1024
ACC_ROWS = HEAD_DIM + BF16_SUBLANES

IN_SPLITS = (
    GROUP_WIDTH, HEAD_DIM, HEAD_DIM, HEAD_DIM, HEAD_DIM, HEAD_DIM, HEAD_DIM, 3 * N_HEADS,
    GROUP_WIDTH, GROUP_WIDTH, GROUP_WIDTH,
    GROUP_WIDTH, GROUP_WIDTH, N_HEADS, N_HEADS, GROUP_WIDTH,
    GROUP_WIDTH, GROUP_WIDTH, GROUP_WIDTH,
)
(A_Q, A_KC, A_VC, A_KS, A_VS, A_KW, A_VW, A_G, B_Q, B_K, B_V,
 C_U, C_V, C_I, C_F, C_O, D_Q, D_K, D_V) = range(19)
SM_AG, SM_CI, SM_CF = 0, 12, 16


def _cparams(sem):
    return pltpu.CompilerParams(dimension_semantics=sem, vmem_limit_bytes=VMEM_LIMIT)


def _iota(shape, dim):
    return lax.broadcasted_iota(jnp.int32, shape, dim)


def _dot(a, b):
    return jnp.dot(a, b, preferred_element_type=F32)


def _dot_nt(a, b):
    return lax.dot_general(a, b, (((1,), (1,)), ((), ())), preferred_element_type=F32)


PACKED_CHUNKS = 26
D_IN_PADDED = 3072


def _column_plan():
    offs = np.concatenate([[0], np.cumsum(IN_SPLITS)])
    order = [A_Q, A_KC, A_VC, A_KS, None, A_VS, None, A_KW, None, A_VW, None,
             A_G, C_I, C_F, ('pad', LANES - 20),
             B_Q, B_K, B_V, C_U, C_V, C_O, D_Q, D_K, D_V]
    src = []
    for item in order:
        if item is None:
            src += [-1] * HEAD_DIM
        elif isinstance(item, tuple):
            src += [-1] * item[1]
        else:
            src += list(range(int(offs[item]), int(offs[item + 1])))
    src = np.asarray(src)
    assert src.size == PACKED_CHUNKS * LANES
    terms, mats = [], []
    for j in range(PACKED_CHUNKS):
        cols = src[j * LANES:(j + 1) * LANES]
        todo = cols >= 0
        while todo.any():
            start = (cols[todo].min() // LANES) * LANES
            take = todo & (cols < start + 2 * LANES)
            sel = np.zeros((2 * LANES, LANES), np.float32)
            sel[cols[take] - start, np.nonzero(take)[0]] = 1.0
            terms.append((j, int(start)))
            mats.append(sel)
            todo &= ~take
    return tuple(terms), np.stack(mats)


def _rope_tables(seq, dim):
    inv = 1.0 / (ROPE_THETA ** (jnp.arange(0, dim, 2, dtype=F32) / dim))
    ang = jnp.arange(seq, dtype=F32)[:, None] * inv[None, :]
    return jnp.cos(ang), jnp.sin(ang)


def _lane_tables(seq):
    c64, s64 = _rope_tables(seq, HEAD_DIM)
    c32, s32 = _rope_tables(seq, DIFF_HALF)
    t64c = jnp.concatenate([c64, c64] * (LANES // HEAD_DIM), axis=1)
    t64s = jnp.concatenate([-s64, s64] * (LANES // HEAD_DIM), axis=1)
    t32c = jnp.concatenate([c32, c32] * (LANES // DIFF_HALF), axis=1)
    t32s = jnp.concatenate([-s32, s32] * (LANES // DIFF_HALF), axis=1)
    pos = np.arange(seq)[:, None]
    lane = np.arange(LANES)[None, :]
    onehot = ((lane >= HEAD_DIM) & ((pos // SLC_LEN) % HEAD_DIM == lane - HEAD_DIM)
              ).astype(np.float32)
    return t64c, t64s, t32c, t32s, jnp.asarray(onehot)


def _swap_halves(x, group):
    width = x.shape[-1]
    half = group // 2
    lane = _iota(x.shape, 1) & (group - 1)
    up = pltpu.roll(x, width - half, axis=1)
    down = pltpu.roll(x, half, axis=1)
    return jnp.where(lane < half, up, down)


def _rope(x, cos_t, sin_t, group):
    reps = x.shape[-1] // LANES
    if reps > 1:
        cos_t = jnp.concatenate([cos_t] * reps, axis=1)
        sin_t = jnp.concatenate([sin_t] * reps, axis=1)
    return x * cos_t + _swap_halves(x, group) * sin_t


def _in_proj_kernel(x_ref, g_ref, wraw_ref, sel_ref, c64_ref, s64_ref, c32_ref, s32_ref, oh_ref,
                    aq_ref, aqr_ref, akvc_ref, aks_ref, avs_ref, akw_ref, avw_ref, sm_ref,
                    bq_ref, bk_ref, bv_ref, cu_ref, cv_ref, co_ref, dq_ref, dk_ref, dv_ref,
                    w_ref, *, terms):
    @pl.when(pl.program_id(0) == 0)
    def _():
        for j in range(PACKED_CHUNKS):
            chunk = jnp.zeros((D_MODEL, LANES), F32)
            for t, (dst, start) in enumerate(terms):
                if dst == j:
                    chunk = chunk + _dot(wraw_ref[:, start:start + 2 * LANES], sel_ref[t])
            w_ref[:, j * LANES:(j + 1) * LANES] = chunk.astype(BF16)

    x = x_ref[...]
    h = x * lax.rsqrt(jnp.mean(x * x, axis=-1, keepdims=True) + EPS)
    h = (h * g_ref[...]).astype(BF16)
    c64, s64 = c64_ref[...], s64_ref[...]
    c32, s32 = c32_ref[...], s32_ref[...]

    def mm(c0, c1):
        return _dot(h, w_ref[:, c0 * LANES:c1 * LANES])

    def channel_major(ref, z, channels=None):
        zt = z.T
        ref[0] = (zt if channels is None else zt[0:channels]).astype(ref.dtype)

    zq = mm(0, 2) * (HEAD_DIM ** -0.5 * LOG2E)
    channel_major(aq_ref, zq)
    channel_major(aqr_ref, _rope(zq, c64, s64, HEAD_DIM))
    akvc_ref[...] = mm(2, 3).astype(BF16)
    aks_ref[...] = (_rope(mm(3, 4), c64, s64, HEAD_DIM) + oh_ref[...]).astype(BF16)
    channel_major(avs_ref, mm(4, 5), HEAD_DIM)
    akw_ref[...] = _rope(mm(5, 6), c64, s64, HEAD_DIM).astype(BF16)
    channel_major(avw_ref, mm(6, 7), HEAD_DIM)
    sm_ref[...] = mm(7, 8)
    channel_major(bq_ref, _rope(mm(8, 10), c32, s32, DIFF_HALF) * (DIFF_HALF ** -0.5 * LOG2E))
    bk_ref[...] = _rope(mm(10, 12), c32, s32, DIFF_HALF).astype(BF16)
    channel_major(bv_ref, mm(12, 14))
    cu_ref[...] = mm(14, 16)
    cv_ref[...] = mm(16, 18).astype(BF16)
    co_ref[...] = mm(18, 20)
    channel_major(dq_ref, _rope(mm(20, 22), c64, s64, HEAD_DIM) * (HEAD_DIM ** -0.5 * LOG2E))
    dk_ref[...] = _rope(mm(22, 24), c64, s64, HEAD_DIM).astype(BF16)
    channel_major(dv_ref, mm(24, 26))


_IN_PROJ_OUTS = (
    (256, BF16, True), (256, BF16, True), (128, BF16, False), (128, BF16, False),
    (HEAD_DIM, BF16, True), (128, BF16, False), (HEAD_DIM, BF16, True), (128, F32, False),
    (256, BF16, True), (256, BF16, False), (256, BF16, True), (256, F32, False),
    (256, BF16, False), (256, F32, False), (256, BF16, True), (256, BF16, False),
    (256, BF16, True))


def _in_proj(x2, gain, w, tables, seq, tm):
    n = x2.shape[0]
    nblk_s = seq // tm
    terms, select = _column_plan()
    select = jnp.asarray(select, BF16)
    w_raw = jnp.pad(w.astype(BF16), ((0, 0), (0, D_IN_PADDED - w.shape[1])))
    row = lambda i: (i, 0)
    tab = lambda i: (i % nblk_s, 0)
    const = lambda i: (0, 0)
    in_specs = [pl.BlockSpec((tm, D_MODEL), row),
                pl.BlockSpec((1, D_MODEL), const),
                pl.BlockSpec(w_raw.shape, const),
                pl.BlockSpec(select.shape, lambda i: (0, 0, 0))]
    in_specs += [pl.BlockSpec((tm, LANES), tab)] * 5
    out_specs = [pl.BlockSpec((1, w, tm), lambda i: (i // nblk_s, 0, i % nblk_s)) if cmaj
                 else pl.BlockSpec((tm, w), row) for w, _, cmaj in _IN_PROJ_OUTS]
    out_shape = [jax.ShapeDtypeStruct((n // seq, w, seq) if cmaj else (n, w), dt)
                 for w, dt, cmaj in _IN_PROJ_OUTS]
    return pl.pallas_call(
        functools.partial(_in_proj_kernel, terms=terms), grid=(n // tm,), in_specs=in_specs,
        out_specs=out_specs, out_shape=out_shape,
        scratch_shapes=[pltpu.VMEM((D_MODEL, PACKED_CHUNKS * LANES), BF16)],
        compiler_params=_cparams(("arbitrary",)), name="in_proj",
    )(x2, gain.reshape(1, D_MODEL), w_raw, select, *tables)


def _compress_kernel(r_ref, w_ref, pos_ref, kc_ref, kvct_ref):
    wk = w_ref[0].reshape(CMP_LEN, HEAD_DIM, HEAD_DIM)
    wv = w_ref[1].reshape(CMP_LEN, HEAD_DIM, HEAD_DIM)
    zeros = jnp.zeros_like(wk)
    full = jnp.concatenate([jnp.concatenate([wk, zeros], axis=2),
                            jnp.concatenate([zeros, wv], axis=2)], axis=1)
    w_first = full[0:CMP_STRIDE].reshape(CMP_STRIDE * LANES, LANES).astype(BF16)
    w_second = full[CMP_STRIDE:].reshape(CMP_STRIDE * LANES, LANES).astype(BF16)

    r = r_ref[0]
    first = _dot(r, w_first)
    second = _dot(r, w_second)
    nrow = first.shape[0]
    pos = pos_ref[...].astype(BF16)
    half = pos.shape[1] // 2
    const = _dot(pos[:, :half], w_first) + _dot(pos[:, half:], w_second)
    out = first + pltpu.roll(second, nrow - 1, axis=0) + const[0:1, :]
    lane = _iota(out.shape, 1)
    kc_ref[0] = jnp.where(lane < HEAD_DIM, out, 0.0).astype(BF16)
    kvct_ref[0] = out.T.astype(BF16)


def _compress(a_kvc, cmp_w, cmp_pos):
    bsz, seq, _ = a_kvc.shape
    nrow = seq // CMP_STRIDE
    r = a_kvc.reshape(bsz, nrow, CMP_STRIDE * LANES)
    pos = jnp.concatenate([cmp_pos[0], cmp_pos[1]], axis=-1)
    pos = jnp.broadcast_to(pos.reshape(1, CMP_LEN * LANES), (8, CMP_LEN * LANES))
    return pl.pallas_call(
        _compress_kernel, grid=(bsz,),
        in_specs=[pl.BlockSpec((1, nrow, CMP_STRIDE * LANES), lambda b: (b, 0, 0)),
                  pl.BlockSpec(cmp_w.shape, lambda b: (0, 0, 0)),
                  pl.BlockSpec(pos.shape, lambda b: (0, 0))],
        out_specs=[pl.BlockSpec((1, nrow, LANES), lambda b: (b, 0, 0)),
                   pl.BlockSpec((1, LANES, nrow), lambda b: (b, 0, 0))],
        out_shape=[jax.ShapeDtypeStruct((bsz, nrow, LANES), BF16),
                   jax.ShapeDtypeStruct((bsz, LANES, nrow), BF16)],
        compiler_params=_cparams(("parallel",)), name="nsa_compress",
    )(r, cmp_w, pos)


def _flash_init(m_ref, acc_ref):
    m_ref[...] = jnp.full(m_ref.shape, NEG, F32)
    acc_ref[...] = jnp.zeros(acc_ref.shape, F32)


def _flash_step(s, values, m_ref, acc_ref):
    m_old = m_ref[...]
    m_new = jnp.maximum(m_old, jnp.max(s, axis=0, keepdims=True))
    alpha = jnp.exp2(m_old - m_new)
    p = jnp.exp2(s - m_new).astype(BF16)
    width = acc_ref.shape[1]
    for g, vals in enumerate(values):
        rows = slice(g * ACC_ROWS, (g + 1) * ACC_ROWS)
        lanes = slice(g * width, (g + 1) * width)
        acc_ref[rows, :] = alpha[:, lanes] * acc_ref[rows, :] + _dot(vals, p[:, lanes])
    m_ref[...] = m_new


def _pipelined_tiles(lo, hi, scores, consume, sa_ref, sb_ref):
    n = hi - lo

    def put(ref, tiles):
        for g, tile in enumerate(tiles):
            ref[g] = tile

    def get(ref):
        return [ref[g] for g in range(ref.shape[0])]

    put(sa_ref, scores(lo))

    def body(i, carry):
        j = lo + 2 * i
        put(sb_ref, scores(j + 1))
        consume(get(sa_ref), j, False)
        put(sa_ref, scores(j + 2))
        consume(get(sb_ref), j + 1, False)
        return carry

    lax.fori_loop(0, n // 2, body, 0)

    @pl.when(n % 2 == 0)
    def _():
        consume(get(sa_ref), hi, True)

    @pl.when(n % 2 == 1)
    def _():
        put(sb_ref, scores(hi))
        consume(get(sa_ref), hi - 1, False)
        consume(get(sb_ref), hi, True)


def _with_ones(vt):
    return jnp.concatenate([vt, jnp.ones((BF16_SUBLANES, vt.shape[1]), BF16)], axis=0)


def _cmp_to_slc_t(seq):
    n_cmp = (seq - CMP_LEN) // CMP_STRIDE + 1
    n_slc = seq // SLC_LEN
    ratio_s, ratio_c = SLC_LEN // CMP_STRIDE, CMP_LEN // CMP_STRIDE
    jj = np.arange(n_slc)[:, None, None]
    src = ratio_s * jj - np.arange(ratio_s)[None, :, None] - np.arange(ratio_c)[None, None, :]
    ok = (src >= 0) & (src < n_cmp)
    m = np.zeros((seq // CMP_STRIDE, n_slc), np.float32)
    np.add.at(m, (np.where(ok, src, 0), np.broadcast_to(jj, src.shape)), ok.astype(np.float32))
    return jnp.asarray(m.T, BF16)


def _nsa_kernel(qt_ref, qrt_ref, kc_ref, kvct_ref, ks_ref, vst_ref, kw_ref, vwt_ref, sm_ref,
                c2st_ref, o_ref, m_ref, acc_ref, sa_ref, sb_ref, *, tq, top_n):
    qi = pl.program_id(1)
    s0 = qi * tq
    rows = N_HEADS * tq
    n_slc = c2st_ref.shape[0]
    lane_t = s0 + (_iota((1, rows), 1) & (tq - 1))

    def heads_on_lanes(ref):
        x = ref[0]
        return jnp.concatenate([x[h * HEAD_DIM:(h + 1) * HEAD_DIM, :] for h in range(N_HEADS)],
                               axis=1)

    q4 = jnp.concatenate([heads_on_lanes(qt_ref), jnp.zeros((HEAD_DIM, rows), BF16)], axis=0)
    sc = _dot(kc_ref[0], q4)
    cmask = (_iota((sc.shape[0], 1), 0) * CMP_STRIDE + (CMP_LEN - 1)) <= lane_t
    sc = jnp.where(cmask, sc, NEG)
    e = jnp.where(cmask, jnp.exp2(sc - jnp.max(sc, axis=0, keepdims=True)), 0.0)
    z = jnp.sum(e, axis=0, keepdims=True)
    p_cmp = e * (1.0 / jnp.where(z > 0, z, 1.0))
    o_cmp = _dot(kvct_ref[0][HEAD_DIM:2 * HEAD_DIM, :], p_cmp.astype(BF16))
    p_heads = p_cmp[:, 0:tq]
    for h in range(1, N_HEADS):
        p_heads = p_heads + p_cmp[:, h * tq:(h + 1) * tq]
    p_hi = p_heads.astype(BF16)
    p_lo = (p_heads - p_hi.astype(F32)).astype(BF16)
    c2st = c2st_ref[...]
    imp = _dot(c2st, p_hi) + _dot(c2st, p_lo)

    blk = _iota((n_slc, 1), 0)
    cur = (s0 + _iota((1, tq), 1)) >> 6
    forced = (blk == 0) | ((blk <= cur) & (blk > cur - FORCED_LOCAL))
    score = jnp.where(blk > cur, -1.0e6, jnp.where(forced, 1.0e6, imp))
    rank = jnp.zeros((n_slc, tq), F32)
    for i in range(n_slc):
        s_i = score[i:i + 1, :]
        rank = rank + jnp.where(blk > i, jnp.where(s_i >= score, 1.0, 0.0),
                                jnp.where(s_i > score, 1.0, 0.0))
    sel = (rank < top_n) & (blk <= cur)
    bias = jnp.where(sel, 0.0, NEG)
    if n_slc < HEAD_DIM:
        bias = jnp.concatenate([bias, jnp.zeros((HEAD_DIM - n_slc, tq), F32)], axis=0)
    bias4 = jnp.concatenate([bias] * N_HEADS, axis=1).astype(BF16)
    qsel = jnp.concatenate([heads_on_lanes(qrt_ref), bias4], axis=0)

    def attend(k_ref, vt_ref, lo, window):
        _flash_init(m_ref, acc_ref)

        def scores(j):
            k0 = pl.multiple_of(j * tq, tq)
            return [_dot(k_ref[0, pl.ds(k0, tq), :], qsel)]

        def consume(tiles, j, diagonal):
            s, = tiles
            k0 = pl.multiple_of(j * tq, tq)
            kpos = k0 + _iota((tq, 1), 0)
            if diagonal:
                s = jnp.where(kpos <= lane_t, s, NEG)
            if window:
                s = jnp.where(kpos > lane_t - NSA_WINDOW, s, NEG)
            _flash_step(s, [_with_ones(vt_ref[0, :, pl.ds(k0, tq)])], m_ref, acc_ref)

        _pipelined_tiles(lo, qi, scores, consume, sa_ref, sb_ref)
        acc = acc_ref[...]
        return acc[0:HEAD_DIM] * (1.0 / acc[HEAD_DIM:HEAD_DIM + 1])

    o_slc = attend(ks_ref, vst_ref, 0, False)
    o_win = attend(kw_ref, vwt_ref, jnp.maximum(qi - NSA_WINDOW // tq, 0), True)

    g = jax.nn.sigmoid(sm_ref[0].T[0:BF16_SUBLANES, :])

    def gate(branch):
        return jnp.concatenate(
            [g[branch * N_HEADS + h:branch * N_HEADS + h + 1, :] for h in range(N_HEADS)], axis=1)

    o = gate(0) * o_cmp + gate(1) * o_slc + gate(2) * o_win
    for h in range(N_HEADS):
        o_ref[0, h * HEAD_DIM:(h + 1) * HEAD_DIM, :] = o[:, h * tq:(h + 1) * tq].astype(o_ref.dtype)


def _nsa(qt, qrt, kc, kvct, ks, vst, kw, vwt, smalls, tq=256):
    bsz, _, seq = qt.shape
    tq = min(tq, seq)
    n_slc = seq // SLC_LEN
    c2st = _cmp_to_slc_t(seq)
    rows = N_HEADS * tq
    qspec = pl.BlockSpec((1, GROUP_WIDTH, tq), lambda b, i: (b, 0, i))
    full = lambda a: pl.BlockSpec((1,) + a.shape[1:], lambda b, i: (b, 0, 0))
    vspec = pl.BlockSpec((1, HEAD_DIM, seq), lambda b, i: (b, 0, 0))
    kern = functools.partial(_nsa_kernel, tq=tq, top_n=min(TOP_N, n_slc))
    return pl.pallas_call(
        kern, grid=(bsz, seq // tq),
        in_specs=[qspec, qspec, full(kc), full(kvct), full(ks), vspec, full(kw), vspec,
                  pl.BlockSpec((1, tq, LANES), lambda b, i: (b, i, 0)),
                  pl.BlockSpec(c2st.shape, lambda b, i: (0, 0))],
        out_specs=qspec,
        out_shape=jax.ShapeDtypeStruct(qt.shape, BF16),
        scratch_shapes=[pltpu.VMEM((1, rows), F32), pltpu.VMEM((ACC_ROWS, rows), F32),
                        pltpu.VMEM((1, tq, rows), F32), pltpu.VMEM((1, tq, rows), F32)],
        compiler_params=_cparams(("parallel", "arbitrary")), name="nsa_attention",
    )(qt, qrt, kc, kvct, ks, vst, kw, vwt, smalls, c2st)


def _diff_kernel(qt_ref, k_ref, vt_ref, lam_ref, g_ref, o_ref, m_ref, acc_ref, sa_ref, sb_ref,
                 *, tq, lam_init):
    qi = pl.program_id(1)
    s0 = qi * tq
    rows = 4 * tq
    pairs = N_HEADS // 2
    pw = 2 * HEAD_DIM
    row = _iota((pw, 1), 0)
    lane_t = s0 + (_iota((1, rows), 1) & (tq - 1))

    def query_matrix(p):
        qt = qt_ref[0, p * pw:(p + 1) * pw, :]
        zero = jnp.zeros_like(qt)
        return jnp.concatenate(
            [jnp.where((row >= DIFF_HALF * c) & (row < DIFF_HALF * (c + 1)), qt, zero)
             for c in range(4)], axis=1)

    qmats = [query_matrix(p) for p in range(pairs)]

    def scores(j):
        k0 = pl.multiple_of(j * tq, tq)
        return [_dot(k_ref[0, pl.ds(k0, tq), p * pw:(p + 1) * pw], qmats[p]) for p in range(pairs)]

    def consume(tiles, j, diagonal):
        k0 = pl.multiple_of(j * tq, tq)
        for p, s in enumerate(tiles):
            if diagonal:
                s = jnp.where(k0 + _iota((tq, 1), 0) <= lane_t, s, NEG)
            vt = vt_ref[0, p * pw:(p + 1) * pw, pl.ds(k0, tq)]
            _flash_step(s, [_with_ones(vt[0:HEAD_DIM]), _with_ones(vt[HEAD_DIM:])],
                        m_ref.at[p], acc_ref.at[p])

    _flash_init(m_ref, acc_ref)
    _pipelined_tiles(0, qi, scores, consume, sa_ref, sb_ref)

    lv = lam_ref[...]
    lam = (jnp.exp(jnp.sum(lv[0:1] * lv[1:2], axis=-1, keepdims=True))
           - jnp.exp(jnp.sum(lv[2:3] * lv[3:4], axis=-1, keepdims=True)) + lam_init)
    for h in range(N_HEADS):
        a = acc_ref[h // 2, ACC_ROWS * (h % 2):ACC_ROWS * (h % 2 + 1), :]
        o_all = a[0:HEAD_DIM] * (1.0 / a[HEAD_DIM:HEAD_DIM + 1])
        o = o_all[:, :tq] - lam * o_all[:, tq:]
        y = o * lax.rsqrt(jnp.mean(o * o, axis=0, keepdims=True) + EPS)
        o_ref[0, HEAD_DIM * h:HEAD_DIM * (h + 1), :] = (
            (y * g_ref[...]) * (1.0 - lam_init)).astype(o_ref.dtype)


def _diff(qt, k, vt, lam_vecs, sub_g, lam_init, tq=256):
    bsz, _, seq = qt.shape
    tq = min(tq, seq)
    pairs = N_HEADS // 2
    qspec = pl.BlockSpec((1, GROUP_WIDTH, tq), lambda b, i: (b, 0, i))
    kern = functools.partial(_diff_kernel, tq=tq, lam_init=lam_init)
    return pl.pallas_call(
        kern, grid=(bsz, seq // tq),
        in_specs=[qspec,
                  pl.BlockSpec((1, seq, GROUP_WIDTH), lambda b, i: (b, 0, 0)),
                  pl.BlockSpec((1, GROUP_WIDTH, seq), lambda b, i: (b, 0, 0)),
                  pl.BlockSpec(lam_vecs.shape, lambda b, i: (0, 0)),
                  pl.BlockSpec((HEAD_DIM, 1), lambda b, i: (0, 0))],
        out_specs=qspec,
        out_shape=jax.ShapeDtypeStruct(qt.shape, BF16),
        scratch_shapes=[pltpu.VMEM((pairs, 1, 4 * tq), F32),
                        pltpu.VMEM((pairs, 2 * ACC_ROWS, 2 * tq), F32),
                        pltpu.VMEM((pairs, tq, 4 * tq), F32), pltpu.VMEM((pairs, tq, 4 * tq), F32)],
        compiler_params=_cparams(("parallel", "arbitrary")), name="diff_attention",
    )(qt, k, vt, lam_vecs, sub_g.reshape(HEAD_DIM, 1))


def _dilated_bias(tq):
    max_back = max(w for w, _ in DILATED_PATTERNS) // tq
    classes = [0, 1, 2, 3, max_back]
    q = np.arange(tq)[None, :]
    k = np.arange(tq)[:, None]
    out = []
    for d in classes:
        delta = d * tq + q - k
        cnt = np.zeros((tq, tq), np.float64)
        for w, dil in DILATED_PATTERNS:
            cnt += (delta >= 0) & (delta <= w) & (delta % dil == 0)
        tab = np.where(cnt > 0, np.log2(np.maximum(cnt, 1.0)), NEG)
        out.append(np.concatenate([tab, tab], axis=1))
    return jnp.asarray(np.stack(out).astype(np.float32)), max_back


def _dilated_kernel(qt_ref, k_ref, vt_ref, bias_ref, o_ref, m_ref, acc_ref, sa_ref, sb_ref,
                    *, tq, max_back):
    qi = pl.program_id(1)
    pairs = N_HEADS // 2
    pw = 2 * HEAD_DIM
    row = _iota((pw, 1), 0)

    def query_matrix(p):
        qt = qt_ref[0, p * pw:(p + 1) * pw, :]
        zero = jnp.zeros_like(qt)
        return jnp.concatenate([jnp.where(row < HEAD_DIM, qt, zero),
                                jnp.where(row >= HEAD_DIM, qt, zero)], axis=1)

    qmats = [query_matrix(p) for p in range(pairs)]
    _flash_init(m_ref, acc_ref)

    def scores(j):
        k0 = pl.multiple_of(j * tq, tq)
        return [_dot(k_ref[0, pl.ds(k0, tq), p * pw:(p + 1) * pw], qmats[p]) for p in range(pairs)]

    def consume(tiles, j, diagonal):
        del diagonal
        k0 = pl.multiple_of(j * tq, tq)
        d = qi - j
        bias = bias_ref[jnp.where(d < 3, d, jnp.where(d == max_back, 4, 3))]
        for p, s in enumerate(tiles):
            vt = vt_ref[0, p * pw:(p + 1) * pw, pl.ds(k0, tq)]
            _flash_step(s + bias, [_with_ones(vt[0:HEAD_DIM]), _with_ones(vt[HEAD_DIM:])],
                        m_ref.at[p], acc_ref.at[p])

    _pipelined_tiles(jnp.maximum(qi - max_back, 0), qi, scores, consume, sa_ref, sb_ref)
    for h in range(N_HEADS):
        a = acc_ref[h // 2, ACC_ROWS * (h % 2):ACC_ROWS * (h % 2 + 1), :]
        o_ref[0, HEAD_DIM * h:HEAD_DIM * (h + 1), :] = (
            a[0:HEAD_DIM] * (1.0 / a[HEAD_DIM:HEAD_DIM + 1])).astype(o_ref.dtype)


def _dilated(qt, k, vt, tq=256):
    bsz, _, seq = qt.shape
    tq = min(tq, seq)
    pairs = N_HEADS // 2
    bias, max_back = _dilated_bias(tq)
    qspec = pl.BlockSpec((1, GROUP_WIDTH, tq), lambda b, i: (b, 0, i))
    kern = functools.partial(_dilated_kernel, tq=tq, max_back=max_back)
    return pl.pallas_call(
        kern, grid=(bsz, seq // tq),
        in_specs=[qspec,
                  pl.BlockSpec((1, seq, GROUP_WIDTH), lambda b, i: (b, 0, 0)),
                  pl.BlockSpec((1, GROUP_WIDTH, seq), lambda b, i: (b, 0, 0)),
                  pl.BlockSpec(bias.shape, lambda b, i: (0, 0, 0))],
        out_specs=qspec,
        out_shape=jax.ShapeDtypeStruct(qt.shape, BF16),
        scratch_shapes=[pltpu.VMEM((pairs, 1, 2 * tq), F32),
                        pltpu.VMEM((pairs, 2 * ACC_ROWS, tq), F32),
                        pltpu.VMEM((pairs, tq, 2 * tq), F32), pltpu.VMEM((pairs, tq, 2 * tq), F32)],
        compiler_params=_cparams(("parallel", "arbitrary")), name="dilated_attention",
    )(qt, k, vt, bias)


def _mlstm_kernel(u_ref, up_ref, v_ref, sm_ref, o_ref, cw_ref, cb_ref, wqt_ref, wk_ref,
                  gb_ref, gcol_ref, hg_ref, out_ref, c_st, m_st):
    ci = pl.program_id(1)
    rows = u_ref.shape[1]

    @pl.when(ci == 0)
    def _():
        c_st[...] = jnp.zeros(c_st.shape, F32)
        m_st[...] = jnp.zeros(m_st.shape, F32)

    tail = jnp.where(ci > 0, up_ref[0], 0.0)
    ext = jnp.concatenate([tail, u_ref[0]], axis=0)
    cw = cw_ref[...]
    uc = cb_ref[...] + cw[MLSTM_CONV - 1:MLSTM_CONV] * ext[8:]
    for j in range(MLSTM_CONV - 1):
        shifted = pltpu.roll(ext, MLSTM_CONV - 1 - j, axis=0)[8:]
        uc = uc + cw[j:j + 1] * shifted
    uc = uc * jax.nn.sigmoid(uc)

    qt_all = _dot(wqt_ref[...], uc.T.astype(BF16))
    k_all = _dot(uc.astype(BF16), wk_ref[...]) * (HEAD_DIM ** -0.5)
    vt_all = v_ref[0].astype(F32).T.astype(BF16)
    ogt = jax.nn.sigmoid(o_ref[0]).T
    sm = sm_ref[0]
    smt = sm.T
    gb = gb_ref[...]
    gcol = gcol_ref[...]
    upper = _iota((rows, rows), 0) <= _iota((rows, rows), 1)
    lower = _iota((rows, rows), 1) <= _iota((rows, rows), 0)
    ig_rows = smt[8:16] + gcol[:, 0:1]
    lf_rows = jax.nn.log_sigmoid(smt[16:24] + gcol[:, 1:2])
    b_rows = jnp.dot(lf_rows, jnp.where(upper, 1.0, 0.0), preferred_element_type=F32,
                     precision=lax.Precision.HIGHEST)
    ig_cols = sm[:, SM_CI:SM_CI + N_HEADS] + gb[0:1]
    lf_cols = jax.nn.log_sigmoid(sm[:, SM_CF:SM_CF + N_HEADS] + gb[1:2])
    b_cols = jnp.dot(jnp.where(lower, 1.0, 0.0), lf_cols, preferred_element_type=F32,
                     precision=lax.Precision.HIGHEST)
    ones = jnp.ones((BF16_SUBLANES, rows), BF16)

    for h in range(N_HEADS):
        hs = slice(h * HEAD_DIM, (h + 1) * HEAD_DIM)
        b_row = b_rows[h:h + 1]
        src_row = ig_rows[N_HEADS + h:N_HEADS + h + 1] - b_row
        src_col = ig_cols[:, h:h + 1] - b_cols[:, h:h + 1]
        dmat = jnp.where(upper, b_row + src_col, NEG)
        a = b_row[:, rows - 1:rows]
        g_end = a + src_row
        m_loc = jnp.max(g_end, axis=-1, keepdims=True)
        w_end = jnp.exp(g_end - m_loc)

        state = c_st[h]
        m_in = m_st[h]
        inter = b_row + m_in
        m_t = jnp.maximum(inter, jnp.max(dmat, axis=0, keepdims=True))
        e_inter = jnp.exp(inter - m_t)
        qt = qt_all[hs].astype(BF16)
        kb = k_all[:, hs].astype(BF16)
        values = jnp.concatenate([vt_all[hs], ones], axis=0)
        p = (_dot(kb, qt) * jnp.exp(dmat - m_t)).astype(BF16)
        from_state = _dot(state.astype(BF16), qt)
        from_chunk = _dot(values, p)
        num = e_inter * from_state[0:HEAD_DIM] + from_chunk[0:HEAD_DIM]
        den = e_inter * from_state[HEAD_DIM:HEAD_DIM + 1] + from_chunk[HEAD_DIM:HEAD_DIM + 1]
        hh = num * (1.0 / jnp.maximum(jnp.abs(den), jnp.exp(-m_t)))
        hh = hh * lax.rsqrt(jnp.mean(hh * hh, axis=0, keepdims=True) + EPS) * hg_ref[hs, :]
        out_ref[0, hs, :] = (hh * ogt[hs]).astype(out_ref.dtype)

        m_new = jnp.maximum(a + m_in, m_loc)
        decay = jnp.exp(a + m_in - m_new)
        fresh = jnp.exp(m_loc - m_new)
        local = _dot((values.astype(F32) * w_end).astype(BF16), kb)
        c_st[h] = decay * state + fresh * local
        m_st[h] = m_new


def _mlstm(u, v, smalls, o_pre, conv_w, conv_b, wq, wk, gate_b, head_g, rows=256):
    bsz, seq, _ = u.shape
    rows = min(rows, seq)
    row = pl.BlockSpec((1, rows, GROUP_WIDTH), lambda b, c: (b, c, 0))
    const2 = lambda a: pl.BlockSpec(a.shape, lambda b, c: (0,) * a.ndim)
    conv_b = conv_b.reshape(1, GROUP_WIDTH)
    head_g = head_g.reshape(GROUP_WIDTH, 1)
    eye = jnp.eye(N_HEADS, dtype=wq.dtype)
    wqt = jnp.einsum('hde,hg->hegd', wq, eye).reshape(GROUP_WIDTH, GROUP_WIDTH).astype(BF16)
    wkb = jnp.einsum('hde,hg->hdge', wk, eye).reshape(GROUP_WIDTH, GROUP_WIDTH).astype(BF16)
    zeros4 = jnp.zeros((N_HEADS,), gate_b.dtype)
    gcol = jnp.stack([jnp.concatenate([zeros4, gate_b[0]]), jnp.concatenate([gate_b[1], zeros4])],
                     axis=1)
    return pl.pallas_call(
        _mlstm_kernel, grid=(bsz, seq // rows),
        in_specs=[row,
                  pl.BlockSpec((1, 8, GROUP_WIDTH),
                               lambda b, c: (b, jnp.maximum(c * (rows // 8) - 1, 0), 0)),
                  row,
                  pl.BlockSpec((1, rows, LANES), lambda b, c: (b, c, 0)),
                  row, const2(conv_w), const2(conv_b), const2(wqt), const2(wkb), const2(gate_b),
                  const2(gcol), const2(head_g)],
        out_specs=pl.BlockSpec((1, GROUP_WIDTH, rows), lambda b, c: (b, 0, c)),
        out_shape=jax.ShapeDtypeStruct((bsz, GROUP_WIDTH, seq), BF16),
        scratch_shapes=[pltpu.VMEM((N_HEADS, ACC_ROWS, HEAD_DIM), F32),
                        pltpu.VMEM((N_HEADS, 1, 1), F32)],
        compiler_params=_cparams(("parallel", "arbitrary")), name="mlstm",
    )(u, u, v, smalls, o_pre, conv_w, conv_b, wqt, wkb, gate_b, gcol, head_g)


def _out_ffn_kernel(x_ref, oa_ref, ob_ref, oc_ref, od_ref, wo_ref, g_ref, wg_ref, wu_ref, wd_ref,
                    gf_ref, y_ref, *, final, ff_chunk):
    mixed_t = jnp.concatenate([oa_ref[0], ob_ref[0], oc_ref[0], od_ref[0]], axis=0)
    x = x_ref[...] + lax.dot_general(mixed_t, wo_ref[...], (((0,), (0,)), ((), ())),
                                     preferred_element_type=F32)
    h = x * lax.rsqrt(jnp.mean(x * x, axis=-1, keepdims=True) + EPS)
    h = (h * g_ref[...]).astype(BF16)
    ffn = None
    for c0 in range(0, D_FF, ff_chunk):
        gate = _dot(h, wg_ref[:, c0:c0 + ff_chunk])
        up = _dot(h, wu_ref[:, c0:c0 + ff_chunk])
        act = (gate * jax.nn.sigmoid(gate) * up).astype(BF16)
        part = _dot(act, wd_ref[c0:c0 + ff_chunk, :])
        ffn = part if ffn is None else ffn + part
    y = x + ffn
    if final:
        y = y * lax.rsqrt(jnp.mean(y * y, axis=-1, keepdims=True) + EPS) * gf_ref[...]
    y_ref[...] = y


def _out_ffn(x2, o_a, o_b, o_c, o_d, w_out, gain, w_gate, w_up, w_down, gain_final, final, tm=512):
    n = x2.shape[0]
    seq = o_a.shape[2]
    tm = min(tm, seq)
    nblk_s = seq // tm
    row = lambda w: pl.BlockSpec((tm, w), lambda i: (i, 0))
    mixer = pl.BlockSpec((1, GROUP_WIDTH, tm), lambda i: (i // nblk_s, 0, i % nblk_s))
    const = lambda a: pl.BlockSpec(a.shape, lambda i: (0, 0), pipeline_mode=pl.Buffered(1))
    gain = gain.reshape(1, D_MODEL)
    gain_final = gain_final.reshape(1, D_MODEL)
    kern = functools.partial(_out_ffn_kernel, final=final, ff_chunk=256)
    return pl.pallas_call(
        kern, grid=(n // tm,),
        in_specs=[row(D_MODEL)] + [mixer] * 4
                 + [const(w_out), const(gain), const(w_gate), const(w_up), const(w_down),
                    const(gain_final)],
        out_specs=row(D_MODEL),
        out_shape=jax.ShapeDtypeStruct((n, D_MODEL), F32),
        compiler_params=_cparams(("parallel",)), name="out_ffn",
    )(x2, o_a, o_b, o_c, o_d, w_out, gain, w_gate, w_up, w_down, gain_final)


def kernel(x, norm_mix, w_in, nsa_cmp_pos, nsa_cmp_w, diff_lambda, diff_norm, mlstm_conv_w,
           mlstm_conv_b, mlstm_wq, mlstm_wk, mlstm_gate_b, mlstm_norm, w_out, norm_ffn, w_gate,
           w_up, w_down, norm_final):
    bsz, seq, _ = x.shape
    depth = w_in.shape[0]
    tables = _lane_tables(seq)
    x2 = x.reshape(bsz * seq, D_MODEL)
    r3 = lambda t: t.reshape(bsz, seq, t.shape[-1])

    for layer in range(depth):
        (a_q, a_qr, a_kvc, a_ks, a_vs, a_kw, a_vw, smalls, b_q, b_k, b_v, c_u, c_v, c_o,
         d_q, d_k, d_v) = _in_proj(x2, norm_mix[layer], w_in[layer], tables, seq,
                                   tm=min(512, seq))
        smalls3 = r3(smalls)

        kc, kvct = _compress(r3(a_kvc), nsa_cmp_w[layer], nsa_cmp_pos[layer])
        o_a = _nsa(a_q, a_qr, kc, kvct, r3(a_ks), a_vs, r3(a_kw), a_vw, smalls3)

        lam_init = 0.8 - 0.6 * math.exp(-0.3 * layer)
        o_b = _diff(b_q, r3(b_k), b_v, diff_lambda[layer], diff_norm[layer], lam_init)

        o_c = _mlstm(r3(c_u), r3(c_v), smalls3, r3(c_o),
                     mlstm_conv_w[layer], mlstm_conv_b[layer], mlstm_wq[layer], mlstm_wk[layer],
                     mlstm_gate_b[layer], mlstm_norm[layer])

        o_d = _dilated(d_q, r3(d_k), d_v)

        x2 = _out_ffn(x2, o_a, o_b, o_c, o_d, w_out[layer].astype(BF16), norm_ffn[layer],
                      w_gate[layer].astype(BF16), w_up[layer].astype(BF16),
                      w_down[layer].astype(BF16), norm_final, final=(layer == depth - 1))
    return x2.reshape(bsz, seq, D_MODEL)
```

```python
import functools
import math

import numpy as np
import jax
import jax.numpy as jnp
from jax import lax
from jax.experimental import pallas as pl
from jax.experimental.pallas import tpu as pltpu

F32 = jnp.float32
BF16 = jnp.bfloat16

D_MODEL = 1024
HEAD_DIM = 64
N_HEADS = 4
GROUP_WIDTH = N_HEADS * HEAD_DIM
ROPE_THETA = 10000.0
EPS = 1e-6
NEG = -1e30
LOG2E = math.log2(math.e)

CMP_LEN = 32
CMP_STRIDE = 16
SLC_LEN = 64
TOP_N = 16
NSA_WINDOW = 512
FORCED_LOCAL = 2
DIFF_HALF = HEAD_DIM // 2
MLSTM_CHUNK = 64
MLSTM_CONV = 4
DILATED_PATTERNS = ((128, 1), (512, 4), (2048, 16))
D_FF = ((8 * D_MODEL + 3 * 256 - 1) // (3 * 256)) * 256

LANES = 128
BF16_SUBLANES = 16
VMEM_LIMIT = 56 * 1024 * 1024
ACC_ROWS = HEAD_DIM + BF16_SUBLANES

IN_SPLITS = (
    GROUP_WIDTH, HEAD_DIM, HEAD_DIM, HEAD_DIM, HEAD_DIM, HEAD_DIM, HEAD_DIM, 3 * N_HEADS,
    GROUP_WIDTH, GROUP_WIDTH, GROUP_WIDTH,
    GROUP_WIDTH, GROUP_WIDTH, N_HEADS, N_HEADS, GROUP_WIDTH,
    GROUP_WIDTH, GROUP_WIDTH, GROUP_WIDTH,
)
(A_Q, A_KC, A_VC, A_KS, A_VS, A_KW, A_VW, A_G, B_Q, B_K, B_V,
 C_U, C_V, C_I, C_F, C_O, D_Q, D_K, D_V) = range(19)
SM_AG, SM_CI, SM_CF = 0, 12, 16


def _cparams(sem):
    return pltpu.CompilerParams(dimension_semantics=sem, vmem_limit_bytes=VMEM_LIMIT)


def _iota(shape, dim):
    return lax.broadcasted_iota(jnp.int32, shape, dim)


def _dot(a, b):
    return jnp.dot(a, b, preferred_element_type=F32)


def _dot_nt(a, b):
    return lax.dot_general(a, b, (((1,), (1,)), ((), ())), preferred_element_type=F32)


PACKED_CHUNKS = 26
D_IN_PADDED = 3072


def _column_plan():
    offs = np.concatenate([[0], np.cumsum(IN_SPLITS)])
    order = [A_Q, A_KC, A_VC, A_KS, None, A_VS, None, A_KW, None, A_VW, None,
             A_G, C_I, C_F, ('pad', LANES - 20),
             B_Q, B_K, B_V, C_U, C_V, C_O, D_Q, D_K, D_V]
    src = []
    for item in order:
        if item is None:
            src += [-1] * HEAD_DIM
        elif isinstance(item, tuple):
            src += [-1] * item[1]
        else:
            src += list(range(int(offs[item]), int(offs[item + 1])))
    src = np.asarray(src)
    assert src.size == PACKED_CHUNKS * LANES
    terms, mats = [], []
    for j in range(PACKED_CHUNKS):
        cols = src[j * LANES:(j + 1) * LANES]
        todo = cols >= 0
        while todo.any():
            start = (cols[todo].min() // LANES) * LANES
            take = todo & (cols < start + 2 * LANES)
            sel = np.zeros((2 * LANES, LANES), np.float32)
            sel[cols[take] - start, np.nonzero(take)[0]] = 1.0
            terms.append((j, int(start)))
            mats.append(sel)
            todo &= ~take
    return tuple(terms), np.stack(mats)


def _rope_tables(seq, dim):
    inv = 1.0 / (ROPE_THETA ** (jnp.arange(0, dim, 2, dtype=F32) / dim))
    ang = jnp.arange(seq, dtype=F32)[:, None] * inv[None, :]
    return jnp.cos(ang), jnp.sin(ang)


def _lane_tables(seq):
    c64, s64 = _rope_tables(seq, HEAD_DIM)
    c32, s32 = _rope_tables(seq, DIFF_HALF)
    t64c = jnp.concatenate([c64, c64] * (LANES // HEAD_DIM), axis=1)
    t64s = jnp.concatenate([-s64, s64] * (LANES // HEAD_DIM), axis=1)
    t32c = jnp.concatenate([c32, c32] * (LANES // DIFF_HALF), axis=1)
    t32s = jnp.concatenate([-s32, s32] * (LANES // DIFF_HALF), axis=1)
    pos = np.arange(seq)[:, None]
    lane = np.arange(LANES)[None, :]
    onehot = ((lane >= HEAD_DIM) & ((pos // SLC_LEN) % HEAD_DIM == lane - HEAD_DIM)
              ).astype(np.float32)
    return t64c, t64s, t32c, t32s, jnp.asarray(onehot)


def _swap_halves(x, group):
    width = x.shape[-1]
    half = group // 2
    lane = _iota(x.shape, 1) & (group - 1)
    up = pltpu.roll(x, width - half, axis=1)
    down = pltpu.roll(x, half, axis=1)
    return jnp.where(lane < half, up, down)


def _rope(x, cos_t, sin_t, group):
    reps = x.shape[-1] // LANES
    if reps > 1:
        cos_t = jnp.concatenate([cos_t] * reps, axis=1)
        sin_t = jnp.concatenate([sin_t] * reps, axis=1)
    return x * cos_t + _swap_halves(x, group) * sin_t


def _in_proj_kernel(x_ref, g_ref, wraw_ref, sel_ref, c64_ref, s64_ref, c32_ref, s32_ref, oh_ref,
                    aq_ref, aqr_ref, akvc_ref, aks_ref, avs_ref, akw_ref, avw_ref, sm_ref,
                    bq_ref, bk_ref, bv_ref, cu_ref, cv_ref, co_ref, dq_ref, dk_ref, dv_ref,
                    w_ref, *, terms):
    @pl.when(pl.program_id(0) == 0)
    def _():
        for j in range(PACKED_CHUNKS):
            chunk = jnp.zeros((D_MODEL, LANES), F32)
            for t, (dst, start) in enumerate(terms):
                if dst == j:
                    chunk = chunk + _dot(wraw_ref[:, start:start + 2 * LANES], sel_ref[t])
            w_ref[:, j * LANES:(j + 1) * LANES] = chunk.astype(BF16)

    x = x_ref[...]
    h = x * lax.rsqrt(jnp.mean(x * x, axis=-1, keepdims=True) + EPS)
    h = (h * g_ref[...]).astype(BF16)
    c64, s64 = c64_ref[...], s64_ref[...]
    c32, s32 = c32_ref[...], s32_ref[...]

    def mm(c0, c1):
        return _dot(h, w_ref[:, c0 * LANES:c1 * LANES])

    def channel_major(ref, z, channels=None):
        zt = z.T
        ref[0] = (zt if channels is None else zt[0:channels]).astype(ref.dtype)

    zq = mm(0, 2) * (HEAD_DIM ** -0.5 * LOG2E)
    channel_major(aq_ref, zq)
    channel_major(aqr_ref, _rope(zq, c64, s64, HEAD_DIM))
    akvc_ref[...] = mm(2, 3).astype(BF16)
    aks_ref[...] = (_rope(mm(3, 4), c64, s64, HEAD_DIM) + oh_ref[...]).astype(BF16)
    channel_major(avs_ref, mm(4, 5), HEAD_DIM)
    akw_ref[...] = _rope(mm(5, 6), c64, s64, HEAD_DIM).astype(BF16)
    channel_major(avw_ref, mm(6, 7), HEAD_DIM)
    sm_ref[...] = mm(7, 8)
    channel_major(bq_ref, _rope(mm(8, 10), c32, s32, DIFF_HALF) * (DIFF_HALF ** -0.5 * LOG2E))
    bk_ref[...] = _rope(mm(10, 12), c32, s32, DIFF_HALF).astype(BF16)
    channel_major(bv_ref, mm(12, 14))
    cu_ref[...] = mm(14, 16)
    cv_ref[...] = mm(16, 18).astype(BF16)
    co_ref[...] = mm(18, 20)
    channel_major(dq_ref, _rope(mm(20, 22), c64, s64, HEAD_DIM) * (HEAD_DIM ** -0.5 * LOG2E))
    dk_ref[...] = _rope(mm(22, 24), c64, s64, HEAD_DIM).astype(BF16)
    channel_major(dv_ref, mm(24, 26))


_IN_PROJ_OUTS = (
    (256, BF16, True), (256, BF16, True), (128, BF16, False), (128, BF16, False),
    (HEAD_DIM, BF16, True), (128, BF16, False), (HEAD_DIM, BF16, True), (128, F32, False),
    (256, BF16, True), (256, BF16, False), (256, BF16, True), (256, F32, False),
    (256, BF16, False), (256, F32, False), (256, BF16, True), (256, BF16, False),
    (256, BF16, True))


def _in_proj(x2, gain, w, tables, seq, tm):
    n = x2.shape[0]
    nblk_s = seq // tm
    terms, select = _column_plan()
    select = jnp.asarray(select, BF16)
    w_raw = jnp.pad(w.astype(BF16), ((0, 0), (0, D_IN_PADDED - w.shape[1])))
    row = lambda i: (i, 0)
    tab = lambda i: (i % nblk_s, 0)
    const = lambda i: (0, 0)
    in_specs = [pl.BlockSpec((tm, D_MODEL), row),
                pl.BlockSpec((1, D_MODEL), const),
                pl.BlockSpec(w_raw.shape, const),
                pl.BlockSpec(select.shape, lambda i: (0, 0, 0))]
    in_specs += [pl.BlockSpec((tm, LANES), tab)] * 5
    out_specs = [pl.BlockSpec((1, w, tm), lambda i: (i // nblk_s, 0, i % nblk_s)) if cmaj
                 else pl.BlockSpec((tm, w), row) for w, _, cmaj in _IN_PROJ_OUTS]
    out_shape = [jax.ShapeDtypeStruct((n // seq, w, seq) if cmaj else (n, w), dt)
                 for w, dt, cmaj in _IN_PROJ_OUTS]
    return pl.pallas_call(
        functools.partial(_in_proj_kernel, terms=terms), grid=(n // tm,), in_specs=in_specs,
        out_specs=out_specs, out_shape=out_shape,
        scratch_shapes=[pltpu.VMEM((D_MODEL, PACKED_CHUNKS * LANES), BF16)],
        compiler_params=_cparams(("arbitrary",)), name="in_proj",
    )(x2, gain.reshape(1, D_MODEL), w_raw, select, *tables)


def _compress_kernel(r_ref, w_ref, pos_ref, kc_ref, kvct_ref):
    wk = w_ref[0].reshape(CMP_LEN, HEAD_DIM, HEAD_DIM)
    wv = w_ref[1].reshape(CMP_LEN, HEAD_DIM, HEAD_DIM)
    zeros = jnp.zeros_like(wk)
    full = jnp.concatenate([jnp.concatenate([wk, zeros], axis=2),
                            jnp.concatenate([zeros, wv], axis=2)], axis=1)
    w_first = full[0:CMP_STRIDE].reshape(CMP_STRIDE * LANES, LANES).astype(BF16)
    w_second = full[CMP_STRIDE:].reshape(CMP_STRIDE * LANES, LANES).astype(BF16)

    r = r_ref[0]
    first = _dot(r, w_first)
    second = _dot(r, w_second)
    nrow = first.shape[0]
    pos = pos_ref[...].astype(BF16)
    half = pos.shape[1] // 2
    const = _dot(pos[:, :half], w_first) + _dot(pos[:, half:], w_second)
    out = first + pltpu.roll(second, nrow - 1, axis=0) + const[0:1, :]
    lane = _iota(out.shape, 1)
    kc_ref[0] = jnp.where(lane < HEAD_DIM, out, 0.0).astype(BF16)
    kvct_ref[0] = out.T.astype(BF16)


def _compress(a_kvc, cmp_w, cmp_pos):
    bsz, seq, _ = a_kvc.shape
    nrow = seq // CMP_STRIDE
    r = a_kvc.reshape(bsz, nrow, CMP_STRIDE * LANES)
    pos = jnp.concatenate([cmp_pos[0], cmp_pos[1]], axis=-1)
    pos = jnp.broadcast_to(pos.reshape(1, CMP_LEN * LANES), (8, CMP_LEN * LANES))
    return pl.pallas_call(
        _compress_kernel, grid=(bsz,),
        in_specs=[pl.BlockSpec((1, nrow, CMP_STRIDE * LANES), lambda b: (b, 0, 0)),
                  pl.BlockSpec(cmp_w.shape, lambda b: (0, 0, 0)),
                  pl.BlockSpec(pos.shape, lambda b: (0, 0))],
        out_specs=[pl.BlockSpec((1, nrow, LANES), lambda b: (b, 0, 0)),
                   pl.BlockSpec((1, LANES, nrow), lambda b: (b, 0, 0))],
        out_shape=[jax.ShapeDtypeStruct((bsz, nrow, LANES), BF16),
                   jax.ShapeDtypeStruct((bsz, LANES, nrow), BF16)],
        compiler_params=_cparams(("parallel",)), name="nsa_compress",
    )(r, cmp_w, pos)


def _flash_init(m_ref, acc_ref):
    m_ref[...] = jnp.full(m_ref.shape, NEG, F32)
    acc_ref[...] = jnp.zeros(acc_ref.shape, F32)


def _flash_step(s, values, m_ref, acc_ref):
    m_old = m_ref[...]
    m_new = jnp.maximum(m_old, jnp.max(s, axis=0, keepdims=True))
    alpha = jnp.exp2(m_old - m_new)
    p = jnp.exp2(s - m_new).astype(BF16)
    width = acc_ref.shape[1]
    for g, vals in enumerate(values):
        rows = slice(g * ACC_ROWS, (g + 1) * ACC_ROWS)
        lanes = slice(g * width, (g + 1) * width)
        acc_ref[rows, :] = alpha[:, lanes] * acc_ref[rows, :] + _dot(vals, p[:, lanes])
    m_ref[...] = m_new


def _pipelined_tiles(lo, hi, scores, consume, sa_ref, sb_ref):
    n = hi - lo

    def put(ref, tiles):
        for g, tile in enumerate(tiles):
            ref[g] = tile

    def get(ref):
        return [ref[g] for g in range(ref.shape[0])]

    put(sa_ref, scores(lo))

    def body(i, carry):
        j = lo + 2 * i
        put(sb_ref, scores(j + 1))
        consume(get(sa_ref), j, False)
        put(sa_ref, scores(j + 2))
        consume(get(sb_ref), j + 1, False)
        return carry

    lax.fori_loop(0, n // 2, body, 0)

    @pl.when(n % 2 == 0)
    def _():
        consume(get(sa_ref), hi, True)

    @pl.when(n % 2 == 1)
    def _():
        put(sb_ref, scores(hi))
        consume(get(sa_ref), hi - 1, False)
        consume(get(sb_ref), hi, True)


def _with_ones(vt):
    return jnp.concatenate([vt, jnp.ones((BF16_SUBLANES, vt.shape[1]), BF16)], axis=0)


def _cmp_to_slc_t(seq):
    n_cmp = (seq - CMP_LEN) // CMP_STRIDE + 1
    n_slc = seq // SLC_LEN
    ratio_s, ratio_c = SLC_LEN // CMP_STRIDE, CMP_LEN // CMP_STRIDE
    jj = np.arange(n_slc)[:, None, None]
    src = ratio_s * jj - np.arange(ratio_s)[None, :, None] - np.arange(ratio_c)[None, None, :]
    ok = (src >= 0) & (src < n_cmp)
    m = np.zeros((seq // CMP_STRIDE, n_slc), np.float32)
    np.add.at(m, (np.where(ok, src, 0), np.broadcast_to(jj, src.shape)), ok.astype(np.float32))
    return jnp.asarray(m.T, BF16)


def _nsa_kernel(qt_ref, qrt_ref, kc_ref, kvct_ref, ks_ref, vst_ref, kw_ref, vwt_ref, sm_ref,
                c2st_ref, o_ref, m_ref, acc_ref, sa_ref, sb_ref, *, tq, top_n):
    qi = pl.program_id(1)
    s0 = qi * tq
    nb = qt_ref.shape[0]
    rows = N_HEADS * tq
    n_slc = c2st_ref.shape[0]
    lane_t = s0 + (_iota((1, rows), 1) & (tq - 1))
    c2st = c2st_ref[...]
    blk = _iota((n_slc, 1), 0)
    cur = (s0 + _iota((1, tq), 1)) >> 6
    forced = (blk == 0) | ((blk <= cur) & (blk > cur - FORCED_LOCAL))

    def heads_on_lanes(x):
        return jnp.concatenate([x[h * HEAD_DIM:(h + 1) * HEAD_DIM, :] for h in range(N_HEADS)],
                               axis=1)

    def compressed_and_selection(b):
        q4 = jnp.concatenate([heads_on_lanes(qt_ref[b]), jnp.zeros((HEAD_DIM, rows), BF16)], axis=0)
        sc = _dot(kc_ref[b], q4)
        cmask = (_iota((sc.shape[0], 1), 0) * CMP_STRIDE + (CMP_LEN - 1)) <= lane_t
        sc = jnp.where(cmask, sc, NEG)
        e = jnp.where(cmask, jnp.exp2(sc - jnp.max(sc, axis=0, keepdims=True)), 0.0)
        z = jnp.sum(e, axis=0, keepdims=True)
        p_cmp = e * (1.0 / jnp.where(z > 0, z, 1.0))
        o_cmp = _dot(kvct_ref[b][HEAD_DIM:2 * HEAD_DIM, :], p_cmp.astype(BF16))
        p_heads = p_cmp[:, 0:tq]
        for h in range(1, N_HEADS):
            p_heads = p_heads + p_cmp[:, h * tq:(h + 1) * tq]
        p_hi = p_heads.astype(BF16)
        p_lo = (p_heads - p_hi.astype(F32)).astype(BF16)
        imp = _dot(c2st, p_hi) + _dot(c2st, p_lo)

        score = jnp.where(blk > cur, -1.0e6, jnp.where(forced, 1.0e6, imp))
        rank = jnp.zeros((n_slc, tq), F32)
        for i in range(n_slc):
            s_i = score[i:i + 1, :]
            rank = rank + jnp.where(blk > i, jnp.where(s_i >= score, 1.0, 0.0),
                                    jnp.where(s_i > score, 1.0, 0.0))
        sel = (rank < top_n) & (blk <= cur)
        bias = jnp.where(sel, 0.0, NEG)
        if n_slc < HEAD_DIM:
            bias = jnp.concatenate([bias, jnp.zeros((HEAD_DIM - n_slc, tq), F32)], axis=0)
        bias4 = jnp.concatenate([bias] * N_HEADS, axis=1).astype(BF16)
        qsel = jnp.concatenate([heads_on_lanes(qrt_ref[b]), bias4], axis=0)
        return o_cmp, qsel

    prepared = [compressed_and_selection(b) for b in range(nb)]

    def attend(k_ref, vt_ref, lo, window):
        _flash_init(m_ref, acc_ref)

        def scores(j):
            k0 = pl.multiple_of(j * tq, tq)
            return [_dot(k_ref[b, pl.ds(k0, tq), :], prepared[b][1]) for b in range(nb)]

        def consume(tiles, j, diagonal):
            k0 = pl.multiple_of(j * tq, tq)
            kpos = k0 + _iota((tq, 1), 0)
            for b, s in enumerate(tiles):
                if diagonal:
                    s = jnp.where(kpos <= lane_t, s, NEG)
                if window:
                    s = jnp.where(kpos > lane_t - NSA_WINDOW, s, NEG)
                _flash_step(s, [_with_ones(vt_ref[b, :, pl.ds(k0, tq)])], m_ref.at[b], acc_ref.at[b])

        _pipelined_tiles(lo, qi, scores, consume, sa_ref, sb_ref)
        return [acc_ref[b, 0:HEAD_DIM, :] * (1.0 / acc_ref[b, HEAD_DIM:HEAD_DIM + 1, :])
                for b in range(nb)]

    o_slc = attend(ks_ref, vst_ref, 0, False)
    o_win = attend(kw_ref, vwt_ref, jnp.maximum(qi - NSA_WINDOW // tq, 0), True)

    for b in range(nb):
        g = jax.nn.sigmoid(sm_ref[b].T[0:BF16_SUBLANES, :])

        def gate(branch):
            return jnp.concatenate(
                [g[branch * N_HEADS + h:branch * N_HEADS + h + 1, :] for h in range(N_HEADS)], axis=1)

        o = gate(0) * prepared[b][0] + gate(1) * o_slc[b] + gate(2) * o_win[b]
        for h in range(N_HEADS):
            o_ref[b, h * HEAD_DIM:(h + 1) * HEAD_DIM, :] = (
                o[:, h * tq:(h + 1) * tq].astype(o_ref.dtype))


def _nsa(qt, qrt, kc, kvct, ks, vst, kw, vwt, smalls, tq=256):
    bsz, _, seq = qt.shape
    tq = min(tq, seq)
    nb = 2 if bsz % 2 == 0 else 1
    n_slc = seq // SLC_LEN
    c2st = _cmp_to_slc_t(seq)
    rows = N_HEADS * tq
    qspec = pl.BlockSpec((nb, GROUP_WIDTH, tq), lambda b, i: (b, 0, i))
    full = lambda a: pl.BlockSpec((nb,) + a.shape[1:], lambda b, i: (b, 0, 0))
    vspec = pl.BlockSpec((nb, HEAD_DIM, seq), lambda b, i: (b, 0, 0))
    kern = functools.partial(_nsa_kernel, tq=tq, top_n=min(TOP_N, n_slc))
    return pl.pallas_call(
        kern, grid=(bsz // nb, seq // tq),
        in_specs=[qspec, qspec, full(kc), full(kvct), full(ks), vspec, full(kw), vspec,
                  pl.BlockSpec((nb, tq, LANES), lambda b, i: (b, i, 0)),
                  pl.BlockSpec(c2st.shape, lambda b, i: (0, 0))],
        out_specs=qspec,
        out_shape=jax.ShapeDtypeStruct(qt.shape, BF16),
        scratch_shapes=[pltpu.VMEM((nb, 1, rows), F32), pltpu.VMEM((nb, ACC_ROWS, rows), F32),
                        pltpu.VMEM((nb, tq, rows), F32), pltpu.VMEM((nb, tq, rows), F32)],
        compiler_params=_cparams(("parallel", "arbitrary")), name="nsa_attention",
    )(qt, qrt, kc, kvct, ks, vst, kw, vwt, smalls, c2st)


def _diff_kernel(qt_ref, k_ref, vt_ref, lam_ref, g_ref, o_ref, m_ref, acc_ref, sa_ref, sb_ref,
                 *, tq, lam_init):
    qi = pl.program_id(1)
    s0 = qi * tq
    rows = 4 * tq
    pairs = N_HEADS // 2
    pw = 2 * HEAD_DIM
    row = _iota((pw, 1), 0)
    lane_t = s0 + (_iota((1, rows), 1) & (tq - 1))

    def query_matrix(p):
        qt = qt_ref[0, p * pw:(p + 1) * pw, :]
        zero = jnp.zeros_like(qt)
        return jnp.concatenate(
            [jnp.where((row >= DIFF_HALF * c) & (row < DIFF_HALF * (c + 1)), qt, zero)
             for c in range(4)], axis=1)

    qmats = [query_matrix(p) for p in range(pairs)]

    def scores(j):
        k0 = pl.multiple_of(j * tq, tq)
        return [_dot(k_ref[0, pl.ds(k0, tq), p * pw:(p + 1) * pw], qmats[p]) for p in range(pairs)]

    def consume(tiles, j, diagonal):
        k0 = pl.multiple_of(j * tq, tq)
        for p, s in enumerate(tiles):
            if diagonal:
                s = jnp.where(k0 + _iota((tq, 1), 0) <= lane_t, s, NEG)
            vt = vt_ref[0, p * pw:(p + 1) * pw, pl.ds(k0, tq)]
            _flash_step(s, [_with_ones(vt[0:HEAD_DIM]), _with_ones(vt[HEAD_DIM:])],
                        m_ref.at[p], acc_ref.at[p])

    _flash_init(m_ref, acc_ref)
    _pipelined_tiles(0, qi, scores, consume, sa_ref, sb_ref)

    lv = lam_ref[...]
    lam = (jnp.exp(jnp.sum(lv[0:1] * lv[1:2], axis=-1, keepdims=True))
           - jnp.exp(jnp.sum(lv[2:3] * lv[3:4], axis=-1, keepdims=True)) + lam_init)
    for h in range(N_HEADS):
        a = acc_ref[h // 2, ACC_ROWS * (h % 2):ACC_ROWS * (h % 2 + 1), :]
        o_all = a[0:HEAD_DIM] * (1.0 / a[HEAD_DIM:HEAD_DIM + 1])
        o = o_all[:, :tq] - lam * o_all[:, tq:]
        y = o * lax.rsqrt(jnp.mean(o * o, axis=0, keepdims=True) + EPS)
        o_ref[0, HEAD_DIM * h:HEAD_DIM * (h + 1), :] = (
            (y * g_ref[...]) * (1.0 - lam_init)).astype(o_ref.dtype)


def _diff(qt, k, vt, lam_vecs, sub_g, lam_init, tq=256):
    bsz, _, seq = qt.shape
    tq = min(tq, seq)
    pairs = N_HEADS // 2
    qspec = pl.BlockSpec((1, GROUP_WIDTH, tq), lambda b, i: (b, 0, i))
    kern = functools.partial(_diff_kernel, tq=tq, lam_init=lam_init)
    return pl.pallas_call(
        kern, grid=(bsz, seq // tq),
        in_specs=[qspec,
                  pl.BlockSpec((1, seq, GROUP_WIDTH), lambda b, i: (b, 0, 0)),
                  pl.BlockSpec((1, GROUP_WIDTH, seq), lambda b, i: (b, 0, 0)),
                  pl.BlockSpec(lam_vecs.shape, lambda b, i: (0, 0)),
                  pl.BlockSpec((HEAD_DIM, 1), lambda b, i: (0, 0))],
        out_specs=qspec,
        out_shape=jax.ShapeDtypeStruct(qt.shape, BF16),
        scratch_shapes=[pltpu.VMEM((pairs, 1, 4 * tq), F32),
                        pltpu.VMEM((pairs, 2 * ACC_ROWS, 2 * tq), F32),
                        pltpu.VMEM((pairs, tq, 4 * tq), F32), pltpu.VMEM((pairs, tq, 4 * tq), F32)],
        compiler_params=_cparams(("parallel", "arbitrary")), name="diff_attention",
    )(qt, k, vt, lam_vecs, sub_g.reshape(HEAD_DIM, 1))


def _dilated_bias(tq):
    max_back = max(w for w, _ in DILATED_PATTERNS) // tq
    classes = [0, 1, 2, 3, max_back]
    q = np.arange(tq)[None, :]
    k = np.arange(tq)[:, None]
    out = []
    for d in classes:
        delta = d * tq + q - k
        cnt = np.zeros((tq, tq), np.float64)
        for w, dil in DILATED_PATTERNS:
            cnt += (delta >= 0) & (delta <= w) & (delta % dil == 0)
        tab = np.where(cnt > 0, np.log2(np.maximum(cnt, 1.0)), NEG)
        out.append(np.concatenate([tab, tab], axis=1))
    return jnp.asarray(np.stack(out).astype(np.float32)), max_back


def _dilated_kernel(qt_ref, k_ref, vt_ref, bias_ref, o_ref, m_ref, acc_ref, sa_ref, sb_ref,
                    *, tq, max_back):
    qi = pl.program_id(1)
    pairs = N_HEADS // 2
    pw = 2 * HEAD_DIM
    row = _iota((pw, 1), 0)

    def query_matrix(p):
        qt = qt_ref[0, p * pw:(p + 1) * pw, :]
        zero = jnp.zeros_like(qt)
        return jnp.concatenate([jnp.where(row < HEAD_DIM, qt, zero),
                                jnp.where(row >= HEAD_DIM, qt, zero)], axis=1)

    qmats = [query_matrix(p) for p in range(pairs)]
    _flash_init(m_ref, acc_ref)

    def scores(j):
        k0 = pl.multiple_of(j * tq, tq)
        return [_dot(k_ref[0, pl.ds(k0, tq), p * pw:(p + 1) * pw], qmats[p]) for p in range(pairs)]

    def consume(tiles, j, diagonal):
        del diagonal
        k0 = pl.multiple_of(j * tq, tq)
        d = qi - j
        bias = bias_ref[jnp.where(d < 3, d, jnp.where(d == max_back, 4, 3))]
        for p, s in enumerate(tiles):
            vt = vt_ref[0, p * pw:(p + 1) * pw, pl.ds(k0, tq)]
            _flash_step(s + bias, [_with_ones(vt[0:HEAD_DIM]), _with_ones(vt[HEAD_DIM:])],
                        m_ref.at[p], acc_ref.at[p])

    _pipelined_tiles(jnp.maximum(qi - max_back, 0), qi, scores, consume, sa_ref, sb_ref)
    for h in range(N_HEADS):
        a = acc_ref[h // 2, ACC_ROWS * (h % 2):ACC_ROWS * (h % 2 + 1), :]
        o_ref[0, HEAD_DIM * h:HEAD_DIM * (h + 1), :] = (
            a[0:HEAD_DIM] * (1.0 / a[HEAD_DIM:HEAD_DIM + 1])).astype(o_ref.dtype)


def _dilated(qt, k, vt, tq=256):
    bsz, _, seq = qt.shape
    tq = min(tq, seq)
    pairs = N_HEADS // 2
    bias, max_back = _dilated_bias(tq)
    qspec = pl.BlockSpec((1, GROUP_WIDTH, tq), lambda b, i: (b, 0, i))
    kern = functools.partial(_dilated_kernel, tq=tq, max_back=max_back)
    return pl.pallas_call(
        kern, grid=(bsz, seq // tq),
        in_specs=[qspec,
                  pl.BlockSpec((1, seq, GROUP_WIDTH), lambda b, i: (b, 0, 0)),
                  pl.BlockSpec((1, GROUP_WIDTH, seq), lambda b, i: (b, 0, 0)),
                  pl.BlockSpec(bias.shape, lambda b, i: (0, 0, 0))],
        out_specs=qspec,
        out_shape=jax.ShapeDtypeStruct(qt.shape, BF16),
        scratch_shapes=[pltpu.VMEM((pairs, 1, 2 * tq), F32),
                        pltpu.VMEM((pairs, 2 * ACC_ROWS, tq), F32),
                        pltpu.VMEM((pairs, tq, 2 * tq), F32), pltpu.VMEM((pairs, tq, 2 * tq), F32)],
        compiler_params=_cparams(("parallel", "arbitrary")), name="dilated_attention",
    )(qt, k, vt, bias)


def _mlstm_kernel(u_ref, up_ref, v_ref, sm_ref, o_ref, cw_ref, cb_ref, wqt_ref, wk_ref,
                  gb_ref, gcol_ref, hg_ref, out_ref, c_st, m_st):
    ci = pl.program_id(1)
    rows = u_ref.shape[1]

    @pl.when(ci == 0)
    def _():
        c_st[...] = jnp.zeros(c_st.shape, F32)
        m_st[...] = jnp.zeros(m_st.shape, F32)

    tail = jnp.where(ci > 0, up_ref[0], 0.0)
    ext = jnp.concatenate([tail, u_ref[0]], axis=0)
    cw = cw_ref[...]
    uc = cb_ref[...] + cw[MLSTM_CONV - 1:MLSTM_CONV] * ext[8:]
    for j in range(MLSTM_CONV - 1):
        shifted = pltpu.roll(ext, MLSTM_CONV - 1 - j, axis=0)[8:]
        uc = uc + cw[j:j + 1] * shifted
    uc = uc * jax.nn.sigmoid(uc)

    qt_all = _dot(wqt_ref[...], uc.T.astype(BF16))
    k_all = _dot(uc.astype(BF16), wk_ref[...]) * (HEAD_DIM ** -0.5)
    vt_all = v_ref[0].astype(F32).T.astype(BF16)
    ogt = jax.nn.sigmoid(o_ref[0]).T
    sm = sm_ref[0]
    smt = sm.T
    gb = gb_ref[...]
    gcol = gcol_ref[...]
    upper = _iota((rows, rows), 0) <= _iota((rows, rows), 1)
    lower = _iota((rows, rows), 1) <= _iota((rows, rows), 0)
    ig_rows = smt[8:16] + gcol[:, 0:1]
    lf_rows = jax.nn.log_sigmoid(smt[16:24] + gcol[:, 1:2])
    b_rows = jnp.dot(lf_rows, jnp.where(upper, 1.0, 0.0), preferred_element_type=F32,
                     precision=lax.Precision.HIGHEST)
    ig_cols = sm[:, SM_CI:SM_CI + N_HEADS] + gb[0:1]
    lf_cols = jax.nn.log_sigmoid(sm[:, SM_CF:SM_CF + N_HEADS] + gb[1:2])
    b_cols = jnp.dot(jnp.where(lower, 1.0, 0.0), lf_cols, preferred_element_type=F32,
                     precision=lax.Precision.HIGHEST)
    ones = jnp.ones((BF16_SUBLANES, rows), BF16)

    for h in range(N_HEADS):
        hs = slice(h * HEAD_DIM, (h + 1) * HEAD_DIM)
        b_row = b_rows[h:h + 1]
        src_row = ig_rows[N_HEADS + h:N_HEADS + h + 1] - b_row
        src_col = ig_cols[:, h:h + 1] - b_cols[:, h:h + 1]
        dmat = jnp.where(upper, b_row + src_col, NEG)
        a = b_row[:, rows - 1:rows]
        g_end = a + src_row
        m_loc = jnp.max(g_end, axis=-1, keepdims=True)
        w_end = jnp.exp(g_end - m_loc)

        state = c_st[h]
        m_in = m_st[h]
        inter = b_row + m_in
        m_t = jnp.maximum(inter, jnp.max(dmat, axis=0, keepdims=True))
        e_inter = jnp.exp(inter - m_t)
        qt = qt_all[hs].astype(BF16)
        kb = k_all[:, hs].astype(BF16)
        values = jnp.concatenate([vt_all[hs], ones], axis=0)
        p = (_dot(kb, qt) * jnp.exp(dmat - m_t)).astype(BF16)
        from_state = _dot(state.astype(BF16), qt)
        from_chunk = _dot(values, p)
        num = e_inter * from_state[0:HEAD_DIM] + from_chunk[0:HEAD_DIM]
        den = e_inter * from_state[HEAD_DIM:HEAD_DIM + 1] + from_chunk[HEAD_DIM:HEAD_DIM + 1]
        hh = num * (1.0 / jnp.maximum(jnp.abs(den), jnp.exp(-m_t)))
        hh = hh * lax.rsqrt(jnp.mean(hh * hh, axis=0, keepdims=True) + EPS) * hg_ref[hs, :]
        out_ref[0, hs, :] = (hh * ogt[hs]).astype(out_ref.dtype)

        m_new = jnp.maximum(a + m_in, m_loc)
        decay = jnp.exp(a + m_in - m_new)
        fresh = jnp.exp(m_loc - m_new)
        local = _dot((values.astype(F32) * w_end).astype(BF16), kb)
        c_st[h] = decay * state + fresh * local
        m_st[h] = m_new


def _mlstm(u, v, smalls, o_pre, conv_w, conv_b, wq, wk, gate_b, head_g, rows=256):
    bsz, seq, _ = u.shape
    rows = min(rows, seq)
    row = pl.BlockSpec((1, rows, GROUP_WIDTH), lambda b, c: (b, c, 0))
    const2 = lambda a: pl.BlockSpec(a.shape, lambda b, c: (0,) * a.ndim)
    conv_b = conv_b.reshape(1, GROUP_WIDTH)
    head_g = head_g.reshape(GROUP_WIDTH, 1)
    eye = jnp.eye(N_HEADS, dtype=wq.dtype)
    wqt = jnp.einsum('hde,hg->hegd', wq, eye).reshape(GROUP_WIDTH, GROUP_WIDTH).astype(BF16)
    wkb = jnp.einsum('hde,hg->hdge', wk, eye).reshape(GROUP_WIDTH, GROUP_WIDTH).astype(BF16)
    zeros4 = jnp.zeros((N_HEADS,), gate_b.dtype)
    gcol = jnp.stack([jnp.concatenate([zeros4, gate_b[0]]), jnp.concatenate([gate_b[1], zeros4])],
                     axis=1)
    return pl.pallas_call(
        _mlstm_kernel, grid=(bsz, seq // rows),
        in_specs=[row,
                  pl.BlockSpec((1, 8, GROUP_WIDTH),
                               lambda b, c: (b, jnp.maximum(c * (rows // 8) - 1, 0), 0)),
                  row,
                  pl.BlockSpec((1, rows, LANES), lambda b, c: (b, c, 0)),
                  row, const2(conv_w), const2(conv_b), const2(wqt), const2(wkb), const2(gate_b),
                  const2(gcol), const2(head_g)],
        out_specs=pl.BlockSpec((1, GROUP_WIDTH, rows), lambda b, c: (b, 0, c)),
        out_shape=jax.ShapeDtypeStruct((bsz, GROUP_WIDTH, seq), BF16),
        scratch_shapes=[pltpu.VMEM((N_HEADS, ACC_ROWS, HEAD_DIM), F32),
                        pltpu.VMEM((N_HEADS, 1, 1), F32)],
        compiler_params=_cparams(("parallel", "arbitrary")), name="mlstm",
    )(u, u, v, smalls, o_pre, conv_w, conv_b, wqt, wkb, gate_b, gcol, head_g)


def _out_ffn_kernel(x_ref, oa_ref, ob_ref, oc_ref, od_ref, wo_ref, g_ref, wg_ref, wu_ref, wd_ref,
                    gf_ref, y_ref, *, final, ff_chunk):
    mixed_t = jnp.concatenate([oa_ref[0], ob_ref[0], oc_ref[0], od_ref[0]], axis=0)
    x = x_ref[...] + lax.dot_general(mixed_t, wo_ref[...], (((0,), (0,)), ((), ())),
                                     preferred_element_type=F32)
    h = x * lax.rsqrt(jnp.mean(x * x, axis=-1, keepdims=True) + EPS)
    h = (h * g_ref[...]).astype(BF16)
    ffn = None
    for c0 in range(0, D_FF, ff_chunk):
        gate = _dot(h, wg_ref[:, c0:c0 + ff_chunk])
        up = _dot(h, wu_ref[:, c0:c0 + ff_chunk])
        act = (gate * jax.nn.sigmoid(gate) * up).astype(BF16)
        part = _dot(act, wd_ref[c0:c0 + ff_chunk, :])
        ffn = part if ffn is None else ffn + part
    y = x + ffn
    if final:
        y = y * lax.rsqrt(jnp.mean(y * y, axis=-1, keepdims=True) + EPS) * gf_ref[...]
    y_ref[...] = y


def _out_ffn(x2, o_a, o_b, o_c, o_d, w_out, gain, w_gate, w_up, w_down, gain_final, final, tm=512):
    n = x2.shape[0]
    seq = o_a.shape[2]
    tm = min(tm, seq)
    nblk_s = seq // tm
    row = lambda w: pl.BlockSpec((tm, w), lambda i: (i, 0))
    mixer = pl.BlockSpec((1, GROUP_WIDTH, tm), lambda i: (i // nblk_s, 0, i % nblk_s))
    const = lambda a: pl.BlockSpec(a.shape, lambda i: (0, 0), pipeline_mode=pl.Buffered(1))
    gain = gain.reshape(1, D_MODEL)
    gain_final = gain_final.reshape(1, D_MODEL)
    kern = functools.partial(_out_ffn_kernel, final=final, ff_chunk=256)
    return pl.pallas_call(
        kern, grid=(n // tm,),
        in_specs=[row(D_MODEL)] + [mixer] * 4
                 + [const(w_out), const(gain), const(w_gate), const(w_up), const(w_down),
                    const(gain_final)],
        out_specs=row(D_MODEL),
        out_shape=jax.ShapeDtypeStruct((n, D_MODEL), F32),
        compiler_params=_cparams(("parallel",)), name="out_ffn",
    )(x2, o_a, o_b, o_c, o_d, w_out, gain, w_gate, w_up, w_down, gain_final)


def kernel(x, norm_mix, w_in, nsa_cmp_pos, nsa_cmp_w, diff_lambda, diff_norm, mlstm_conv_w,
           mlstm_conv_b, mlstm_wq, mlstm_wk, mlstm_gate_b, mlstm_norm, w_out, norm_ffn, w_gate,
           w_up, w_down, norm_final):
    bsz, seq, _ = x.shape
    depth = w_in.shape[0]
    tables = _lane_tables(seq)
    x2 = x.reshape(bsz * seq, D_MODEL)
    r3 = lambda t: t.reshape(bsz, seq, t.shape[-1])

    for layer in range(depth):
        (a_q, a_qr, a_kvc, a_ks, a_vs, a_kw, a_vw, smalls, b_q, b_k, b_v, c_u, c_v, c_o,
         d_q, d_k, d_v) = _in_proj(x2, norm_mix[layer], w_in[layer], tables, seq,
                                   tm=min(512, seq))
        smalls3 = r3(smalls)

        kc, kvct = _compress(r3(a_kvc), nsa_cmp_w[layer], nsa_cmp_pos[layer])
        o_a = _nsa(a_q, a_qr, kc, kvct, r3(a_ks), a_vs, r3(a_kw), a_vw, smalls3)

        lam_init = 0.8 - 0.6 * math.exp(-0.3 * layer)
        o_b = _diff(b_q, r3(b_k), b_v, diff_lambda[layer], diff_norm[layer], lam_init)

        o_c = _mlstm(r3(c_u), r3(c_v), smalls3, r3(c_o),
                     mlstm_conv_w[layer], mlstm_conv_b[layer], mlstm_wq[layer], mlstm_wk[layer],
                     mlstm_gate_b[layer], mlstm_norm[layer])

        o_d = _dilated(d_q, r3(d_k), d_v)

        x2 = _out_ffn(x2, o_a, o_b, o_c, o_d, w_out[layer].astype(BF16), norm_ffn[layer],
                      w_gate[layer].astype(BF16), w_up[layer].astype(BF16),
                      w_down[layer].astype(BF16), norm_final, final=(layer == depth - 1))
    return x2.reshape(bsz, seq, D_MODEL)
```

```python
import functools
import math

import numpy as np
import jax
import jax.numpy as jnp
from jax import lax
from jax.experimental import pallas as pl
from jax.experimental.pallas import tpu as pltpu

F32 = jnp.float32
BF16 = jnp.bfloat16

D_MODEL = 1024
HEAD_DIM = 64
N_HEADS = 4
GROUP_WIDTH = N_HEADS * HEAD_DIM
ROPE_THETA = 10000.0
EPS = 1e-6
NEG = -1e30
LOG2E = math.log2(math.e)

CMP_LEN = 32
CMP_STRIDE = 16
SLC_LEN = 64
TOP_N = 16
NSA_WINDOW = 512
FORCED_LOCAL = 2
DIFF_HALF = HEAD_DIM // 2
MLSTM_CHUNK = 64
MLSTM_CONV = 4
DILATED_PATTERNS = ((128, 1), (512, 4), (2048, 16))
D_FF = ((8 * D_MODEL + 3 * 256 - 1) // (3 * 256)) * 256

LANES = 128
BF16_SUBLANES = 16
VMEM_LIMIT = 56 * 1024 * 1024
ACC_ROWS = HEAD_DIM + BF16_SUBLANES

IN_SPLITS = (
    GROUP_WIDTH, HEAD_DIM, HEAD_DIM, HEAD_DIM, HEAD_DIM, HEAD_DIM, HEAD_DIM, 3 * N_HEADS,
    GROUP_WIDTH, GROUP_WIDTH, GROUP_WIDTH,
    GROUP_WIDTH, GROUP_WIDTH, N_HEADS, N_HEADS, GROUP_WIDTH,
    GROUP_WIDTH, GROUP_WIDTH, GROUP_WIDTH,
)
(A_Q, A_KC, A_VC, A_KS, A_VS, A_KW, A_VW, A_G, B_Q, B_K, B_V,
 C_U, C_V, C_I, C_F, C_O, D_Q, D_K, D_V) = range(19)
SM_AG, SM_CI, SM_CF = 0, 12, 16


def _cparams(sem):
    return pltpu.CompilerParams(dimension_semantics=sem, vmem_limit_bytes=VMEM_LIMIT)


def _iota(shape, dim):
    return lax.broadcasted_iota(jnp.int32, shape, dim)


def _dot(a, b):
    return jnp.dot(a, b, preferred_element_type=F32)


def _bf16_pieces(x):
    hi = x.astype(BF16)
    rest = x - hi.astype(F32)
    mid = rest.astype(BF16)
    return hi, mid, (rest - mid.astype(F32)).astype(BF16)


PACKED_CHUNKS = 26
D_IN_PADDED = 3072


def _column_plan():
    offs = np.concatenate([[0], np.cumsum(IN_SPLITS)])
    order = [A_Q, A_KC, A_VC, A_KS, None, A_VS, None, A_KW, None, A_VW, None,
             A_G, C_I, C_F, ('pad', LANES - 20),
             B_Q, B_K, B_V, C_U, C_V, C_O, D_Q, D_K, D_V]
    src = []
    for item in order:
        if item is None:
            src += [-1] * HEAD_DIM
        elif isinstance(item, tuple):
            src += [-1] * item[1]
        else:
            src += list(range(int(offs[item]), int(offs[item + 1])))
    src = np.asarray(src)
    assert src.size == PACKED_CHUNKS * LANES
    terms, mats = [], []
    for j in range(PACKED_CHUNKS):
        cols = src[j * LANES:(j + 1) * LANES]
        todo = cols >= 0
        while todo.any():
            start = (cols[todo].min() // LANES) * LANES
            take = todo & (cols < start + 2 * LANES)
            sel = np.zeros((2 * LANES, LANES), np.float32)
            sel[cols[take] - start, np.nonzero(take)[0]] = 1.0
            terms.append((j, int(start)))
            mats.append(sel)
            todo &= ~take
    return tuple(terms), np.stack(mats)


def _rope_tables(seq, dim):
    inv = 1.0 / (ROPE_THETA ** (jnp.arange(0, dim, 2, dtype=F32) / dim))
    ang = jnp.arange(seq, dtype=F32)[:, None] * inv[None, :]
    return jnp.cos(ang), jnp.sin(ang)


def _lane_tables(seq):
    c64, s64 = _rope_tables(seq, HEAD_DIM)
    c32, s32 = _rope_tables(seq, DIFF_HALF)
    t64c = jnp.concatenate([c64, c64] * (LANES // HEAD_DIM), axis=1)
    t64s = jnp.concatenate([-s64, s64] * (LANES // HEAD_DIM), axis=1)
    t32c = jnp.concatenate([c32, c32] * (LANES // DIFF_HALF), axis=1)
    t32s = jnp.concatenate([-s32, s32] * (LANES // DIFF_HALF), axis=1)
    pos = np.arange(seq)[:, None]
    lane = np.arange(LANES)[None, :]
    onehot = ((lane >= HEAD_DIM) & ((pos // SLC_LEN) % HEAD_DIM == lane - HEAD_DIM)
              ).astype(np.float32)
    return t64c, t64s, t32c, t32s, jnp.asarray(onehot)


def _swap_halves(x, group):
    width = x.shape[-1]
    half = group // 2
    lane = _iota(x.shape, 1) & (group - 1)
    up = pltpu.roll(x, width - half, axis=1)
    down = pltpu.roll(x, half, axis=1)
    return jnp.where(lane < half, up, down)


def _rope(x, cos_t, sin_t, group):
    reps = x.shape[-1] // LANES
    if reps > 1:
        cos_t = jnp.concatenate([cos_t] * reps, axis=1)
        sin_t = jnp.concatenate([sin_t] * reps, axis=1)
    return x * cos_t + _swap_halves(x, group) * sin_t


def _in_proj_kernel(x_ref, g_ref, wraw_ref, sel_ref, c64_ref, s64_ref, c32_ref, s32_ref, oh_ref,
                    aq_ref, aqr_ref, akvc_ref, aks_ref, avs_ref, akw_ref, avw_ref, sm_ref,
                    bq_ref, bk_ref, bv_ref, cu_ref, cv_ref, co_ref, dq_ref, dk_ref, dv_ref,
                    w_ref, *, terms):
    @pl.when(pl.program_id(0) == 0)
    def _():
        for j in range(PACKED_CHUNKS):
            chunk = jnp.zeros((D_MODEL, LANES), F32)
            for t, (dst, start) in enumerate(terms):
                if dst == j:
                    chunk = chunk + _dot(wraw_ref[:, start:start + 2 * LANES], sel_ref[t])
            w_ref[:, j * LANES:(j + 1) * LANES] = chunk.astype(BF16)

    x = x_ref[...]
    h = x * lax.rsqrt(jnp.mean(x * x, axis=-1, keepdims=True) + EPS)
    h = (h * g_ref[...]).astype(BF16)
    c64, s64 = c64_ref[...], s64_ref[...]
    c32, s32 = c32_ref[...], s32_ref[...]

    def mm(c0, c1):
        return _dot(h, w_ref[:, c0 * LANES:c1 * LANES])

    def channel_major(ref, z, channels=None):
        zt = z.T
        ref[0] = (zt if channels is None else zt[0:channels]).astype(ref.dtype)

    zq = mm(0, 2) * (HEAD_DIM ** -0.5 * LOG2E)
    channel_major(aq_ref, zq)
    channel_major(aqr_ref, _rope(zq, c64, s64, HEAD_DIM))
    akvc_ref[...] = mm(2, 3).astype(BF16)
    aks_ref[...] = (_rope(mm(3, 4), c64, s64, HEAD_DIM) + oh_ref[...]).astype(BF16)
    channel_major(avs_ref, mm(4, 5), HEAD_DIM)
    akw_ref[...] = _rope(mm(5, 6), c64, s64, HEAD_DIM).astype(BF16)
    channel_major(avw_ref, mm(6, 7), HEAD_DIM)
    sm_ref[...] = mm(7, 8)
    channel_major(bq_ref, _rope(mm(8, 10), c32, s32, DIFF_HALF) * (DIFF_HALF ** -0.5 * LOG2E))
    bk_ref[...] = _rope(mm(10, 12), c32, s32, DIFF_HALF).astype(BF16)
    channel_major(bv_ref, mm(12, 14))
    cu_ref[...] = mm(14, 16)
    cv_ref[...] = mm(16, 18).astype(BF16)
    co_ref[...] = mm(18, 20)
    channel_major(dq_ref, _rope(mm(20, 22), c64, s64, HEAD_DIM) * (HEAD_DIM ** -0.5 * LOG2E))
    dk_ref[...] = _rope(mm(22, 24), c64, s64, HEAD_DIM).astype(BF16)
    channel_major(dv_ref, mm(24, 26))


_IN_PROJ_OUTS = (
    (256, BF16, True), (256, BF16, True), (128, BF16, False), (128, BF16, False),
    (HEAD_DIM, BF16, True), (128, BF16, False), (HEAD_DIM, BF16, True), (128, F32, False),
    (256, BF16, True), (256, BF16, False), (256, BF16, True), (256, F32, False),
    (256, BF16, False), (256, F32, False), (256, BF16, True), (256, BF16, False),
    (256, BF16, True))


def _in_proj(x2, gain, w, tables, seq, tm):
    n = x2.shape[0]
    nblk_s = seq // tm
    terms, select = _column_plan()
    select = jnp.asarray(select, BF16)
    w_raw = jnp.pad(w.astype(BF16), ((0, 0), (0, D_IN_PADDED - w.shape[1])))
    row = lambda i: (i, 0)
    tab = lambda i: (i % nblk_s, 0)
    const = lambda i: (0, 0)
    in_specs = [pl.BlockSpec((tm, D_MODEL), row),
                pl.BlockSpec((1, D_MODEL), const),
                pl.BlockSpec(w_raw.shape, const),
                pl.BlockSpec(select.shape, lambda i: (0, 0, 0))]
    in_specs += [pl.BlockSpec((tm, LANES), tab)] * 5
    out_specs = [pl.BlockSpec((1, w, tm), lambda i: (i // nblk_s, 0, i % nblk_s)) if cmaj
                 else pl.BlockSpec((tm, w), row) for w, _, cmaj in _IN_PROJ_OUTS]
    out_shape = [jax.ShapeDtypeStruct((n // seq, w, seq) if cmaj else (n, w), dt)
                 for w, dt, cmaj in _IN_PROJ_OUTS]
    return pl.pallas_call(
        functools.partial(_in_proj_kernel, terms=terms), grid=(n // tm,), in_specs=in_specs,
        out_specs=out_specs, out_shape=out_shape,
        scratch_shapes=[pltpu.VMEM((D_MODEL, PACKED_CHUNKS * LANES), BF16)],
        compiler_params=_cparams(("arbitrary",)), name="in_proj",
    )(x2, gain.reshape(1, D_MODEL), w_raw, select, *tables)


def _compress_kernel(r_ref, w_ref, pos_ref, kc_ref, kvct_ref):
    wk = w_ref[0].reshape(CMP_LEN, HEAD_DIM, HEAD_DIM)
    wv = w_ref[1].reshape(CMP_LEN, HEAD_DIM, HEAD_DIM)
    zeros = jnp.zeros_like(wk)
    full = jnp.concatenate([jnp.concatenate([wk, zeros], axis=2),
                            jnp.concatenate([zeros, wv], axis=2)], axis=1)
    w_first = full[0:CMP_STRIDE].reshape(CMP_STRIDE * LANES, LANES).astype(BF16)
    w_second = full[CMP_STRIDE:].reshape(CMP_STRIDE * LANES, LANES).astype(BF16)

    r = r_ref[0]
    first = _dot(r, w_first)
    second = _dot(r, w_second)
    nrow = first.shape[0]
    pos = pos_ref[...].astype(BF16)
    half = pos.shape[1] // 2
    const = _dot(pos[:, :half], w_first) + _dot(pos[:, half:], w_second)
    out = first + pltpu.roll(second, nrow - 1, axis=0) + const[0:1, :]
    lane = _iota(out.shape, 1)
    kc_ref[0] = jnp.where(lane < HEAD_DIM, out, 0.0).astype(BF16)
    kvct_ref[0] = out.T.astype(BF16)


def _compress(a_kvc, cmp_w, cmp_pos):
    bsz, seq, _ = a_kvc.shape
    nrow = seq // CMP_STRIDE
    r = a_kvc.reshape(bsz, nrow, CMP_STRIDE * LANES)
    pos = jnp.concatenate([cmp_pos[0], cmp_pos[1]], axis=-1)
    pos = jnp.broadcast_to(pos.reshape(1, CMP_LEN * LANES), (8, CMP_LEN * LANES))
    return pl.pallas_call(
        _compress_kernel, grid=(bsz,),
        in_specs=[pl.BlockSpec((1, nrow, CMP_STRIDE * LANES), lambda b: (b, 0, 0)),
                  pl.BlockSpec(cmp_w.shape, lambda b: (0, 0, 0)),
                  pl.BlockSpec(pos.shape, lambda b: (0, 0))],
        out_specs=[pl.BlockSpec((1, nrow, LANES), lambda b: (b, 0, 0)),
                   pl.BlockSpec((1, LANES, nrow), lambda b: (b, 0, 0))],
        out_shape=[jax.ShapeDtypeStruct((bsz, nrow, LANES), BF16),
                   jax.ShapeDtypeStruct((bsz, LANES, nrow), BF16)],
        compiler_params=_cparams(("parallel",)), name="nsa_compress",
    )(r, cmp_w, pos)


def _flash_init(m_ref, acc_ref):
    m_ref[...] = jnp.full(m_ref.shape, NEG, F32)
    acc_ref[...] = jnp.zeros(acc_ref.shape, F32)


def _flash_step(s, values, m_ref, acc_ref):
    m_old = m_ref[...]
    m_new = jnp.maximum(m_old, jnp.max(s, axis=0, keepdims=True))
    alpha = jnp.exp2(m_old - m_new)
    p = jnp.exp2(s - m_new).astype(BF16)
    width = acc_ref.shape[1]
    for g, vals in enumerate(values):
        rows = slice(g * ACC_ROWS, (g + 1) * ACC_ROWS)
        lanes = slice(g * width, (g + 1) * width)
        acc_ref[rows, :] = alpha[:, lanes] * acc_ref[rows, :] + _dot(vals, p[:, lanes])
    m_ref[...] = m_new


def _pipelined_tiles(lo, hi, scores, consume, sa_ref, sb_ref):
    n = hi - lo

    def put(ref, tiles):
        for g, tile in enumerate(tiles):
            ref[g] = tile

    def get(ref):
        return [ref[g] for g in range(ref.shape[0])]

    put(sa_ref, scores(lo))

    def body(i, carry):
        j = lo + 2 * i
        put(sb_ref, scores(j + 1))
        consume(get(sa_ref), j, False)
        put(sa_ref, scores(j + 2))
        consume(get(sb_ref), j + 1, False)
        return carry

    lax.fori_loop(0, n // 2, body, 0)

    @pl.when(n % 2 == 0)
    def _():
        consume(get(sa_ref), hi, True)

    @pl.when(n % 2 == 1)
    def _():
        put(sb_ref, scores(hi))
        consume(get(sa_ref), hi - 1, False)
        consume(get(sb_ref), hi, True)


def _with_ones(vt):
    return jnp.concatenate([vt, jnp.ones((BF16_SUBLANES, vt.shape[1]), BF16)], axis=0)


def _cmp_to_slc_t(seq):
    n_cmp = (seq - CMP_LEN) // CMP_STRIDE + 1
    n_slc = seq // SLC_LEN
    ratio_s, ratio_c = SLC_LEN // CMP_STRIDE, CMP_LEN // CMP_STRIDE
    jj = np.arange(n_slc)[:, None, None]
    src = ratio_s * jj - np.arange(ratio_s)[None, :, None] - np.arange(ratio_c)[None, None, :]
    ok = (src >= 0) & (src < n_cmp)
    m = np.zeros((seq // CMP_STRIDE, n_slc), np.float32)
    np.add.at(m, (np.where(ok, src, 0), np.broadcast_to(jj, src.shape)), ok.astype(np.float32))
    return jnp.asarray(m.T, BF16)


def _nsa_kernel(qt_ref, qrt_ref, kc_ref, kvct_ref, ks_ref, vst_ref, kw_ref, vwt_ref, sm_ref,
                c2st_ref, wmask_ref, o_ref, m_ref, acc_ref, sa_ref, sb_ref, *, tq, top_n):
    qi = pl.program_id(1)
    s0 = qi * tq
    nb = qt_ref.shape[0]
    rows = N_HEADS * tq
    n_slc = c2st_ref.shape[0]
    lane_t = s0 + (_iota((1, rows), 1) & (tq - 1))
    c2st = c2st_ref[...]
    blk = _iota((n_slc, 1), 0)
    cur = (s0 + _iota((1, tq), 1)) >> 6
    forced = (blk == 0) | ((blk <= cur) & (blk > cur - FORCED_LOCAL))

    def heads_on_lanes(x):
        return jnp.concatenate([x[h * HEAD_DIM:(h + 1) * HEAD_DIM, :] for h in range(N_HEADS)],
                               axis=1)

    def compressed_and_selection(b):
        q4 = jnp.concatenate([heads_on_lanes(qt_ref[b]), jnp.zeros((HEAD_DIM, rows), BF16)], axis=0)
        sc = _dot(kc_ref[b], q4)
        cmask = (_iota((sc.shape[0], 1), 0) * CMP_STRIDE + (CMP_LEN - 1)) <= lane_t
        sc = jnp.where(cmask, sc, NEG)
        e = jnp.where(cmask, jnp.exp2(sc - jnp.max(sc, axis=0, keepdims=True)), 0.0)
        z = jnp.sum(e, axis=0, keepdims=True)
        p_cmp = e * (1.0 / jnp.where(z > 0, z, 1.0))
        o_cmp = _dot(kvct_ref[b][HEAD_DIM:2 * HEAD_DIM, :], p_cmp.astype(BF16))
        p_heads = p_cmp[:, 0:tq]
        for h in range(1, N_HEADS):
            p_heads = p_heads + p_cmp[:, h * tq:(h + 1) * tq]
        p_hi = p_heads.astype(BF16)
        p_lo = (p_heads - p_hi.astype(F32)).astype(BF16)
        imp = _dot(c2st, p_hi) + _dot(c2st, p_lo)

        score = jnp.where(blk > cur, -1.0e6, jnp.where(forced, 1.0e6, imp))
        rank = jnp.zeros((n_slc, tq), F32)
        for i in range(n_slc):
            s_i = score[i:i + 1, :]
            rank = rank + jnp.where(blk > i, jnp.where(s_i >= score, 1.0, 0.0),
                                    jnp.where(s_i > score, 1.0, 0.0))
        sel = (rank < top_n) & (blk <= cur)
        bias = jnp.where(sel, 0.0, NEG)
        if n_slc < HEAD_DIM:
            bias = jnp.concatenate([bias, jnp.zeros((HEAD_DIM - n_slc, tq), F32)], axis=0)
        bias4 = jnp.concatenate([bias] * N_HEADS, axis=1).astype(BF16)
        qsel = jnp.concatenate([heads_on_lanes(qrt_ref[b]), bias4], axis=0)
        return o_cmp, qsel

    prepared = [compressed_and_selection(b) for b in range(nb)]

    def attend(k_ref, vt_ref, lo, window):
        _flash_init(m_ref, acc_ref)

        def scores(j):
            k0 = pl.multiple_of(j * tq, tq)
            return [_dot(k_ref[b, pl.ds(k0, tq), :], prepared[b][1]) for b in range(nb)]

        def consume(tiles, j, diagonal):
            k0 = pl.multiple_of(j * tq, tq)
            if window:
                far = qi - NSA_WINDOW // tq
                mask = wmask_ref[jnp.where(j == qi, 0, jnp.where(j == far, 2, 1))]
            elif diagonal:
                mask = wmask_ref[0]
            for b, s in enumerate(tiles):
                if window or diagonal:
                    s = s + mask
                _flash_step(s, [_with_ones(vt_ref[b, :, pl.ds(k0, tq)])], m_ref.at[b], acc_ref.at[b])

        _pipelined_tiles(lo, qi, scores, consume, sa_ref, sb_ref)
        return [acc_ref[b, 0:HEAD_DIM, :] * (1.0 / acc_ref[b, HEAD_DIM:HEAD_DIM + 1, :])
                for b in range(nb)]

    o_slc = attend(ks_ref, vst_ref, 0, False)
    o_win = attend(kw_ref, vwt_ref, jnp.maximum(qi - NSA_WINDOW // tq, 0), True)

    for b in range(nb):
        g = jax.nn.sigmoid(sm_ref[b].T[0:BF16_SUBLANES, :])

        def gate(branch):
            return jnp.concatenate(
                [g[branch * N_HEADS + h:branch * N_HEADS + h + 1, :] for h in range(N_HEADS)], axis=1)

        o = gate(0) * prepared[b][0] + gate(1) * o_slc[b] + gate(2) * o_win[b]
        for h in range(N_HEADS):
            o_ref[b, h * HEAD_DIM:(h + 1) * HEAD_DIM, :] = (
                o[:, h * tq:(h + 1) * tq].astype(o_ref.dtype))


def _nsa(qt, qrt, kc, kvct, ks, vst, kw, vwt, smalls, tq=256):
    bsz, _, seq = qt.shape
    tq = min(tq, seq)
    nb = 2 if bsz % 2 == 0 else 1
    n_slc = seq // SLC_LEN
    c2st = _cmp_to_slc_t(seq)
    rows = N_HEADS * tq
    qspec = pl.BlockSpec((nb, GROUP_WIDTH, tq), lambda b, i: (b, 0, i))
    full = lambda a: pl.BlockSpec((nb,) + a.shape[1:], lambda b, i: (b, 0, 0))
    vspec = pl.BlockSpec((nb, HEAD_DIM, seq), lambda b, i: (b, 0, 0))
    assert NSA_WINDOW % tq == 0
    k_off = np.arange(tq)[:, None]
    q_off = np.tile(np.arange(tq), N_HEADS)[None, :]
    visible = np.stack([k_off <= q_off, np.ones((tq, rows), bool), k_off > q_off])
    wmask = jnp.asarray(np.where(visible, 0.0, NEG).astype(np.float32))
    kern = functools.partial(_nsa_kernel, tq=tq, top_n=min(TOP_N, n_slc))
    return pl.pallas_call(
        kern, grid=(bsz // nb, seq // tq),
        in_specs=[qspec, qspec, full(kc), full(kvct), full(ks), vspec, full(kw), vspec,
                  pl.BlockSpec((nb, tq, LANES), lambda b, i: (b, i, 0)),
                  pl.BlockSpec(c2st.shape, lambda b, i: (0, 0)),
                  pl.BlockSpec(wmask.shape, lambda b, i: (0, 0, 0), pipeline_mode=pl.Buffered(1))],
        out_specs=qspec,
        out_shape=jax.ShapeDtypeStruct(qt.shape, BF16),
        scratch_shapes=[pltpu.VMEM((nb, 1, rows), F32), pltpu.VMEM((nb, ACC_ROWS, rows), F32),
                        pltpu.VMEM((nb, tq, rows), F32), pltpu.VMEM((nb, tq, rows), F32)],
        compiler_params=_cparams(("parallel", "arbitrary")), name="nsa_attention",
    )(qt, qrt, kc, kvct, ks, vst, kw, vwt, smalls, c2st, wmask)


def _diff_kernel(qt_ref, k_ref, vt_ref, lam_ref, g_ref, o_ref, m_ref, acc_ref, sa_ref, sb_ref,
                 *, tq, lam_init):
    qi = pl.program_id(1)
    s0 = qi * tq
    rows = 4 * tq
    pairs = N_HEADS // 2
    pw = 2 * HEAD_DIM
    row = _iota((pw, 1), 0)
    lane_t = s0 + (_iota((1, rows), 1) & (tq - 1))

    def query_matrix(p):
        qt = qt_ref[0, p * pw:(p + 1) * pw, :]
        zero = jnp.zeros_like(qt)
        return jnp.concatenate(
            [jnp.where((row >= DIFF_HALF * c) & (row < DIFF_HALF * (c + 1)), qt, zero)
             for c in range(4)], axis=1)

    qmats = [query_matrix(p) for p in range(pairs)]

    def scores(j):
        k0 = pl.multiple_of(j * tq, tq)
        return [_dot(k_ref[0, pl.ds(k0, tq), p * pw:(p + 1) * pw], qmats[p]) for p in range(pairs)]

    def consume(tiles, j, diagonal):
        k0 = pl.multiple_of(j * tq, tq)
        for p, s in enumerate(tiles):
            if diagonal:
                s = jnp.where(k0 + _iota((tq, 1), 0) <= lane_t, s, NEG)
            vt = vt_ref[0, p * pw:(p + 1) * pw, pl.ds(k0, tq)]
            _flash_step(s, [_with_ones(vt[0:HEAD_DIM]), _with_ones(vt[HEAD_DIM:])],
                        m_ref.at[p], acc_ref.at[p])

    _flash_init(m_ref, acc_ref)
    _pipelined_tiles(0, qi, scores, consume, sa_ref, sb_ref)

    lv = lam_ref[...]
    lam = (jnp.exp(jnp.sum(lv[0:1] * lv[1:2], axis=-1, keepdims=True))
           - jnp.exp(jnp.sum(lv[2:3] * lv[3:4], axis=-1, keepdims=True)) + lam_init)
    for h in range(N_HEADS):
        a = acc_ref[h // 2, ACC_ROWS * (h % 2):ACC_ROWS * (h % 2 + 1), :]
        o_all = a[0:HEAD_DIM] * (1.0 / a[HEAD_DIM:HEAD_DIM + 1])
        o = o_all[:, :tq] - lam * o_all[:, tq:]
        y = o * lax.rsqrt(jnp.mean(o * o, axis=0, keepdims=True) + EPS)
        o_ref[0, HEAD_DIM * h:HEAD_DIM * (h + 1), :] = (
            (y * g_ref[...]) * (1.0 - lam_init)).astype(o_ref.dtype)


def _diff(qt, k, vt, lam_vecs, sub_g, lam_init, tq=256):
    bsz, _, seq = qt.shape
    tq = min(tq, seq)
    pairs = N_HEADS // 2
    qspec = pl.BlockSpec((1, GROUP_WIDTH, tq), lambda b, i: (b, 0, i))
    kern = functools.partial(_diff_kernel, tq=tq, lam_init=lam_init)
    return pl.pallas_call(
        kern, grid=(bsz, seq // tq),
        in_specs=[qspec,
                  pl.BlockSpec((1, seq, GROUP_WIDTH), lambda b, i: (b, 0, 0)),
                  pl.BlockSpec((1, GROUP_WIDTH, seq), lambda b, i: (b, 0, 0)),
                  pl.BlockSpec(lam_vecs.shape, lambda b, i: (0, 0)),
                  pl.BlockSpec((HEAD_DIM, 1), lambda b, i: (0, 0))],
        out_specs=qspec,
        out_shape=jax.ShapeDtypeStruct(qt.shape, BF16),
        scratch_shapes=[pltpu.VMEM((pairs, 1, 4 * tq), F32),
                        pltpu.VMEM((pairs, 2 * ACC_ROWS, 2 * tq), F32),
                        pltpu.VMEM((pairs, tq, 4 * tq), F32), pltpu.VMEM((pairs, tq, 4 * tq), F32)],
        compiler_params=_cparams(("parallel", "arbitrary")), name="diff_attention",
    )(qt, k, vt, lam_vecs, sub_g.reshape(HEAD_DIM, 1))


def _dilated_bias(tq):
    max_back = max(w for w, _ in DILATED_PATTERNS) // tq
    classes = [0, 1, 2, 3, max_back]
    q = np.arange(tq)[None, :]
    k = np.arange(tq)[:, None]
    out = []
    for d in classes:
        delta = d * tq + q - k
        cnt = np.zeros((tq, tq), np.float64)
        for w, dil in DILATED_PATTERNS:
            cnt += (delta >= 0) & (delta <= w) & (delta % dil == 0)
        tab = np.where(cnt > 0, np.log2(np.maximum(cnt, 1.0)), NEG)
        out.append(np.concatenate([tab, tab], axis=1))
    return jnp.asarray(np.stack(out).astype(np.float32)), max_back


def _dilated_kernel(qt_ref, k_ref, vt_ref, bias_ref, o_ref, m_ref, acc_ref, sa_ref, sb_ref,
                    *, tq, max_back):
    qi = pl.program_id(1)
    nb = qt_ref.shape[0]
    pairs = N_HEADS // 2
    pw = 2 * HEAD_DIM
    row = _iota((pw, 1), 0)
    streams = [(b, p) for b in range(nb) for p in range(pairs)]

    def query_matrix(b, p):
        qt = qt_ref[b, p * pw:(p + 1) * pw, :]
        zero = jnp.zeros_like(qt)
        return jnp.concatenate([jnp.where(row < HEAD_DIM, qt, zero),
                                jnp.where(row >= HEAD_DIM, qt, zero)], axis=1)

    qmats = [query_matrix(b, p) for b, p in streams]
    _flash_init(m_ref, acc_ref)

    def scores(j):
        k0 = pl.multiple_of(j * tq, tq)
        return [_dot(k_ref[b, pl.ds(k0, tq), p * pw:(p + 1) * pw], qmats[g])
                for g, (b, p) in enumerate(streams)]

    def consume(tiles, j, diagonal):
        del diagonal
        k0 = pl.multiple_of(j * tq, tq)
        d = qi - j
        bias = bias_ref[jnp.where(d < 3, d, jnp.where(d == max_back, 4, 3))]
        for g, (b, p) in enumerate(streams):
            vt = vt_ref[b, p * pw:(p + 1) * pw, pl.ds(k0, tq)]
            _flash_step(tiles[g] + bias, [_with_ones(vt[0:HEAD_DIM]), _with_ones(vt[HEAD_DIM:])],
                        m_ref.at[g], acc_ref.at[g])

    _pipelined_tiles(jnp.maximum(qi - max_back, 0), qi, scores, consume, sa_ref, sb_ref)
    for g, (b, p) in enumerate(streams):
        for hh in range(2):
            a = acc_ref[g, ACC_ROWS * hh:ACC_ROWS * (hh + 1), :]
            h = 2 * p + hh
            o_ref[b, HEAD_DIM * h:HEAD_DIM * (h + 1), :] = (
                a[0:HEAD_DIM] * (1.0 / a[HEAD_DIM:HEAD_DIM + 1])).astype(o_ref.dtype)


def _dilated(qt, k, vt, tq=256):
    bsz, _, seq = qt.shape
    tq = min(tq, seq)
    nb = 1
    streams = nb * (N_HEADS // 2)
    bias, max_back = _dilated_bias(tq)
    qspec = pl.BlockSpec((nb, GROUP_WIDTH, tq), lambda b, i: (b, 0, i))
    kern = functools.partial(_dilated_kernel, tq=tq, max_back=max_back)
    return pl.pallas_call(
        kern, grid=(bsz // nb, seq // tq),
        in_specs=[qspec,
                  pl.BlockSpec((nb, seq, GROUP_WIDTH), lambda b, i: (b, 0, 0)),
                  pl.BlockSpec((nb, GROUP_WIDTH, seq), lambda b, i: (b, 0, 0)),
                  pl.BlockSpec(bias.shape, lambda b, i: (0, 0, 0))],
        out_specs=qspec,
        out_shape=jax.ShapeDtypeStruct(qt.shape, BF16),
        scratch_shapes=[pltpu.VMEM((streams, 1, 2 * tq), F32),
                        pltpu.VMEM((streams, 2 * ACC_ROWS, tq), F32),
                        pltpu.VMEM((streams, tq, 2 * tq), F32),
                        pltpu.VMEM((streams, tq, 2 * tq), F32)],
        compiler_params=_cparams(("parallel", "arbitrary")), name="dilated_attention",
    )(qt, k, vt, bias)


def _mlstm_kernel(u_ref, up_ref, v_ref, sm_ref, o_ref, cw_ref, cb_ref, wqt_ref, wk_ref,
                  gb_ref, gcol_ref, hg_ref, out_ref, c_st, m_st):
    ci = pl.program_id(1)
    rows = u_ref.shape[1]

    @pl.when(ci == 0)
    def _():
        c_st[...] = jnp.zeros(c_st.shape, F32)
        m_st[...] = jnp.zeros(m_st.shape, F32)

    tail = jnp.where(ci > 0, up_ref[0], 0.0)
    ext = jnp.concatenate([tail, u_ref[0]], axis=0)
    cw = cw_ref[...]
    uc = cb_ref[...] + cw[MLSTM_CONV - 1:MLSTM_CONV] * ext[8:]
    for j in range(MLSTM_CONV - 1):
        shifted = pltpu.roll(ext, MLSTM_CONV - 1 - j, axis=0)[8:]
        uc = uc + cw[j:j + 1] * shifted
    uc = uc * jax.nn.sigmoid(uc)

    qt_all = _dot(wqt_ref[...], uc.T.astype(BF16))
    k_all = _dot(uc.astype(BF16), wk_ref[...]) * (HEAD_DIM ** -0.5)
    vt_all = v_ref[0].astype(F32).T.astype(BF16)
    ogt = jax.nn.sigmoid(o_ref[0]).T
    sm = sm_ref[0]
    smt = sm.T
    gb = gb_ref[...]
    gcol = gcol_ref[...]
    upper = _iota((rows, rows), 0) <= _iota((rows, rows), 1)
    lower = _iota((rows, rows), 1) <= _iota((rows, rows), 0)
    ig_rows = smt[8:16] + gcol[:, 0:1]
    lf_rows = jax.nn.log_sigmoid(smt[16:24] + gcol[:, 1:2])
    tri_upper = jnp.where(upper, 1.0, 0.0).astype(BF16)
    tri_lower = jnp.where(lower, 1.0, 0.0).astype(BF16)
    b_rows = sum(_dot(piece, tri_upper) for piece in _bf16_pieces(lf_rows))
    ig_cols = sm[:, SM_CI:SM_CI + N_HEADS] + gb[0:1]
    lf_cols = jax.nn.log_sigmoid(sm[:, SM_CF:SM_CF + N_HEADS] + gb[1:2])
    b_cols = sum(_dot(tri_lower, piece) for piece in _bf16_pieces(lf_cols))
    ones = jnp.ones((BF16_SUBLANES, rows), BF16)

    for h in range(N_HEADS):
        hs = slice(h * HEAD_DIM, (h + 1) * HEAD_DIM)
        b_row = b_rows[h:h + 1]
        src_row = ig_rows[N_HEADS + h:N_HEADS + h + 1] - b_row
        src_col = ig_cols[:, h:h + 1] - b_cols[:, h:h + 1]
        dmat = jnp.where(upper, b_row + src_col, NEG)
        a = b_row[:, rows - 1:rows]
        g_end = a + src_row
        m_loc = jnp.max(g_end, axis=-1, keepdims=True)
        w_end = jnp.exp(g_end - m_loc)

        state = c_st[h]
        m_in = m_st[h]
        inter = b_row + m_in
        m_t = jnp.maximum(inter, jnp.max(dmat, axis=0, keepdims=True))
        e_inter = jnp.exp(inter - m_t)
        qt = qt_all[hs].astype(BF16)
        kb = k_all[:, hs].astype(BF16)
        values = jnp.concatenate([vt_all[hs], ones], axis=0)
        p = (_dot(kb, qt) * jnp.exp(dmat - m_t)).astype(BF16)
        from_state = _dot(state.astype(BF16), qt)
        from_chunk = _dot(values, p)
        num = e_inter * from_state[0:HEAD_DIM] + from_chunk[0:HEAD_DIM]
        den = e_inter * from_state[HEAD_DIM:HEAD_DIM + 1] + from_chunk[HEAD_DIM:HEAD_DIM + 1]
        hh = num * (1.0 / jnp.maximum(jnp.abs(den), jnp.exp(-m_t)))
        hh = hh * lax.rsqrt(jnp.mean(hh * hh, axis=0, keepdims=True) + EPS) * hg_ref[hs, :]
        out_ref[0, hs, :] = (hh * ogt[hs]).astype(out_ref.dtype)

        m_new = jnp.maximum(a + m_in, m_loc)
        decay = jnp.exp(a + m_in - m_new)
        fresh = jnp.exp(m_loc - m_new)
        local = _dot((values.astype(F32) * w_end).astype(BF16), kb)
        c_st[h] = decay * state + fresh * local
        m_st[h] = m_new


def _mlstm(u, v, smalls, o_pre, conv_w, conv_b, wq, wk, gate_b, head_g, rows=256):
    bsz, seq, _ = u.shape
    rows = min(rows, seq)
    row = pl.BlockSpec((1, rows, GROUP_WIDTH), lambda b, c: (b, c, 0))
    const2 = lambda a: pl.BlockSpec(a.shape, lambda b, c: (0,) * a.ndim)
    conv_b = conv_b.reshape(1, GROUP_WIDTH)
    head_g = head_g.reshape(GROUP_WIDTH, 1)
    eye = jnp.eye(N_HEADS, dtype=wq.dtype)
    wqt = jnp.einsum('hde,hg->hegd', wq, eye).reshape(GROUP_WIDTH, GROUP_WIDTH).astype(BF16)
    wkb = jnp.einsum('hde,hg->hdge', wk, eye).reshape(GROUP_WIDTH, GROUP_WIDTH).astype(BF16)
    zeros4 = jnp.zeros((N_HEADS,), gate_b.dtype)
    gcol = jnp.stack([jnp.concatenate([zeros4, gate_b[0]]), jnp.concatenate([gate_b[1], zeros4])],
                     axis=1)
    return pl.pallas_call(
        _mlstm_kernel, grid=(bsz, seq // rows),
        in_specs=[row,
                  pl.BlockSpec((1, 8, GROUP_WIDTH),
                               lambda b, c: (b, jnp.maximum(c * (rows // 8) - 1, 0), 0)),
                  row,
                  pl.BlockSpec((1, rows, LANES), lambda b, c: (b, c, 0)),
                  row, const2(conv_w), const2(conv_b), const2(wqt), const2(wkb), const2(gate_b),
                  const2(gcol), const2(head_g)],
        out_specs=pl.BlockSpec((1, GROUP_WIDTH, rows), lambda b, c: (b, 0, c)),
        out_shape=jax.ShapeDtypeStruct((bsz, GROUP_WIDTH, seq), BF16),
        scratch_shapes=[pltpu.VMEM((N_HEADS, ACC_ROWS, HEAD_DIM), F32),
                        pltpu.VMEM((N_HEADS, 1, 1), F32)],
        compiler_params=_cparams(("parallel", "arbitrary")), name="mlstm",
    )(u, u, v, smalls, o_pre, conv_w, conv_b, wqt, wkb, gate_b, gcol, head_g)


def _out_ffn_kernel(x_ref, oa_ref, ob_ref, oc_ref, od_ref, wo_ref, g_ref, wg_ref, wu_ref, wd_ref,
                    gf_ref, y_ref, *, final, ff_chunk):
    mixed_t = jnp.concatenate([oa_ref[0], ob_ref[0], oc_ref[0], od_ref[0]], axis=0)
    x = x_ref[...] + lax.dot_general(mixed_t, wo_ref[...], (((0,), (0,)), ((), ())),
                                     preferred_element_type=F32)
    h = x * lax.rsqrt(jnp.mean(x * x, axis=-1, keepdims=True) + EPS)
    h = (h * g_ref[...]).astype(BF16)
    ffn = None
    for c0 in range(0, D_FF, ff_chunk):
        gate = _dot(h, wg_ref[:, c0:c0 + ff_chunk])
        up = _dot(h, wu_ref[:, c0:c0 + ff_chunk])
        act = (gate * jax.nn.sigmoid(gate) * up).astype(BF16)
        part = _dot(act, wd_ref[c0:c0 + ff_chunk, :])
        ffn = part if ffn is None else ffn + part
    y = x + ffn
    if final:
        y = y * lax.rsqrt(jnp.mean(y * y, axis=-1, keepdims=True) + EPS) * gf_ref[...]
    y_ref[...] = y


def _out_ffn(x2, o_a, o_b, o_c, o_d, w_out, gain, w_gate, w_up, w_down, gain_final, final, tm=512):
    n = x2.shape[0]
    seq = o_a.shape[2]
    tm = min(tm, seq)
    nblk_s = seq // tm
    row = lambda w: pl.BlockSpec((tm, w), lambda i: (i, 0))
    mixer = pl.BlockSpec((1, GROUP_WIDTH, tm), lambda i: (i // nblk_s, 0, i % nblk_s))
    const = lambda a: pl.BlockSpec(a.shape, lambda i: (0, 0), pipeline_mode=pl.Buffered(1))
    gain = gain.reshape(1, D_MODEL)
    gain_final = gain_final.reshape(1, D_MODEL)
    kern = functools.partial(_out_ffn_kernel, final=final, ff_chunk=256)
    return pl.pallas_call(
        kern, grid=(n // tm,),
        in_specs=[row(D_MODEL)] + [mixer] * 4
                 + [const(w_out), const(gain), const(w_gate), const(w_up), const(w_down),
                    const(gain_final)],
        out_specs=row(D_MODEL),
        out_shape=jax.ShapeDtypeStruct((n, D_MODEL), F32),
        compiler_params=_cparams(("parallel",)), name="out_ffn",
    )(x2, o_a, o_b, o_c, o_d, w_out, gain, w_gate, w_up, w_down, gain_final)


def kernel(x, norm_mix, w_in, nsa_cmp_pos, nsa_cmp_w, diff_lambda, diff_norm, mlstm_conv_w,
           mlstm_conv_b, mlstm_wq, mlstm_wk, mlstm_gate_b, mlstm_norm, w_out, norm_ffn, w_gate,
           w_up, w_down, norm_final):
    bsz, seq, _ = x.shape
    depth = w_in.shape[0]
    tables = _lane_tables(seq)
    x2 = x.reshape(bsz * seq, D_MODEL)
    r3 = lambda t: t.reshape(bsz, seq, t.shape[-1])

    for layer in range(depth):
        (a_q, a_qr, a_kvc, a_ks, a_vs, a_kw, a_vw, smalls, b_q, b_k, b_v, c_u, c_v, c_o,
         d_q, d_k, d_v) = _in_proj(x2, norm_mix[layer], w_in[layer], tables, seq,
                                   tm=min(512, seq))
        smalls3 = r3(smalls)

        kc, kvct = _compress(r3(a_kvc), nsa_cmp_w[layer], nsa_cmp_pos[layer])
        o_a = _nsa(a_q, a_qr, kc, kvct, r3(a_ks), a_vs, r3(a_kw), a_vw, smalls3)

        lam_init = 0.8 - 0.6 * math.exp(-0.3 * layer)
        o_b = _diff(b_q, r3(b_k), b_v, diff_lambda[layer], diff_norm[layer], lam_init)

        o_c = _mlstm(r3(c_u), r3(c_v), smalls3, r3(c_o),
                     mlstm_conv_w[layer], mlstm_conv_b[layer], mlstm_wq[layer], mlstm_wk[layer],
                     mlstm_gate_b[layer], mlstm_norm[layer])

        o_d = _dilated(d_q, r3(d_k), d_v)

        x2 = _out_ffn(x2, o_a, o_b, o_c, o_d, w_out[layer].astype(BF16), norm_ffn[layer],
                      w_gate[layer].astype(BF16), w_up[layer].astype(BF16),
                      w_down[layer].astype(BF16), norm_final, final=(layer == depth - 1))
    return x2.reshape(bsz, seq, D_MODEL)
```

```python
import functools
import math

import numpy as np
import jax
import jax.numpy as jnp
from jax import lax
from jax.experimental import pallas as pl
from jax.experimental.pallas import tpu as pltpu

F32 = jnp.float32
BF16 = jnp.bfloat16

D_MODEL = 1024
HEAD_DIM = 64
N_HEADS = 4
GROUP_WIDTH = N_HEADS * HEAD_DIM
ROPE_THETA = 10000.0
EPS = 1e-6
NEG = -1e30
LOG2E = math.log2(math.e)

CMP_LEN = 32
CMP_STRIDE = 16
SLC_LEN = 64
TOP_N = 16
NSA_WINDOW = 512
FORCED_LOCAL = 2
DIFF_HALF = HEAD_DIM // 2
MLSTM_CHUNK = 64
MLSTM_CONV = 4
DILATED_PATTERNS = ((128, 1), (512, 4), (2048, 16))
D_FF = ((8 * D_MODEL + 3 * 256 - 1) // (3 * 256)) * 256

LANES = 128
BF16_SUBLANES = 16
VMEM_LIMIT = 56 * 1024 * 1024
ACC_ROWS = HEAD_DIM + BF16_SUBLANES

IN_SPLITS = (
    GROUP_WIDTH, HEAD_DIM, HEAD_DIM, HEAD_DIM, HEAD_DIM, HEAD_DIM, HEAD_DIM, 3 * N_HEADS,
    GROUP_WIDTH, GROUP_WIDTH, GROUP_WIDTH,
    GROUP_WIDTH, GROUP_WIDTH, N_HEADS, N_HEADS, GROUP_WIDTH,
    GROUP_WIDTH, GROUP_WIDTH, GROUP_WIDTH,
)
(A_Q, A_KC, A_VC, A_KS, A_VS, A_KW, A_VW, A_G, B_Q, B_K, B_V,
 C_U, C_V, C_I, C_F, C_O, D_Q, D_K, D_V) = range(19)
SM_AG, SM_CI, SM_CF = 0, 12, 16


def _cparams(sem):
    return pltpu.CompilerParams(dimension_semantics=sem, vmem_limit_bytes=VMEM_LIMIT)


def _iota(shape, dim):
    return lax.broadcasted_iota(jnp.int32, shape, dim)


def _dot(a, b):
    return jnp.dot(a, b, preferred_element_type=F32)


def _bf16_pieces(x):
    hi = x.astype(BF16)
    rest = x - hi.astype(F32)
    mid = rest.astype(BF16)
    return hi, mid, (rest - mid.astype(F32)).astype(BF16)


PACKED_CHUNKS = 26
D_IN_PADDED = 3072


def _column_plan():
    offs = np.concatenate([[0], np.cumsum(IN_SPLITS)])
    order = [A_Q, A_KC, A_VC, A_KS, None, A_VS, None, A_KW, None, A_VW, None,
             A_G, C_I, C_F, ('pad', LANES - 20),
             B_Q, B_K, B_V, C_U, C_V, C_O, D_Q, D_K, D_V]
    src = []
    for item in order:
        if item is None:
            src += [-1] * HEAD_DIM
        elif isinstance(item, tuple):
            src += [-1] * item[1]
        else:
            src += list(range(int(offs[item]), int(offs[item + 1])))
    src = np.asarray(src)
    assert src.size == PACKED_CHUNKS * LANES
    terms, mats = [], []
    for j in range(PACKED_CHUNKS):
        cols = src[j * LANES:(j + 1) * LANES]
        todo = cols >= 0
        while todo.any():
            start = (cols[todo].min() // LANES) * LANES
            take = todo & (cols < start + 2 * LANES)
            sel = np.zeros((2 * LANES, LANES), np.float32)
            sel[cols[take] - start, np.nonzero(take)[0]] = 1.0
            terms.append((j, int(start)))
            mats.append(sel)
            todo &= ~take
    return tuple(terms), np.stack(mats)


def _rope_tables(seq, dim):
    inv = 1.0 / (ROPE_THETA ** (jnp.arange(0, dim, 2, dtype=F32) / dim))
    ang = jnp.arange(seq, dtype=F32)[:, None] * inv[None, :]
    return jnp.cos(ang), jnp.sin(ang)


def _lane_tables(seq):
    c64, s64 = _rope_tables(seq, HEAD_DIM)
    c32, s32 = _rope_tables(seq, DIFF_HALF)
    t64c = jnp.concatenate([c64, c64] * (LANES // HEAD_DIM), axis=1)
    t64s = jnp.concatenate([-s64, s64] * (LANES // HEAD_DIM), axis=1)
    t32c = jnp.concatenate([c32, c32] * (LANES // DIFF_HALF), axis=1)
    t32s = jnp.concatenate([-s32, s32] * (LANES // DIFF_HALF), axis=1)
    pos = np.arange(seq)[:, None]
    lane = np.arange(LANES)[None, :]
    onehot = ((lane >= HEAD_DIM) & ((pos // SLC_LEN) % HEAD_DIM == lane - HEAD_DIM)
              ).astype(np.float32)
    return t64c, t64s, t32c, t32s, jnp.asarray(onehot)


def _swap_halves(x, group):
    width = x.shape[-1]
    half = group // 2
    lane = _iota(x.shape, 1) & (group - 1)
    up = pltpu.roll(x, width - half, axis=1)
    down = pltpu.roll(x, half, axis=1)
    return jnp.where(lane < half, up, down)


def _rope(x, cos_t, sin_t, group):
    reps = x.shape[-1] // LANES
    if reps > 1:
        cos_t = jnp.concatenate([cos_t] * reps, axis=1)
        sin_t = jnp.concatenate([sin_t] * reps, axis=1)
    return x * cos_t + _swap_halves(x, group) * sin_t


def _in_proj_kernel(x_ref, g_ref, wraw_ref, sel_ref, c64_ref, s64_ref, c32_ref, s32_ref, oh_ref,
                    aq_ref, aqr_ref, akvc_ref, aks_ref, avs_ref, akw_ref, avw_ref, sm_ref,
                    bq_ref, bk_ref, bv_ref, cu_ref, cv_ref, co_ref, dq_ref, dk_ref, dv_ref,
                    w_ref, *, terms):
    @pl.when(pl.program_id(0) == 0)
    def _():
        for j in range(PACKED_CHUNKS):
            chunk = jnp.zeros((D_MODEL, LANES), F32)
            for t, (dst, start) in enumerate(terms):
                if dst == j:
                    chunk = chunk + _dot(wraw_ref[:, start:start + 2 * LANES], sel_ref[t])
            w_ref[:, j * LANES:(j + 1) * LANES] = chunk.astype(BF16)

    x = x_ref[...]
    h = x * lax.rsqrt(jnp.mean(x * x, axis=-1, keepdims=True) + EPS)
    h = (h * g_ref[...]).astype(BF16)
    c64, s64 = c64_ref[...], s64_ref[...]
    c32, s32 = c32_ref[...], s32_ref[...]

    def mm(c0, c1):
        return _dot(h, w_ref[:, c0 * LANES:c1 * LANES])

    def channel_major(ref, z, channels=None):
        zt = z.T
        ref[0] = (zt if channels is None else zt[0:channels]).astype(ref.dtype)

    zq = mm(0, 2) * (HEAD_DIM ** -0.5 * LOG2E)
    channel_major(aq_ref, zq)
    channel_major(aqr_ref, _rope(zq, c64, s64, HEAD_DIM))
    akvc_ref[...] = mm(2, 3).astype(BF16)
    aks_ref[...] = (_rope(mm(3, 4), c64, s64, HEAD_DIM) + oh_ref[...]).astype(BF16)
    channel_major(avs_ref, mm(4, 5), HEAD_DIM)
    akw_ref[...] = _rope(mm(5, 6), c64, s64, HEAD_DIM).astype(BF16)
    channel_major(avw_ref, mm(6, 7), HEAD_DIM)
    sm_ref[...] = mm(7, 8)
    channel_major(bq_ref, _rope(mm(8, 10), c32, s32, DIFF_HALF) * (DIFF_HALF ** -0.5 * LOG2E))
    bk_ref[...] = _rope(mm(10, 12), c32, s32, DIFF_HALF).astype(BF16)
    channel_major(bv_ref, mm(12, 14))
    cu_ref[...] = mm(14, 16)
    cv_ref[...] = mm(16, 18).astype(BF16)
    co_ref[...] = mm(18, 20)
    channel_major(dq_ref, _rope(mm(20, 22), c64, s64, HEAD_DIM) * (HEAD_DIM ** -0.5 * LOG2E))
    dk_ref[...] = _rope(mm(22, 24), c64, s64, HEAD_DIM).astype(BF16)
    channel_major(dv_ref, mm(24, 26))


_IN_PROJ_OUTS = (
    (256, BF16, True), (256, BF16, True), (128, BF16, False), (128, BF16, False),
    (HEAD_DIM, BF16, True), (128, BF16, False), (HEAD_DIM, BF16, True), (128, F32, False),
    (256, BF16, True), (256, BF16, False), (256, BF16, True), (256, F32, False),
    (256, BF16, False), (256, F32, False), (256, BF16, True), (256, BF16, False),
    (256, BF16, True))


def _in_proj(x2, gain, w, tables, seq, tm):
    n = x2.shape[0]
    nblk_s = seq // tm
    terms, select = _column_plan()
    select = jnp.asarray(select, BF16)
    w_raw = jnp.pad(w.astype(BF16), ((0, 0), (0, D_IN_PADDED - w.shape[1])))
    row = lambda i: (i, 0)
    tab = lambda i: (i % nblk_s, 0)
    const = lambda i: (0, 0)
    in_specs = [pl.BlockSpec((tm, D_MODEL), row),
                pl.BlockSpec((1, D_MODEL), const),
                pl.BlockSpec(w_raw.shape, const),
                pl.BlockSpec(select.shape, lambda i: (0, 0, 0))]
    in_specs += [pl.BlockSpec((tm, LANES), tab)] * 5
    out_specs = [pl.BlockSpec((1, w, tm), lambda i: (i // nblk_s, 0, i % nblk_s)) if cmaj
                 else pl.BlockSpec((tm, w), row) for w, _, cmaj in _IN_PROJ_OUTS]
    out_shape = [jax.ShapeDtypeStruct((n // seq, w, seq) if cmaj else (n, w), dt)
                 for w, dt, cmaj in _IN_PROJ_OUTS]
    return pl.pallas_call(
        functools.partial(_in_proj_kernel, terms=terms), grid=(n // tm,), in_specs=in_specs,
        out_specs=out_specs, out_shape=out_shape,
        scratch_shapes=[pltpu.VMEM((D_MODEL, PACKED_CHUNKS * LANES), BF16)],
        compiler_params=_cparams(("arbitrary",)), name="in_proj",
    )(x2, gain.reshape(1, D_MODEL), w_raw, select, *tables)


def _compress_kernel(r_ref, w_ref, pos_ref, kc_ref, kvct_ref):
    wk = w_ref[0].reshape(CMP_LEN, HEAD_DIM, HEAD_DIM)
    wv = w_ref[1].reshape(CMP_LEN, HEAD_DIM, HEAD_DIM)
    zeros = jnp.zeros_like(wk)
    full = jnp.concatenate([jnp.concatenate([wk, zeros], axis=2),
                            jnp.concatenate([zeros, wv], axis=2)], axis=1)
    w_first = full[0:CMP_STRIDE].reshape(CMP_STRIDE * LANES, LANES).astype(BF16)
    w_second = full[CMP_STRIDE:].reshape(CMP_STRIDE * LANES, LANES).astype(BF16)

    r = r_ref[0]
    first = _dot(r, w_first)
    second = _dot(r, w_second)
    nrow = first.shape[0]
    pos = pos_ref[...].astype(BF16)
    half = pos.shape[1] // 2
    const = _dot(pos[:, :half], w_first) + _dot(pos[:, half:], w_second)
    out = first + pltpu.roll(second, nrow - 1, axis=0) + const[0:1, :]
    lane = _iota(out.shape, 1)
    kc_ref[0] = jnp.where(lane < HEAD_DIM, out, 0.0).astype(BF16)
    kvct_ref[0] = out.T.astype(BF16)


def _compress(a_kvc, cmp_w, cmp_pos):
    bsz, seq, _ = a_kvc.shape
    nrow = seq // CMP_STRIDE
    r = a_kvc.reshape(bsz, nrow, CMP_STRIDE * LANES)
    pos = jnp.concatenate([cmp_pos[0], cmp_pos[1]], axis=-1)
    pos = jnp.broadcast_to(pos.reshape(1, CMP_LEN * LANES), (8, CMP_LEN * LANES))
    return pl.pallas_call(
        _compress_kernel, grid=(bsz,),
        in_specs=[pl.BlockSpec((1, nrow, CMP_STRIDE * LANES), lambda b: (b, 0, 0)),
                  pl.BlockSpec(cmp_w.shape, lambda b: (0, 0, 0)),
                  pl.BlockSpec(pos.shape, lambda b: (0, 0))],
        out_specs=[pl.BlockSpec((1, nrow, LANES), lambda b: (b, 0, 0)),
                   pl.BlockSpec((1, LANES, nrow), lambda b: (b, 0, 0))],
        out_shape=[jax.ShapeDtypeStruct((bsz, nrow, LANES), BF16),
                   jax.ShapeDtypeStruct((bsz, LANES, nrow), BF16)],
        compiler_params=_cparams(("parallel",)), name="nsa_compress",
    )(r, cmp_w, pos)


def _flash_init(m_ref, acc_ref):
    m_ref[...] = jnp.full(m_ref.shape, NEG, F32)
    acc_ref[...] = jnp.zeros(acc_ref.shape, F32)


def _flash_step(s, values, m_ref, acc_ref):
    width = acc_ref.shape[1]
    chunk = min(2 * LANES, width)
    for g, vals in enumerate(values):
        rows = slice(g * ACC_ROWS, (g + 1) * ACC_ROWS)
        for c0 in range(0, width, chunk):
            lanes = slice(g * width + c0, g * width + c0 + chunk)
            s_c = s[:, lanes]
            m_old = m_ref[:, lanes]
            m_new = jnp.maximum(m_old, jnp.max(s_c, axis=0, keepdims=True))
            alpha = jnp.exp2(m_old - m_new)
            p = jnp.exp2(s_c - m_new).astype(BF16)
            acc_ref[rows, c0:c0 + chunk] = alpha * acc_ref[rows, c0:c0 + chunk] + _dot(vals, p)
            m_ref[:, lanes] = m_new


def _pipelined_tiles(lo, hi, scores, consume, sa_ref, sb_ref):
    n = hi - lo

    def put(ref, tiles):
        for g, tile in enumerate(tiles):
            ref[g] = tile

    def get(ref):
        return [ref[g] for g in range(ref.shape[0])]

    put(sa_ref, scores(lo))

    def body(i, carry):
        j = lo + 2 * i
        put(sb_ref, scores(j + 1))
        consume(get(sa_ref), j, False)
        put(sa_ref, scores(j + 2))
        consume(get(sb_ref), j + 1, False)
        return carry

    lax.fori_loop(0, n // 2, body, 0)

    @pl.when(n % 2 == 0)
    def _():
        consume(get(sa_ref), hi, True)

    @pl.when(n % 2 == 1)
    def _():
        put(sb_ref, scores(hi))
        consume(get(sa_ref), hi - 1, False)
        consume(get(sb_ref), hi, True)


def _with_ones(vt):
    return jnp.concatenate([vt, jnp.ones((BF16_SUBLANES, vt.shape[1]), BF16)], axis=0)


def _cmp_to_slc_t(seq):
    n_cmp = (seq - CMP_LEN) // CMP_STRIDE + 1
    n_slc = seq // SLC_LEN
    ratio_s, ratio_c = SLC_LEN // CMP_STRIDE, CMP_LEN // CMP_STRIDE
    jj = np.arange(n_slc)[:, None, None]
    src = ratio_s * jj - np.arange(ratio_s)[None, :, None] - np.arange(ratio_c)[None, None, :]
    ok = (src >= 0) & (src < n_cmp)
    m = np.zeros((seq // CMP_STRIDE, n_slc), np.float32)
    np.add.at(m, (np.where(ok, src, 0), np.broadcast_to(jj, src.shape)), ok.astype(np.float32))
    return jnp.asarray(m.T, BF16)


def _nsa_kernel(qt_ref, qrt_ref, kc_ref, kvct_ref, ks_ref, vst_ref, kw_ref, vwt_ref, sm_ref,
                c2st_ref, wmask_ref, o_ref, m_ref, acc_ref, sa_ref, sb_ref, *, tq, top_n):
    qi = pl.program_id(1)
    s0 = qi * tq
    nb = qt_ref.shape[0]
    rows = N_HEADS * tq
    n_slc = c2st_ref.shape[0]
    lane_t = s0 + (_iota((1, rows), 1) & (tq - 1))
    c2st = c2st_ref[...]
    blk = _iota((n_slc, 1), 0)
    cur = (s0 + _iota((1, tq), 1)) >> 6
    forced = (blk == 0) | ((blk <= cur) & (blk > cur - FORCED_LOCAL))

    def heads_on_lanes(x):
        return jnp.concatenate([x[h * HEAD_DIM:(h + 1) * HEAD_DIM, :] for h in range(N_HEADS)],
                               axis=1)

    def compressed_and_selection(b):
        q4 = jnp.concatenate([heads_on_lanes(qt_ref[b]), jnp.zeros((HEAD_DIM, rows), BF16)], axis=0)
        sc = _dot(kc_ref[b], q4)
        cmask = (_iota((sc.shape[0], 1), 0) * CMP_STRIDE + (CMP_LEN - 1)) <= lane_t
        sc = jnp.where(cmask, sc, NEG)
        e = jnp.where(cmask, jnp.exp2(sc - jnp.max(sc, axis=0, keepdims=True)), 0.0)
        z = jnp.sum(e, axis=0, keepdims=True)
        p_cmp = e * (1.0 / jnp.where(z > 0, z, 1.0))
        o_cmp = _dot(kvct_ref[b][HEAD_DIM:2 * HEAD_DIM, :], p_cmp.astype(BF16))
        p_heads = p_cmp[:, 0:tq]
        for h in range(1, N_HEADS):
            p_heads = p_heads + p_cmp[:, h * tq:(h + 1) * tq]
        p_hi = p_heads.astype(BF16)
        p_lo = (p_heads - p_hi.astype(F32)).astype(BF16)
        imp = _dot(c2st, p_hi) + _dot(c2st, p_lo)

        score = jnp.where(blk > cur, -1.0e6, jnp.where(forced, 1.0e6, imp))
        rank = jnp.zeros((n_slc, tq), F32)
        for i in range(n_slc):
            s_i = score[i:i + 1, :]
            rank = rank + jnp.where(blk > i, jnp.where(s_i >= score, 1.0, 0.0),
                                    jnp.where(s_i > score, 1.0, 0.0))
        sel = (rank < top_n) & (blk <= cur)
        bias = jnp.where(sel, 0.0, NEG)
        if n_slc < HEAD_DIM:
            bias = jnp.concatenate([bias, jnp.zeros((HEAD_DIM - n_slc, tq), F32)], axis=0)
        bias4 = jnp.concatenate([bias] * N_HEADS, axis=1).astype(BF16)
        qsel = jnp.concatenate([heads_on_lanes(qrt_ref[b]), bias4], axis=0)
        return o_cmp, qsel

    prepared = [compressed_and_selection(b) for b in range(nb)]

    def attend(k_ref, vt_ref, lo, window):
        _flash_init(m_ref, acc_ref)

        def scores(j):
            k0 = pl.multiple_of(j * tq, tq)
            return [_dot(k_ref[b, pl.ds(k0, tq), :], prepared[b][1]) for b in range(nb)]

        def consume(tiles, j, diagonal):
            k0 = pl.multiple_of(j * tq, tq)
            if window:
                far = qi - NSA_WINDOW // tq
                mask = wmask_ref[jnp.where(j == qi, 0, jnp.where(j == far, 2, 1))]
            elif diagonal:
                mask = wmask_ref[0]
            for b, s in enumerate(tiles):
                if window or diagonal:
                    s = s + mask
                _flash_step(s, [_with_ones(vt_ref[b, :, pl.ds(k0, tq)])], m_ref.at[b], acc_ref.at[b])

        _pipelined_tiles(lo, qi, scores, consume, sa_ref, sb_ref)
        return [acc_ref[b, 0:HEAD_DIM, :] * (1.0 / acc_ref[b, HEAD_DIM:HEAD_DIM + 1, :])
                for b in range(nb)]

    o_slc = attend(ks_ref, vst_ref, 0, False)
    o_win = attend(kw_ref, vwt_ref, jnp.maximum(qi - NSA_WINDOW // tq, 0), True)

    for b in range(nb):
        g = jax.nn.sigmoid(sm_ref[b].T[0:BF16_SUBLANES, :])

        def gate(branch):
            return jnp.concatenate(
                [g[branch * N_HEADS + h:branch * N_HEADS + h + 1, :] for h in range(N_HEADS)], axis=1)

        o = gate(0) * prepared[b][0] + gate(1) * o_slc[b] + gate(2) * o_win[b]
        for h in range(N_HEADS):
            o_ref[b, h * HEAD_DIM:(h + 1) * HEAD_DIM, :] = (
                o[:, h * tq:(h + 1) * tq].astype(o_ref.dtype))


def _nsa(qt, qrt, kc, kvct, ks, vst, kw, vwt, smalls, tq=256):
    bsz, _, seq = qt.shape
    tq = min(tq, seq)
    nb = 2 if bsz % 2 == 0 else 1
    n_slc = seq // SLC_LEN
    c2st = _cmp_to_slc_t(seq)
    rows = N_HEADS * tq
    qspec = pl.BlockSpec((nb, GROUP_WIDTH, tq), lambda b, i: (b, 0, i))
    full = lambda a: pl.BlockSpec((nb,) + a.shape[1:], lambda b, i: (b, 0, 0))
    vspec = pl.BlockSpec((nb, HEAD_DIM, seq), lambda b, i: (b, 0, 0))
    assert NSA_WINDOW % tq == 0
    k_off = np.arange(tq)[:, None]
    q_off = np.tile(np.arange(tq), N_HEADS)[None, :]
    visible = np.stack([k_off <= q_off, np.ones((tq, rows), bool), k_off > q_off])
    wmask = jnp.asarray(np.where(visible, 0.0, NEG).astype(np.float32))
    kern = functools.partial(_nsa_kernel, tq=tq, top_n=min(TOP_N, n_slc))
    return pl.pallas_call(
        kern, grid=(bsz // nb, seq // tq),
        in_specs=[qspec, qspec, full(kc), full(kvct), full(ks), vspec, full(kw), vspec,
                  pl.BlockSpec((nb, tq, LANES), lambda b, i: (b, i, 0)),
                  pl.BlockSpec(c2st.shape, lambda b, i: (0, 0)),
                  pl.BlockSpec(wmask.shape, lambda b, i: (0, 0, 0), pipeline_mode=pl.Buffered(1))],
        out_specs=qspec,
        out_shape=jax.ShapeDtypeStruct(qt.shape, BF16),
        scratch_shapes=[pltpu.VMEM((nb, 1, rows), F32), pltpu.VMEM((nb, ACC_ROWS, rows), F32),
                        pltpu.VMEM((nb, tq, rows), F32), pltpu.VMEM((nb, tq, rows), F32)],
        compiler_params=_cparams(("parallel", "arbitrary")), name="nsa_attention",
    )(qt, qrt, kc, kvct, ks, vst, kw, vwt, smalls, c2st, wmask)


def _diff_kernel(qt_ref, k_ref, vt_ref, lam_ref, g_ref, o_ref, m_ref, acc_ref, sa_ref, sb_ref,
                 *, tq, lam_init):
    qi = pl.program_id(1)
    s0 = qi * tq
    rows = 4 * tq
    pairs = N_HEADS // 2
    pw = 2 * HEAD_DIM
    row = _iota((pw, 1), 0)
    lane_t = s0 + (_iota((1, rows), 1) & (tq - 1))

    def query_matrix(p):
        qt = qt_ref[0, p * pw:(p + 1) * pw, :]
        zero = jnp.zeros_like(qt)
        return jnp.concatenate(
            [jnp.where((row >= DIFF_HALF * c) & (row < DIFF_HALF * (c + 1)), qt, zero)
             for c in range(4)], axis=1)

    qmats = [query_matrix(p) for p in range(pairs)]

    def scores(j):
        k0 = pl.multiple_of(j * tq, tq)
        return [_dot(k_ref[0, pl.ds(k0, tq), p * pw:(p + 1) * pw], qmats[p]) for p in range(pairs)]

    def consume(tiles, j, diagonal):
        k0 = pl.multiple_of(j * tq, tq)
        for p, s in enumerate(tiles):
            if diagonal:
                s = jnp.where(k0 + _iota((tq, 1), 0) <= lane_t, s, NEG)
            vt = vt_ref[0, p * pw:(p + 1) * pw, pl.ds(k0, tq)]
            _flash_step(s, [_with_ones(vt[0:HEAD_DIM]), _with_ones(vt[HEAD_DIM:])],
                        m_ref.at[p], acc_ref.at[p])

    _flash_init(m_ref, acc_ref)
    _pipelined_tiles(0, qi, scores, consume, sa_ref, sb_ref)

    lv = lam_ref[...]
    lam = (jnp.exp(jnp.sum(lv[0:1] * lv[1:2], axis=-1, keepdims=True))
           - jnp.exp(jnp.sum(lv[2:3] * lv[3:4], axis=-1, keepdims=True)) + lam_init)
    for h in range(N_HEADS):
        a = acc_ref[h // 2, ACC_ROWS * (h % 2):ACC_ROWS * (h % 2 + 1), :]
        o_all = a[0:HEAD_DIM] * (1.0 / a[HEAD_DIM:HEAD_DIM + 1])
        o = o_all[:, :tq] - lam * o_all[:, tq:]
        y = o * lax.rsqrt(jnp.mean(o * o, axis=0, keepdims=True) + EPS)
        o_ref[0, HEAD_DIM * h:HEAD_DIM * (h + 1), :] = (
            (y * g_ref[...]) * (1.0 - lam_init)).astype(o_ref.dtype)


def _diff(qt, k, vt, lam_vecs, sub_g, lam_init, tq=256):
    bsz, _, seq = qt.shape
    tq = min(tq, seq)
    pairs = N_HEADS // 2
    qspec = pl.BlockSpec((1, GROUP_WIDTH, tq), lambda b, i: (b, 0, i))
    kern = functools.partial(_diff_kernel, tq=tq, lam_init=lam_init)
    return pl.pallas_call(
        kern, grid=(bsz, seq // tq),
        in_specs=[qspec,
                  pl.BlockSpec((1, seq, GROUP_WIDTH), lambda b, i: (b, 0, 0)),
                  pl.BlockSpec((1, GROUP_WIDTH, seq), lambda b, i: (b, 0, 0)),
                  pl.BlockSpec(lam_vecs.shape, lambda b, i: (0, 0)),
                  pl.BlockSpec((HEAD_DIM, 1), lambda b, i: (0, 0))],
        out_specs=qspec,
        out_shape=jax.ShapeDtypeStruct(qt.shape, BF16),
        scratch_shapes=[pltpu.VMEM((pairs, 1, 4 * tq), F32),
                        pltpu.VMEM((pairs, 2 * ACC_ROWS, 2 * tq), F32),
                        pltpu.VMEM((pairs, tq, 4 * tq), F32), pltpu.VMEM((pairs, tq, 4 * tq), F32)],
        compiler_params=_cparams(("parallel", "arbitrary")), name="diff_attention",
    )(qt, k, vt, lam_vecs, sub_g.reshape(HEAD_DIM, 1))


def _dilated_bias(tq):
    max_back = max(w for w, _ in DILATED_PATTERNS) // tq
    classes = [0, 1, 2, 3, max_back]
    q = np.arange(tq)[None, :]
    k = np.arange(tq)[:, None]
    out = []
    for d in classes:
        delta = d * tq + q - k
        cnt = np.zeros((tq, tq), np.float64)
        for w, dil in DILATED_PATTERNS:
            cnt += (delta >= 0) & (delta <= w) & (delta % dil == 0)
        tab = np.where(cnt > 0, np.log2(np.maximum(cnt, 1.0)), NEG)
        out.append(np.concatenate([tab, tab], axis=1))
    return jnp.asarray(np.stack(out).astype(np.float32)), max_back


def _dilated_kernel(qt_ref, k_ref, vt_ref, bias_ref, o_ref, m_ref, acc_ref, sa_ref, sb_ref,
                    *, tq, max_back):
    qi = pl.program_id(1)
    nb = qt_ref.shape[0]
    pairs = N_HEADS // 2
    pw = 2 * HEAD_DIM
    row = _iota((pw, 1), 0)
    streams = [(b, p) for b in range(nb) for p in range(pairs)]

    def query_matrix(b, p):
        qt = qt_ref[b, p * pw:(p + 1) * pw, :]
        zero = jnp.zeros_like(qt)
        return jnp.concatenate([jnp.where(row < HEAD_DIM, qt, zero),
                                jnp.where(row >= HEAD_DIM, qt, zero)], axis=1)

    qmats = [query_matrix(b, p) for b, p in streams]
    _flash_init(m_ref, acc_ref)

    def scores(j):
        k0 = pl.multiple_of(j * tq, tq)
        return [_dot(k_ref[b, pl.ds(k0, tq), p * pw:(p + 1) * pw], qmats[g])
                for g, (b, p) in enumerate(streams)]

    def consume(tiles, j, diagonal):
        del diagonal
        k0 = pl.multiple_of(j * tq, tq)
        d = qi - j
        bias = bias_ref[jnp.where(d < 3, d, jnp.where(d == max_back, 4, 3))]
        for g, (b, p) in enumerate(streams):
            vt = vt_ref[b, p * pw:(p + 1) * pw, pl.ds(k0, tq)]
            _flash_step(tiles[g] + bias, [_with_ones(vt[0:HEAD_DIM]), _with_ones(vt[HEAD_DIM:])],
                        m_ref.at[g], acc_ref.at[g])

    _pipelined_tiles(jnp.maximum(qi - max_back, 0), qi, scores, consume, sa_ref, sb_ref)
    for g, (b, p) in enumerate(streams):
        for hh in range(2):
            a = acc_ref[g, ACC_ROWS * hh:ACC_ROWS * (hh + 1), :]
            h = 2 * p + hh
            o_ref[b, HEAD_DIM * h:HEAD_DIM * (h + 1), :] = (
                a[0:HEAD_DIM] * (1.0 / a[HEAD_DIM:HEAD_DIM + 1])).astype(o_ref.dtype)


def _dilated(qt, k, vt, tq=256):
    bsz, _, seq = qt.shape
    tq = min(tq, seq)
    nb = 1
    streams = nb * (N_HEADS // 2)
    bias, max_back = _dilated_bias(tq)
    qspec = pl.BlockSpec((nb, GROUP_WIDTH, tq), lambda b, i: (b, 0, i))
    kern = functools.partial(_dilated_kernel, tq=tq, max_back=max_back)
    return pl.pallas_call(
        kern, grid=(bsz // nb, seq // tq),
        in_specs=[qspec,
                  pl.BlockSpec((nb, seq, GROUP_WIDTH), lambda b, i: (b, 0, 0)),
                  pl.BlockSpec((nb, GROUP_WIDTH, seq), lambda b, i: (b, 0, 0)),
                  pl.BlockSpec(bias.shape, lambda b, i: (0, 0, 0))],
        out_specs=qspec,
        out_shape=jax.ShapeDtypeStruct(qt.shape, BF16),
        scratch_shapes=[pltpu.VMEM((streams, 1, 2 * tq), F32),
                        pltpu.VMEM((streams, 2 * ACC_ROWS, tq), F32),
                        pltpu.VMEM((streams, tq, 2 * tq), F32),
                        pltpu.VMEM((streams, tq, 2 * tq), F32)],
        compiler_params=_cparams(("parallel", "arbitrary")), name="dilated_attention",
    )(qt, k, vt, bias)


def _mlstm_kernel(u_ref, up_ref, v_ref, sm_ref, o_ref, cw_ref, cb_ref, wqt_ref, wk_ref,
                  gb_ref, gcol_ref, hg_ref, out_ref, c_st, m_st):
    ci = pl.program_id(1)
    rows = u_ref.shape[1]

    @pl.when(ci == 0)
    def _():
        c_st[...] = jnp.zeros(c_st.shape, F32)
        m_st[...] = jnp.zeros(m_st.shape, F32)

    tail = jnp.where(ci > 0, up_ref[0], 0.0)
    ext = jnp.concatenate([tail, u_ref[0]], axis=0)
    cw = cw_ref[...]
    uc = cb_ref[...] + cw[MLSTM_CONV - 1:MLSTM_CONV] * ext[8:]
    for j in range(MLSTM_CONV - 1):
        shifted = pltpu.roll(ext, MLSTM_CONV - 1 - j, axis=0)[8:]
        uc = uc + cw[j:j + 1] * shifted
    uc = uc * jax.nn.sigmoid(uc)

    qt_all = _dot(wqt_ref[...], uc.T.astype(BF16))
    k_all = _dot(uc.astype(BF16), wk_ref[...]) * (HEAD_DIM ** -0.5)
    vt_all = v_ref[0].astype(F32).T.astype(BF16)
    ogt = jax.nn.sigmoid(o_ref[0]).T
    sm = sm_ref[0]
    smt = sm.T
    gb = gb_ref[...]
    gcol = gcol_ref[...]
    upper = _iota((rows, rows), 0) <= _iota((rows, rows), 1)
    lower = _iota((rows, rows), 1) <= _iota((rows, rows), 0)
    ig_rows = smt[8:16] + gcol[:, 0:1]
    lf_rows = jax.nn.log_sigmoid(smt[16:24] + gcol[:, 1:2])
    tri_upper = jnp.where(upper, 1.0, 0.0).astype(BF16)
    tri_lower = jnp.where(lower, 1.0, 0.0).astype(BF16)
    b_rows = sum(_dot(piece, tri_upper) for piece in _bf16_pieces(lf_rows))
    ig_cols = sm[:, SM_CI:SM_CI + N_HEADS] + gb[0:1]
    lf_cols = jax.nn.log_sigmoid(sm[:, SM_CF:SM_CF + N_HEADS] + gb[1:2])
    b_cols = sum(_dot(tri_lower, piece) for piece in _bf16_pieces(lf_cols))
    ones = jnp.ones((BF16_SUBLANES, rows), BF16)

    for h in range(N_HEADS):
        hs = slice(h * HEAD_DIM, (h + 1) * HEAD_DIM)
        b_row = b_rows[h:h + 1]
        src_row = ig_rows[N_HEADS + h:N_HEADS + h + 1] - b_row
        src_col = ig_cols[:, h:h + 1] - b_cols[:, h:h + 1]
        dmat = jnp.where(upper, b_row + src_col, NEG)
        a = b_row[:, rows - 1:rows]
        g_end = a + src_row
        m_loc = jnp.max(g_end, axis=-1, keepdims=True)
        w_end = jnp.exp(g_end - m_loc)

        state = c_st[h]
        m_in = m_st[h]
        inter = b_row + m_in
        m_t = jnp.maximum(inter, jnp.max(dmat, axis=0, keepdims=True))
        e_inter = jnp.exp(inter - m_t)
        qt = qt_all[hs].astype(BF16)
        kb = k_all[:, hs].astype(BF16)
        values = jnp.concatenate([vt_all[hs], ones], axis=0)
        p = (_dot(kb, qt) * jnp.exp(dmat - m_t)).astype(BF16)
        from_state = _dot(state.astype(BF16), qt)
        from_chunk = _dot(values, p)
        num = e_inter * from_state[0:HEAD_DIM] + from_chunk[0:HEAD_DIM]
        den = e_inter * from_state[HEAD_DIM:HEAD_DIM + 1] + from_chunk[HEAD_DIM:HEAD_DIM + 1]
        hh = num * (1.0 / jnp.maximum(jnp.abs(den), jnp.exp(-m_t)))
        hh = hh * lax.rsqrt(jnp.mean(hh * hh, axis=0, keepdims=True) + EPS) * hg_ref[hs, :]
        out_ref[0, hs, :] = (hh * ogt[hs]).astype(out_ref.dtype)

        m_new = jnp.maximum(a + m_in, m_loc)
        decay = jnp.exp(a + m_in - m_new)
        fresh = jnp.exp(m_loc - m_new)
        local = _dot((values.astype(F32) * w_end).astype(BF16), kb)
        c_st[h] = decay * state + fresh * local
        m_st[h] = m_new


def _mlstm(u, v, smalls, o_pre, conv_w, conv_b, wq, wk, gate_b, head_g, rows=256):
    bsz, seq, _ = u.shape
    rows = min(rows, seq)
    row = pl.BlockSpec((1, rows, GROUP_WIDTH), lambda b, c: (b, c, 0))
    const2 = lambda a: pl.BlockSpec(a.shape, lambda b, c: (0,) * a.ndim)
    conv_b = conv_b.reshape(1, GROUP_WIDTH)
    head_g = head_g.reshape(GROUP_WIDTH, 1)
    eye = jnp.eye(N_HEADS, dtype=wq.dtype)
    wqt = jnp.einsum('hde,hg->hegd', wq, eye).reshape(GROUP_WIDTH, GROUP_WIDTH).astype(BF16)
    wkb = jnp.einsum('hde,hg->hdge', wk, eye).reshape(GROUP_WIDTH, GROUP_WIDTH).astype(BF16)
    zeros4 = jnp.zeros((N_HEADS,), gate_b.dtype)
    gcol = jnp.stack([jnp.concatenate([zeros4, gate_b[0]]), jnp.concatenate([gate_b[1], zeros4])],
                     axis=1)
    return pl.pallas_call(
        _mlstm_kernel, grid=(bsz, seq // rows),
        in_specs=[row,
                  pl.BlockSpec((1, 8, GROUP_WIDTH),
                               lambda b, c: (b, jnp.maximum(c * (rows // 8) - 1, 0), 0)),
                  row,
                  pl.BlockSpec((1, rows, LANES), lambda b, c: (b, c, 0)),
                  row, const2(conv_w), const2(conv_b), const2(wqt), const2(wkb), const2(gate_b),
                  const2(gcol), const2(head_g)],
        out_specs=pl.BlockSpec((1, GROUP_WIDTH, rows), lambda b, c: (b, 0, c)),
        out_shape=jax.ShapeDtypeStruct((bsz, GROUP_WIDTH, seq), BF16),
        scratch_shapes=[pltpu.VMEM((N_HEADS, ACC_ROWS, HEAD_DIM), F32),
                        pltpu.VMEM((N_HEADS, 1, 1), F32)],
        compiler_params=_cparams(("parallel", "arbitrary")), name="mlstm",
    )(u, u, v, smalls, o_pre, conv_w, conv_b, wqt, wkb, gate_b, gcol, head_g)


def _out_ffn_kernel(x_ref, oa_ref, ob_ref, oc_ref, od_ref, wo_ref, g_ref, wg_ref, wu_ref, wd_ref,
                    gf_ref, y_ref, *, final, ff_chunk):
    mixed_t = jnp.concatenate([oa_ref[0], ob_ref[0], oc_ref[0], od_ref[0]], axis=0)
    x = x_ref[...] + lax.dot_general(mixed_t, wo_ref[...], (((0,), (0,)), ((), ())),
                                     preferred_element_type=F32)
    h = x * lax.rsqrt(jnp.mean(x * x, axis=-1, keepdims=True) + EPS)
    h = (h * g_ref[...]).astype(BF16)
    ffn = None
    for c0 in range(0, D_FF, ff_chunk):
        gate = _dot(h, wg_ref[:, c0:c0 + ff_chunk])
        up = _dot(h, wu_ref[:, c0:c0 + ff_chunk])
        act = (gate * jax.nn.sigmoid(gate) * up).astype(BF16)
        part = _dot(act, wd_ref[c0:c0 + ff_chunk, :])
        ffn = part if ffn is None else ffn + part
    y = x + ffn
    if final:
        y = y * lax.rsqrt(jnp.mean(y * y, axis=-1, keepdims=True) + EPS) * gf_ref[...]
    y_ref[...] = y


def _out_ffn(x2, o_a, o_b, o_c, o_d, w_out, gain, w_gate, w_up, w_down, gain_final, final, tm=512):
    n = x2.shape[0]
    seq = o_a.shape[2]
    tm = min(tm, seq)
    nblk_s = seq // tm
    row = lambda w: pl.BlockSpec((tm, w), lambda i: (i, 0))
    mixer = pl.BlockSpec((1, GROUP_WIDTH, tm), lambda i: (i // nblk_s, 0, i % nblk_s))
    const = lambda a: pl.BlockSpec(a.shape, lambda i: (0, 0), pipeline_mode=pl.Buffered(1))
    gain = gain.reshape(1, D_MODEL)
    gain_final = gain_final.reshape(1, D_MODEL)
    kern = functools.partial(_out_ffn_kernel, final=final, ff_chunk=256)
    return pl.pallas_call(
        kern, grid=(n // tm,),
        in_specs=[row(D_MODEL)] + [mixer] * 4
                 + [const(w_out), const(gain), const(w_gate), const(w_up), const(w_down),
                    const(gain_final)],
        out_specs=row(D_MODEL),
        out_shape=jax.ShapeDtypeStruct((n, D_MODEL), F32),
        compiler_params=_cparams(("parallel",)), name="out_ffn",
    )(x2, o_a, o_b, o_c, o_d, w_out, gain, w_gate, w_up, w_down, gain_final)


def kernel(x, norm_mix, w_in, nsa_cmp_pos, nsa_cmp_w, diff_lambda, diff_norm, mlstm_conv_w,
           mlstm_conv_b, mlstm_wq, mlstm_wk, mlstm_gate_b, mlstm_norm, w_out, norm_ffn, w_gate,
           w_up, w_down, norm_final):
    bsz, seq, _ = x.shape
    depth = w_in.shape[0]
    tables = _lane_tables(seq)
    x2 = x.reshape(bsz * seq, D_MODEL)
    r3 = lambda t: t.reshape(bsz, seq, t.shape[-1])

    for layer in range(depth):
        (a_q, a_qr, a_kvc, a_ks, a_vs, a_kw, a_vw, smalls, b_q, b_k, b_v, c_u, c_v, c_o,
         d_q, d_k, d_v) = _in_proj(x2, norm_mix[layer], w_in[layer], tables, seq,
                                   tm=min(512, seq))
        smalls3 = r3(smalls)

        kc, kvct = _compress(r3(a_kvc), nsa_cmp_w[layer], nsa_cmp_pos[layer])
        o_a = _nsa(a_q, a_qr, kc, kvct, r3(a_ks), a_vs, r3(a_kw), a_vw, smalls3)

        lam_init = 0.8 - 0.6 * math.exp(-0.3 * layer)
        o_b = _diff(b_q, r3(b_k), b_v, diff_lambda[layer], diff_norm[layer], lam_init)

        o_c = _mlstm(r3(c_u), r3(c_v), smalls3, r3(c_o),
                     mlstm_conv_w[layer], mlstm_conv_b[layer], mlstm_wq[layer], mlstm_wk[layer],
                     mlstm_gate_b[layer], mlstm_norm[layer])

        o_d = _dilated(d_q, r3(d_k), d_v)

        x2 = _out_ffn(x2, o_a, o_b, o_c, o_d, w_out[layer].astype(BF16), norm_ffn[layer],
                      w_gate[layer].astype(BF16), w_up[layer].astype(BF16),
                      w_down[layer].astype(BF16), norm_final, final=(layer == depth - 1))
    return x2.reshape(bsz, seq, D_MODEL)
```

```python
import functools
import math

import numpy as np
import jax
import jax.numpy as jnp
from jax import lax
from jax.experimental import pallas as pl
from jax.experimental.pallas import tpu as pltpu

F32 = jnp.float32
BF16 = jnp.bfloat16

D_MODEL = 1024
HEAD_DIM = 64
N_HEADS = 4
GROUP_WIDTH = N_HEADS * HEAD_DIM
ROPE_THETA = 10000.0
EPS = 1e-6
NEG = -1e30
LOG2E = math.log2(math.e)

CMP_LEN = 32
CMP_STRIDE = 16
SLC_LEN = 64
TOP_N = 16
NSA_WINDOW = 512
FORCED_LOCAL = 2
DIFF_HALF = HEAD_DIM // 2
MLSTM_CHUNK = 64
MLSTM_CONV = 4
DILATED_PATTERNS = ((128, 1), (512, 4), (2048, 16))
D_FF = ((8 * D_MODEL + 3 * 256 - 1) // (3 * 256)) * 256

LANES = 128
BF16_SUBLANES = 16
VMEM_LIMIT = 56 * 1024 * 1024
ACC_ROWS = HEAD_DIM + BF16_SUBLANES

ROW_TILE = 512
ATTN_TILE = 256
MLSTM_ROWS = 256

IN_SPLITS = (
    GROUP_WIDTH, HEAD_DIM, HEAD_DIM, HEAD_DIM, HEAD_DIM, HEAD_DIM, HEAD_DIM, 3 * N_HEADS,
    GROUP_WIDTH, GROUP_WIDTH, GROUP_WIDTH,
    GROUP_WIDTH, GROUP_WIDTH, N_HEADS, N_HEADS, GROUP_WIDTH,
    GROUP_WIDTH, GROUP_WIDTH, GROUP_WIDTH,
)
(A_Q, A_KC, A_VC, A_KS, A_VS, A_KW, A_VW, A_G, B_Q, B_K, B_V,
 C_U, C_V, C_I, C_F, C_O, D_Q, D_K, D_V) = range(19)
SM_AG, SM_CI, SM_CF = 0, 12, 16


def _cparams(sem):
    return pltpu.CompilerParams(dimension_semantics=sem, vmem_limit_bytes=VMEM_LIMIT)


def _iota(shape, dim):
    return lax.broadcasted_iota(jnp.int32, shape, dim)


def _dot(a, b):
    return jnp.dot(a, b, preferred_element_type=F32)


def _bf16_pieces(x):
    hi = x.astype(BF16)
    rest = x - hi.astype(F32)
    mid = rest.astype(BF16)
    return hi, mid, (rest - mid.astype(F32)).astype(BF16)


PACKED_CHUNKS = 24
D_IN_PADDED = -(-sum(IN_SPLITS) // LANES) * LANES


def _column_plan():
    offs = np.concatenate([[0], np.cumsum(IN_SPLITS)])
    order = [A_Q, A_KC, A_VC, A_KS, A_KW, A_VS, A_VW,
             A_G, C_I, C_F, ('pad', LANES - 20),
             B_Q, B_K, B_V, C_U, C_V, C_O, D_Q, D_K, D_V]
    src = []
    for item in order:
        if item is None:
            src += [-1] * HEAD_DIM
        elif isinstance(item, tuple):
            src += [-1] * item[1]
        else:
            src += list(range(int(offs[item]), int(offs[item + 1])))
    src = np.asarray(src)
    assert src.size == PACKED_CHUNKS * LANES
    terms, mats = [], []
    for j in range(PACKED_CHUNKS):
        cols = src[j * LANES:(j + 1) * LANES]
        todo = cols >= 0
        while todo.any():
            start = (cols[todo].min() // LANES) * LANES
            take = todo & (cols < start + 2 * LANES)
            sel = np.zeros((2 * LANES, LANES), np.float32)
            sel[cols[take] - start, np.nonzero(take)[0]] = 1.0
            terms.append((j, int(start)))
            mats.append(sel)
            todo &= ~take
    return tuple(terms), np.stack(mats)


def _rope_tables(seq, dim):
    inv = 1.0 / (ROPE_THETA ** (jnp.arange(0, dim, 2, dtype=F32) / dim))
    ang = jnp.arange(seq, dtype=F32)[:, None] * inv[None, :]
    return jnp.cos(ang), jnp.sin(ang)


def _lane_tables(seq):
    c64, s64 = _rope_tables(seq, HEAD_DIM)
    c32, s32 = _rope_tables(seq, DIFF_HALF)
    t64c = jnp.concatenate([c64, c64] * (LANES // HEAD_DIM), axis=1)
    t64s = jnp.concatenate([-s64, s64] * (LANES // HEAD_DIM), axis=1)
    t32c = jnp.concatenate([c32, c32] * (LANES // DIFF_HALF), axis=1)
    t32s = jnp.concatenate([-s32, s32] * (LANES // DIFF_HALF), axis=1)
    pos = np.arange(seq)[:, None]
    lane = np.arange(LANES)[None, :]
    onehot = ((lane >= HEAD_DIM) & ((pos // SLC_LEN) % HEAD_DIM == lane - HEAD_DIM)
              ).astype(np.float32)
    return t64c, t64s, t32c, t32s, jnp.asarray(onehot)


def _swap_halves(x, group):
    width = x.shape[-1]
    half = group // 2
    lane = _iota(x.shape, 1) & (group - 1)
    up = pltpu.roll(x, width - half, axis=1)
    down = pltpu.roll(x, half, axis=1)
    return jnp.where(lane < half, up, down)


def _rope(x, cos_t, sin_t, group):
    reps = x.shape[-1] // LANES
    if reps > 1:
        cos_t = jnp.concatenate([cos_t] * reps, axis=1)
        sin_t = jnp.concatenate([sin_t] * reps, axis=1)
    return x * cos_t + _swap_halves(x, group) * sin_t


def _in_proj_kernel(x_ref, g_ref, wraw_ref, sel_ref, c64_ref, s64_ref, c32_ref, s32_ref, oh_ref,
                    aq_ref, aqr_ref, akvc_ref, aks_ref, avs_ref, akw_ref, avw_ref, sm_ref,
                    bq_ref, bk_ref, bv_ref, cu_ref, cv_ref, co_ref, dq_ref, dk_ref, dv_ref,
                    w_ref, *, terms):
    @pl.when(pl.program_id(0) == 0)
    def _():
        for j in range(PACKED_CHUNKS):
            chunk = jnp.zeros((D_MODEL, LANES), F32)
            for t, (dst, start) in enumerate(terms):
                if dst == j:
                    chunk = chunk + _dot(wraw_ref[:, start:start + 2 * LANES], sel_ref[t])
            w_ref[:, j * LANES:(j + 1) * LANES] = chunk.astype(BF16)

    x = x_ref[...]
    h = x * lax.rsqrt(jnp.mean(x * x, axis=-1, keepdims=True) + EPS)
    h = (h * g_ref[...]).astype(BF16)
    c64, s64 = c64_ref[...], s64_ref[...]
    c32, s32 = c32_ref[...], s32_ref[...]

    def mm(c0, c1):
        return _dot(h, w_ref[:, c0 * LANES:c1 * LANES])

    def channel_major(ref, z):
        ref[0] = z.T.astype(ref.dtype)

    zq = mm(0, 2) * (HEAD_DIM ** -0.5 * LOG2E)
    channel_major(aq_ref, zq)
    channel_major(aqr_ref, _rope(zq, c64, s64, HEAD_DIM))
    akvc_ref[...] = mm(2, 3).astype(BF16)
    keys = _rope(mm(3, 4), c64, s64, HEAD_DIM)
    low = _iota(keys.shape, 1) < HEAD_DIM
    aks_ref[...] = (jnp.where(low, keys, 0.0) + oh_ref[...]).astype(BF16)
    akw_ref[...] = jnp.where(low, pltpu.roll(keys, HEAD_DIM, axis=1), 0.0).astype(BF16)
    vals_t = mm(4, 5).T
    avs_ref[0] = vals_t[0:HEAD_DIM].astype(BF16)
    avw_ref[0] = vals_t[HEAD_DIM:2 * HEAD_DIM].astype(BF16)
    sm_ref[...] = mm(5, 6)
    channel_major(bq_ref, _rope(mm(6, 8), c32, s32, DIFF_HALF) * (DIFF_HALF ** -0.5 * LOG2E))
    bk_ref[...] = _rope(mm(8, 10), c32, s32, DIFF_HALF).astype(BF16)
    channel_major(bv_ref, mm(10, 12))
    cu_ref[...] = mm(12, 14)
    cv_ref[...] = mm(14, 16).astype(BF16)
    co_ref[...] = mm(16, 18)
    channel_major(dq_ref, _rope(mm(18, 20), c64, s64, HEAD_DIM) * (HEAD_DIM ** -0.5 * LOG2E))
    dk_ref[...] = _rope(mm(20, 22), c64, s64, HEAD_DIM).astype(BF16)
    channel_major(dv_ref, mm(22, 24))


_IN_PROJ_OUTS = (
    (256, BF16, True), (256, BF16, True), (128, BF16, False), (128, BF16, False),
    (HEAD_DIM, BF16, True), (128, BF16, False), (HEAD_DIM, BF16, True), (128, F32, False),
    (256, BF16, True), (256, BF16, False), (256, BF16, True), (256, F32, False),
    (256, BF16, False), (256, F32, False), (256, BF16, True), (256, BF16, False),
    (256, BF16, True))


def _in_proj(x2, gain, w, tables, seq, tm):
    n = x2.shape[0]
    nblk_s = seq // tm
    terms, select = _column_plan()
    select = jnp.asarray(select, BF16)
    w_raw = jnp.pad(w.astype(BF16), ((0, 0), (0, D_IN_PADDED - w.shape[1])))
    row = lambda i: (i, 0)
    tab = lambda i: (i % nblk_s, 0)
    const = lambda i: (0, 0)
    in_specs = [pl.BlockSpec((tm, D_MODEL), row),
                pl.BlockSpec((1, D_MODEL), const),
                pl.BlockSpec(w_raw.shape, const),
                pl.BlockSpec(select.shape, lambda i: (0, 0, 0))]
    in_specs += [pl.BlockSpec((tm, LANES), tab)] * 5
    out_specs = [pl.BlockSpec((1, w, tm), lambda i: (i // nblk_s, 0, i % nblk_s)) if cmaj
                 else pl.BlockSpec((tm, w), row) for w, _, cmaj in _IN_PROJ_OUTS]
    out_shape = [jax.ShapeDtypeStruct((n // seq, w, seq) if cmaj else (n, w), dt)
                 for w, dt, cmaj in _IN_PROJ_OUTS]
    return pl.pallas_call(
        functools.partial(_in_proj_kernel, terms=terms), grid=(n // tm,), in_specs=in_specs,
        out_specs=out_specs, out_shape=out_shape,
        scratch_shapes=[pltpu.VMEM((D_MODEL, PACKED_CHUNKS * LANES), BF16)],
        compiler_params=_cparams(("arbitrary",)), name="in_proj",
    )(x2, gain.reshape(1, D_MODEL), w_raw, select, *tables)


def _compress_kernel(r_ref, w_ref, pos_ref, kc_ref, kvct_ref):
    wk = w_ref[0].reshape(CMP_LEN, HEAD_DIM, HEAD_DIM)
    wv = w_ref[1].reshape(CMP_LEN, HEAD_DIM, HEAD_DIM)
    zeros = jnp.zeros_like(wk)
    full = jnp.concatenate([jnp.concatenate([wk, zeros], axis=2),
                            jnp.concatenate([zeros, wv], axis=2)], axis=1)
    w_first = full[0:CMP_STRIDE].reshape(CMP_STRIDE * LANES, LANES).astype(BF16)
    w_second = full[CMP_STRIDE:].reshape(CMP_STRIDE * LANES, LANES).astype(BF16)

    r = r_ref[0]
    first = _dot(r, w_first)
    second = _dot(r, w_second)
    nrow = first.shape[0]
    pos = pos_ref[...].astype(BF16)
    half = pos.shape[1] // 2
    const = _dot(pos[:, :half], w_first) + _dot(pos[:, half:], w_second)
    out = first + pltpu.roll(second, nrow - 1, axis=0) + const[0:1, :]
    lane = _iota(out.shape, 1)
    kc_ref[0] = jnp.where(lane < HEAD_DIM, out, 0.0).astype(BF16)
    kvct_ref[0] = out.T.astype(BF16)


def _compress(a_kvc, cmp_w, cmp_pos):
    bsz, seq, _ = a_kvc.shape
    nrow = seq // CMP_STRIDE
    r = a_kvc.reshape(bsz, nrow, CMP_STRIDE * LANES)
    pos = jnp.concatenate([cmp_pos[0], cmp_pos[1]], axis=-1)
    pos = jnp.broadcast_to(pos.reshape(1, CMP_LEN * LANES), (8, CMP_LEN * LANES))
    return pl.pallas_call(
        _compress_kernel, grid=(bsz,),
        in_specs=[pl.BlockSpec((1, nrow, CMP_STRIDE * LANES), lambda b: (b, 0, 0)),
                  pl.BlockSpec(cmp_w.shape, lambda b: (0, 0, 0)),
                  pl.BlockSpec(pos.shape, lambda b: (0, 0))],
        out_specs=[pl.BlockSpec((1, nrow, LANES), lambda b: (b, 0, 0)),
                   pl.BlockSpec((1, LANES, nrow), lambda b: (b, 0, 0))],
        out_shape=[jax.ShapeDtypeStruct((bsz, nrow, LANES), BF16),
                   jax.ShapeDtypeStruct((bsz, LANES, nrow), BF16)],
        compiler_params=_cparams(("parallel",)), name="nsa_compress",
    )(r, cmp_w, pos)


def _flash_init(m_ref, acc_ref):
    m_ref[...] = jnp.full(m_ref.shape, NEG, F32)
    acc_ref[...] = jnp.zeros(acc_ref.shape, F32)


def _flash_step(s, values, m_ref, acc_ref):
    width = acc_ref.shape[1]
    chunk = min(2 * LANES, width)
    for g, vals in enumerate(values):
        rows = slice(g * ACC_ROWS, (g + 1) * ACC_ROWS)
        for c0 in range(0, width, chunk):
            lanes = slice(g * width + c0, g * width + c0 + chunk)
            s_c = s[:, lanes]
            m_old = m_ref[:, lanes]
            m_new = jnp.maximum(m_old, jnp.max(s_c, axis=0, keepdims=True))
            alpha = jnp.exp2(m_old - m_new)
            p = jnp.exp2(s_c - m_new).astype(BF16)
            acc_ref[rows, c0:c0 + chunk] = alpha * acc_ref[rows, c0:c0 + chunk] + _dot(vals, p)
            m_ref[:, lanes] = m_new


def _pipelined_tiles(lo, hi, scores, consume, sa_ref, sb_ref):
    n = hi - lo

    def put(ref, tiles):
        for g, tile in enumerate(tiles):
            ref[g] = tile

    def get(ref):
        return [ref[g] for g in range(ref.shape[0])]

    put(sa_ref, scores(lo))

    def body(i, carry):
        j = lo + 2 * i
        put(sb_ref, scores(j + 1))
        consume(get(sa_ref), j, False)
        put(sa_ref, scores(j + 2))
        consume(get(sb_ref), j + 1, False)
        return carry

    lax.fori_loop(0, n // 2, body, 0)

    @pl.when(n % 2 == 0)
    def _():
        consume(get(sa_ref), hi, True)

    @pl.when(n % 2 == 1)
    def _():
        put(sb_ref, scores(hi))
        consume(get(sa_ref), hi - 1, False)
        consume(get(sb_ref), hi, True)


def _with_ones(vt):
    return jnp.concatenate([vt, jnp.ones((BF16_SUBLANES, vt.shape[1]), BF16)], axis=0)


def _cmp_to_slc_t(seq):
    n_cmp = (seq - CMP_LEN) // CMP_STRIDE + 1
    n_slc = seq // SLC_LEN
    ratio_s, ratio_c = SLC_LEN // CMP_STRIDE, CMP_LEN // CMP_STRIDE
    jj = np.arange(n_slc)[:, None, None]
    src = ratio_s * jj - np.arange(ratio_s)[None, :, None] - np.arange(ratio_c)[None, None, :]
    ok = (src >= 0) & (src < n_cmp)
    m = np.zeros((seq // CMP_STRIDE, n_slc), np.float32)
    np.add.at(m, (np.where(ok, src, 0), np.broadcast_to(jj, src.shape)), ok.astype(np.float32))
    return jnp.asarray(m.T, BF16)


def _nsa_kernel(qt_ref, qrt_ref, kc_ref, kvct_ref, ks_ref, vst_ref, kw_ref, vwt_ref, sm_ref,
                c2st_ref, wmask_ref, o_ref, m_ref, acc_ref, sa_ref, sb_ref, *, tq, top_n):
    qi = pl.program_id(1)
    s0 = qi * tq
    nb = qt_ref.shape[0]
    rows = N_HEADS * tq
    n_slc = c2st_ref.shape[0]
    lane_t = s0 + (_iota((1, rows), 1) & (tq - 1))
    c2st = c2st_ref[...]
    blk = _iota((n_slc, 1), 0)
    cur = (s0 + _iota((1, tq), 1)) >> 6
    forced = (blk == 0) | ((blk <= cur) & (blk > cur - FORCED_LOCAL))

    def heads_on_lanes(x):
        return jnp.concatenate([x[h * HEAD_DIM:(h + 1) * HEAD_DIM, :] for h in range(N_HEADS)],
                               axis=1)

    def compressed_and_selection(b):
        q4 = jnp.concatenate([heads_on_lanes(qt_ref[b]), jnp.zeros((HEAD_DIM, rows), BF16)], axis=0)
        sc = _dot(kc_ref[b], q4)
        cmask = (_iota((sc.shape[0], 1), 0) * CMP_STRIDE + (CMP_LEN - 1)) <= lane_t
        sc = jnp.where(cmask, sc, NEG)
        e = jnp.where(cmask, jnp.exp2(sc - jnp.max(sc, axis=0, keepdims=True)), 0.0)
        z = jnp.sum(e, axis=0, keepdims=True)
        p_cmp = e * (1.0 / jnp.where(z > 0, z, 1.0))
        o_cmp = _dot(kvct_ref[b][HEAD_DIM:2 * HEAD_DIM, :], p_cmp.astype(BF16))
        p_heads = p_cmp[:, 0:tq]
        for h in range(1, N_HEADS):
            p_heads = p_heads + p_cmp[:, h * tq:(h + 1) * tq]
        p_hi = p_heads.astype(BF16)
        p_lo = (p_heads - p_hi.astype(F32)).astype(BF16)
        imp = _dot(c2st, p_hi) + _dot(c2st, p_lo)

        score = jnp.where(blk > cur, -1.0e6, jnp.where(forced, 1.0e6, imp))
        rank = jnp.zeros((n_slc, tq), F32)
        for i in range(n_slc):
            s_i = score[i:i + 1, :]
            rank = rank + jnp.where(blk > i, jnp.where(s_i >= score, 1.0, 0.0),
                                    jnp.where(s_i > score, 1.0, 0.0))
        sel = (rank < top_n) & (blk <= cur)
        bias = jnp.where(sel, 0.0, NEG)
        if n_slc < HEAD_DIM:
            bias = jnp.concatenate([bias, jnp.zeros((HEAD_DIM - n_slc, tq), F32)], axis=0)
        bias4 = jnp.concatenate([bias] * N_HEADS, axis=1).astype(BF16)
        qsel = jnp.concatenate([heads_on_lanes(qrt_ref[b]), bias4], axis=0)
        return o_cmp, qsel

    prepared = [compressed_and_selection(b) for b in range(nb)]

    def attend(k_ref, vt_ref, lo, window):
        _flash_init(m_ref, acc_ref)

        def scores(j):
            k0 = pl.multiple_of(j * tq, tq)
            return [_dot(k_ref[b, pl.ds(k0, tq), :], prepared[b][1]) for b in range(nb)]

        def consume(tiles, j, diagonal):
            k0 = pl.multiple_of(j * tq, tq)
            if window:
                far = qi - NSA_WINDOW // tq
                mask = wmask_ref[jnp.where(j == qi, 0, jnp.where(j == far, 2, 1))]
            elif diagonal:
                mask = wmask_ref[0]
            for b, s in enumerate(tiles):
                if window or diagonal:
                    s = s + mask
                _flash_step(s, [_with_ones(vt_ref[b, :, pl.ds(k0, tq)])], m_ref.at[b], acc_ref.at[b])

        _pipelined_tiles(lo, qi, scores, consume, sa_ref, sb_ref)
        return [acc_ref[b, 0:HEAD_DIM, :] * (1.0 / acc_ref[b, HEAD_DIM:HEAD_DIM + 1, :])
                for b in range(nb)]

    o_slc = attend(ks_ref, vst_ref, 0, False)
    o_win = attend(kw_ref, vwt_ref, jnp.maximum(qi - NSA_WINDOW // tq, 0), True)

    for b in range(nb):
        g = jax.nn.sigmoid(sm_ref[b].T[0:BF16_SUBLANES, :])

        def gate(branch):
            return jnp.concatenate(
                [g[branch * N_HEADS + h:branch * N_HEADS + h + 1, :] for h in range(N_HEADS)], axis=1)

        o = gate(0) * prepared[b][0] + gate(1) * o_slc[b] + gate(2) * o_win[b]
        for h in range(N_HEADS):
            o_ref[b, h * HEAD_DIM:(h + 1) * HEAD_DIM, :] = (
                o[:, h * tq:(h + 1) * tq].astype(o_ref.dtype))


def _nsa(qt, qrt, kc, kvct, ks, vst, kw, vwt, smalls, tq=ATTN_TILE):
    bsz, _, seq = qt.shape
    tq = min(tq, seq)
    nb = 2 if bsz % 2 == 0 else 1
    n_slc = seq // SLC_LEN
    c2st = _cmp_to_slc_t(seq)
    rows = N_HEADS * tq
    qspec = pl.BlockSpec((nb, GROUP_WIDTH, tq), lambda b, i: (b, 0, i))
    full = lambda a: pl.BlockSpec((nb,) + a.shape[1:], lambda b, i: (b, 0, 0))
    vspec = pl.BlockSpec((nb, HEAD_DIM, seq), lambda b, i: (b, 0, 0))
    assert NSA_WINDOW % tq == 0
    k_off = np.arange(tq)[:, None]
    q_off = np.tile(np.arange(tq), N_HEADS)[None, :]
    visible = np.stack([k_off <= q_off, np.ones((tq, rows), bool), k_off > q_off])
    wmask = jnp.asarray(np.where(visible, 0.0, NEG).astype(np.float32))
    kern = functools.partial(_nsa_kernel, tq=tq, top_n=min(TOP_N, n_slc))
    return pl.pallas_call(
        kern, grid=(bsz // nb, seq // tq),
        in_specs=[qspec, qspec, full(kc), full(kvct), full(ks), vspec, full(kw), vspec,
                  pl.BlockSpec((nb, tq, LANES), lambda b, i: (b, i, 0)),
                  pl.BlockSpec(c2st.shape, lambda b, i: (0, 0)),
                  pl.BlockSpec(wmask.shape, lambda b, i: (0, 0, 0), pipeline_mode=pl.Buffered(1))],
        out_specs=qspec,
        out_shape=jax.ShapeDtypeStruct(qt.shape, BF16),
        scratch_shapes=[pltpu.VMEM((nb, 1, rows), F32), pltpu.VMEM((nb, ACC_ROWS, rows), F32),
                        pltpu.VMEM((nb, tq, rows), F32), pltpu.VMEM((nb, tq, rows), F32)],
        compiler_params=_cparams(("parallel", "arbitrary")), name="nsa_attention",
    )(qt, qrt, kc, kvct, ks, vst, kw, vwt, smalls, c2st, wmask)


def _diff_kernel(qt_ref, k_ref, vt_ref, lam_ref, g_ref, o_ref, m_ref, acc_ref, sa_ref, sb_ref,
                 *, tq, lam_init):
    qi = pl.program_id(1)
    s0 = qi * tq
    rows = 4 * tq
    pairs = N_HEADS // 2
    pw = 2 * HEAD_DIM
    row = _iota((pw, 1), 0)
    lane_t = s0 + (_iota((1, rows), 1) & (tq - 1))

    def query_matrix(p):
        qt = qt_ref[0, p * pw:(p + 1) * pw, :]
        zero = jnp.zeros_like(qt)
        return jnp.concatenate(
            [jnp.where((row >= DIFF_HALF * c) & (row < DIFF_HALF * (c + 1)), qt, zero)
             for c in range(4)], axis=1)

    qmats = [query_matrix(p) for p in range(pairs)]

    def scores(j):
        k0 = pl.multiple_of(j * tq, tq)
        return [_dot(k_ref[0, pl.ds(k0, tq), p * pw:(p + 1) * pw], qmats[p]) for p in range(pairs)]

    def consume(tiles, j, diagonal):
        k0 = pl.multiple_of(j * tq, tq)
        for p, s in enumerate(tiles):
            if diagonal:
                s = jnp.where(k0 + _iota((tq, 1), 0) <= lane_t, s, NEG)
            vt = vt_ref[0, p * pw:(p + 1) * pw, pl.ds(k0, tq)]
            _flash_step(s, [_with_ones(vt[0:HEAD_DIM]), _with_ones(vt[HEAD_DIM:])],
                        m_ref.at[p], acc_ref.at[p])

    _flash_init(m_ref, acc_ref)
    _pipelined_tiles(0, qi, scores, consume, sa_ref, sb_ref)

    lv = lam_ref[...]
    lam = (jnp.exp(jnp.sum(lv[0:1] * lv[1:2], axis=-1, keepdims=True))
           - jnp.exp(jnp.sum(lv[2:3] * lv[3:4], axis=-1, keepdims=True)) + lam_init)
    for h in range(N_HEADS):
        a = acc_ref[h // 2, ACC_ROWS * (h % 2):ACC_ROWS * (h % 2 + 1), :]
        o_all = a[0:HEAD_DIM] * (1.0 / a[HEAD_DIM:HEAD_DIM + 1])
        o = o_all[:, :tq] - lam * o_all[:, tq:]
        y = o * lax.rsqrt(jnp.mean(o * o, axis=0, keepdims=True) + EPS)
        o_ref[0, HEAD_DIM * h:HEAD_DIM * (h + 1), :] = (
            (y * g_ref[...]) * (1.0 - lam_init)).astype(o_ref.dtype)


def _diff(qt, k, vt, lam_vecs, sub_g, lam_init, tq=ATTN_TILE):
    bsz, _, seq = qt.shape
    tq = min(tq, seq)
    pairs = N_HEADS // 2
    qspec = pl.BlockSpec((1, GROUP_WIDTH, tq), lambda b, i: (b, 0, i))
    kern = functools.partial(_diff_kernel, tq=tq, lam_init=lam_init)
    return pl.pallas_call(
        kern, grid=(bsz, seq // tq),
        in_specs=[qspec,
                  pl.BlockSpec((1, seq, GROUP_WIDTH), lambda b, i: (b, 0, 0)),
                  pl.BlockSpec((1, GROUP_WIDTH, seq), lambda b, i: (b, 0, 0)),
                  pl.BlockSpec(lam_vecs.shape, lambda b, i: (0, 0)),
                  pl.BlockSpec((HEAD_DIM, 1), lambda b, i: (0, 0))],
        out_specs=qspec,
        out_shape=jax.ShapeDtypeStruct(qt.shape, BF16),
        scratch_shapes=[pltpu.VMEM((pairs, 1, 4 * tq), F32),
                        pltpu.VMEM((pairs, 2 * ACC_ROWS, 2 * tq), F32),
                        pltpu.VMEM((pairs, tq, 4 * tq), F32), pltpu.VMEM((pairs, tq, 4 * tq), F32)],
        compiler_params=_cparams(("parallel", "arbitrary")), name="diff_attention",
    )(qt, k, vt, lam_vecs, sub_g.reshape(HEAD_DIM, 1))


def _dilated_bias(tq):
    max_back = max(w for w, _ in DILATED_PATTERNS) // tq
    classes = [0, 1, 2, 3, max_back]
    q = np.arange(tq)[None, :]
    k = np.arange(tq)[:, None]
    out = []
    for d in classes:
        delta = d * tq + q - k
        cnt = np.zeros((tq, tq), np.float64)
        for w, dil in DILATED_PATTERNS:
            cnt += (delta >= 0) & (delta <= w) & (delta % dil == 0)
        tab = np.where(cnt > 0, np.log2(np.maximum(cnt, 1.0)), NEG)
        out.append(np.concatenate([tab, tab], axis=1))
    return jnp.asarray(np.stack(out).astype(np.float32)), max_back


def _dilated_kernel(qt_ref, k_ref, vt_ref, bias_ref, o_ref, m_ref, acc_ref, sa_ref, sb_ref,
                    *, tq, max_back):
    qi = pl.program_id(1)
    nb = qt_ref.shape[0]
    pairs = N_HEADS // 2
    pw = 2 * HEAD_DIM
    row = _iota((pw, 1), 0)
    streams = [(b, p) for b in range(nb) for p in range(pairs)]

    def query_matrix(b, p):
        qt = qt_ref[b, p * pw:(p + 1) * pw, :]
        zero = jnp.zeros_like(qt)
        return jnp.concatenate([jnp.where(row < HEAD_DIM, qt, zero),
                                jnp.where(row >= HEAD_DIM, qt, zero)], axis=1)

    qmats = [query_matrix(b, p) for b, p in streams]
    _flash_init(m_ref, acc_ref)

    def scores(j):
        k0 = pl.multiple_of(j * tq, tq)
        return [_dot(k_ref[b, pl.ds(k0, tq), p * pw:(p + 1) * pw], qmats[g])
                for g, (b, p) in enumerate(streams)]

    def consume(tiles, j, diagonal):
        del diagonal
        k0 = pl.multiple_of(j * tq, tq)
        d = qi - j
        bias = bias_ref[jnp.where(d < 3, d, jnp.where(d == max_back, 4, 3))]
        for g, (b, p) in enumerate(streams):
            vt = vt_ref[b, p * pw:(p + 1) * pw, pl.ds(k0, tq)]
            _flash_step(tiles[g] + bias, [_with_ones(vt[0:HEAD_DIM]), _with_ones(vt[HEAD_DIM:])],
                        m_ref.at[g], acc_ref.at[g])

    _pipelined_tiles(jnp.maximum(qi - max_back, 0), qi, scores, consume, sa_ref, sb_ref)
    for g, (b, p) in enumerate(streams):
        for hh in range(2):
            a = acc_ref[g, ACC_ROWS * hh:ACC_ROWS * (hh + 1), :]
            h = 2 * p + hh
            o_ref[b, HEAD_DIM * h:HEAD_DIM * (h + 1), :] = (
                a[0:HEAD_DIM] * (1.0 / a[HEAD_DIM:HEAD_DIM + 1])).astype(o_ref.dtype)


def _dilated(qt, k, vt, tq=ATTN_TILE):
    bsz, _, seq = qt.shape
    tq = min(tq, seq)
    nb = 1
    streams = nb * (N_HEADS // 2)
    bias, max_back = _dilated_bias(tq)
    qspec = pl.BlockSpec((nb, GROUP_WIDTH, tq), lambda b, i: (b, 0, i))
    kern = functools.partial(_dilated_kernel, tq=tq, max_back=max_back)
    return pl.pallas_call(
        kern, grid=(bsz // nb, seq // tq),
        in_specs=[qspec,
                  pl.BlockSpec((nb, seq, GROUP_WIDTH), lambda b, i: (b, 0, 0)),
                  pl.BlockSpec((nb, GROUP_WIDTH, seq), lambda b, i: (b, 0, 0)),
                  pl.BlockSpec(bias.shape, lambda b, i: (0, 0, 0))],
        out_specs=qspec,
        out_shape=jax.ShapeDtypeStruct(qt.shape, BF16),
        scratch_shapes=[pltpu.VMEM((streams, 1, 2 * tq), F32),
                        pltpu.VMEM((streams, 2 * ACC_ROWS, tq), F32),
                        pltpu.VMEM((streams, tq, 2 * tq), F32),
                        pltpu.VMEM((streams, tq, 2 * tq), F32)],
        compiler_params=_cparams(("parallel", "arbitrary")), name="dilated_attention",
    )(qt, k, vt, bias)


def _mlstm_kernel(u_ref, up_ref, v_ref, sm_ref, o_ref, cw_ref, cb_ref, wqt_ref, wk_ref,
                  gb_ref, gcol_ref, hg_ref, out_ref, c_st, m_st):
    ci = pl.program_id(1)
    rows = u_ref.shape[1]

    @pl.when(ci == 0)
    def _():
        c_st[...] = jnp.zeros(c_st.shape, F32)
        m_st[...] = jnp.zeros(m_st.shape, F32)

    tail = jnp.where(ci > 0, up_ref[0], 0.0)
    ext = jnp.concatenate([tail, u_ref[0]], axis=0)
    cw = cw_ref[...]
    uc = cb_ref[...] + cw[MLSTM_CONV - 1:MLSTM_CONV] * ext[8:]
    for j in range(MLSTM_CONV - 1):
        shifted = pltpu.roll(ext, MLSTM_CONV - 1 - j, axis=0)[8:]
        uc = uc + cw[j:j + 1] * shifted
    uc = uc * jax.nn.sigmoid(uc)

    qt_all = _dot(wqt_ref[...], uc.T.astype(BF16))
    k_all = _dot(uc.astype(BF16), wk_ref[...]) * (HEAD_DIM ** -0.5)
    vt_all = v_ref[0].astype(F32).T.astype(BF16)
    ogt = jax.nn.sigmoid(o_ref[0]).T
    sm = sm_ref[0]
    smt = sm.T
    gb = gb_ref[...]
    gcol = gcol_ref[...]
    upper = _iota((rows, rows), 0) <= _iota((rows, rows), 1)
    lower = _iota((rows, rows), 1) <= _iota((rows, rows), 0)
    ig_rows = smt[8:16] + gcol[:, 0:1]
    lf_rows = jax.nn.log_sigmoid(smt[16:24] + gcol[:, 1:2])
    tri_upper = jnp.where(upper, 1.0, 0.0).astype(BF16)
    tri_lower = jnp.where(lower, 1.0, 0.0).astype(BF16)
    b_rows = sum(_dot(piece, tri_upper) for piece in _bf16_pieces(lf_rows))
    ig_cols = sm[:, SM_CI:SM_CI + N_HEADS] + gb[0:1]
    lf_cols = jax.nn.log_sigmoid(sm[:, SM_CF:SM_CF + N_HEADS] + gb[1:2])
    b_cols = sum(_dot(tri_lower, piece) for piece in _bf16_pieces(lf_cols))
    ones = jnp.ones((BF16_SUBLANES, rows), BF16)

    for h in range(N_HEADS):
        hs = slice(h * HEAD_DIM, (h + 1) * HEAD_DIM)
        b_row = b_rows[h:h + 1]
        src_row = ig_rows[N_HEADS + h:N_HEADS + h + 1] - b_row
        src_col = ig_cols[:, h:h + 1] - b_cols[:, h:h + 1]
        dmat = jnp.where(upper, b_row + src_col, NEG)
        a = b_row[:, rows - 1:rows]
        g_end = a + src_row
        m_loc = jnp.max(g_end, axis=-1, keepdims=True)
        w_end = jnp.exp(g_end - m_loc)

        state = c_st[h]
        m_in = m_st[h]
        inter = b_row + m_in
        m_t = jnp.maximum(inter, jnp.max(dmat, axis=0, keepdims=True))
        e_inter = jnp.exp(inter - m_t)
        qt = qt_all[hs].astype(BF16)
        kb = k_all[:, hs].astype(BF16)
        values = jnp.concatenate([vt_all[hs], ones], axis=0)
        p = (_dot(kb, qt) * jnp.exp(dmat - m_t)).astype(BF16)
        from_state = _dot(state.astype(BF16), qt)
        from_chunk = _dot(values, p)
        num = e_inter * from_state[0:HEAD_DIM] + from_chunk[0:HEAD_DIM]
        den = e_inter * from_state[HEAD_DIM:HEAD_DIM + 1] + from_chunk[HEAD_DIM:HEAD_DIM + 1]
        hh = num * (1.0 / jnp.maximum(jnp.abs(den), jnp.exp(-m_t)))
        hh = hh * lax.rsqrt(jnp.mean(hh * hh, axis=0, keepdims=True) + EPS) * hg_ref[hs, :]
        out_ref[0, hs, :] = (hh * ogt[hs]).astype(out_ref.dtype)

        m_new = jnp.maximum(a + m_in, m_loc)
        decay = jnp.exp(a + m_in - m_new)
        fresh = jnp.exp(m_loc - m_new)
        local = _dot((values.astype(F32) * w_end).astype(BF16), kb)
        c_st[h] = decay * state + fresh * local
        m_st[h] = m_new


def _mlstm(u, v, smalls, o_pre, conv_w, conv_b, wq, wk, gate_b, head_g, rows=MLSTM_ROWS):
    bsz, seq, _ = u.shape
    rows = min(rows, seq)
    row = pl.BlockSpec((1, rows, GROUP_WIDTH), lambda b, c: (b, c, 0))
    const2 = lambda a: pl.BlockSpec(a.shape, lambda b, c: (0,) * a.ndim)
    conv_b = conv_b.reshape(1, GROUP_WIDTH)
    head_g = head_g.reshape(GROUP_WIDTH, 1)
    eye = jnp.eye(N_HEADS, dtype=wq.dtype)
    wqt = jnp.einsum('hde,hg->hegd', wq, eye).reshape(GROUP_WIDTH, GROUP_WIDTH).astype(BF16)
    wkb = jnp.einsum('hde,hg->hdge', wk, eye).reshape(GROUP_WIDTH, GROUP_WIDTH).astype(BF16)
    zeros4 = jnp.zeros((N_HEADS,), gate_b.dtype)
    gcol = jnp.stack([jnp.concatenate([zeros4, gate_b[0]]), jnp.concatenate([gate_b[1], zeros4])],
                     axis=1)
    return pl.pallas_call(
        _mlstm_kernel, grid=(bsz, seq // rows),
        in_specs=[row,
                  pl.BlockSpec((1, 8, GROUP_WIDTH),
                               lambda b, c: (b, jnp.maximum(c * (rows // 8) - 1, 0), 0)),
                  row,
                  pl.BlockSpec((1, rows, LANES), lambda b, c: (b, c, 0)),
                  row, const2(conv_w), const2(conv_b), const2(wqt), const2(wkb), const2(gate_b),
                  const2(gcol), const2(head_g)],
        out_specs=pl.BlockSpec((1, GROUP_WIDTH, rows), lambda b, c: (b, 0, c)),
        out_shape=jax.ShapeDtypeStruct((bsz, GROUP_WIDTH, seq), BF16),
        scratch_shapes=[pltpu.VMEM((N_HEADS, ACC_ROWS, HEAD_DIM), F32),
                        pltpu.VMEM((N_HEADS, 1, 1), F32)],
        compiler_params=_cparams(("parallel", "arbitrary")), name="mlstm",
    )(u, u, v, smalls, o_pre, conv_w, conv_b, wqt, wkb, gate_b, gcol, head_g)


def _out_ffn_kernel(x_ref, oa_ref, ob_ref, oc_ref, od_ref, wo_ref, g_ref, wg_ref, wu_ref, wd_ref,
                    gf_ref, y_ref, *, final, ff_chunk):
    mixed_t = jnp.concatenate([oa_ref[0], ob_ref[0], oc_ref[0], od_ref[0]], axis=0)
    x = x_ref[...] + lax.dot_general(mixed_t, wo_ref[...], (((0,), (0,)), ((), ())),
                                     preferred_element_type=F32)
    h = x * lax.rsqrt(jnp.mean(x * x, axis=-1, keepdims=True) + EPS)
    h = (h * g_ref[...]).astype(BF16)
    ffn = None
    for c0 in range(0, D_FF, ff_chunk):
        gate = _dot(h, wg_ref[:, c0:c0 + ff_chunk])
        up = _dot(h, wu_ref[:, c0:c0 + ff_chunk])
        act = (gate * jax.nn.sigmoid(gate) * up).astype(BF16)
        part = _dot(act, wd_ref[c0:c0 + ff_chunk, :])
        ffn = part if ffn is None else ffn + part
    y = x + ffn
    if final:
        y = y * lax.rsqrt(jnp.mean(y * y, axis=-1, keepdims=True) + EPS) * gf_ref[...]
    y_ref[...] = y


def _out_ffn(x2, o_a, o_b, o_c, o_d, w_out, gain, w_gate, w_up, w_down, gain_final, final, tm=ROW_TILE):
    n = x2.shape[0]
    seq = o_a.shape[2]
    tm = min(tm, seq)
    nblk_s = seq // tm
    row = lambda w: pl.BlockSpec((tm, w), lambda i: (i, 0))
    mixer = pl.BlockSpec((1, GROUP_WIDTH, tm), lambda i: (i // nblk_s, 0, i % nblk_s))
    const = lambda a: pl.BlockSpec(a.shape, lambda i: (0, 0), pipeline_mode=pl.Buffered(1))
    gain = gain.reshape(1, D_MODEL)
    gain_final = gain_final.reshape(1, D_MODEL)
    kern = functools.partial(_out_ffn_kernel, final=final, ff_chunk=256)
    return pl.pallas_call(
        kern, grid=(n // tm,),
        in_specs=[row(D_MODEL)] + [mixer] * 4
                 + [const(w_out), const(gain), const(w_gate), const(w_up), const(w_down),
                    const(gain_final)],
        out_specs=row(D_MODEL),
        out_shape=jax.ShapeDtypeStruct((n, D_MODEL), F32),
        compiler_params=_cparams(("parallel",)), name="out_ffn",
    )(x2, o_a, o_b, o_c, o_d, w_out, gain, w_gate, w_up, w_down, gain_final)


def kernel(x, norm_mix, w_in, nsa_cmp_pos, nsa_cmp_w, diff_lambda, diff_norm, mlstm_conv_w,
           mlstm_conv_b, mlstm_wq, mlstm_wk, mlstm_gate_b, mlstm_norm, w_out, norm_ffn, w_gate,
           w_up, w_down, norm_final):
    bsz, seq, _ = x.shape
    depth = w_in.shape[0]
    tables = _lane_tables(seq)
    x2 = x.reshape(bsz * seq, D_MODEL)
    r3 = lambda t: t.reshape(bsz, seq, t.shape[-1])

    for layer in range(depth):
        (a_q, a_qr, a_kvc, a_ks, a_vs, a_kw, a_vw, smalls, b_q, b_k, b_v, c_u, c_v, c_o,
         d_q, d_k, d_v) = _in_proj(x2, norm_mix[layer], w_in[layer], tables, seq,
                                   tm=min(ROW_TILE, seq))
        smalls3 = r3(smalls)

        kc, kvct = _compress(r3(a_kvc), nsa_cmp_w[layer], nsa_cmp_pos[layer])
        o_a = _nsa(a_q, a_qr, kc, kvct, r3(a_ks), a_vs, r3(a_kw), a_vw, smalls3)

        lam_init = 0.8 - 0.6 * math.exp(-0.3 * layer)
        o_b = _diff(b_q, r3(b_k), b_v, diff_lambda[layer], diff_norm[layer], lam_init)

        o_c = _mlstm(r3(c_u), r3(c_v), smalls3, r3(c_o),
                     mlstm_conv_w[layer], mlstm_conv_b[layer], mlstm_wq[layer], mlstm_wk[layer],
                     mlstm_gate_b[layer], mlstm_norm[layer])

        o_d = _dilated(d_q, r3(d_k), d_v)

        x2 = _out_ffn(x2, o_a, o_b, o_c, o_d, w_out[layer].astype(BF16), norm_ffn[layer],
                      w_gate[layer].astype(BF16), w_up[layer].astype(BF16),
                      w_down[layer].astype(BF16), norm_final, final=(layer == depth - 1))
    return x2.reshape(bsz, seq, D_MODEL)
```

```python
import functools
import math

import numpy as np
import jax
import jax.numpy as jnp
from jax import lax
from jax.experimental import pallas as pl
from jax.experimental.pallas import tpu as pltpu

F32 = jnp.float32
BF16 = jnp.bfloat16

D_MODEL = 1024
HEAD_DIM = 64
N_HEADS = 4
GROUP_WIDTH = N_HEADS * HEAD_DIM
ROPE_THETA = 10000.0
EPS = 1e-6
NEG = -1e30
LOG2E = math.log2(math.e)

CMP_LEN = 32
CMP_STRIDE = 16
SLC_LEN = 64
TOP_N = 16
NSA_WINDOW = 512
FORCED_LOCAL = 2
DIFF_HALF = HEAD_DIM // 2
MLSTM_CHUNK = 64
MLSTM_CONV = 4
DILATED_PATTERNS = ((128, 1), (512, 4), (2048, 16))
D_FF = ((8 * D_MODEL + 3 * 256 - 1) // (3 * 256)) * 256

LANES = 128
BF16_SUBLANES = 16
VMEM_LIMIT = 56 * 1024 * 1024
ACC_ROWS = HEAD_DIM + BF16_SUBLANES

ROW_TILE = 512
ATTN_TILE = 256
MLSTM_ROWS = 256

IN_SPLITS = (
    GROUP_WIDTH, HEAD_DIM, HEAD_DIM, HEAD_DIM, HEAD_DIM, HEAD_DIM, HEAD_DIM, 3 * N_HEADS,
    GROUP_WIDTH, GROUP_WIDTH, GROUP_WIDTH,
    GROUP_WIDTH, GROUP_WIDTH, N_HEADS, N_HEADS, GROUP_WIDTH,
    GROUP_WIDTH, GROUP_WIDTH, GROUP_WIDTH,
)
(A_Q, A_KC, A_VC, A_KS, A_VS, A_KW, A_VW, A_G, B_Q, B_K, B_V,
 C_U, C_V, C_I, C_F, C_O, D_Q, D_K, D_V) = range(19)
SM_AG, SM_CI, SM_CF = 0, 12, 16


def _cparams(sem):
    return pltpu.CompilerParams(dimension_semantics=sem, vmem_limit_bytes=VMEM_LIMIT)


def _iota(shape, dim):
    return lax.broadcasted_iota(jnp.int32, shape, dim)


def _dot(a, b):
    return jnp.dot(a, b, preferred_element_type=F32)


def _bf16_pieces(x):
    hi = x.astype(BF16)
    rest = x - hi.astype(F32)
    mid = rest.astype(BF16)
    return hi, mid, (rest - mid.astype(F32)).astype(BF16)


PACKED_CHUNKS = 24
D_IN = sum(IN_SPLITS)
D_IN_ALIGNED = (D_IN // LANES) * LANES


def _column_plan():
    offs = np.concatenate([[0], np.cumsum(IN_SPLITS)])
    order = [A_Q, A_KC, A_VC, A_KS, A_KW, A_VS, A_VW,
             A_G, C_I, C_F, ('pad', LANES - 20),
             B_Q, B_K, B_V, C_U, C_V, C_O, D_Q, D_K, D_V]
    src = []
    for item in order:
        if isinstance(item, tuple):
            src += [-1] * item[1]
        else:
            src += list(range(int(offs[item]), int(offs[item + 1])))
    src = np.asarray(src)
    assert src.size == PACKED_CHUNKS * LANES
    terms, mats = [], []

    def add(dst, from_tail, start, width, rows, lanes):
        sel = np.zeros((2 * LANES, LANES), np.float32)
        sel[rows, lanes] = 1.0
        terms.append((dst, from_tail, int(start), int(width)))
        mats.append(sel)

    for j in range(PACKED_CHUNKS):
        cols = src[j * LANES:(j + 1) * LANES]
        tail = cols >= D_IN_ALIGNED
        if tail.any():
            add(j, True, 0, LANES, cols[tail] - D_IN_ALIGNED, np.nonzero(tail)[0])
        todo = (cols >= 0) & ~tail
        while todo.any():
            start = (cols[todo].min() // LANES) * LANES
            width = 2 * LANES if start + 2 * LANES <= D_IN_ALIGNED else LANES
            take = todo & (cols < start + width)
            add(j, False, start, width, cols[take] - start, np.nonzero(take)[0])
            todo &= ~take
    return tuple(terms), np.stack(mats)


def _rope_tables(seq, dim):
    inv = 1.0 / (ROPE_THETA ** (jnp.arange(0, dim, 2, dtype=F32) / dim))
    ang = jnp.arange(seq, dtype=F32)[:, None] * inv[None, :]
    return jnp.cos(ang), jnp.sin(ang)


def _lane_tables(seq):
    c64, s64 = _rope_tables(seq, HEAD_DIM)
    c32, s32 = _rope_tables(seq, DIFF_HALF)
    t64c = jnp.concatenate([c64, c64] * (LANES // HEAD_DIM), axis=1)
    t64s = jnp.concatenate([-s64, s64] * (LANES // HEAD_DIM), axis=1)
    t32c = jnp.concatenate([c32, c32] * (LANES // DIFF_HALF), axis=1)
    t32s = jnp.concatenate([-s32, s32] * (LANES // DIFF_HALF), axis=1)
    pos = np.arange(seq)[:, None]
    lane = np.arange(LANES)[None, :]
    onehot = ((lane >= HEAD_DIM) & ((pos // SLC_LEN) % HEAD_DIM == lane - HEAD_DIM)
              ).astype(np.float32)
    return t64c, t64s, t32c, t32s, jnp.asarray(onehot)


def _swap_halves(x, group):
    width = x.shape[-1]
    half = group // 2
    lane = _iota(x.shape, 1) & (group - 1)
    up = pltpu.roll(x, width - half, axis=1)
    down = pltpu.roll(x, half, axis=1)
    return jnp.where(lane < half, up, down)


def _rope(x, cos_t, sin_t, group):
    reps = x.shape[-1] // LANES
    if reps > 1:
        cos_t = jnp.concatenate([cos_t] * reps, axis=1)
        sin_t = jnp.concatenate([sin_t] * reps, axis=1)
    return x * cos_t + _swap_halves(x, group) * sin_t


def _in_proj_kernel(x_ref, g_ref, wraw_ref, wtail_ref, sel_ref, c64_ref, s64_ref, c32_ref, s32_ref,
                    oh_ref,
                    aq_ref, aqr_ref, akvc_ref, aks_ref, avs_ref, akw_ref, avw_ref, sm_ref,
                    bq_ref, bk_ref, bv_ref, cu_ref, cv_ref, co_ref, dq_ref, dk_ref, dv_ref,
                    w_ref, *, terms):
    @pl.when(pl.program_id(0) == 0)
    def _():
        for j in range(PACKED_CHUNKS):
            chunk = jnp.zeros((D_MODEL, LANES), F32)
            for t, (dst, from_tail, start, width) in enumerate(terms):
                if dst == j:
                    source = wtail_ref if from_tail else wraw_ref
                    piece = source[:, start:start + width].astype(BF16)
                    chunk = chunk + _dot(piece, sel_ref[t][0:width])
            w_ref[:, j * LANES:(j + 1) * LANES] = chunk.astype(BF16)

    x = x_ref[...]
    h = x * lax.rsqrt(jnp.mean(x * x, axis=-1, keepdims=True) + EPS)
    h = (h * g_ref[...]).astype(BF16)
    c64, s64 = c64_ref[...], s64_ref[...]
    c32, s32 = c32_ref[...], s32_ref[...]

    def mm(c0, c1):
        return _dot(h, w_ref[:, c0 * LANES:c1 * LANES])

    def channel_major(ref, z):
        ref[0] = z.T.astype(ref.dtype)

    zq = mm(0, 2) * (HEAD_DIM ** -0.5 * LOG2E)
    channel_major(aq_ref, zq)
    channel_major(aqr_ref, _rope(zq, c64, s64, HEAD_DIM))
    akvc_ref[...] = mm(2, 3).astype(BF16)
    keys = _rope(mm(3, 4), c64, s64, HEAD_DIM)
    low = _iota(keys.shape, 1) < HEAD_DIM
    aks_ref[...] = (jnp.where(low, keys, 0.0) + oh_ref[...]).astype(BF16)
    akw_ref[...] = jnp.where(low, pltpu.roll(keys, HEAD_DIM, axis=1), 0.0).astype(BF16)
    vals_t = mm(4, 5).T
    avs_ref[0] = vals_t[0:HEAD_DIM].astype(BF16)
    avw_ref[0] = vals_t[HEAD_DIM:2 * HEAD_DIM].astype(BF16)
    sm_ref[...] = mm(5, 6)
    channel_major(bq_ref, _rope(mm(6, 8), c32, s32, DIFF_HALF) * (DIFF_HALF ** -0.5 * LOG2E))
    bk_ref[...] = _rope(mm(8, 10), c32, s32, DIFF_HALF).astype(BF16)
    channel_major(bv_ref, mm(10, 12))
    cu_ref[...] = mm(12, 14)
    cv_ref[...] = mm(14, 16).astype(BF16)
    co_ref[...] = mm(16, 18)
    channel_major(dq_ref, _rope(mm(18, 20), c64, s64, HEAD_DIM) * (HEAD_DIM ** -0.5 * LOG2E))
    dk_ref[...] = _rope(mm(20, 22), c64, s64, HEAD_DIM).astype(BF16)
    channel_major(dv_ref, mm(22, 24))


_IN_PROJ_OUTS = (
    (256, BF16, True), (256, BF16, True), (128, BF16, False), (128, BF16, False),
    (HEAD_DIM, BF16, True), (128, BF16, False), (HEAD_DIM, BF16, True), (128, F32, False),
    (256, BF16, True), (256, BF16, False), (256, BF16, True), (256, F32, False),
    (256, BF16, False), (256, F32, False), (256, BF16, True), (256, BF16, False),
    (256, BF16, True))


def _in_proj(x2, gain, w, tables, seq, tm):
    n = x2.shape[0]
    nblk_s = seq // tm
    terms, select = _column_plan()
    select = jnp.asarray(select, BF16)
    w_tail = jnp.pad(w[:, D_IN_ALIGNED:], ((0, 0), (0, LANES - (D_IN - D_IN_ALIGNED))))
    row = lambda i: (i, 0)
    tab = lambda i: (i % nblk_s, 0)
    const = lambda i: (0, 0)
    in_specs = [pl.BlockSpec((tm, D_MODEL), row),
                pl.BlockSpec((1, D_MODEL), const),
                pl.BlockSpec(w.shape, const, pipeline_mode=pl.Buffered(1)),
                pl.BlockSpec(w_tail.shape, const),
                pl.BlockSpec(select.shape, lambda i: (0, 0, 0))]
    in_specs += [pl.BlockSpec((tm, LANES), tab)] * 5
    out_specs = [pl.BlockSpec((1, w, tm), lambda i: (i // nblk_s, 0, i % nblk_s)) if cmaj
                 else pl.BlockSpec((tm, w), row) for w, _, cmaj in _IN_PROJ_OUTS]
    out_shape = [jax.ShapeDtypeStruct((n // seq, w, seq) if cmaj else (n, w), dt)
                 for w, dt, cmaj in _IN_PROJ_OUTS]
    return pl.pallas_call(
        functools.partial(_in_proj_kernel, terms=terms), grid=(n // tm,), in_specs=in_specs,
        out_specs=out_specs, out_shape=out_shape,
        scratch_shapes=[pltpu.VMEM((D_MODEL, PACKED_CHUNKS * LANES), BF16)],
        compiler_params=_cparams(("arbitrary",)), name="in_proj",
    )(x2, gain.reshape(1, D_MODEL), w, w_tail, select, *tables)


def _compress_kernel(r_ref, w_ref, pos_ref, kc_ref, kvct_ref):
    wk = w_ref[0].reshape(CMP_LEN, HEAD_DIM, HEAD_DIM)
    wv = w_ref[1].reshape(CMP_LEN, HEAD_DIM, HEAD_DIM)
    zeros = jnp.zeros_like(wk)
    full = jnp.concatenate([jnp.concatenate([wk, zeros], axis=2),
                            jnp.concatenate([zeros, wv], axis=2)], axis=1)
    w_first = full[0:CMP_STRIDE].reshape(CMP_STRIDE * LANES, LANES).astype(BF16)
    w_second = full[CMP_STRIDE:].reshape(CMP_STRIDE * LANES, LANES).astype(BF16)

    r = r_ref[0]
    first = _dot(r, w_first)
    second = _dot(r, w_second)
    nrow = first.shape[0]
    pos = pos_ref[...].astype(BF16)
    half = pos.shape[1] // 2
    const = _dot(pos[:, :half], w_first) + _dot(pos[:, half:], w_second)
    out = first + pltpu.roll(second, nrow - 1, axis=0) + const[0:1, :]
    lane = _iota(out.shape, 1)
    kc_ref[0] = jnp.where(lane < HEAD_DIM, out, 0.0).astype(BF16)
    kvct_ref[0] = out.T.astype(BF16)


def _compress(a_kvc, cmp_w, cmp_pos):
    bsz, seq, _ = a_kvc.shape
    nrow = seq // CMP_STRIDE
    r = a_kvc.reshape(bsz, nrow, CMP_STRIDE * LANES)
    pos = jnp.concatenate([cmp_pos[0], cmp_pos[1]], axis=-1)
    pos = jnp.broadcast_to(pos.reshape(1, CMP_LEN * LANES), (8, CMP_LEN * LANES))
    return pl.pallas_call(
        _compress_kernel, grid=(bsz,),
        in_specs=[pl.BlockSpec((1, nrow, CMP_STRIDE * LANES), lambda b: (b, 0, 0)),
                  pl.BlockSpec(cmp_w.shape, lambda b: (0, 0, 0)),
                  pl.BlockSpec(pos.shape, lambda b: (0, 0))],
        out_specs=[pl.BlockSpec((1, nrow, LANES), lambda b: (b, 0, 0)),
                   pl.BlockSpec((1, LANES, nrow), lambda b: (b, 0, 0))],
        out_shape=[jax.ShapeDtypeStruct((bsz, nrow, LANES), BF16),
                   jax.ShapeDtypeStruct((bsz, LANES, nrow), BF16)],
        compiler_params=_cparams(("parallel",)), name="nsa_compress",
    )(r, cmp_w, pos)


def _flash_init(m_ref, acc_ref):
    m_ref[...] = jnp.full(m_ref.shape, NEG, F32)
    acc_ref[...] = jnp.zeros(acc_ref.shape, F32)


def _flash_step(s, values, m_ref, acc_ref):
    width = acc_ref.shape[1]
    chunk = min(2 * LANES, width)
    for g, vals in enumerate(values):
        rows = slice(g * ACC_ROWS, (g + 1) * ACC_ROWS)
        for c0 in range(0, width, chunk):
            lanes = slice(g * width + c0, g * width + c0 + chunk)
            s_c = s[:, lanes]
            m_old = m_ref[:, lanes]
            m_new = jnp.maximum(m_old, jnp.max(s_c, axis=0, keepdims=True))
            alpha = jnp.exp2(m_old - m_new)
            p = jnp.exp2(s_c - m_new).astype(BF16)
            acc_ref[rows, c0:c0 + chunk] = alpha * acc_ref[rows, c0:c0 + chunk] + _dot(vals, p)
            m_ref[:, lanes] = m_new


def _pipelined_tiles(lo, hi, scores, consume, sa_ref, sb_ref):
    n = hi - lo

    def put(ref, tiles):
        for g, tile in enumerate(tiles):
            ref[g] = tile

    def get(ref):
        return [ref[g] for g in range(ref.shape[0])]

    put(sa_ref, scores(lo))

    def body(i, carry):
        j = lo + 2 * i
        put(sb_ref, scores(j + 1))
        consume(get(sa_ref), j, False)
        put(sa_ref, scores(j + 2))
        consume(get(sb_ref), j + 1, False)
        return carry

    lax.fori_loop(0, n // 2, body, 0)

    @pl.when(n % 2 == 0)
    def _():
        consume(get(sa_ref), hi, True)

    @pl.when(n % 2 == 1)
    def _():
        put(sb_ref, scores(hi))
        consume(get(sa_ref), hi - 1, False)
        consume(get(sb_ref), hi, True)


def _with_ones(vt):
    return jnp.concatenate([vt, jnp.ones((BF16_SUBLANES, vt.shape[1]), BF16)], axis=0)


def _cmp_to_slc_t(seq):
    n_cmp = (seq - CMP_LEN) // CMP_STRIDE + 1
    n_slc = seq // SLC_LEN
    ratio_s, ratio_c = SLC_LEN // CMP_STRIDE, CMP_LEN // CMP_STRIDE
    jj = np.arange(n_slc)[:, None, None]
    src = ratio_s * jj - np.arange(ratio_s)[None, :, None] - np.arange(ratio_c)[None, None, :]
    ok = (src >= 0) & (src < n_cmp)
    m = np.zeros((seq // CMP_STRIDE, n_slc), np.float32)
    np.add.at(m, (np.where(ok, src, 0), np.broadcast_to(jj, src.shape)), ok.astype(np.float32))
    return jnp.asarray(m.T, BF16)


def _nsa_kernel(qt_ref, qrt_ref, kc_ref, kvct_ref, ks_ref, vst_ref, kw_ref, vwt_ref, sm_ref,
                c2st_ref, wmask_ref, o_ref, m_ref, acc_ref, sa_ref, sb_ref, *, tq, top_n):
    qi = pl.program_id(1)
    s0 = qi * tq
    nb = qt_ref.shape[0]
    rows = N_HEADS * tq
    n_slc = c2st_ref.shape[0]
    lane_t = s0 + (_iota((1, rows), 1) & (tq - 1))
    c2st = c2st_ref[...]
    blk = _iota((n_slc, 1), 0)
    cur = (s0 + _iota((1, tq), 1)) >> 6
    forced = (blk == 0) | ((blk <= cur) & (blk > cur - FORCED_LOCAL))

    def heads_on_lanes(x):
        return jnp.concatenate([x[h * HEAD_DIM:(h + 1) * HEAD_DIM, :] for h in range(N_HEADS)],
                               axis=1)

    def compressed_and_selection(b):
        q4 = jnp.concatenate([heads_on_lanes(qt_ref[b]), jnp.zeros((HEAD_DIM, rows), BF16)], axis=0)
        sc = _dot(kc_ref[b], q4)
        cmask = (_iota((sc.shape[0], 1), 0) * CMP_STRIDE + (CMP_LEN - 1)) <= lane_t
        sc = jnp.where(cmask, sc, NEG)
        e = jnp.where(cmask, jnp.exp2(sc - jnp.max(sc, axis=0, keepdims=True)), 0.0)
        z = jnp.sum(e, axis=0, keepdims=True)
        p_cmp = e * (1.0 / jnp.where(z > 0, z, 1.0))
        o_cmp = _dot(kvct_ref[b][HEAD_DIM:2 * HEAD_DIM, :], p_cmp.astype(BF16))
        p_heads = p_cmp[:, 0:tq]
        for h in range(1, N_HEADS):
            p_heads = p_heads + p_cmp[:, h * tq:(h + 1) * tq]
        p_hi = p_heads.astype(BF16)
        p_lo = (p_heads - p_hi.astype(F32)).astype(BF16)
        imp = _dot(c2st, p_hi) + _dot(c2st, p_lo)

        score = jnp.where(blk > cur, -1.0e6, jnp.where(forced, 1.0e6, imp))
        rank = jnp.zeros((n_slc, tq), F32)
        for i in range(n_slc):
            s_i = score[i:i + 1, :]
            rank = rank + jnp.where(blk > i, jnp.where(s_i >= score, 1.0, 0.0),
                                    jnp.where(s_i > score, 1.0, 0.0))
        sel = (rank < top_n) & (blk <= cur)
        bias = jnp.where(sel, 0.0, NEG)
        if n_slc < HEAD_DIM:
            bias = jnp.concatenate([bias, jnp.zeros((HEAD_DIM - n_slc, tq), F32)], axis=0)
        bias4 = jnp.concatenate([bias] * N_HEADS, axis=1).astype(BF16)
        qsel = jnp.concatenate([heads_on_lanes(qrt_ref[b]), bias4], axis=0)
        return o_cmp, qsel

    prepared = [compressed_and_selection(b) for b in range(nb)]

    def attend(k_ref, vt_ref, lo, window):
        _flash_init(m_ref, acc_ref)

        def scores(j):
            k0 = pl.multiple_of(j * tq, tq)
            return [_dot(k_ref[b, pl.ds(k0, tq), :], prepared[b][1]) for b in range(nb)]

        def consume(tiles, j, diagonal):
            k0 = pl.multiple_of(j * tq, tq)
            if window:
                far = qi - NSA_WINDOW // tq
                mask = wmask_ref[jnp.where(j == qi, 0, jnp.where(j == far, 2, 1))]
            elif diagonal:
                mask = wmask_ref[0]
            for b, s in enumerate(tiles):
                if window or diagonal:
                    s = s + mask
                _flash_step(s, [_with_ones(vt_ref[b, :, pl.ds(k0, tq)])], m_ref.at[b], acc_ref.at[b])

        _pipelined_tiles(lo, qi, scores, consume, sa_ref, sb_ref)
        return [acc_ref[b, 0:HEAD_DIM, :] * (1.0 / acc_ref[b, HEAD_DIM:HEAD_DIM + 1, :])
                for b in range(nb)]

    o_slc = attend(ks_ref, vst_ref, 0, False)
    o_win = attend(kw_ref, vwt_ref, jnp.maximum(qi - NSA_WINDOW // tq, 0), True)

    for b in range(nb):
        g = jax.nn.sigmoid(sm_ref[b].T[0:BF16_SUBLANES, :])

        def gate(branch):
            return jnp.concatenate(
                [g[branch * N_HEADS + h:branch * N_HEADS + h + 1, :] for h in range(N_HEADS)], axis=1)

        o = gate(0) * prepared[b][0] + gate(1) * o_slc[b] + gate(2) * o_win[b]
        for h in range(N_HEADS):
            o_ref[b, h * HEAD_DIM:(h + 1) * HEAD_DIM, :] = (
                o[:, h * tq:(h + 1) * tq].astype(o_ref.dtype))


def _nsa(qt, qrt, kc, kvct, ks, vst, kw, vwt, smalls, tq=ATTN_TILE):
    bsz, _, seq = qt.shape
    tq = min(tq, seq)
    nb = 2 if bsz % 2 == 0 else 1
    n_slc = seq // SLC_LEN
    c2st = _cmp_to_slc_t(seq)
    rows = N_HEADS * tq
    qspec = pl.BlockSpec((nb, GROUP_WIDTH, tq), lambda b, i: (b, 0, i))
    full = lambda a: pl.BlockSpec((nb,) + a.shape[1:], lambda b, i: (b, 0, 0))
    vspec = pl.BlockSpec((nb, HEAD_DIM, seq), lambda b, i: (b, 0, 0))
    assert NSA_WINDOW % tq == 0
    k_off = np.arange(tq)[:, None]
    q_off = np.tile(np.arange(tq), N_HEADS)[None, :]
    visible = np.stack([k_off <= q_off, np.ones((tq, rows), bool), k_off > q_off])
    wmask = jnp.asarray(np.where(visible, 0.0, NEG).astype(np.float32))
    kern = functools.partial(_nsa_kernel, tq=tq, top_n=min(TOP_N, n_slc))
    return pl.pallas_call(
        kern, grid=(bsz // nb, seq // tq),
        in_specs=[qspec, qspec, full(kc), full(kvct), full(ks), vspec, full(kw), vspec,
                  pl.BlockSpec((nb, tq, LANES), lambda b, i: (b, i, 0)),
                  pl.BlockSpec(c2st.shape, lambda b, i: (0, 0)),
                  pl.BlockSpec(wmask.shape, lambda b, i: (0, 0, 0), pipeline_mode=pl.Buffered(1))],
        out_specs=qspec,
        out_shape=jax.ShapeDtypeStruct(qt.shape, BF16),
        scratch_shapes=[pltpu.VMEM((nb, 1, rows), F32), pltpu.VMEM((nb, ACC_ROWS, rows), F32),
                        pltpu.VMEM((nb, tq, rows), F32), pltpu.VMEM((nb, tq, rows), F32)],
        compiler_params=_cparams(("parallel", "arbitrary")), name="nsa_attention",
    )(qt, qrt, kc, kvct, ks, vst, kw, vwt, smalls, c2st, wmask)


def _diff_kernel(qt_ref, k_ref, vt_ref, lam_ref, g_ref, o_ref, m_ref, acc_ref, sa_ref, sb_ref,
                 *, tq, lam_init):
    qi = pl.program_id(1)
    s0 = qi * tq
    rows = 4 * tq
    pairs = N_HEADS // 2
    pw = 2 * HEAD_DIM
    row = _iota((pw, 1), 0)
    lane_t = s0 + (_iota((1, rows), 1) & (tq - 1))

    def query_matrix(p):
        qt = qt_ref[0, p * pw:(p + 1) * pw, :]
        zero = jnp.zeros_like(qt)
        return jnp.concatenate(
            [jnp.where((row >= DIFF_HALF * c) & (row < DIFF_HALF * (c + 1)), qt, zero)
             for c in range(4)], axis=1)

    qmats = [query_matrix(p) for p in range(pairs)]

    def scores(j):
        k0 = pl.multiple_of(j * tq, tq)
        return [_dot(k_ref[0, pl.ds(k0, tq), p * pw:(p + 1) * pw], qmats[p]) for p in range(pairs)]

    def consume(tiles, j, diagonal):
        k0 = pl.multiple_of(j * tq, tq)
        for p, s in enumerate(tiles):
            if diagonal:
                s = jnp.where(k0 + _iota((tq, 1), 0) <= lane_t, s, NEG)
            vt = vt_ref[0, p * pw:(p + 1) * pw, pl.ds(k0, tq)]
            _flash_step(s, [_with_ones(vt[0:HEAD_DIM]), _with_ones(vt[HEAD_DIM:])],
                        m_ref.at[p], acc_ref.at[p])

    _flash_init(m_ref, acc_ref)
    _pipelined_tiles(0, qi, scores, consume, sa_ref, sb_ref)

    lv = lam_ref[...]
    lam = (jnp.exp(jnp.sum(lv[0:1] * lv[1:2], axis=-1, keepdims=True))
           - jnp.exp(jnp.sum(lv[2:3] * lv[3:4], axis=-1, keepdims=True)) + lam_init)
    for h in range(N_HEADS):
        a = acc_ref[h // 2, ACC_ROWS * (h % 2):ACC_ROWS * (h % 2 + 1), :]
        o_all = a[0:HEAD_DIM] * (1.0 / a[HEAD_DIM:HEAD_DIM + 1])
        o = o_all[:, :tq] - lam * o_all[:, tq:]
        y = o * lax.rsqrt(jnp.mean(o * o, axis=0, keepdims=True) + EPS)
        o_ref[0, HEAD_DIM * h:HEAD_DIM * (h + 1), :] = (
            (y * g_ref[...]) * (1.0 - lam_init)).astype(o_ref.dtype)


def _diff(qt, k, vt, lam_vecs, sub_g, lam_init, tq=ATTN_TILE):
    bsz, _, seq = qt.shape
    tq = min(tq, seq)
    pairs = N_HEADS // 2
    qspec = pl.BlockSpec((1, GROUP_WIDTH, tq), lambda b, i: (b, 0, i))
    kern = functools.partial(_diff_kernel, tq=tq, lam_init=lam_init)
    return pl.pallas_call(
        kern, grid=(bsz, seq // tq),
        in_specs=[qspec,
                  pl.BlockSpec((1, seq, GROUP_WIDTH), lambda b, i: (b, 0, 0)),
                  pl.BlockSpec((1, GROUP_WIDTH, seq), lambda b, i: (b, 0, 0)),
                  pl.BlockSpec(lam_vecs.shape, lambda b, i: (0, 0)),
                  pl.BlockSpec((HEAD_DIM, 1), lambda b, i: (0, 0))],
        out_specs=qspec,
        out_shape=jax.ShapeDtypeStruct(qt.shape, BF16),
        scratch_shapes=[pltpu.VMEM((pairs, 1, 4 * tq), F32),
                        pltpu.VMEM((pairs, 2 * ACC_ROWS, 2 * tq), F32),
                        pltpu.VMEM((pairs, tq, 4 * tq), F32), pltpu.VMEM((pairs, tq, 4 * tq), F32)],
        compiler_params=_cparams(("parallel", "arbitrary")), name="diff_attention",
    )(qt, k, vt, lam_vecs, sub_g.reshape(HEAD_DIM, 1))


def _dilated_bias(tq):
    max_back = max(w for w, _ in DILATED_PATTERNS) // tq
    classes = [0, 1, 2, 3, max_back]
    q = np.arange(tq)[None, :]
    k = np.arange(tq)[:, None]
    out = []
    for d in classes:
        delta = d * tq + q - k
        cnt = np.zeros((tq, tq), np.float64)
        for w, dil in DILATED_PATTERNS:
            cnt += (delta >= 0) & (delta <= w) & (delta % dil == 0)
        tab = np.where(cnt > 0, np.log2(np.maximum(cnt, 1.0)), NEG)
        out.append(np.concatenate([tab, tab], axis=1))
    return jnp.asarray(np.stack(out).astype(np.float32)), max_back


def _dilated_kernel(qt_ref, k_ref, vt_ref, bias_ref, o_ref, m_ref, acc_ref, sa_ref, sb_ref,
                    *, tq, max_back):
    qi = pl.program_id(1)
    nb = qt_ref.shape[0]
    pairs = N_HEADS // 2
    pw = 2 * HEAD_DIM
    row = _iota((pw, 1), 0)
    streams = [(b, p) for b in range(nb) for p in range(pairs)]

    def query_matrix(b, p):
        qt = qt_ref[b, p * pw:(p + 1) * pw, :]
        zero = jnp.zeros_like(qt)
        return jnp.concatenate([jnp.where(row < HEAD_DIM, qt, zero),
                                jnp.where(row >= HEAD_DIM, qt, zero)], axis=1)

    qmats = [query_matrix(b, p) for b, p in streams]
    _flash_init(m_ref, acc_ref)

    def scores(j):
        k0 = pl.multiple_of(j * tq, tq)
        return [_dot(k_ref[b, pl.ds(k0, tq), p * pw:(p + 1) * pw], qmats[g])
                for g, (b, p) in enumerate(streams)]

    def consume(tiles, j, diagonal):
        del diagonal
        k0 = pl.multiple_of(j * tq, tq)
        d = qi - j
        bias = bias_ref[jnp.where(d < 3, d, jnp.where(d == max_back, 4, 3))]
        for g, (b, p) in enumerate(streams):
            vt = vt_ref[b, p * pw:(p + 1) * pw, pl.ds(k0, tq)]
            _flash_step(tiles[g] + bias, [_with_ones(vt[0:HEAD_DIM]), _with_ones(vt[HEAD_DIM:])],
                        m_ref.at[g], acc_ref.at[g])

    _pipelined_tiles(jnp.maximum(qi - max_back, 0), qi, scores, consume, sa_ref, sb_ref)
    for g, (b, p) in enumerate(streams):
        for hh in range(2):
            a = acc_ref[g, ACC_ROWS * hh:ACC_ROWS * (hh + 1), :]
            h = 2 * p + hh
            o_ref[b, HEAD_DIM * h:HEAD_DIM * (h + 1), :] = (
                a[0:HEAD_DIM] * (1.0 / a[HEAD_DIM:HEAD_DIM + 1])).astype(o_ref.dtype)


def _dilated(qt, k, vt, tq=ATTN_TILE):
    bsz, _, seq = qt.shape
    tq = min(tq, seq)
    nb = 1
    streams = nb * (N_HEADS // 2)
    bias, max_back = _dilated_bias(tq)
    qspec = pl.BlockSpec((nb, GROUP_WIDTH, tq), lambda b, i: (b, 0, i))
    kern = functools.partial(_dilated_kernel, tq=tq, max_back=max_back)
    return pl.pallas_call(
        kern, grid=(bsz // nb, seq // tq),
        in_specs=[qspec,
                  pl.BlockSpec((nb, seq, GROUP_WIDTH), lambda b, i: (b, 0, 0)),
                  pl.BlockSpec((nb, GROUP_WIDTH, seq), lambda b, i: (b, 0, 0)),
                  pl.BlockSpec(bias.shape, lambda b, i: (0, 0, 0))],
        out_specs=qspec,
        out_shape=jax.ShapeDtypeStruct(qt.shape, BF16),
        scratch_shapes=[pltpu.VMEM((streams, 1, 2 * tq), F32),
                        pltpu.VMEM((streams, 2 * ACC_ROWS, tq), F32),
                        pltpu.VMEM((streams, tq, 2 * tq), F32),
                        pltpu.VMEM((streams, tq, 2 * tq), F32)],
        compiler_params=_cparams(("parallel", "arbitrary")), name="dilated_attention",
    )(qt, k, vt, bias)


def _mlstm_kernel(u_ref, up_ref, v_ref, sm_ref, o_ref, cw_ref, cb_ref, wqt_ref, wk_ref,
                  gb_ref, gcol_ref, hg_ref, out_ref, c_st, m_st):
    ci = pl.program_id(1)
    rows = u_ref.shape[1]

    @pl.when(ci == 0)
    def _():
        c_st[...] = jnp.zeros(c_st.shape, F32)
        m_st[...] = jnp.zeros(m_st.shape, F32)

    tail = jnp.where(ci > 0, up_ref[0], 0.0)
    ext = jnp.concatenate([tail, u_ref[0]], axis=0)
    cw = cw_ref[...]
    uc = cb_ref[...] + cw[MLSTM_CONV - 1:MLSTM_CONV] * ext[8:]
    for j in range(MLSTM_CONV - 1):
        shifted = pltpu.roll(ext, MLSTM_CONV - 1 - j, axis=0)[8:]
        uc = uc + cw[j:j + 1] * shifted
    uc = uc * jax.nn.sigmoid(uc)

    qt_all = _dot(wqt_ref[...], uc.T.astype(BF16))
    k_all = _dot(uc.astype(BF16), wk_ref[...]) * (HEAD_DIM ** -0.5)
    vt_all = v_ref[0].astype(F32).T.astype(BF16)
    ogt = jax.nn.sigmoid(o_ref[0]).T
    sm = sm_ref[0]
    smt = sm.T
    gb = gb_ref[...]
    gcol = gcol_ref[...]
    upper = _iota((rows, rows), 0) <= _iota((rows, rows), 1)
    lower = _iota((rows, rows), 1) <= _iota((rows, rows), 0)
    ig_rows = smt[8:16] + gcol[:, 0:1]
    lf_rows = jax.nn.log_sigmoid(smt[16:24] + gcol[:, 1:2])
    tri_upper = jnp.where(upper, 1.0, 0.0).astype(BF16)
    tri_lower = jnp.where(lower, 1.0, 0.0).astype(BF16)
    b_rows = sum(_dot(piece, tri_upper) for piece in _bf16_pieces(lf_rows))
    ig_cols = sm[:, SM_CI:SM_CI + N_HEADS] + gb[0:1]
    lf_cols = jax.nn.log_sigmoid(sm[:, SM_CF:SM_CF + N_HEADS] + gb[1:2])
    b_cols = sum(_dot(tri_lower, piece) for piece in _bf16_pieces(lf_cols))
    ones = jnp.ones((BF16_SUBLANES, rows), BF16)

    for h in range(N_HEADS):
        hs = slice(h * HEAD_DIM, (h + 1) * HEAD_DIM)
        b_row = b_rows[h:h + 1]
        src_row = ig_rows[N_HEADS + h:N_HEADS + h + 1] - b_row
        src_col = ig_cols[:, h:h + 1] - b_cols[:, h:h + 1]
        dmat = jnp.where(upper, b_row + src_col, NEG)
        a = b_row[:, rows - 1:rows]
        g_end = a + src_row
        m_loc = jnp.max(g_end, axis=-1, keepdims=True)
        w_end = jnp.exp(g_end - m_loc)

        state = c_st[h]
        m_in = m_st[h]
        inter = b_row + m_in
        m_t = jnp.maximum(inter, jnp.max(dmat, axis=0, keepdims=True))
        e_inter = jnp.exp(inter - m_t)
        qt = qt_all[hs].astype(BF16)
        kb = k_all[:, hs].astype(BF16)
        values = jnp.concatenate([vt_all[hs], ones], axis=0)
        p = (_dot(kb, qt) * jnp.exp(dmat - m_t)).astype(BF16)
        from_state = _dot(state.astype(BF16), qt)
        from_chunk = _dot(values, p)
        num = e_inter * from_state[0:HEAD_DIM] + from_chunk[0:HEAD_DIM]
        den = e_inter * from_state[HEAD_DIM:HEAD_DIM + 1] + from_chunk[HEAD_DIM:HEAD_DIM + 1]
        hh = num * (1.0 / jnp.maximum(jnp.abs(den), jnp.exp(-m_t)))
        hh = hh * lax.rsqrt(jnp.mean(hh * hh, axis=0, keepdims=True) + EPS) * hg_ref[hs, :]
        out_ref[0, hs, :] = (hh * ogt[hs]).astype(out_ref.dtype)

        m_new = jnp.maximum(a + m_in, m_loc)
        decay = jnp.exp(a + m_in - m_new)
        fresh = jnp.exp(m_loc - m_new)
        local = _dot((values.astype(F32) * w_end).astype(BF16), kb)
        c_st[h] = decay * state + fresh * local
        m_st[h] = m_new


def _mlstm(u, v, smalls, o_pre, conv_w, conv_b, wq, wk, gate_b, head_g, rows=MLSTM_ROWS):
    bsz, seq, _ = u.shape
    rows = min(rows, seq)
    row = pl.BlockSpec((1, rows, GROUP_WIDTH), lambda b, c: (b, c, 0))
    const2 = lambda a: pl.BlockSpec(a.shape, lambda b, c: (0,) * a.ndim)
    conv_b = conv_b.reshape(1, GROUP_WIDTH)
    head_g = head_g.reshape(GROUP_WIDTH, 1)
    eye = jnp.eye(N_HEADS, dtype=wq.dtype)
    wqt = jnp.einsum('hde,hg->hegd', wq, eye).reshape(GROUP_WIDTH, GROUP_WIDTH).astype(BF16)
    wkb = jnp.einsum('hde,hg->hdge', wk, eye).reshape(GROUP_WIDTH, GROUP_WIDTH).astype(BF16)
    zeros4 = jnp.zeros((N_HEADS,), gate_b.dtype)
    gcol = jnp.stack([jnp.concatenate([zeros4, gate_b[0]]), jnp.concatenate([gate_b[1], zeros4])],
                     axis=1)
    return pl.pallas_call(
        _mlstm_kernel, grid=(bsz, seq // rows),
        in_specs=[row,
                  pl.BlockSpec((1, 8, GROUP_WIDTH),
                               lambda b, c: (b, jnp.maximum(c * (rows // 8) - 1, 0), 0)),
                  row,
                  pl.BlockSpec((1, rows, LANES), lambda b, c: (b, c, 0)),
                  row, const2(conv_w), const2(conv_b), const2(wqt), const2(wkb), const2(gate_b),
                  const2(gcol), const2(head_g)],
        out_specs=pl.BlockSpec((1, GROUP_WIDTH, rows), lambda b, c: (b, 0, c)),
        out_shape=jax.ShapeDtypeStruct((bsz, GROUP_WIDTH, seq), BF16),
        scratch_shapes=[pltpu.VMEM((N_HEADS, ACC_ROWS, HEAD_DIM), F32),
                        pltpu.VMEM((N_HEADS, 1, 1), F32)],
        compiler_params=_cparams(("parallel", "arbitrary")), name="mlstm",
    )(u, u, v, smalls, o_pre, conv_w, conv_b, wqt, wkb, gate_b, gcol, head_g)


def _out_ffn_kernel(x_ref, oa_ref, ob_ref, oc_ref, od_ref, wo_ref, g_ref, wg_ref, wu_ref, wd_ref,
                    gf_ref, y_ref, *, final, ff_chunk):
    mixed_t = jnp.concatenate([oa_ref[0], ob_ref[0], oc_ref[0], od_ref[0]], axis=0)
    x = x_ref[...] + lax.dot_general(mixed_t, wo_ref[...], (((0,), (0,)), ((), ())),
                                     preferred_element_type=F32)
    h = x * lax.rsqrt(jnp.mean(x * x, axis=-1, keepdims=True) + EPS)
    h = (h * g_ref[...]).astype(BF16)
    ffn = None
    for c0 in range(0, D_FF, ff_chunk):
        gate = _dot(h, wg_ref[:, c0:c0 + ff_chunk])
        up = _dot(h, wu_ref[:, c0:c0 + ff_chunk])
        act = (gate * jax.nn.sigmoid(gate) * up).astype(BF16)
        part = _dot(act, wd_ref[c0:c0 + ff_chunk, :])
        ffn = part if ffn is None else ffn + part
    y = x + ffn
    if final:
        y = y * lax.rsqrt(jnp.mean(y * y, axis=-1, keepdims=True) + EPS) * gf_ref[...]
    y_ref[...] = y


def _out_ffn(x2, o_a, o_b, o_c, o_d, w_out, gain, w_gate, w_up, w_down, gain_final, final, tm=ROW_TILE):
    n = x2.shape[0]
    seq = o_a.shape[2]
    tm = min(tm, seq)
    nblk_s = seq // tm
    row = lambda w: pl.BlockSpec((tm, w), lambda i: (i, 0))
    mixer = pl.BlockSpec((1, GROUP_WIDTH, tm), lambda i: (i // nblk_s, 0, i % nblk_s))
    const = lambda a: pl.BlockSpec(a.shape, lambda i: (0, 0), pipeline_mode=pl.Buffered(1))
    gain = gain.reshape(1, D_MODEL)
    gain_final = gain_final.reshape(1, D_MODEL)
    kern = functools.partial(_out_ffn_kernel, final=final, ff_chunk=256)
    return pl.pallas_call(
        kern, grid=(n // tm,),
        in_specs=[row(D_MODEL)] + [mixer] * 4
                 + [const(w_out), const(gain), const(w_gate), const(w_up), const(w_down),
                    const(gain_final)],
        out_specs=row(D_MODEL),
        out_shape=jax.ShapeDtypeStruct((n, D_MODEL), F32),
        compiler_params=_cparams(("parallel",)), name="out_ffn",
    )(x2, o_a, o_b, o_c, o_d, w_out, gain, w_gate, w_up, w_down, gain_final)


def kernel(x, norm_mix, w_in, nsa_cmp_pos, nsa_cmp_w, diff_lambda, diff_norm, mlstm_conv_w,
           mlstm_conv_b, mlstm_wq, mlstm_wk, mlstm_gate_b, mlstm_norm, w_out, norm_ffn, w_gate,
           w_up, w_down, norm_final):
    bsz, seq, _ = x.shape
    depth = w_in.shape[0]
    tables = _lane_tables(seq)
    x2 = x.reshape(bsz * seq, D_MODEL)
    r3 = lambda t: t.reshape(bsz, seq, t.shape[-1])

    for layer in range(depth):
        (a_q, a_qr, a_kvc, a_ks, a_vs, a_kw, a_vw, smalls, b_q, b_k, b_v, c_u, c_v, c_o,
         d_q, d_k, d_v) = _in_proj(x2, norm_mix[layer], w_in[layer], tables, seq,
                                   tm=min(ROW_TILE, seq))
        smalls3 = r3(smalls)

        kc, kvct = _compress(r3(a_kvc), nsa_cmp_w[layer], nsa_cmp_pos[layer])
        o_a = _nsa(a_q, a_qr, kc, kvct, r3(a_ks), a_vs, r3(a_kw), a_vw, smalls3)

        lam_init = 0.8 - 0.6 * math.exp(-0.3 * layer)
        o_b = _diff(b_q, r3(b_k), b_v, diff_lambda[layer], diff_norm[layer], lam_init)

        o_c = _mlstm(r3(c_u), r3(c_v), smalls3, r3(c_o),
                     mlstm_conv_w[layer], mlstm_conv_b[layer], mlstm_wq[layer], mlstm_wk[layer],
                     mlstm_gate_b[layer], mlstm_norm[layer])

        o_d = _dilated(d_q, r3(d_k), d_v)

        x2 = _out_ffn(x2, o_a, o_b, o_c, o_d, w_out[layer].astype(BF16), norm_ffn[layer],
                      w_gate[layer].astype(BF16), w_up[layer].astype(BF16),
                      w_down[layer].astype(BF16), norm_final, final=(layer == depth - 1))
    return x2.reshape(bsz, seq, D_MODEL)
```

```python
import functools
import math

import numpy as np
import jax
import jax.numpy as jnp
from jax import lax
from jax.experimental import pallas as pl
from jax.experimental.pallas import tpu as pltpu

F32 = jnp.float32
BF16 = jnp.bfloat16

D_MODEL = 1024
HEAD_DIM = 64
N_HEADS = 4
GROUP_WIDTH = N_HEADS * HEAD_DIM
ROPE_THETA = 10000.0
EPS = 1e-6
NEG = -1e30
LOG2E = math.log2(math.e)

CMP_LEN = 32
CMP_STRIDE = 16
SLC_LEN = 64
TOP_N = 16
NSA_WINDOW = 512
FORCED_LOCAL = 2
DIFF_HALF = HEAD_DIM // 2
MLSTM_CHUNK = 64
MLSTM_CONV = 4
DILATED_PATTERNS = ((128, 1), (512, 4), (2048, 16))
D_FF = ((8 * D_MODEL + 3 * 256 - 1) // (3 * 256)) * 256

LANES = 128
BF16_SUBLANES = 16
VMEM_LIMIT = 56 * 1024 * 1024
ACC_ROWS = HEAD_DIM + BF16_SUBLANES

ROW_TILE = 512
ATTN_TILE = 256
MLSTM_ROWS = 256

IN_SPLITS = (
    GROUP_WIDTH, HEAD_DIM, HEAD_DIM, HEAD_DIM, HEAD_DIM, HEAD_DIM, HEAD_DIM, 3 * N_HEADS,
    GROUP_WIDTH, GROUP_WIDTH, GROUP_WIDTH,
    GROUP_WIDTH, GROUP_WIDTH, N_HEADS, N_HEADS, GROUP_WIDTH,
    GROUP_WIDTH, GROUP_WIDTH, GROUP_WIDTH,
)
(A_Q, A_KC, A_VC, A_KS, A_VS, A_KW, A_VW, A_G, B_Q, B_K, B_V,
 C_U, C_V, C_I, C_F, C_O, D_Q, D_K, D_V) = range(19)
SM_AG, SM_CI, SM_CF = 0, 12, 16


def _cparams(sem):
    return pltpu.CompilerParams(dimension_semantics=sem, vmem_limit_bytes=VMEM_LIMIT)


def _iota(shape, dim):
    return lax.broadcasted_iota(jnp.int32, shape, dim)


def _dot(a, b):
    return jnp.dot(a, b, preferred_element_type=F32)


def _bf16_pieces(x):
    hi = x.astype(BF16)
    rest = x - hi.astype(F32)
    mid = rest.astype(BF16)
    return hi, mid, (rest - mid.astype(F32)).astype(BF16)


PACKED_CHUNKS = 24
D_IN = sum(IN_SPLITS)
D_IN_ALIGNED = (D_IN // LANES) * LANES


def _column_plan():
    offs = np.concatenate([[0], np.cumsum(IN_SPLITS)])
    order = [A_Q, A_KC, A_VC, A_KS, A_KW, A_VS, A_VW,
             A_G, C_I, C_F, ('pad', LANES - 20),
             B_Q, B_K, B_V, C_U, C_V, C_O, D_Q, D_K, D_V]
    src = []
    for item in order:
        if isinstance(item, tuple):
            src += [-1] * item[1]
        else:
            src += list(range(int(offs[item]), int(offs[item + 1])))
    src = np.asarray(src)
    assert src.size == PACKED_CHUNKS * LANES
    terms, mats = [], []

    def add(dst, from_tail, start, width, rows, lanes):
        sel = np.zeros((2 * LANES, LANES), np.float32)
        sel[rows, lanes] = 1.0
        terms.append((dst, from_tail, int(start), int(width)))
        mats.append(sel)

    for j in range(PACKED_CHUNKS):
        cols = src[j * LANES:(j + 1) * LANES]
        tail = cols >= D_IN_ALIGNED
        if tail.any():
            add(j, True, 0, LANES, cols[tail] - D_IN_ALIGNED, np.nonzero(tail)[0])
        todo = (cols >= 0) & ~tail
        while todo.any():
            start = (cols[todo].min() // LANES) * LANES
            width = 2 * LANES if start + 2 * LANES <= D_IN_ALIGNED else LANES
            take = todo & (cols < start + width)
            add(j, False, start, width, cols[take] - start, np.nonzero(take)[0])
            todo &= ~take
    return tuple(terms), np.stack(mats)


def _rope_tables(seq, dim):
    inv = 1.0 / (ROPE_THETA ** (jnp.arange(0, dim, 2, dtype=F32) / dim))
    ang = jnp.arange(seq, dtype=F32)[:, None] * inv[None, :]
    return jnp.cos(ang), jnp.sin(ang)


def _lane_tables(seq):
    c64, s64 = _rope_tables(seq, HEAD_DIM)
    c32, s32 = _rope_tables(seq, DIFF_HALF)
    t64c = jnp.concatenate([c64, c64] * (LANES // HEAD_DIM), axis=1)
    t64s = jnp.concatenate([-s64, s64] * (LANES // HEAD_DIM), axis=1)
    t32c = jnp.concatenate([c32, c32] * (LANES // DIFF_HALF), axis=1)
    t32s = jnp.concatenate([-s32, s32] * (LANES // DIFF_HALF), axis=1)
    pos = np.arange(seq)[:, None]
    lane = np.arange(LANES)[None, :]
    onehot = ((lane >= HEAD_DIM) & ((pos // SLC_LEN) % HEAD_DIM == lane - HEAD_DIM)
              ).astype(np.float32)
    return t64c, t64s, t32c, t32s, jnp.asarray(onehot)


def _swap_halves(x, group):
    width = x.shape[-1]
    half = group // 2
    lane = _iota(x.shape, 1) & (group - 1)
    up = pltpu.roll(x, width - half, axis=1)
    down = pltpu.roll(x, half, axis=1)
    return jnp.where(lane < half, up, down)


def _rope(x, cos_t, sin_t, group):
    reps = x.shape[-1] // LANES
    if reps > 1:
        cos_t = jnp.concatenate([cos_t] * reps, axis=1)
        sin_t = jnp.concatenate([sin_t] * reps, axis=1)
    return x * cos_t + _swap_halves(x, group) * sin_t


def _in_proj_kernel(x_ref, g_ref, wraw_ref, wtail_ref, sel_ref, c64_ref, s64_ref, c32_ref, s32_ref,
                    oh_ref,
                    aq_ref, aqr_ref, akvc_ref, aks_ref, avs_ref, akw_ref, avw_ref, sm_ref,
                    bq_ref, bk_ref, bv_ref, cu_ref, cv_ref, co_ref, dq_ref, dk_ref, dv_ref,
                    w_ref, *, terms):
    @pl.when(pl.program_id(0) == 0)
    def _():
        for j in range(PACKED_CHUNKS):
            chunk = jnp.zeros((D_MODEL, LANES), F32)
            for t, (dst, from_tail, start, width) in enumerate(terms):
                if dst == j:
                    source = wtail_ref if from_tail else wraw_ref
                    piece = source[:, start:start + width].astype(BF16)
                    chunk = chunk + _dot(piece, sel_ref[t][0:width])
            w_ref[:, j * LANES:(j + 1) * LANES] = chunk.astype(BF16)

    x = x_ref[...]
    h = x * lax.rsqrt(jnp.mean(x * x, axis=-1, keepdims=True) + EPS)
    h = (h * g_ref[...]).astype(BF16)
    c64, s64 = c64_ref[...], s64_ref[...]
    c32, s32 = c32_ref[...], s32_ref[...]

    def mm(c0, c1):
        return _dot(h, w_ref[:, c0 * LANES:c1 * LANES])

    def channel_major(ref, z):
        ref[0] = z.T.astype(ref.dtype)

    zq = mm(0, 2) * (HEAD_DIM ** -0.5 * LOG2E)
    channel_major(aq_ref, zq)
    channel_major(aqr_ref, _rope(zq, c64, s64, HEAD_DIM))
    akvc_ref[...] = mm(2, 3).astype(BF16)
    keys = _rope(mm(3, 4), c64, s64, HEAD_DIM)
    low = _iota(keys.shape, 1) < HEAD_DIM
    aks_ref[...] = (jnp.where(low, keys, 0.0) + oh_ref[...]).astype(BF16)
    akw_ref[...] = jnp.where(low, pltpu.roll(keys, HEAD_DIM, axis=1), 0.0).astype(BF16)
    vals_t = mm(4, 5).T
    avs_ref[0] = vals_t[0:HEAD_DIM].astype(BF16)
    avw_ref[0] = vals_t[HEAD_DIM:2 * HEAD_DIM].astype(BF16)
    sm_ref[...] = mm(5, 6)
    channel_major(bq_ref, _rope(mm(6, 8), c32, s32, DIFF_HALF) * (DIFF_HALF ** -0.5 * LOG2E))
    bk_ref[...] = _rope(mm(8, 10), c32, s32, DIFF_HALF).astype(BF16)
    channel_major(bv_ref, mm(10, 12))
    cu_ref[...] = mm(12, 14)
    cv_ref[...] = mm(14, 16).astype(BF16)
    co_ref[...] = mm(16, 18)
    channel_major(dq_ref, _rope(mm(18, 20), c64, s64, HEAD_DIM) * (HEAD_DIM ** -0.5 * LOG2E))
    dk_ref[...] = _rope(mm(20, 22), c64, s64, HEAD_DIM).astype(BF16)
    channel_major(dv_ref, mm(22, 24))


_IN_PROJ_OUTS = (
    (256, BF16, True), (256, BF16, True), (128, BF16, False), (128, BF16, False),
    (HEAD_DIM, BF16, True), (128, BF16, False), (HEAD_DIM, BF16, True), (128, F32, False),
    (256, BF16, True), (256, BF16, False), (256, BF16, True), (256, F32, False),
    (256, BF16, False), (256, F32, False), (256, BF16, True), (256, BF16, False),
    (256, BF16, True))


def _in_proj(x2, gain, w_all, layer, tables, seq, tm):
    n = x2.shape[0]
    nblk_s = seq // tm
    terms, select = _column_plan()
    select = jnp.asarray(select, BF16)
    w_tail = jnp.pad(w_all[layer, :, D_IN_ALIGNED:], ((0, 0), (0, LANES - (D_IN - D_IN_ALIGNED))))
    row = lambda i: (i, 0)
    tab = lambda i: (i % nblk_s, 0)
    const = lambda i: (0, 0)
    in_specs = [pl.BlockSpec((tm, D_MODEL), row),
                pl.BlockSpec((1, D_MODEL), const),
                pl.BlockSpec((None,) + w_all.shape[1:], lambda i: (layer, 0, 0),
                             pipeline_mode=pl.Buffered(1)),
                pl.BlockSpec(w_tail.shape, const),
                pl.BlockSpec(select.shape, lambda i: (0, 0, 0))]
    in_specs += [pl.BlockSpec((tm, LANES), tab)] * 5
    out_specs = [pl.BlockSpec((1, w, tm), lambda i: (i // nblk_s, 0, i % nblk_s)) if cmaj
                 else pl.BlockSpec((tm, w), row) for w, _, cmaj in _IN_PROJ_OUTS]
    out_shape = [jax.ShapeDtypeStruct((n // seq, w, seq) if cmaj else (n, w), dt)
                 for w, dt, cmaj in _IN_PROJ_OUTS]
    return pl.pallas_call(
        functools.partial(_in_proj_kernel, terms=terms), grid=(n // tm,), in_specs=in_specs,
        out_specs=out_specs, out_shape=out_shape,
        scratch_shapes=[pltpu.VMEM((D_MODEL, PACKED_CHUNKS * LANES), BF16)],
        compiler_params=_cparams(("arbitrary",)), name="in_proj",
    )(x2, gain.reshape(1, D_MODEL), w_all, w_tail, select, *tables)


def _compress_kernel(r_ref, w_ref, pos_ref, kc_ref, kvct_ref):
    wk = w_ref[0].reshape(CMP_LEN, HEAD_DIM, HEAD_DIM)
    wv = w_ref[1].reshape(CMP_LEN, HEAD_DIM, HEAD_DIM)
    zeros = jnp.zeros_like(wk)
    full = jnp.concatenate([jnp.concatenate([wk, zeros], axis=2),
                            jnp.concatenate([zeros, wv], axis=2)], axis=1)
    w_first = full[0:CMP_STRIDE].reshape(CMP_STRIDE * LANES, LANES).astype(BF16)
    w_second = full[CMP_STRIDE:].reshape(CMP_STRIDE * LANES, LANES).astype(BF16)

    r = r_ref[0]
    first = _dot(r, w_first)
    second = _dot(r, w_second)
    nrow = first.shape[0]
    pos = pos_ref[...].astype(BF16)
    half = pos.shape[1] // 2
    const = _dot(pos[:, :half], w_first) + _dot(pos[:, half:], w_second)
    out = first + pltpu.roll(second, nrow - 1, axis=0) + const[0:1, :]
    lane = _iota(out.shape, 1)
    kc_ref[0] = jnp.where(lane < HEAD_DIM, out, 0.0).astype(BF16)
    kvct_ref[0] = out.T.astype(BF16)


def _compress(a_kvc, cmp_w, cmp_pos):
    bsz, seq, _ = a_kvc.shape
    nrow = seq // CMP_STRIDE
    r = a_kvc.reshape(bsz, nrow, CMP_STRIDE * LANES)
    pos = jnp.concatenate([cmp_pos[0], cmp_pos[1]], axis=-1)
    pos = jnp.broadcast_to(pos.reshape(1, CMP_LEN * LANES), (8, CMP_LEN * LANES))
    return pl.pallas_call(
        _compress_kernel, grid=(bsz,),
        in_specs=[pl.BlockSpec((1, nrow, CMP_STRIDE * LANES), lambda b: (b, 0, 0)),
                  pl.BlockSpec(cmp_w.shape, lambda b: (0, 0, 0)),
                  pl.BlockSpec(pos.shape, lambda b: (0, 0))],
        out_specs=[pl.BlockSpec((1, nrow, LANES), lambda b: (b, 0, 0)),
                   pl.BlockSpec((1, LANES, nrow), lambda b: (b, 0, 0))],
        out_shape=[jax.ShapeDtypeStruct((bsz, nrow, LANES), BF16),
                   jax.ShapeDtypeStruct((bsz, LANES, nrow), BF16)],
        compiler_params=_cparams(("parallel",)), name="nsa_compress",
    )(r, cmp_w, pos)


def _flash_init(m_ref, acc_ref):
    m_ref[...] = jnp.full(m_ref.shape, NEG, F32)
    acc_ref[...] = jnp.zeros(acc_ref.shape, F32)


def _flash_step(s, values, m_ref, acc_ref):
    width = acc_ref.shape[1]
    chunk = min(2 * LANES, width)
    for g, vals in enumerate(values):
        rows = slice(g * ACC_ROWS, (g + 1) * ACC_ROWS)
        for c0 in range(0, width, chunk):
            lanes = slice(g * width + c0, g * width + c0 + chunk)
            s_c = s[:, lanes]
            m_old = m_ref[:, lanes]
            m_new = jnp.maximum(m_old, jnp.max(s_c, axis=0, keepdims=True))
            alpha = jnp.exp2(m_old - m_new)
            p = jnp.exp2(s_c - m_new).astype(BF16)
            acc_ref[rows, c0:c0 + chunk] = alpha * acc_ref[rows, c0:c0 + chunk] + _dot(vals, p)
            m_ref[:, lanes] = m_new


def _pipelined_tiles(lo, hi, scores, consume, sa_ref, sb_ref):
    n = hi - lo

    def put(ref, tiles):
        for g, tile in enumerate(tiles):
            ref[g] = tile

    def get(ref):
        return [ref[g] for g in range(ref.shape[0])]

    put(sa_ref, scores(lo))

    def body(i, carry):
        j = lo + 2 * i
        put(sb_ref, scores(j + 1))
        consume(get(sa_ref), j, False)
        put(sa_ref, scores(j + 2))
        consume(get(sb_ref), j + 1, False)
        return carry

    lax.fori_loop(0, n // 2, body, 0)

    @pl.when(n % 2 == 0)
    def _():
        consume(get(sa_ref), hi, True)

    @pl.when(n % 2 == 1)
    def _():
        put(sb_ref, scores(hi))
        consume(get(sa_ref), hi - 1, False)
        consume(get(sb_ref), hi, True)


def _with_ones(vt):
    return jnp.concatenate([vt, jnp.ones((BF16_SUBLANES, vt.shape[1]), BF16)], axis=0)


def _cmp_to_slc_t(seq):
    n_cmp = (seq - CMP_LEN) // CMP_STRIDE + 1
    n_slc = seq // SLC_LEN
    ratio_s, ratio_c = SLC_LEN // CMP_STRIDE, CMP_LEN // CMP_STRIDE
    jj = np.arange(n_slc)[:, None, None]
    src = ratio_s * jj - np.arange(ratio_s)[None, :, None] - np.arange(ratio_c)[None, None, :]
    ok = (src >= 0) & (src < n_cmp)
    m = np.zeros((seq // CMP_STRIDE, n_slc), np.float32)
    np.add.at(m, (np.where(ok, src, 0), np.broadcast_to(jj, src.shape)), ok.astype(np.float32))
    return jnp.asarray(m.T, BF16)


def _nsa_kernel(qt_ref, qrt_ref, kc_ref, kvct_ref, ks_ref, vst_ref, kw_ref, vwt_ref, sm_ref,
                c2st_ref, wmask_ref, o_ref, m_ref, acc_ref, sa_ref, sb_ref, *, tq, top_n):
    qi = pl.program_id(1)
    s0 = qi * tq
    nb = qt_ref.shape[0]
    rows = N_HEADS * tq
    n_slc = c2st_ref.shape[0]
    lane_t = s0 + (_iota((1, rows), 1) & (tq - 1))
    c2st = c2st_ref[...]
    blk = _iota((n_slc, 1), 0)
    cur = (s0 + _iota((1, tq), 1)) >> 6
    forced = (blk == 0) | ((blk <= cur) & (blk > cur - FORCED_LOCAL))

    def heads_on_lanes(x):
        return jnp.concatenate([x[h * HEAD_DIM:(h + 1) * HEAD_DIM, :] for h in range(N_HEADS)],
                               axis=1)

    def compressed_and_selection(b):
        q4 = jnp.concatenate([heads_on_lanes(qt_ref[b]), jnp.zeros((HEAD_DIM, rows), BF16)], axis=0)
        sc = _dot(kc_ref[b], q4)
        cmask = (_iota((sc.shape[0], 1), 0) * CMP_STRIDE + (CMP_LEN - 1)) <= lane_t
        sc = jnp.where(cmask, sc, NEG)
        e = jnp.where(cmask, jnp.exp2(sc - jnp.max(sc, axis=0, keepdims=True)), 0.0)
        z = jnp.sum(e, axis=0, keepdims=True)
        p_cmp = e * (1.0 / jnp.where(z > 0, z, 1.0))
        o_cmp = _dot(kvct_ref[b][HEAD_DIM:2 * HEAD_DIM, :], p_cmp.astype(BF16))
        p_heads = p_cmp[:, 0:tq]
        for h in range(1, N_HEADS):
            p_heads = p_heads + p_cmp[:, h * tq:(h + 1) * tq]
        p_hi = p_heads.astype(BF16)
        p_lo = (p_heads - p_hi.astype(F32)).astype(BF16)
        imp = _dot(c2st, p_hi) + _dot(c2st, p_lo)

        score = jnp.where(blk > cur, -1.0e6, jnp.where(forced, 1.0e6, imp))
        rank = jnp.zeros((n_slc, tq), F32)
        for i in range(n_slc):
            s_i = score[i:i + 1, :]
            rank = rank + jnp.where(blk > i, jnp.where(s_i >= score, 1.0, 0.0),
                                    jnp.where(s_i > score, 1.0, 0.0))
        sel = (rank < top_n) & (blk <= cur)
        bias = jnp.where(sel, 0.0, NEG)
        if n_slc < HEAD_DIM:
            bias = jnp.concatenate([bias, jnp.zeros((HEAD_DIM - n_slc, tq), F32)], axis=0)
        bias4 = jnp.concatenate([bias] * N_HEADS, axis=1).astype(BF16)
        qsel = jnp.concatenate([heads_on_lanes(qrt_ref[b]), bias4], axis=0)
        return o_cmp, qsel

    prepared = [compressed_and_selection(b) for b in range(nb)]

    def attend(k_ref, vt_ref, lo, window):
        _flash_init(m_ref, acc_ref)

        def scores(j):
            k0 = pl.multiple_of(j * tq, tq)
            return [_dot(k_ref[b, pl.ds(k0, tq), :], prepared[b][1]) for b in range(nb)]

        def consume(tiles, j, diagonal):
            k0 = pl.multiple_of(j * tq, tq)
            if window:
                far = qi - NSA_WINDOW // tq
                mask = wmask_ref[jnp.where(j == qi, 0, jnp.where(j == far, 2, 1))]
            elif diagonal:
                mask = wmask_ref[0]
            for b, s in enumerate(tiles):
                if window or diagonal:
                    s = s + mask
                _flash_step(s, [_with_ones(vt_ref[b, :, pl.ds(k0, tq)])], m_ref.at[b], acc_ref.at[b])

        _pipelined_tiles(lo, qi, scores, consume, sa_ref, sb_ref)
        return [acc_ref[b, 0:HEAD_DIM, :] * (1.0 / acc_ref[b, HEAD_DIM:HEAD_DIM + 1, :])
                for b in range(nb)]

    o_slc = attend(ks_ref, vst_ref, 0, False)
    o_win = attend(kw_ref, vwt_ref, jnp.maximum(qi - NSA_WINDOW // tq, 0), True)

    for b in range(nb):
        g = jax.nn.sigmoid(sm_ref[b].T[0:BF16_SUBLANES, :])

        def gate(branch):
            return jnp.concatenate(
                [g[branch * N_HEADS + h:branch * N_HEADS + h + 1, :] for h in range(N_HEADS)], axis=1)

        o = gate(0) * prepared[b][0] + gate(1) * o_slc[b] + gate(2) * o_win[b]
        for h in range(N_HEADS):
            o_ref[b, h * HEAD_DIM:(h + 1) * HEAD_DIM, :] = (
                o[:, h * tq:(h + 1) * tq].astype(o_ref.dtype))


def _nsa(qt, qrt, kc, kvct, ks, vst, kw, vwt, smalls, tq=ATTN_TILE):
    bsz, _, seq = qt.shape
    tq = min(tq, seq)
    nb = 2 if bsz % 2 == 0 else 1
    n_slc = seq // SLC_LEN
    c2st = _cmp_to_slc_t(seq)
    rows = N_HEADS * tq
    qspec = pl.BlockSpec((nb, GROUP_WIDTH, tq), lambda b, i: (b, 0, i))
    full = lambda a: pl.BlockSpec((nb,) + a.shape[1:], lambda b, i: (b, 0, 0))
    vspec = pl.BlockSpec((nb, HEAD_DIM, seq), lambda b, i: (b, 0, 0))
    assert NSA_WINDOW % tq == 0
    k_off = np.arange(tq)[:, None]
    q_off = np.tile(np.arange(tq), N_HEADS)[None, :]
    visible = np.stack([k_off <= q_off, np.ones((tq, rows), bool), k_off > q_off])
    wmask = jnp.asarray(np.where(visible, 0.0, NEG).astype(np.float32))
    kern = functools.partial(_nsa_kernel, tq=tq, top_n=min(TOP_N, n_slc))
    return pl.pallas_call(
        kern, grid=(bsz // nb, seq // tq),
        in_specs=[qspec, qspec, full(kc), full(kvct), full(ks), vspec, full(kw), vspec,
                  pl.BlockSpec((nb, tq, LANES), lambda b, i: (b, i, 0)),
                  pl.BlockSpec(c2st.shape, lambda b, i: (0, 0)),
                  pl.BlockSpec(wmask.shape, lambda b, i: (0, 0, 0), pipeline_mode=pl.Buffered(1))],
        out_specs=qspec,
        out_shape=jax.ShapeDtypeStruct(qt.shape, BF16),
        scratch_shapes=[pltpu.VMEM((nb, 1, rows), F32), pltpu.VMEM((nb, ACC_ROWS, rows), F32),
                        pltpu.VMEM((nb, tq, rows), F32), pltpu.VMEM((nb, tq, rows), F32)],
        compiler_params=_cparams(("parallel", "arbitrary")), name="nsa_attention",
    )(qt, qrt, kc, kvct, ks, vst, kw, vwt, smalls, c2st, wmask)


def _diff_kernel(qt_ref, k_ref, vt_ref, lam_ref, g_ref, o_ref, m_ref, acc_ref, sa_ref, sb_ref,
                 *, tq, lam_init):
    qi = pl.program_id(1)
    s0 = qi * tq
    rows = 4 * tq
    pairs = N_HEADS // 2
    pw = 2 * HEAD_DIM
    row = _iota((pw, 1), 0)
    lane_t = s0 + (_iota((1, rows), 1) & (tq - 1))

    def query_matrix(p):
        qt = qt_ref[0, p * pw:(p + 1) * pw, :]
        zero = jnp.zeros_like(qt)
        return jnp.concatenate(
            [jnp.where((row >= DIFF_HALF * c) & (row < DIFF_HALF * (c + 1)), qt, zero)
             for c in range(4)], axis=1)

    qmats = [query_matrix(p) for p in range(pairs)]

    def scores(j):
        k0 = pl.multiple_of(j * tq, tq)
        return [_dot(k_ref[0, pl.ds(k0, tq), p * pw:(p + 1) * pw], qmats[p]) for p in range(pairs)]

    def consume(tiles, j, diagonal):
        k0 = pl.multiple_of(j * tq, tq)
        for p, s in enumerate(tiles):
            if diagonal:
                s = jnp.where(k0 + _iota((tq, 1), 0) <= lane_t, s, NEG)
            vt = vt_ref[0, p * pw:(p + 1) * pw, pl.ds(k0, tq)]
            _flash_step(s, [_with_ones(vt[0:HEAD_DIM]), _with_ones(vt[HEAD_DIM:])],
                        m_ref.at[p], acc_ref.at[p])

    _flash_init(m_ref, acc_ref)
    _pipelined_tiles(0, qi, scores, consume, sa_ref, sb_ref)

    lv = lam_ref[...]
    lam = (jnp.exp(jnp.sum(lv[0:1] * lv[1:2], axis=-1, keepdims=True))
           - jnp.exp(jnp.sum(lv[2:3] * lv[3:4], axis=-1, keepdims=True)) + lam_init)
    for h in range(N_HEADS):
        a = acc_ref[h // 2, ACC_ROWS * (h % 2):ACC_ROWS * (h % 2 + 1), :]
        o_all = a[0:HEAD_DIM] * (1.0 / a[HEAD_DIM:HEAD_DIM + 1])
        o = o_all[:, :tq] - lam * o_all[:, tq:]
        y = o * lax.rsqrt(jnp.mean(o * o, axis=0, keepdims=True) + EPS)
        o_ref[0, HEAD_DIM * h:HEAD_DIM * (h + 1), :] = (
            (y * g_ref[...]) * (1.0 - lam_init)).astype(o_ref.dtype)


def _diff(qt, k, vt, lam_vecs, sub_g, lam_init, tq=ATTN_TILE):
    bsz, _, seq = qt.shape
    tq = min(tq, seq)
    pairs = N_HEADS // 2
    qspec = pl.BlockSpec((1, GROUP_WIDTH, tq), lambda b, i: (b, 0, i))
    kern = functools.partial(_diff_kernel, tq=tq, lam_init=lam_init)
    return pl.pallas_call(
        kern, grid=(bsz, seq // tq),
        in_specs=[qspec,
                  pl.BlockSpec((1, seq, GROUP_WIDTH), lambda b, i: (b, 0, 0)),
                  pl.BlockSpec((1, GROUP_WIDTH, seq), lambda b, i: (b, 0, 0)),
                  pl.BlockSpec(lam_vecs.shape, lambda b, i: (0, 0)),
                  pl.BlockSpec((HEAD_DIM, 1), lambda b, i: (0, 0))],
        out_specs=qspec,
        out_shape=jax.ShapeDtypeStruct(qt.shape, BF16),
        scratch_shapes=[pltpu.VMEM((pairs, 1, 4 * tq), F32),
                        pltpu.VMEM((pairs, 2 * ACC_ROWS, 2 * tq), F32),
                        pltpu.VMEM((pairs, tq, 4 * tq), F32), pltpu.VMEM((pairs, tq, 4 * tq), F32)],
        compiler_params=_cparams(("parallel", "arbitrary")), name="diff_attention",
    )(qt, k, vt, lam_vecs, sub_g.reshape(HEAD_DIM, 1))


def _dilated_bias(tq):
    max_back = max(w for w, _ in DILATED_PATTERNS) // tq
    classes = [0, 1, 2, 3, max_back]
    q = np.arange(tq)[None, :]
    k = np.arange(tq)[:, None]
    out = []
    for d in classes:
        delta = d * tq + q - k
        cnt = np.zeros((tq, tq), np.float64)
        for w, dil in DILATED_PATTERNS:
            cnt += (delta >= 0) & (delta <= w) & (delta % dil == 0)
        tab = np.where(cnt > 0, np.log2(np.maximum(cnt, 1.0)), NEG)
        out.append(np.concatenate([tab, tab], axis=1))
    return jnp.asarray(np.stack(out).astype(np.float32)), max_back


def _dilated_kernel(qt_ref, k_ref, vt_ref, bias_ref, o_ref, m_ref, acc_ref, sa_ref, sb_ref,
                    *, tq, max_back):
    qi = pl.program_id(1)
    nb = qt_ref.shape[0]
    pairs = N_HEADS // 2
    pw = 2 * HEAD_DIM
    row = _iota((pw, 1), 0)
    streams = [(b, p) for b in range(nb) for p in range(pairs)]

    def query_matrix(b, p):
        qt = qt_ref[b, p * pw:(p + 1) * pw, :]
        zero = jnp.zeros_like(qt)
        return jnp.concatenate([jnp.where(row < HEAD_DIM, qt, zero),
                                jnp.where(row >= HEAD_DIM, qt, zero)], axis=1)

    qmats = [query_matrix(b, p) for b, p in streams]
    _flash_init(m_ref, acc_ref)

    def scores(j):
        k0 = pl.multiple_of(j * tq, tq)
        return [_dot(k_ref[b, pl.ds(k0, tq), p * pw:(p + 1) * pw], qmats[g])
                for g, (b, p) in enumerate(streams)]

    def consume(tiles, j, diagonal):
        del diagonal
        k0 = pl.multiple_of(j * tq, tq)
        d = qi - j
        bias = bias_ref[jnp.where(d < 3, d, jnp.where(d == max_back, 4, 3))]
        for g, (b, p) in enumerate(streams):
            vt = vt_ref[b, p * pw:(p + 1) * pw, pl.ds(k0, tq)]
            _flash_step(tiles[g] + bias, [_with_ones(vt[0:HEAD_DIM]), _with_ones(vt[HEAD_DIM:])],
                        m_ref.at[g], acc_ref.at[g])

    _pipelined_tiles(jnp.maximum(qi - max_back, 0), qi, scores, consume, sa_ref, sb_ref)
    for g, (b, p) in enumerate(streams):
        for hh in range(2):
            a = acc_ref[g, ACC_ROWS * hh:ACC_ROWS * (hh + 1), :]
            h = 2 * p + hh
            o_ref[b, HEAD_DIM * h:HEAD_DIM * (h + 1), :] = (
                a[0:HEAD_DIM] * (1.0 / a[HEAD_DIM:HEAD_DIM + 1])).astype(o_ref.dtype)


def _dilated(qt, k, vt, tq=ATTN_TILE):
    bsz, _, seq = qt.shape
    tq = min(tq, seq)
    nb = 1
    streams = nb * (N_HEADS // 2)
    bias, max_back = _dilated_bias(tq)
    qspec = pl.BlockSpec((nb, GROUP_WIDTH, tq), lambda b, i: (b, 0, i))
    kern = functools.partial(_dilated_kernel, tq=tq, max_back=max_back)
    return pl.pallas_call(
        kern, grid=(bsz // nb, seq // tq),
        in_specs=[qspec,
                  pl.BlockSpec((nb, seq, GROUP_WIDTH), lambda b, i: (b, 0, 0)),
                  pl.BlockSpec((nb, GROUP_WIDTH, seq), lambda b, i: (b, 0, 0)),
                  pl.BlockSpec(bias.shape, lambda b, i: (0, 0, 0))],
        out_specs=qspec,
        out_shape=jax.ShapeDtypeStruct(qt.shape, BF16),
        scratch_shapes=[pltpu.VMEM((streams, 1, 2 * tq), F32),
                        pltpu.VMEM((streams, 2 * ACC_ROWS, tq), F32),
                        pltpu.VMEM((streams, tq, 2 * tq), F32),
                        pltpu.VMEM((streams, tq, 2 * tq), F32)],
        compiler_params=_cparams(("parallel", "arbitrary")), name="dilated_attention",
    )(qt, k, vt, bias)


def _mlstm_kernel(u_ref, up_ref, v_ref, sm_ref, o_ref, cw_ref, cb_ref, wqt_ref, wk_ref,
                  gb_ref, gcol_ref, hg_ref, out_ref, c_st, m_st):
    ci = pl.program_id(1)
    rows = u_ref.shape[1]

    @pl.when(ci == 0)
    def _():
        c_st[...] = jnp.zeros(c_st.shape, F32)
        m_st[...] = jnp.zeros(m_st.shape, F32)

    tail = jnp.where(ci > 0, up_ref[0], 0.0)
    ext = jnp.concatenate([tail, u_ref[0]], axis=0)
    cw = cw_ref[...]
    uc = cb_ref[...] + cw[MLSTM_CONV - 1:MLSTM_CONV] * ext[8:]
    for j in range(MLSTM_CONV - 1):
        shifted = pltpu.roll(ext, MLSTM_CONV - 1 - j, axis=0)[8:]
        uc = uc + cw[j:j + 1] * shifted
    uc = uc * jax.nn.sigmoid(uc)

    qt_all = _dot(wqt_ref[...], uc.T.astype(BF16))
    k_all = _dot(uc.astype(BF16), wk_ref[...]) * (HEAD_DIM ** -0.5)
    vt_all = v_ref[0].astype(F32).T.astype(BF16)
    ogt = jax.nn.sigmoid(o_ref[0]).T
    sm = sm_ref[0]
    smt = sm.T
    gb = gb_ref[...]
    gcol = gcol_ref[...]
    upper = _iota((rows, rows), 0) <= _iota((rows, rows), 1)
    lower = _iota((rows, rows), 1) <= _iota((rows, rows), 0)
    ig_rows = smt[8:16] + gcol[:, 0:1]
    lf_rows = jax.nn.log_sigmoid(smt[16:24] + gcol[:, 1:2])
    tri_upper = jnp.where(upper, 1.0, 0.0).astype(BF16)
    tri_lower = jnp.where(lower, 1.0, 0.0).astype(BF16)
    b_rows = sum(_dot(piece, tri_upper) for piece in _bf16_pieces(lf_rows))
    ig_cols = sm[:, SM_CI:SM_CI + N_HEADS] + gb[0:1]
    lf_cols = jax.nn.log_sigmoid(sm[:, SM_CF:SM_CF + N_HEADS] + gb[1:2])
    b_cols = sum(_dot(tri_lower, piece) for piece in _bf16_pieces(lf_cols))
    ones = jnp.ones((BF16_SUBLANES, rows), BF16)

    for h in range(N_HEADS):
        hs = slice(h * HEAD_DIM, (h + 1) * HEAD_DIM)
        b_row = b_rows[h:h + 1]
        src_row = ig_rows[N_HEADS + h:N_HEADS + h + 1] - b_row
        src_col = ig_cols[:, h:h + 1] - b_cols[:, h:h + 1]
        dmat = jnp.where(upper, b_row + src_col, NEG)
        a = b_row[:, rows - 1:rows]
        g_end = a + src_row
        m_loc = jnp.max(g_end, axis=-1, keepdims=True)
        w_end = jnp.exp(g_end - m_loc)

        state = c_st[h]
        m_in = m_st[h]
        inter = b_row + m_in
        m_t = jnp.maximum(inter, jnp.max(dmat, axis=0, keepdims=True))
        e_inter = jnp.exp(inter - m_t)
        qt = qt_all[hs].astype(BF16)
        kb = k_all[:, hs].astype(BF16)
        values = jnp.concatenate([vt_all[hs], ones], axis=0)
        p = (_dot(kb, qt) * jnp.exp(dmat - m_t)).astype(BF16)
        from_state = _dot(state.astype(BF16), qt)
        from_chunk = _dot(values, p)
        num = e_inter * from_state[0:HEAD_DIM] + from_chunk[0:HEAD_DIM]
        den = e_inter * from_state[HEAD_DIM:HEAD_DIM + 1] + from_chunk[HEAD_DIM:HEAD_DIM + 1]
        hh = num * (1.0 / jnp.maximum(jnp.abs(den), jnp.exp(-m_t)))
        hh = hh * lax.rsqrt(jnp.mean(hh * hh, axis=0, keepdims=True) + EPS) * hg_ref[hs, :]
        out_ref[0, hs, :] = (hh * ogt[hs]).astype(out_ref.dtype)

        m_new = jnp.maximum(a + m_in, m_loc)
        decay = jnp.exp(a + m_in - m_new)
        fresh = jnp.exp(m_loc - m_new)
        local = _dot((values.astype(F32) * w_end).astype(BF16), kb)
        c_st[h] = decay * state + fresh * local
        m_st[h] = m_new


def _mlstm(u, v, smalls, o_pre, conv_w, conv_b, wq, wk, gate_b, head_g, rows=MLSTM_ROWS):
    bsz, seq, _ = u.shape
    rows = min(rows, seq)
    row = pl.BlockSpec((1, rows, GROUP_WIDTH), lambda b, c: (b, c, 0))
    const2 = lambda a: pl.BlockSpec(a.shape, lambda b, c: (0,) * a.ndim)
    conv_b = conv_b.reshape(1, GROUP_WIDTH)
    head_g = head_g.reshape(GROUP_WIDTH, 1)
    eye = jnp.eye(N_HEADS, dtype=wq.dtype)
    wqt = jnp.einsum('hde,hg->hegd', wq, eye).reshape(GROUP_WIDTH, GROUP_WIDTH).astype(BF16)
    wkb = jnp.einsum('hde,hg->hdge', wk, eye).reshape(GROUP_WIDTH, GROUP_WIDTH).astype(BF16)
    zeros4 = jnp.zeros((N_HEADS,), gate_b.dtype)
    gcol = jnp.stack([jnp.concatenate([zeros4, gate_b[0]]), jnp.concatenate([gate_b[1], zeros4])],
                     axis=1)
    return pl.pallas_call(
        _mlstm_kernel, grid=(bsz, seq // rows),
        in_specs=[row,
                  pl.BlockSpec((1, 8, GROUP_WIDTH),
                               lambda b, c: (b, jnp.maximum(c * (rows // 8) - 1, 0), 0)),
                  row,
                  pl.BlockSpec((1, rows, LANES), lambda b, c: (b, c, 0)),
                  row, const2(conv_w), const2(conv_b), const2(wqt), const2(wkb), const2(gate_b),
                  const2(gcol), const2(head_g)],
        out_specs=pl.BlockSpec((1, GROUP_WIDTH, rows), lambda b, c: (b, 0, c)),
        out_shape=jax.ShapeDtypeStruct((bsz, GROUP_WIDTH, seq), BF16),
        scratch_shapes=[pltpu.VMEM((N_HEADS, ACC_ROWS, HEAD_DIM), F32),
                        pltpu.VMEM((N_HEADS, 1, 1), F32)],
        compiler_params=_cparams(("parallel", "arbitrary")), name="mlstm",
    )(u, u, v, smalls, o_pre, conv_w, conv_b, wqt, wkb, gate_b, gcol, head_g)


def _out_ffn_kernel(x_ref, oa_ref, ob_ref, oc_ref, od_ref, wo_ref, g_ref, wg_ref, wu_ref, wd_ref,
                    gf_ref, y_ref, *, final, ff_chunk):
    mixed_t = jnp.concatenate([oa_ref[0], ob_ref[0], oc_ref[0], od_ref[0]], axis=0)
    x = x_ref[...] + lax.dot_general(mixed_t, wo_ref[...], (((0,), (0,)), ((), ())),
                                     preferred_element_type=F32)
    h = x * lax.rsqrt(jnp.mean(x * x, axis=-1, keepdims=True) + EPS)
    h = (h * g_ref[...]).astype(BF16)
    ffn = None
    for c0 in range(0, D_FF, ff_chunk):
        gate = _dot(h, wg_ref[:, c0:c0 + ff_chunk])
        up = _dot(h, wu_ref[:, c0:c0 + ff_chunk])
        act = (gate * jax.nn.sigmoid(gate) * up).astype(BF16)
        part = _dot(act, wd_ref[c0:c0 + ff_chunk, :])
        ffn = part if ffn is None else ffn + part
    y = x + ffn
    if final:
        y = y * lax.rsqrt(jnp.mean(y * y, axis=-1, keepdims=True) + EPS) * gf_ref[...]
    y_ref[...] = y


def _out_ffn(x2, o_a, o_b, o_c, o_d, w_out, gain, w_gate, w_up, w_down, gain_final, final, tm=ROW_TILE):
    n = x2.shape[0]
    seq = o_a.shape[2]
    tm = min(tm, seq)
    nblk_s = seq // tm
    row = lambda w: pl.BlockSpec((tm, w), lambda i: (i, 0))
    mixer = pl.BlockSpec((1, GROUP_WIDTH, tm), lambda i: (i // nblk_s, 0, i % nblk_s))
    const = lambda a: pl.BlockSpec(a.shape, lambda i: (0, 0), pipeline_mode=pl.Buffered(1))
    gain = gain.reshape(1, D_MODEL)
    gain_final = gain_final.reshape(1, D_MODEL)
    kern = functools.partial(_out_ffn_kernel, final=final, ff_chunk=256)
    return pl.pallas_call(
        kern, grid=(n // tm,),
        in_specs=[row(D_MODEL)] + [mixer] * 4
                 + [const(w_out), const(gain), const(w_gate), const(w_up), const(w_down),
                    const(gain_final)],
        out_specs=row(D_MODEL),
        out_shape=jax.ShapeDtypeStruct((n, D_MODEL), F32),
        compiler_params=_cparams(("parallel",)), name="out_ffn",
    )(x2, o_a, o_b, o_c, o_d, w_out, gain, w_gate, w_up, w_down, gain_final)


def kernel(x, norm_mix, w_in, nsa_cmp_pos, nsa_cmp_w, diff_lambda, diff_norm, mlstm_conv_w,
           mlstm_conv_b, mlstm_wq, mlstm_wk, mlstm_gate_b, mlstm_norm, w_out, norm_ffn, w_gate,
           w_up, w_down, norm_final):
    bsz, seq, _ = x.shape
    depth = w_in.shape[0]
    tables = _lane_tables(seq)
    x2 = x.reshape(bsz * seq, D_MODEL)
    r3 = lambda t: t.reshape(bsz, seq, t.shape[-1])

    for layer in range(depth):
        (a_q, a_qr, a_kvc, a_ks, a_vs, a_kw, a_vw, smalls, b_q, b_k, b_v, c_u, c_v, c_o,
         d_q, d_k, d_v) = _in_proj(x2, norm_mix[layer], w_in, layer, tables, seq,
                                   tm=min(ROW_TILE, seq))
        smalls3 = r3(smalls)

        kc, kvct = _compress(r3(a_kvc), nsa_cmp_w[layer], nsa_cmp_pos[layer])
        o_a = _nsa(a_q, a_qr, kc, kvct, r3(a_ks), a_vs, r3(a_kw), a_vw, smalls3)

        lam_init = 0.8 - 0.6 * math.exp(-0.3 * layer)
        o_b = _diff(b_q, r3(b_k), b_v, diff_lambda[layer], diff_norm[layer], lam_init)

        o_c = _mlstm(r3(c_u), r3(c_v), smalls3, r3(c_o),
                     mlstm_conv_w[layer], mlstm_conv_b[layer], mlstm_wq[layer], mlstm_wk[layer],
                     mlstm_gate_b[layer], mlstm_norm[layer])

        o_d = _dilated(d_q, r3(d_k), d_v)

        x2 = _out_ffn(x2, o_a, o_b, o_c, o_d, w_out[layer].astype(BF16), norm_ffn[layer],
                      w_gate[layer].astype(BF16), w_up[layer].astype(BF16),
                      w_down[layer].astype(BF16), norm_final, final=(layer == depth - 1))
    return x2.reshape(bsz, seq, D_MODEL)
```

```python
import functools
import math

import numpy as np
import jax
import jax.numpy as jnp
from jax import lax
from jax.experimental import pallas as pl
from jax.experimental.pallas import tpu as pltpu

F32 = jnp.float32
BF16 = jnp.bfloat16

D_MODEL = 1024
HEAD_DIM = 64
N_HEADS = 4
GROUP_WIDTH = N_HEADS * HEAD_DIM
ROPE_THETA = 10000.0
EPS = 1e-6
NEG = -1e30
LOG2E = math.log2(math.e)

CMP_LEN = 32
CMP_STRIDE = 16
SLC_LEN = 64
TOP_N = 16
NSA_WINDOW = 512
FORCED_LOCAL = 2
DIFF_HALF = HEAD_DIM // 2
MLSTM_CHUNK = 64
MLSTM_CONV = 4
DILATED_PATTERNS = ((128, 1), (512, 4), (2048, 16))
D_FF = ((8 * D_MODEL + 3 * 256 - 1) // (3 * 256)) * 256

LANES = 128
F32_SUBLANES = 8
BF16_SUBLANES = 16
VMEM_LIMIT = 56 * 1024 * 1024
ACC_ROWS = HEAD_DIM + BF16_SUBLANES

ROW_TILE = 512
ATTN_TILE = 256
MLSTM_ROWS = 256

IN_SPLITS = (
    GROUP_WIDTH, HEAD_DIM, HEAD_DIM, HEAD_DIM, HEAD_DIM, HEAD_DIM, HEAD_DIM, 3 * N_HEADS,
    GROUP_WIDTH, GROUP_WIDTH, GROUP_WIDTH,
    GROUP_WIDTH, GROUP_WIDTH, N_HEADS, N_HEADS, GROUP_WIDTH,
    GROUP_WIDTH, GROUP_WIDTH, GROUP_WIDTH,
)
(A_Q, A_KC, A_VC, A_KS, A_VS, A_KW, A_VW, A_G, B_Q, B_K, B_V,
 C_U, C_V, C_I, C_F, C_O, D_Q, D_K, D_V) = range(19)
SM_AG, SM_CI, SM_CF = 0, 12, 16


def _cparams(sem):
    return pltpu.CompilerParams(dimension_semantics=sem, vmem_limit_bytes=VMEM_LIMIT)


def _iota(shape, dim):
    return lax.broadcasted_iota(jnp.int32, shape, dim)


def _dot(a, b):
    return jnp.dot(a, b, preferred_element_type=F32)


def _bf16_pieces(x):
    hi = x.astype(BF16)
    rest = x - hi.astype(F32)
    mid = rest.astype(BF16)
    return hi, mid, (rest - mid.astype(F32)).astype(BF16)


PACKED_CHUNKS = 24
D_IN = sum(IN_SPLITS)
D_IN_ALIGNED = (D_IN // LANES) * LANES


def _column_plan():
    offs = np.concatenate([[0], np.cumsum(IN_SPLITS)])
    order = [A_Q, A_KC, A_VC, A_KS, A_KW, A_VS, A_VW,
             A_G, C_I, C_F, ('pad', LANES - 20),
             B_Q, B_K, B_V, C_U, C_V, C_O, D_Q, D_K, D_V]
    src = []
    for item in order:
        if isinstance(item, tuple):
            src += [-1] * item[1]
        else:
            src += list(range(int(offs[item]), int(offs[item + 1])))
    src = np.asarray(src)
    assert src.size == PACKED_CHUNKS * LANES
    terms, mats = [], []

    def add(dst, from_tail, start, width, rows, lanes):
        sel = np.zeros((2 * LANES, LANES), np.float32)
        sel[rows, lanes] = 1.0
        terms.append((dst, from_tail, int(start), int(width)))
        mats.append(sel)

    for j in range(PACKED_CHUNKS):
        cols = src[j * LANES:(j + 1) * LANES]
        tail = cols >= D_IN_ALIGNED
        if tail.any():
            add(j, True, 0, LANES, cols[tail] - D_IN_ALIGNED, np.nonzero(tail)[0])
        todo = (cols >= 0) & ~tail
        while todo.any():
            start = (cols[todo].min() // LANES) * LANES
            width = 2 * LANES if start + 2 * LANES <= D_IN_ALIGNED else LANES
            take = todo & (cols < start + width)
            add(j, False, start, width, cols[take] - start, np.nonzero(take)[0])
            todo &= ~take
    return tuple(terms), np.stack(mats)


def _rope_tables(seq, dim):
    inv = 1.0 / (ROPE_THETA ** (jnp.arange(0, dim, 2, dtype=F32) / dim))
    ang = jnp.arange(seq, dtype=F32)[:, None] * inv[None, :]
    return jnp.cos(ang), jnp.sin(ang)


def _lane_tables(seq):
    c64, s64 = _rope_tables(seq, HEAD_DIM)
    c32, s32 = _rope_tables(seq, DIFF_HALF)
    t64c = jnp.concatenate([c64, c64] * (LANES // HEAD_DIM), axis=1)
    t64s = jnp.concatenate([-s64, s64] * (LANES // HEAD_DIM), axis=1)
    t32c = jnp.concatenate([c32, c32] * (LANES // DIFF_HALF), axis=1)
    t32s = jnp.concatenate([-s32, s32] * (LANES // DIFF_HALF), axis=1)
    pos = np.arange(seq)[:, None]
    lane = np.arange(LANES)[None, :]
    onehot = ((lane >= HEAD_DIM) & ((pos // SLC_LEN) % HEAD_DIM == lane - HEAD_DIM)
              ).astype(np.float32)
    return t64c, t64s, t32c, t32s, jnp.asarray(onehot)


def _swap_halves(x, group):
    width = x.shape[-1]
    half = group // 2
    lane = _iota(x.shape, 1) & (group - 1)
    up = pltpu.roll(x, width - half, axis=1)
    down = pltpu.roll(x, half, axis=1)
    return jnp.where(lane < half, up, down)


def _rope(x, cos_t, sin_t, group):
    reps = x.shape[-1] // LANES
    if reps > 1:
        cos_t = jnp.concatenate([cos_t] * reps, axis=1)
        sin_t = jnp.concatenate([sin_t] * reps, axis=1)
    return x * cos_t + _swap_halves(x, group) * sin_t


def _in_proj_kernel(x_ref, g_ref, wraw_ref, wtail_ref, sel_ref, c64_ref, s64_ref, c32_ref, s32_ref,
                    oh_ref,
                    aq_ref, aqr_ref, akvc_ref, aks_ref, avs_ref, akw_ref, avw_ref, sm_ref,
                    bq_ref, bk_ref, bv_ref, cu_ref, cv_ref, co_ref, dq_ref, dk_ref, dv_ref,
                    w_ref, *, terms):
    @pl.when(pl.program_id(0) == 0)
    def _():
        for j in range(PACKED_CHUNKS):
            chunk = jnp.zeros((D_MODEL, LANES), F32)
            for t, (dst, from_tail, start, width) in enumerate(terms):
                if dst == j:
                    source = wtail_ref if from_tail else wraw_ref
                    piece = source[:, start:start + width].astype(BF16)
                    chunk = chunk + _dot(piece, sel_ref[t][0:width])
            w_ref[:, j * LANES:(j + 1) * LANES] = chunk.astype(BF16)

    x = x_ref[...]
    h = x * lax.rsqrt(jnp.mean(x * x, axis=-1, keepdims=True) + EPS)
    h = (h * g_ref[...]).astype(BF16)
    c64, s64 = c64_ref[...], s64_ref[...]
    c32, s32 = c32_ref[...], s32_ref[...]

    def mm(c0, c1):
        return _dot(h, w_ref[:, c0 * LANES:c1 * LANES])

    def channel_major(ref, z):
        ref[0] = z.T.astype(ref.dtype)

    zq = mm(0, 2) * (HEAD_DIM ** -0.5 * LOG2E)
    channel_major(aq_ref, zq)
    channel_major(aqr_ref, _rope(zq, c64, s64, HEAD_DIM))
    akvc_ref[...] = mm(2, 3).astype(BF16)
    keys = _rope(mm(3, 4), c64, s64, HEAD_DIM)
    low = _iota(keys.shape, 1) < HEAD_DIM
    aks_ref[...] = (jnp.where(low, keys, 0.0) + oh_ref[...]).astype(BF16)
    akw_ref[...] = jnp.where(low, pltpu.roll(keys, HEAD_DIM, axis=1), 0.0).astype(BF16)
    vals_t = mm(4, 5).T
    avs_ref[0] = vals_t[0:HEAD_DIM].astype(BF16)
    avw_ref[0] = vals_t[HEAD_DIM:2 * HEAD_DIM].astype(BF16)
    sm_ref[...] = mm(5, 6)
    channel_major(bq_ref, _rope(mm(6, 8), c32, s32, DIFF_HALF) * (DIFF_HALF ** -0.5 * LOG2E))
    bk_ref[...] = _rope(mm(8, 10), c32, s32, DIFF_HALF).astype(BF16)
    channel_major(bv_ref, mm(10, 12))
    cu_ref[...] = mm(12, 14)
    cv_ref[...] = mm(14, 16).astype(BF16)
    co_ref[...] = mm(16, 18)
    channel_major(dq_ref, _rope(mm(18, 20), c64, s64, HEAD_DIM) * (HEAD_DIM ** -0.5 * LOG2E))
    dk_ref[...] = _rope(mm(20, 22), c64, s64, HEAD_DIM).astype(BF16)
    channel_major(dv_ref, mm(22, 24))


_IN_PROJ_OUTS = (
    (256, BF16, True), (256, BF16, True), (128, BF16, False), (128, BF16, False),
    (HEAD_DIM, BF16, True), (128, BF16, False), (HEAD_DIM, BF16, True), (128, F32, False),
    (256, BF16, True), (256, BF16, False), (256, BF16, True), (256, F32, False),
    (256, BF16, False), (256, F32, False), (256, BF16, True), (256, BF16, False),
    (256, BF16, True))


def _in_proj(x2, gain, w, tables, seq, tm):
    n = x2.shape[0]
    nblk_s = seq // tm
    terms, select = _column_plan()
    select = jnp.asarray(select, BF16)
    w_tail = jnp.pad(w[:, D_IN_ALIGNED:], ((0, 0), (0, LANES - (D_IN - D_IN_ALIGNED))))
    row = lambda i: (i, 0)
    tab = lambda i: (i % nblk_s, 0)
    const = lambda i: (0, 0)
    in_specs = [pl.BlockSpec((tm, D_MODEL), row),
                pl.BlockSpec((1, D_MODEL), const),
                pl.BlockSpec(w.shape, const, pipeline_mode=pl.Buffered(1)),
                pl.BlockSpec(w_tail.shape, const),
                pl.BlockSpec(select.shape, lambda i: (0, 0, 0))]
    in_specs += [pl.BlockSpec((tm, LANES), tab)] * 5
    out_specs = [pl.BlockSpec((1, w, tm), lambda i: (i // nblk_s, 0, i % nblk_s)) if cmaj
                 else pl.BlockSpec((tm, w), row) for w, _, cmaj in _IN_PROJ_OUTS]
    out_shape = [jax.ShapeDtypeStruct((n // seq, w, seq) if cmaj else (n, w), dt)
                 for w, dt, cmaj in _IN_PROJ_OUTS]
    return pl.pallas_call(
        functools.partial(_in_proj_kernel, terms=terms), grid=(n // tm,), in_specs=in_specs,
        out_specs=out_specs, out_shape=out_shape,
        scratch_shapes=[pltpu.VMEM((D_MODEL, PACKED_CHUNKS * LANES), BF16)],
        compiler_params=_cparams(("arbitrary",)), name="in_proj",
    )(x2, gain.reshape(1, D_MODEL), w, w_tail, select, *tables)


def _compress_kernel(r_ref, w_ref, pos_ref, kc_ref, kvct_ref):
    wk = w_ref[0].reshape(CMP_LEN, HEAD_DIM, HEAD_DIM)
    wv = w_ref[1].reshape(CMP_LEN, HEAD_DIM, HEAD_DIM)
    zeros = jnp.zeros_like(wk)
    full = jnp.concatenate([jnp.concatenate([wk, zeros], axis=2),
                            jnp.concatenate([zeros, wv], axis=2)], axis=1)
    w_first = full[0:CMP_STRIDE].reshape(CMP_STRIDE * LANES, LANES).astype(BF16)
    w_second = full[CMP_STRIDE:].reshape(CMP_STRIDE * LANES, LANES).astype(BF16)

    r = r_ref[0]
    first = _dot(r, w_first)
    second = _dot(r, w_second)
    nrow = first.shape[0]
    pos = pos_ref[...].astype(BF16)
    half = pos.shape[1] // 2
    const = _dot(pos[:, :half], w_first) + _dot(pos[:, half:], w_second)
    out = first + pltpu.roll(second, nrow - 1, axis=0) + const[0:1, :]
    lane = _iota(out.shape, 1)
    kc_ref[0] = jnp.where(lane < HEAD_DIM, out, 0.0).astype(BF16)
    kvct_ref[0] = out.T.astype(BF16)


def _compress(a_kvc, cmp_w, cmp_pos):
    bsz, seq, _ = a_kvc.shape
    nrow = seq // CMP_STRIDE
    r = a_kvc.reshape(bsz, nrow, CMP_STRIDE * LANES)
    pos = jnp.concatenate([cmp_pos[0], cmp_pos[1]], axis=-1)
    pos = jnp.broadcast_to(pos.reshape(1, CMP_LEN * LANES), (F32_SUBLANES, CMP_LEN * LANES))
    return pl.pallas_call(
        _compress_kernel, grid=(bsz,),
        in_specs=[pl.BlockSpec((1, nrow, CMP_STRIDE * LANES), lambda b: (b, 0, 0)),
                  pl.BlockSpec(cmp_w.shape, lambda b: (0, 0, 0)),
                  pl.BlockSpec(pos.shape, lambda b: (0, 0))],
        out_specs=[pl.BlockSpec((1, nrow, LANES), lambda b: (b, 0, 0)),
                   pl.BlockSpec((1, LANES, nrow), lambda b: (b, 0, 0))],
        out_shape=[jax.ShapeDtypeStruct((bsz, nrow, LANES), BF16),
                   jax.ShapeDtypeStruct((bsz, LANES, nrow), BF16)],
        compiler_params=_cparams(("parallel",)), name="nsa_compress",
    )(r, cmp_w, pos)


def _flash_init(m_ref, acc_ref):
    m_ref[...] = jnp.full(m_ref.shape, NEG, F32)
    acc_ref[...] = jnp.zeros(acc_ref.shape, F32)


def _flash_step(s, values, m_ref, acc_ref):
    width = acc_ref.shape[1]
    chunk = min(2 * LANES, width)
    for g, vals in enumerate(values):
        rows = slice(g * ACC_ROWS, (g + 1) * ACC_ROWS)
        for c0 in range(0, width, chunk):
            lanes = slice(g * width + c0, g * width + c0 + chunk)
            s_c = s[:, lanes]
            m_old = m_ref[:, lanes]
            m_new = jnp.maximum(m_old, jnp.max(s_c, axis=0, keepdims=True))
            alpha = jnp.exp2(m_old - m_new)
            p = jnp.exp2(s_c - m_new).astype(BF16)
            acc_ref[rows, c0:c0 + chunk] = alpha * acc_ref[rows, c0:c0 + chunk] + _dot(vals, p)
            m_ref[:, lanes] = m_new


def _pipelined_tiles(lo, hi, scores, consume, sa_ref, sb_ref):
    n = hi - lo

    def put(ref, tiles):
        for g, tile in enumerate(tiles):
            ref[g] = tile

    def get(ref):
        return [ref[g] for g in range(ref.shape[0])]

    put(sa_ref, scores(lo))

    def body(i, carry):
        j = lo + 2 * i
        put(sb_ref, scores(j + 1))
        consume(get(sa_ref), j, False)
        put(sa_ref, scores(j + 2))
        consume(get(sb_ref), j + 1, False)
        return carry

    lax.fori_loop(0, n // 2, body, 0)

    @pl.when(n % 2 == 0)
    def _():
        consume(get(sa_ref), hi, True)

    @pl.when(n % 2 == 1)
    def _():
        put(sb_ref, scores(hi))
        consume(get(sa_ref), hi - 1, False)
        consume(get(sb_ref), hi, True)


def _with_ones(vt):
    return jnp.concatenate([vt, jnp.ones((BF16_SUBLANES, vt.shape[1]), BF16)], axis=0)


def _cmp_to_slc_t(seq):
    n_cmp = (seq - CMP_LEN) // CMP_STRIDE + 1
    n_slc = seq // SLC_LEN
    ratio_s, ratio_c = SLC_LEN // CMP_STRIDE, CMP_LEN // CMP_STRIDE
    jj = np.arange(n_slc)[:, None, None]
    src = ratio_s * jj - np.arange(ratio_s)[None, :, None] - np.arange(ratio_c)[None, None, :]
    ok = (src >= 0) & (src < n_cmp)
    m = np.zeros((seq // CMP_STRIDE, n_slc), np.float32)
    np.add.at(m, (np.where(ok, src, 0), np.broadcast_to(jj, src.shape)), ok.astype(np.float32))
    return jnp.asarray(m.T, BF16)


def _nsa_kernel(qt_ref, qrt_ref, kc_ref, kvct_ref, ks_ref, vst_ref, kw_ref, vwt_ref, sm_ref,
                c2st_ref, wmask_ref, o_ref, m_ref, acc_ref, sa_ref, sb_ref, *, tq, top_n):
    qi = pl.program_id(1)
    s0 = qi * tq
    nb = qt_ref.shape[0]
    rows = N_HEADS * tq
    n_slc = c2st_ref.shape[0]
    lane_t = s0 + (_iota((1, rows), 1) & (tq - 1))
    c2st = c2st_ref[...]
    blk = _iota((n_slc, 1), 0)
    cur = (s0 + _iota((1, tq), 1)) >> (SLC_LEN.bit_length() - 1)
    forced = (blk == 0) | ((blk <= cur) & (blk > cur - FORCED_LOCAL))

    def heads_on_lanes(x):
        return jnp.concatenate([x[h * HEAD_DIM:(h + 1) * HEAD_DIM, :] for h in range(N_HEADS)],
                               axis=1)

    def compressed_and_selection(b):
        q4 = jnp.concatenate([heads_on_lanes(qt_ref[b]), jnp.zeros((HEAD_DIM, rows), BF16)], axis=0)
        sc = _dot(kc_ref[b], q4)
        cmask = (_iota((sc.shape[0], 1), 0) * CMP_STRIDE + (CMP_LEN - 1)) <= lane_t
        sc = jnp.where(cmask, sc, NEG)
        e = jnp.where(cmask, jnp.exp2(sc - jnp.max(sc, axis=0, keepdims=True)), 0.0)
        z = jnp.sum(e, axis=0, keepdims=True)
        p_cmp = e * (1.0 / jnp.where(z > 0, z, 1.0))
        o_cmp = _dot(kvct_ref[b][HEAD_DIM:2 * HEAD_DIM, :], p_cmp.astype(BF16))
        p_heads = p_cmp[:, 0:tq]
        for h in range(1, N_HEADS):
            p_heads = p_heads + p_cmp[:, h * tq:(h + 1) * tq]
        p_hi = p_heads.astype(BF16)
        p_lo = (p_heads - p_hi.astype(F32)).astype(BF16)
        imp = _dot(c2st, p_hi) + _dot(c2st, p_lo)

        score = jnp.where(blk > cur, -1.0e6, jnp.where(forced, 1.0e6, imp))
        rank = jnp.zeros((n_slc, tq), F32)
        for i in range(n_slc):
            s_i = score[i:i + 1, :]
            rank = rank + jnp.where(blk > i, jnp.where(s_i >= score, 1.0, 0.0),
                                    jnp.where(s_i > score, 1.0, 0.0))
        sel = (rank < top_n) & (blk <= cur)
        bias = jnp.where(sel, 0.0, NEG)
        if n_slc < HEAD_DIM:
            bias = jnp.concatenate([bias, jnp.zeros((HEAD_DIM - n_slc, tq), F32)], axis=0)
        bias4 = jnp.concatenate([bias] * N_HEADS, axis=1).astype(BF16)
        qsel = jnp.concatenate([heads_on_lanes(qrt_ref[b]), bias4], axis=0)
        return o_cmp, qsel

    prepared = [compressed_and_selection(b) for b in range(nb)]

    def attend(k_ref, vt_ref, lo, window):
        _flash_init(m_ref, acc_ref)

        def scores(j):
            k0 = pl.multiple_of(j * tq, tq)
            return [_dot(k_ref[b, pl.ds(k0, tq), :], prepared[b][1]) for b in range(nb)]

        def consume(tiles, j, diagonal):
            k0 = pl.multiple_of(j * tq, tq)
            if window:
                far = qi - NSA_WINDOW // tq
                mask = wmask_ref[jnp.where(j == qi, 0, jnp.where(j == far, 2, 1))]
            elif diagonal:
                mask = wmask_ref[0]
            for b, s in enumerate(tiles):
                if window or diagonal:
                    s = s + mask
                _flash_step(s, [_with_ones(vt_ref[b, :, pl.ds(k0, tq)])], m_ref.at[b], acc_ref.at[b])

        _pipelined_tiles(lo, qi, scores, consume, sa_ref, sb_ref)
        return [acc_ref[b, 0:HEAD_DIM, :] * (1.0 / acc_ref[b, HEAD_DIM:HEAD_DIM + 1, :])
                for b in range(nb)]

    o_slc = attend(ks_ref, vst_ref, 0, False)
    o_win = attend(kw_ref, vwt_ref, jnp.maximum(qi - NSA_WINDOW // tq, 0), True)

    for b in range(nb):
        g = jax.nn.sigmoid(sm_ref[b].T[0:BF16_SUBLANES, :])

        def gate(branch):
            return jnp.concatenate(
                [g[branch * N_HEADS + h:branch * N_HEADS + h + 1, :] for h in range(N_HEADS)], axis=1)

        o = gate(0) * prepared[b][0] + gate(1) * o_slc[b] + gate(2) * o_win[b]
        for h in range(N_HEADS):
            o_ref[b, h * HEAD_DIM:(h + 1) * HEAD_DIM, :] = (
                o[:, h * tq:(h + 1) * tq].astype(o_ref.dtype))


def _nsa(qt, qrt, kc, kvct, ks, vst, kw, vwt, smalls, tq=ATTN_TILE):
    bsz, _, seq = qt.shape
    tq = min(tq, seq)
    nb = 2 if bsz % 2 == 0 else 1
    n_slc = seq // SLC_LEN
    c2st = _cmp_to_slc_t(seq)
    rows = N_HEADS * tq
    qspec = pl.BlockSpec((nb, GROUP_WIDTH, tq), lambda b, i: (b, 0, i))
    full = lambda a: pl.BlockSpec((nb,) + a.shape[1:], lambda b, i: (b, 0, 0))
    vspec = pl.BlockSpec((nb, HEAD_DIM, seq), lambda b, i: (b, 0, 0))
    assert NSA_WINDOW % tq == 0
    k_off = np.arange(tq)[:, None]
    q_off = np.tile(np.arange(tq), N_HEADS)[None, :]
    visible = np.stack([k_off <= q_off, np.ones((tq, rows), bool), k_off > q_off])
    wmask = jnp.asarray(np.where(visible, 0.0, NEG).astype(np.float32))
    kern = functools.partial(_nsa_kernel, tq=tq, top_n=min(TOP_N, n_slc))
    return pl.pallas_call(
        kern, grid=(bsz // nb, seq // tq),
        in_specs=[qspec, qspec, full(kc), full(kvct), full(ks), vspec, full(kw), vspec,
                  pl.BlockSpec((nb, tq, LANES), lambda b, i: (b, i, 0)),
                  pl.BlockSpec(c2st.shape, lambda b, i: (0, 0)),
                  pl.BlockSpec(wmask.shape, lambda b, i: (0, 0, 0), pipeline_mode=pl.Buffered(1))],
        out_specs=qspec,
        out_shape=jax.ShapeDtypeStruct(qt.shape, BF16),
        scratch_shapes=[pltpu.VMEM((nb, 1, rows), F32), pltpu.VMEM((nb, ACC_ROWS, rows), F32),
                        pltpu.VMEM((nb, tq, rows), F32), pltpu.VMEM((nb, tq, rows), F32)],
        compiler_params=_cparams(("parallel", "arbitrary")), name="nsa_attention",
    )(qt, qrt, kc, kvct, ks, vst, kw, vwt, smalls, c2st, wmask)


def _diff_kernel(qt_ref, k_ref, vt_ref, lam_ref, g_ref, o_ref, m_ref, acc_ref, sa_ref, sb_ref,
                 *, tq, lam_init):
    qi = pl.program_id(1)
    s0 = qi * tq
    rows = 4 * tq
    pairs = N_HEADS // 2
    pw = 2 * HEAD_DIM
    row = _iota((pw, 1), 0)
    lane_t = s0 + (_iota((1, rows), 1) & (tq - 1))

    def query_matrix(p):
        qt = qt_ref[0, p * pw:(p + 1) * pw, :]
        zero = jnp.zeros_like(qt)
        return jnp.concatenate(
            [jnp.where((row >= DIFF_HALF * c) & (row < DIFF_HALF * (c + 1)), qt, zero)
             for c in range(4)], axis=1)

    qmats = [query_matrix(p) for p in range(pairs)]

    def scores(j):
        k0 = pl.multiple_of(j * tq, tq)
        return [_dot(k_ref[0, pl.ds(k0, tq), p * pw:(p + 1) * pw], qmats[p]) for p in range(pairs)]

    def consume(tiles, j, diagonal):
        k0 = pl.multiple_of(j * tq, tq)
        for p, s in enumerate(tiles):
            if diagonal:
                s = jnp.where(k0 + _iota((tq, 1), 0) <= lane_t, s, NEG)
            vt = vt_ref[0, p * pw:(p + 1) * pw, pl.ds(k0, tq)]
            _flash_step(s, [_with_ones(vt[0:HEAD_DIM]), _with_ones(vt[HEAD_DIM:])],
                        m_ref.at[p], acc_ref.at[p])

    _flash_init(m_ref, acc_ref)
    _pipelined_tiles(0, qi, scores, consume, sa_ref, sb_ref)

    lv = lam_ref[...]
    lam = (jnp.exp(jnp.sum(lv[0:1] * lv[1:2], axis=-1, keepdims=True))
           - jnp.exp(jnp.sum(lv[2:3] * lv[3:4], axis=-1, keepdims=True)) + lam_init)
    for h in range(N_HEADS):
        a = acc_ref[h // 2, ACC_ROWS * (h % 2):ACC_ROWS * (h % 2 + 1), :]
        o_all = a[0:HEAD_DIM] * (1.0 / a[HEAD_DIM:HEAD_DIM + 1])
        o = o_all[:, :tq] - lam * o_all[:, tq:]
        y = o * lax.rsqrt(jnp.mean(o * o, axis=0, keepdims=True) + EPS)
        o_ref[0, HEAD_DIM * h:HEAD_DIM * (h + 1), :] = (
            (y * g_ref[...]) * (1.0 - lam_init)).astype(o_ref.dtype)


def _diff(qt, k, vt, lam_vecs, sub_g, lam_init, tq=ATTN_TILE):
    bsz, _, seq = qt.shape
    tq = min(tq, seq)
    pairs = N_HEADS // 2
    qspec = pl.BlockSpec((1, GROUP_WIDTH, tq), lambda b, i: (b, 0, i))
    kern = functools.partial(_diff_kernel, tq=tq, lam_init=lam_init)
    return pl.pallas_call(
        kern, grid=(bsz, seq // tq),
        in_specs=[qspec,
                  pl.BlockSpec((1, seq, GROUP_WIDTH), lambda b, i: (b, 0, 0)),
                  pl.BlockSpec((1, GROUP_WIDTH, seq), lambda b, i: (b, 0, 0)),
                  pl.BlockSpec(lam_vecs.shape, lambda b, i: (0, 0)),
                  pl.BlockSpec((HEAD_DIM, 1), lambda b, i: (0, 0))],
        out_specs=qspec,
        out_shape=jax.ShapeDtypeStruct(qt.shape, BF16),
        scratch_shapes=[pltpu.VMEM((pairs, 1, 4 * tq), F32),
                        pltpu.VMEM((pairs, 2 * ACC_ROWS, 2 * tq), F32),
                        pltpu.VMEM((pairs, tq, 4 * tq), F32), pltpu.VMEM((pairs, tq, 4 * tq), F32)],
        compiler_params=_cparams(("parallel", "arbitrary")), name="diff_attention",
    )(qt, k, vt, lam_vecs, sub_g.reshape(HEAD_DIM, 1))


def _dilated_bias(tq):
    max_back = max(w for w, _ in DILATED_PATTERNS) // tq
    classes = [0, 1, 2, 3, max_back]
    q = np.arange(tq)[None, :]
    k = np.arange(tq)[:, None]
    out = []
    for d in classes:
        delta = d * tq + q - k
        cnt = np.zeros((tq, tq), np.float64)
        for w, dil in DILATED_PATTERNS:
            cnt += (delta >= 0) & (delta <= w) & (delta % dil == 0)
        tab = np.where(cnt > 0, np.log2(np.maximum(cnt, 1.0)), NEG)
        out.append(np.concatenate([tab, tab], axis=1))
    return jnp.asarray(np.stack(out).astype(np.float32)), max_back


def _dilated_kernel(qt_ref, k_ref, vt_ref, bias_ref, o_ref, m_ref, acc_ref, sa_ref, sb_ref,
                    *, tq, max_back):
    qi = pl.program_id(1)
    nb = qt_ref.shape[0]
    pairs = N_HEADS // 2
    pw = 2 * HEAD_DIM
    row = _iota((pw, 1), 0)
    streams = [(b, p) for b in range(nb) for p in range(pairs)]

    def query_matrix(b, p):
        qt = qt_ref[b, p * pw:(p + 1) * pw, :]
        zero = jnp.zeros_like(qt)
        return jnp.concatenate([jnp.where(row < HEAD_DIM, qt, zero),
                                jnp.where(row >= HEAD_DIM, qt, zero)], axis=1)

    qmats = [query_matrix(b, p) for b, p in streams]
    _flash_init(m_ref, acc_ref)

    def scores(j):
        k0 = pl.multiple_of(j * tq, tq)
        return [_dot(k_ref[b, pl.ds(k0, tq), p * pw:(p + 1) * pw], qmats[g])
                for g, (b, p) in enumerate(streams)]

    def consume(tiles, j, diagonal):
        del diagonal
        k0 = pl.multiple_of(j * tq, tq)
        d = qi - j
        bias = bias_ref[jnp.where(d < 3, d, jnp.where(d == max_back, 4, 3))]
        for g, (b, p) in enumerate(streams):
            vt = vt_ref[b, p * pw:(p + 1) * pw, pl.ds(k0, tq)]
            _flash_step(tiles[g] + bias, [_with_ones(vt[0:HEAD_DIM]), _with_ones(vt[HEAD_DIM:])],
                        m_ref.at[g], acc_ref.at[g])

    _pipelined_tiles(jnp.maximum(qi - max_back, 0), qi, scores, consume, sa_ref, sb_ref)
    for g, (b, p) in enumerate(streams):
        for hh in range(2):
            a = acc_ref[g, ACC_ROWS * hh:ACC_ROWS * (hh + 1), :]
            h = 2 * p + hh
            o_ref[b, HEAD_DIM * h:HEAD_DIM * (h + 1), :] = (
                a[0:HEAD_DIM] * (1.0 / a[HEAD_DIM:HEAD_DIM + 1])).astype(o_ref.dtype)


def _dilated(qt, k, vt, tq=ATTN_TILE):
    bsz, _, seq = qt.shape
    tq = min(tq, seq)
    nb = 1
    streams = nb * (N_HEADS // 2)
    bias, max_back = _dilated_bias(tq)
    qspec = pl.BlockSpec((nb, GROUP_WIDTH, tq), lambda b, i: (b, 0, i))
    kern = functools.partial(_dilated_kernel, tq=tq, max_back=max_back)
    return pl.pallas_call(
        kern, grid=(bsz // nb, seq // tq),
        in_specs=[qspec,
                  pl.BlockSpec((nb, seq, GROUP_WIDTH), lambda b, i: (b, 0, 0)),
                  pl.BlockSpec((nb, GROUP_WIDTH, seq), lambda b, i: (b, 0, 0)),
                  pl.BlockSpec(bias.shape, lambda b, i: (0, 0, 0))],
        out_specs=qspec,
        out_shape=jax.ShapeDtypeStruct(qt.shape, BF16),
        scratch_shapes=[pltpu.VMEM((streams, 1, 2 * tq), F32),
                        pltpu.VMEM((streams, 2 * ACC_ROWS, tq), F32),
                        pltpu.VMEM((streams, tq, 2 * tq), F32),
                        pltpu.VMEM((streams, tq, 2 * tq), F32)],
        compiler_params=_cparams(("parallel", "arbitrary")), name="dilated_attention",
    )(qt, k, vt, bias)


def _mlstm_kernel(u_ref, up_ref, v_ref, sm_ref, o_ref, cw_ref, cb_ref, wqt_ref, wk_ref,
                  gb_ref, gcol_ref, hg_ref, out_ref, c_st, m_st):
    ci = pl.program_id(1)
    rows = u_ref.shape[1]

    @pl.when(ci == 0)
    def _():
        c_st[...] = jnp.zeros(c_st.shape, F32)
        m_st[...] = jnp.zeros(m_st.shape, F32)

    tail = jnp.where(ci > 0, up_ref[0], 0.0)
    ext = jnp.concatenate([tail, u_ref[0]], axis=0)
    cw = cw_ref[...]
    uc = cb_ref[...] + cw[MLSTM_CONV - 1:MLSTM_CONV] * ext[F32_SUBLANES:]
    for j in range(MLSTM_CONV - 1):
        shifted = pltpu.roll(ext, MLSTM_CONV - 1 - j, axis=0)[F32_SUBLANES:]
        uc = uc + cw[j:j + 1] * shifted
    uc = uc * jax.nn.sigmoid(uc)

    qt_all = _dot(wqt_ref[...], uc.T.astype(BF16))
    k_all = _dot(uc.astype(BF16), wk_ref[...]) * (HEAD_DIM ** -0.5)
    vt_all = v_ref[0].astype(F32).T.astype(BF16)
    ogt = jax.nn.sigmoid(o_ref[0]).T
    sm = sm_ref[0]
    smt = sm.T
    gb = gb_ref[...]
    gcol = gcol_ref[...]
    upper = _iota((rows, rows), 0) <= _iota((rows, rows), 1)
    lower = _iota((rows, rows), 1) <= _iota((rows, rows), 0)
    ig_tile = (SM_CI // F32_SUBLANES) * F32_SUBLANES
    lf_tile = (SM_CF // F32_SUBLANES) * F32_SUBLANES
    ig_rows = smt[ig_tile:ig_tile + F32_SUBLANES] + gcol[:, 0:1]
    lf_rows = jax.nn.log_sigmoid(smt[lf_tile:lf_tile + F32_SUBLANES] + gcol[:, 1:2])
    tri_upper = jnp.where(upper, 1.0, 0.0).astype(BF16)
    tri_lower = jnp.where(lower, 1.0, 0.0).astype(BF16)
    b_rows = sum(_dot(piece, tri_upper) for piece in _bf16_pieces(lf_rows))
    ig_cols = sm[:, SM_CI:SM_CI + N_HEADS] + gb[0:1]
    lf_cols = jax.nn.log_sigmoid(sm[:, SM_CF:SM_CF + N_HEADS] + gb[1:2])
    b_cols = sum(_dot(tri_lower, piece) for piece in _bf16_pieces(lf_cols))
    ones = jnp.ones((BF16_SUBLANES, rows), BF16)

    for h in range(N_HEADS):
        hs = slice(h * HEAD_DIM, (h + 1) * HEAD_DIM)
        b_row = b_rows[h:h + 1]
        src_row = ig_rows[N_HEADS + h:N_HEADS + h + 1] - b_row
        src_col = ig_cols[:, h:h + 1] - b_cols[:, h:h + 1]
        dmat = jnp.where(upper, b_row + src_col, NEG)
        a = b_row[:, rows - 1:rows]
        g_end = a + src_row
        m_loc = jnp.max(g_end, axis=-1, keepdims=True)
        w_end = jnp.exp(g_end - m_loc)

        state = c_st[h]
        m_in = m_st[h]
        inter = b_row + m_in
        m_t = jnp.maximum(inter, jnp.max(dmat, axis=0, keepdims=True))
        e_inter = jnp.exp(inter - m_t)
        qt = qt_all[hs].astype(BF16)
        kb = k_all[:, hs].astype(BF16)
        values = jnp.concatenate([vt_all[hs], ones], axis=0)
        p = (_dot(kb, qt) * jnp.exp(dmat - m_t)).astype(BF16)
        from_state = _dot(state.astype(BF16), qt)
        from_chunk = _dot(values, p)
        num = e_inter * from_state[0:HEAD_DIM] + from_chunk[0:HEAD_DIM]
        den = e_inter * from_state[HEAD_DIM:HEAD_DIM + 1] + from_chunk[HEAD_DIM:HEAD_DIM + 1]
        hh = num * (1.0 / jnp.maximum(jnp.abs(den), jnp.exp(-m_t)))
        hh = hh * lax.rsqrt(jnp.mean(hh * hh, axis=0, keepdims=True) + EPS) * hg_ref[hs, :]
        out_ref[0, hs, :] = (hh * ogt[hs]).astype(out_ref.dtype)

        m_new = jnp.maximum(a + m_in, m_loc)
        decay = jnp.exp(a + m_in - m_new)
        fresh = jnp.exp(m_loc - m_new)
        local = _dot((values.astype(F32) * w_end).astype(BF16), kb)
        c_st[h] = decay * state + fresh * local
        m_st[h] = m_new


def _mlstm(u, v, smalls, o_pre, conv_w, conv_b, wq, wk, gate_b, head_g, rows=MLSTM_ROWS):
    bsz, seq, _ = u.shape
    rows = min(rows, seq)
    row = pl.BlockSpec((1, rows, GROUP_WIDTH), lambda b, c: (b, c, 0))
    const2 = lambda a: pl.BlockSpec(a.shape, lambda b, c: (0,) * a.ndim)
    conv_b = conv_b.reshape(1, GROUP_WIDTH)
    head_g = head_g.reshape(GROUP_WIDTH, 1)
    eye = jnp.eye(N_HEADS, dtype=wq.dtype)
    wqt = jnp.einsum('hde,hg->hegd', wq, eye).reshape(GROUP_WIDTH, GROUP_WIDTH).astype(BF16)
    wkb = jnp.einsum('hde,hg->hdge', wk, eye).reshape(GROUP_WIDTH, GROUP_WIDTH).astype(BF16)
    zeros4 = jnp.zeros((N_HEADS,), gate_b.dtype)
    gcol = jnp.stack([jnp.concatenate([zeros4, gate_b[0]]), jnp.concatenate([gate_b[1], zeros4])],
                     axis=1)
    return pl.pallas_call(
        _mlstm_kernel, grid=(bsz, seq // rows),
        in_specs=[row,
                  pl.BlockSpec((1, F32_SUBLANES, GROUP_WIDTH),
                               lambda b, c: (b, jnp.maximum(c * (rows // F32_SUBLANES) - 1, 0), 0)),
                  row,
                  pl.BlockSpec((1, rows, LANES), lambda b, c: (b, c, 0)),
                  row, const2(conv_w), const2(conv_b), const2(wqt), const2(wkb), const2(gate_b),
                  const2(gcol), const2(head_g)],
        out_specs=pl.BlockSpec((1, GROUP_WIDTH, rows), lambda b, c: (b, 0, c)),
        out_shape=jax.ShapeDtypeStruct((bsz, GROUP_WIDTH, seq), BF16),
        scratch_shapes=[pltpu.VMEM((N_HEADS, ACC_ROWS, HEAD_DIM), F32),
                        pltpu.VMEM((N_HEADS, 1, 1), F32)],
        compiler_params=_cparams(("parallel", "arbitrary")), name="mlstm",
    )(u, u, v, smalls, o_pre, conv_w, conv_b, wqt, wkb, gate_b, gcol, head_g)


def _out_ffn_kernel(x_ref, oa_ref, ob_ref, oc_ref, od_ref, wo_ref, g_ref, wg_ref, wu_ref, wd_ref,
                    gf_ref, y_ref, *, final, ff_chunk):
    mixed_t = jnp.concatenate([oa_ref[0], ob_ref[0], oc_ref[0], od_ref[0]], axis=0)
    x = x_ref[...] + lax.dot_general(mixed_t, wo_ref[...], (((0,), (0,)), ((), ())),
                                     preferred_element_type=F32)
    h = x * lax.rsqrt(jnp.mean(x * x, axis=-1, keepdims=True) + EPS)
    h = (h * g_ref[...]).astype(BF16)
    ffn = None
    for c0 in range(0, D_FF, ff_chunk):
        gate = _dot(h, wg_ref[:, c0:c0 + ff_chunk])
        up = _dot(h, wu_ref[:, c0:c0 + ff_chunk])
        act = (gate * jax.nn.sigmoid(gate) * up).astype(BF16)
        part = _dot(act, wd_ref[c0:c0 + ff_chunk, :])
        ffn = part if ffn is None else ffn + part
    y = x + ffn
    if final:
        y = y * lax.rsqrt(jnp.mean(y * y, axis=-1, keepdims=True) + EPS) * gf_ref[...]
    y_ref[...] = y


def _out_ffn(x2, o_a, o_b, o_c, o_d, w_out, gain, w_gate, w_up, w_down, gain_final, final, tm=ROW_TILE):
    n = x2.shape[0]
    seq = o_a.shape[2]
    tm = min(tm, seq)
    nblk_s = seq // tm
    row = lambda w: pl.BlockSpec((tm, w), lambda i: (i, 0))
    mixer = pl.BlockSpec((1, GROUP_WIDTH, tm), lambda i: (i // nblk_s, 0, i % nblk_s))
    const = lambda a: pl.BlockSpec(a.shape, lambda i: (0, 0), pipeline_mode=pl.Buffered(1))
    gain = gain.reshape(1, D_MODEL)
    gain_final = gain_final.reshape(1, D_MODEL)
    kern = functools.partial(_out_ffn_kernel, final=final, ff_chunk=256)
    return pl.pallas_call(
        kern, grid=(n // tm,),
        in_specs=[row(D_MODEL)] + [mixer] * 4
                 + [const(w_out), const(gain), const(w_gate), const(w_up), const(w_down),
                    const(gain_final)],
        out_specs=row(D_MODEL),
        out_shape=jax.ShapeDtypeStruct((n, D_MODEL), F32),
        compiler_params=_cparams(("parallel",)), name="out_ffn",
    )(x2, o_a, o_b, o_c, o_d, w_out, gain, w_gate, w_up, w_down, gain_final)


def kernel(x, norm_mix, w_in, nsa_cmp_pos, nsa_cmp_w, diff_lambda, diff_norm, mlstm_conv_w,
           mlstm_conv_b, mlstm_wq, mlstm_wk, mlstm_gate_b, mlstm_norm, w_out, norm_ffn, w_gate,
           w_up, w_down, norm_final):
    bsz, seq, _ = x.shape
    depth = w_in.shape[0]
    tables = _lane_tables(seq)
    x2 = x.reshape(bsz * seq, D_MODEL)
    r3 = lambda t: t.reshape(bsz, seq, t.shape[-1])

    for layer in range(depth):
        (a_q, a_qr, a_kvc, a_ks, a_vs, a_kw, a_vw, smalls, b_q, b_k, b_v, c_u, c_v, c_o,
         d_q, d_k, d_v) = _in_proj(x2, norm_mix[layer], w_in[layer], tables, seq,
                                   tm=min(ROW_TILE, seq))
        smalls3 = r3(smalls)

        kc, kvct = _compress(r3(a_kvc), nsa_cmp_w[layer], nsa_cmp_pos[layer])
        o_a = _nsa(a_q, a_qr, kc, kvct, r3(a_ks), a_vs, r3(a_kw), a_vw, smalls3)

        lam_init = 0.8 - 0.6 * math.exp(-0.3 * layer)
        o_b = _diff(b_q, r3(b_k), b_v, diff_lambda[layer], diff_norm[layer], lam_init)

        o_c = _mlstm(r3(c_u), r3(c_v), smalls3, r3(c_o),
                     mlstm_conv_w[layer], mlstm_conv_b[layer], mlstm_wq[layer], mlstm_wk[layer],
                     mlstm_gate_b[layer], mlstm_norm[layer])

        o_d = _dilated(d_q, r3(d_k), d_v)

        x2 = _out_ffn(x2, o_a, o_b, o_c, o_d, w_out[layer].astype(BF16), norm_ffn[layer],
                      w_gate[layer].astype(BF16), w_up[layer].astype(BF16),
                      w_down[layer].astype(BF16), norm_final, final=(layer == depth - 1))
    return x2.reshape(bsz, seq, D_MODEL)
```

```python
import functools
import math

import numpy as np
import jax
import jax.numpy as jnp
from jax import lax
from jax.experimental import pallas as pl
from jax.experimental.pallas import tpu as pltpu

F32 = jnp.float32
BF16 = jnp.bfloat16

D_MODEL = 1024
HEAD_DIM = 64
N_HEADS = 4
GROUP_WIDTH = N_HEADS * HEAD_DIM
ROPE_THETA = 10000.0
EPS = 1e-6
NEG = -1e30
LOG2E = math.log2(math.e)

CMP_LEN = 32
CMP_STRIDE = 16
SLC_LEN = 64
TOP_N = 16
NSA_WINDOW = 512
FORCED_LOCAL = 2
DIFF_HALF = HEAD_DIM // 2
MLSTM_CHUNK = 64
MLSTM_CONV = 4
DILATED_PATTERNS = ((128, 1), (512, 4), (2048, 16))
D_FF = ((8 * D_MODEL + 3 * 256 - 1) // (3 * 256)) * 256

LANES = 128
BF16_SUBLANES = 16
VMEM_LIMIT = 56 * 1024 * 1024
ACC_ROWS = HEAD_DIM + BF16_SUBLANES

ROW_TILE = 512
ATTN_TILE = 256
MLSTM_ROWS = 256

IN_SPLITS = (
    GROUP_WIDTH, HEAD_DIM, HEAD_DIM, HEAD_DIM, HEAD_DIM, HEAD_DIM, HEAD_DIM, 3 * N_HEADS,
    GROUP_WIDTH, GROUP_WIDTH, GROUP_WIDTH,
    GROUP_WIDTH, GROUP_WIDTH, N_HEADS, N_HEADS, GROUP_WIDTH,
    GROUP_WIDTH, GROUP_WIDTH, GROUP_WIDTH,
)
(A_Q, A_KC, A_VC, A_KS, A_VS, A_KW, A_VW, A_G, B_Q, B_K, B_V,
 C_U, C_V, C_I, C_F, C_O, D_Q, D_K, D_V) = range(19)
SM_AG, SM_CI, SM_CF = 0, 12, 16


def _cparams(sem):
    return pltpu.CompilerParams(dimension_semantics=sem, vmem_limit_bytes=VMEM_LIMIT)


def _iota(shape, dim):
    return lax.broadcasted_iota(jnp.int32, shape, dim)


def _dot(a, b):
    return jnp.dot(a, b, preferred_element_type=F32)


def _bf16_pieces(x):
    hi = x.astype(BF16)
    rest = x - hi.astype(F32)
    mid = rest.astype(BF16)
    return hi, mid, (rest - mid.astype(F32)).astype(BF16)


PACKED_CHUNKS = 24
D_IN = sum(IN_SPLITS)
D_IN_ALIGNED = (D_IN // LANES) * LANES


def _column_plan():
    offs = np.concatenate([[0], np.cumsum(IN_SPLITS)])
    order = [A_Q, A_KC, A_VC, A_KS, A_KW, A_VS, A_VW,
             A_G, C_I, C_F, ('pad', LANES - 20),
             B_Q, B_K, B_V, C_U, C_V, C_O, D_Q, D_K, D_V]
    src = []
    for item in order:
        if isinstance(item, tuple):
            src += [-1] * item[1]
        else:
            src += list(range(int(offs[item]), int(offs[item + 1])))
    src = np.asarray(src)
    assert src.size == PACKED_CHUNKS * LANES
    terms, mats = [], []

    def add(dst, from_tail, start, width, rows, lanes):
        sel = np.zeros((2 * LANES, LANES), np.float32)
        sel[rows, lanes] = 1.0
        terms.append((dst, from_tail, int(start), int(width)))
        mats.append(sel)

    for j in range(PACKED_CHUNKS):
        cols = src[j * LANES:(j + 1) * LANES]
        tail = cols >= D_IN_ALIGNED
        if tail.any():
            add(j, True, 0, LANES, cols[tail] - D_IN_ALIGNED, np.nonzero(tail)[0])
        todo = (cols >= 0) & ~tail
        while todo.any():
            start = (cols[todo].min() // LANES) * LANES
            width = 2 * LANES if start + 2 * LANES <= D_IN_ALIGNED else LANES
            take = todo & (cols < start + width)
            add(j, False, start, width, cols[take] - start, np.nonzero(take)[0])
            todo &= ~take
    return tuple(terms), np.stack(mats)


def _rope_tables(seq, dim):
    inv = 1.0 / (ROPE_THETA ** (jnp.arange(0, dim, 2, dtype=F32) / dim))
    ang = jnp.arange(seq, dtype=F32)[:, None] * inv[None, :]
    return jnp.cos(ang), jnp.sin(ang)


def _lane_tables(seq):
    c64, s64 = _rope_tables(seq, HEAD_DIM)
    c32, s32 = _rope_tables(seq, DIFF_HALF)
    t64c = jnp.concatenate([c64, c64] * (LANES // HEAD_DIM), axis=1)
    t64s = jnp.concatenate([-s64, s64] * (LANES // HEAD_DIM), axis=1)
    t32c = jnp.concatenate([c32, c32] * (LANES // DIFF_HALF), axis=1)
    t32s = jnp.concatenate([-s32, s32] * (LANES // DIFF_HALF), axis=1)
    pos = np.arange(seq)[:, None]
    lane = np.arange(LANES)[None, :]
    onehot = ((lane >= HEAD_DIM) & ((pos // SLC_LEN) % HEAD_DIM == lane - HEAD_DIM)
              ).astype(np.float32)
    return t64c, t64s, t32c, t32s, jnp.asarray(onehot)


def _swap_halves(x, group):
    width = x.shape[-1]
    half = group // 2
    lane = _iota(x.shape, 1) & (group - 1)
    up = pltpu.roll(x, width - half, axis=1)
    down = pltpu.roll(x, half, axis=1)
    return jnp.where(lane < half, up, down)


def _rope(x, cos_t, sin_t, group):
    reps = x.shape[-1] // LANES
    if reps > 1:
        cos_t = jnp.concatenate([cos_t] * reps, axis=1)
        sin_t = jnp.concatenate([sin_t] * reps, axis=1)
    return x * cos_t + _swap_halves(x, group) * sin_t


def _in_proj_kernel(x_ref, g_ref, wraw_ref, wtail_ref, sel_ref, c64_ref, s64_ref, c32_ref, s32_ref,
                    oh_ref,
                    aq_ref, aqr_ref, akvc_ref, aks_ref, avs_ref, akw_ref, avw_ref, sm_ref,
                    bq_ref, bk_ref, bv_ref, cu_ref, cv_ref, co_ref, dq_ref, dk_ref, dv_ref,
                    w_ref, *, terms):
    @pl.when(pl.program_id(0) == 0)
    def _():
        for j in range(PACKED_CHUNKS):
            chunk = jnp.zeros((D_MODEL, LANES), F32)
            for t, (dst, from_tail, start, width) in enumerate(terms):
                if dst == j:
                    source = wtail_ref if from_tail else wraw_ref
                    piece = source[:, start:start + width].astype(BF16)
                    chunk = chunk + _dot(piece, sel_ref[t][0:width])
            w_ref[:, j * LANES:(j + 1) * LANES] = chunk.astype(BF16)

    x = x_ref[...]
    h = x * lax.rsqrt(jnp.mean(x * x, axis=-1, keepdims=True) + EPS)
    h = (h * g_ref[...]).astype(BF16)
    c64, s64 = c64_ref[...], s64_ref[...]
    c32, s32 = c32_ref[...], s32_ref[...]

    def mm(c0, c1):
        return _dot(h, w_ref[:, c0 * LANES:c1 * LANES])

    def channel_major(ref, z):
        ref[0] = z.T.astype(ref.dtype)

    zq = mm(0, 2) * (HEAD_DIM ** -0.5 * LOG2E)
    channel_major(aq_ref, zq)
    channel_major(aqr_ref, _rope(zq, c64, s64, HEAD_DIM))
    akvc_ref[...] = mm(2, 3).astype(BF16)
    keys = _rope(mm(3, 4), c64, s64, HEAD_DIM)
    low = _iota(keys.shape, 1) < HEAD_DIM
    aks_ref[...] = (jnp.where(low, keys, 0.0) + oh_ref[...]).astype(BF16)
    akw_ref[...] = jnp.where(low, pltpu.roll(keys, HEAD_DIM, axis=1), 0.0).astype(BF16)
    vals_t = mm(4, 5).T
    avs_ref[0] = vals_t[0:HEAD_DIM].astype(BF16)
    avw_ref[0] = vals_t[HEAD_DIM:2 * HEAD_DIM].astype(BF16)
    sm_ref[...] = mm(5, 6)
    channel_major(bq_ref, _rope(mm(6, 8), c32, s32, DIFF_HALF) * (DIFF_HALF ** -0.5 * LOG2E))
    bk_ref[...] = _rope(mm(8, 10), c32, s32, DIFF_HALF).astype(BF16)
    channel_major(bv_ref, mm(10, 12))
    cu_ref[...] = mm(12, 14)
    cv_ref[...] = mm(14, 16).astype(BF16)
    co_ref[...] = mm(16, 18)
    channel_major(dq_ref, _rope(mm(18, 20), c64, s64, HEAD_DIM) * (HEAD_DIM ** -0.5 * LOG2E))
    dk_ref[...] = _rope(mm(20, 22), c64, s64, HEAD_DIM).astype(BF16)
    channel_major(dv_ref, mm(22, 24))


_IN_PROJ_OUTS = (
    (256, BF16, True), (256, BF16, True), (128, BF16, False), (128, BF16, False),
    (HEAD_DIM, BF16, True), (128, BF16, False), (HEAD_DIM, BF16, True), (128, F32, False),
    (256, BF16, True), (256, BF16, False), (256, BF16, True), (256, F32, False),
    (256, BF16, False), (256, F32, False), (256, BF16, True), (256, BF16, False),
    (256, BF16, True))


def _in_proj(x2, gain, w, tables, seq, tm):
    n = x2.shape[0]
    nblk_s = seq // tm
    terms, select = _column_plan()
    select = jnp.asarray(select, BF16)
    w_tail = jnp.pad(w[:, D_IN_ALIGNED:], ((0, 0), (0, LANES - (D_IN - D_IN_ALIGNED))))
    row = lambda i: (i, 0)
    tab = lambda i: (i % nblk_s, 0)
    const = lambda i: (0, 0)
    in_specs = [pl.BlockSpec((tm, D_MODEL), row),
                pl.BlockSpec((1, D_MODEL), const),
                pl.BlockSpec(w.shape, const, pipeline_mode=pl.Buffered(1)),
                pl.BlockSpec(w_tail.shape, const),
                pl.BlockSpec(select.shape, lambda i: (0, 0, 0))]
    in_specs += [pl.BlockSpec((tm, LANES), tab)] * 5
    out_specs = [pl.BlockSpec((1, w, tm), lambda i: (i // nblk_s, 0, i % nblk_s)) if cmaj
                 else pl.BlockSpec((tm, w), row) for w, _, cmaj in _IN_PROJ_OUTS]
    out_shape = [jax.ShapeDtypeStruct((n // seq, w, seq) if cmaj else (n, w), dt)
                 for w, dt, cmaj in _IN_PROJ_OUTS]
    return pl.pallas_call(
        functools.partial(_in_proj_kernel, terms=terms), grid=(n // tm,), in_specs=in_specs,
        out_specs=out_specs, out_shape=out_shape,
        scratch_shapes=[pltpu.VMEM((D_MODEL, PACKED_CHUNKS * LANES), BF16)],
        compiler_params=_cparams(("arbitrary",)), name="in_proj",
    )(x2, gain.reshape(1, D_MODEL), w, w_tail, select, *tables)


def _compress_kernel(r_ref, w_ref, pos_ref, kc_ref, kvct_ref):
    wk = w_ref[0].reshape(CMP_LEN, HEAD_DIM, HEAD_DIM)
    wv = w_ref[1].reshape(CMP_LEN, HEAD_DIM, HEAD_DIM)
    zeros = jnp.zeros_like(wk)
    full = jnp.concatenate([jnp.concatenate([wk, zeros], axis=2),
                            jnp.concatenate([zeros, wv], axis=2)], axis=1)
    w_first = full[0:CMP_STRIDE].reshape(CMP_STRIDE * LANES, LANES).astype(BF16)
    w_second = full[CMP_STRIDE:].reshape(CMP_STRIDE * LANES, LANES).astype(BF16)

    r = r_ref[0]
    first = _dot(r, w_first)
    second = _dot(r, w_second)
    nrow = first.shape[0]
    pos = pos_ref[...].astype(BF16)
    half = pos.shape[1] // 2
    const = _dot(pos[:, :half], w_first) + _dot(pos[:, half:], w_second)
    out = first + pltpu.roll(second, nrow - 1, axis=0) + const[0:1, :]
    lane = _iota(out.shape, 1)
    kc_ref[0] = jnp.where(lane < HEAD_DIM, out, 0.0).astype(BF16)
    kvct_ref[0] = out.T.astype(BF16)


def _compress(a_kvc, cmp_w, cmp_pos):
    bsz, seq, _ = a_kvc.shape
    nrow = seq // CMP_STRIDE
    r = a_kvc.reshape(bsz, nrow, CMP_STRIDE * LANES)
    pos = jnp.concatenate([cmp_pos[0], cmp_pos[1]], axis=-1)
    pos = jnp.broadcast_to(pos.reshape(1, CMP_LEN * LANES), (8, CMP_LEN * LANES))
    return pl.pallas_call(
        _compress_kernel, grid=(bsz,),
        in_specs=[pl.BlockSpec((1, nrow, CMP_STRIDE * LANES), lambda b: (b, 0, 0)),
                  pl.BlockSpec(cmp_w.shape, lambda b: (0, 0, 0)),
                  pl.BlockSpec(pos.shape, lambda b: (0, 0))],
        out_specs=[pl.BlockSpec((1, nrow, LANES), lambda b: (b, 0, 0)),
                   pl.BlockSpec((1, LANES, nrow), lambda b: (b, 0, 0))],
        out_shape=[jax.ShapeDtypeStruct((bsz, nrow, LANES), BF16),
                   jax.ShapeDtypeStruct((bsz, LANES, nrow), BF16)],
        compiler_params=_cparams(("parallel",)), name="nsa_compress",
    )(r, cmp_w, pos)


def _flash_init(m_ref, acc_ref):
    m_ref[...] = jnp.full(m_ref.shape, NEG, F32)
    acc_ref[...] = jnp.zeros(acc_ref.shape, F32)


def _flash_step(s, values, m_ref, acc_ref):
    width = acc_ref.shape[1]
    chunk = min(2 * LANES, width)
    for g, vals in enumerate(values):
        rows = slice(g * ACC_ROWS, (g + 1) * ACC_ROWS)
        for c0 in range(0, width, chunk):
            lanes = slice(g * width + c0, g * width + c0 + chunk)
            s_c = s[:, lanes]
            m_old = m_ref[:, lanes]
            m_new = jnp.maximum(m_old, jnp.max(s_c, axis=0, keepdims=True))
            alpha = jnp.exp2(m_old - m_new)
            p = jnp.exp2(s_c - m_new).astype(BF16)
            acc_ref[rows, c0:c0 + chunk] = alpha * acc_ref[rows, c0:c0 + chunk] + _dot(vals, p)
            m_ref[:, lanes] = m_new


def _pipelined_tiles(lo, hi, scores, consume, sa_ref, sb_ref):
    n = hi - lo

    def put(ref, tiles):
        for g, tile in enumerate(tiles):
            ref[g] = tile

    def get(ref):
        return [ref[g] for g in range(ref.shape[0])]

    put(sa_ref, scores(lo))

    def body(i, carry):
        j = lo + 2 * i
        put(sb_ref, scores(j + 1))
        consume(get(sa_ref), j, False)
        put(sa_ref, scores(j + 2))
        consume(get(sb_ref), j + 1, False)
        return carry

    lax.fori_loop(0, n // 2, body, 0)

    @pl.when(n % 2 == 0)
    def _():
        consume(get(sa_ref), hi, True)

    @pl.when(n % 2 == 1)
    def _():
        put(sb_ref, scores(hi))
        consume(get(sa_ref), hi - 1, False)
        consume(get(sb_ref), hi, True)


def _with_ones(vt):
    return jnp.concatenate([vt, jnp.ones((BF16_SUBLANES, vt.shape[1]), BF16)], axis=0)


def _cmp_to_slc_t(seq):
    n_cmp = (seq - CMP_LEN) // CMP_STRIDE + 1
    n_slc = seq // SLC_LEN
    ratio_s, ratio_c = SLC_LEN // CMP_STRIDE, CMP_LEN // CMP_STRIDE
    jj = np.arange(n_slc)[:, None, None]
    src = ratio_s * jj - np.arange(ratio_s)[None, :, None] - np.arange(ratio_c)[None, None, :]
    ok = (src >= 0) & (src < n_cmp)
    m = np.zeros((seq // CMP_STRIDE, n_slc), np.float32)
    np.add.at(m, (np.where(ok, src, 0), np.broadcast_to(jj, src.shape)), ok.astype(np.float32))
    return jnp.asarray(m.T, BF16)


def _nsa_kernel(qt_ref, qrt_ref, kc_ref, kvct_ref, ks_ref, vst_ref, kw_ref, vwt_ref, sm_ref,
                c2st_ref, wmask_ref, o_ref, m_ref, acc_ref, sa_ref, sb_ref, *, tq, top_n):
    qi = pl.program_id(1)
    s0 = qi * tq
    nb = qt_ref.shape[0]
    rows = N_HEADS * tq
    n_slc = c2st_ref.shape[0]
    lane_t = s0 + (_iota((1, rows), 1) & (tq - 1))
    c2st = c2st_ref[...]
    blk = _iota((n_slc, 1), 0)
    cur = (s0 + _iota((1, tq), 1)) >> 6
    forced = (blk == 0) | ((blk <= cur) & (blk > cur - FORCED_LOCAL))

    def heads_on_lanes(x):
        return jnp.concatenate([x[h * HEAD_DIM:(h + 1) * HEAD_DIM, :] for h in range(N_HEADS)],
                               axis=1)

    def compressed_and_selection(b):
        q4 = jnp.concatenate([heads_on_lanes(qt_ref[b]), jnp.zeros((HEAD_DIM, rows), BF16)], axis=0)
        sc = _dot(kc_ref[b], q4)
        cmask = (_iota((sc.shape[0], 1), 0) * CMP_STRIDE + (CMP_LEN - 1)) <= lane_t
        sc = jnp.where(cmask, sc, NEG)
        e = jnp.where(cmask, jnp.exp2(sc - jnp.max(sc, axis=0, keepdims=True)), 0.0)
        z = jnp.sum(e, axis=0, keepdims=True)
        p_cmp = e * (1.0 / jnp.where(z > 0, z, 1.0))
        o_cmp = _dot(kvct_ref[b][HEAD_DIM:2 * HEAD_DIM, :], p_cmp.astype(BF16))
        p_heads = p_cmp[:, 0:tq]
        for h in range(1, N_HEADS):
            p_heads = p_heads + p_cmp[:, h * tq:(h + 1) * tq]
        p_hi = p_heads.astype(BF16)
        p_lo = (p_heads - p_hi.astype(F32)).astype(BF16)
        imp = _dot(c2st, p_hi) + _dot(c2st, p_lo)

        score = jnp.where(blk > cur, -1.0e6, jnp.where(forced, 1.0e6, imp))
        rank = jnp.zeros((n_slc, tq), F32)
        for i in range(n_slc):
            s_i = score[i:i + 1, :]
            rank = rank + jnp.where(blk > i, jnp.where(s_i >= score, 1.0, 0.0),
                                    jnp.where(s_i > score, 1.0, 0.0))
        sel = (rank < top_n) & (blk <= cur)
        bias = jnp.where(sel, 0.0, NEG)
        if n_slc < HEAD_DIM:
            bias = jnp.concatenate([bias, jnp.zeros((HEAD_DIM - n_slc, tq), F32)], axis=0)
        bias4 = jnp.concatenate([bias] * N_HEADS, axis=1).astype(BF16)
        qsel = jnp.concatenate([heads_on_lanes(qrt_ref[b]), bias4], axis=0)
        return o_cmp, qsel

    prepared = [compressed_and_selection(b) for b in range(nb)]

    def attend(k_ref, vt_ref, lo, window):
        _flash_init(m_ref, acc_ref)

        def scores(j):
            k0 = pl.multiple_of(j * tq, tq)
            return [_dot(k_ref[b, pl.ds(k0, tq), :], prepared[b][1]) for b in range(nb)]

        def consume(tiles, j, diagonal):
            k0 = pl.multiple_of(j * tq, tq)
            if window:
                far = qi - NSA_WINDOW // tq
                mask = wmask_ref[jnp.where(j == qi, 0, jnp.where(j == far, 2, 1))]
            elif diagonal:
                mask = wmask_ref[0]
            for b, s in enumerate(tiles):
                if window or diagonal:
                    s = s + mask
                _flash_step(s, [_with_ones(vt_ref[b, :, pl.ds(k0, tq)])], m_ref.at[b], acc_ref.at[b])

        _pipelined_tiles(lo, qi, scores, consume, sa_ref, sb_ref)
        return [acc_ref[b, 0:HEAD_DIM, :] * (1.0 / acc_ref[b, HEAD_DIM:HEAD_DIM + 1, :])
                for b in range(nb)]

    o_slc = attend(ks_ref, vst_ref, 0, False)
    o_win = attend(kw_ref, vwt_ref, jnp.maximum(qi - NSA_WINDOW // tq, 0), True)

    for b in range(nb):
        g = jax.nn.sigmoid(sm_ref[b].T[0:BF16_SUBLANES, :])

        def gate(branch):
            return jnp.concatenate(
                [g[branch * N_HEADS + h:branch * N_HEADS + h + 1, :] for h in range(N_HEADS)], axis=1)

        o = gate(0) * prepared[b][0] + gate(1) * o_slc[b] + gate(2) * o_win[b]
        for h in range(N_HEADS):
            o_ref[b, h * HEAD_DIM:(h + 1) * HEAD_DIM, :] = (
                o[:, h * tq:(h + 1) * tq].astype(o_ref.dtype))


def _nsa(qt, qrt, kc, kvct, ks, vst, kw, vwt, smalls, tq=ATTN_TILE):
    bsz, _, seq = qt.shape
    tq = min(tq, seq)
    nb = 2 if bsz % 2 == 0 else 1
    n_slc = seq // SLC_LEN
    c2st = _cmp_to_slc_t(seq)
    rows = N_HEADS * tq
    qspec = pl.BlockSpec((nb, GROUP_WIDTH, tq), lambda b, i: (b, 0, i))
    full = lambda a: pl.BlockSpec((nb,) + a.shape[1:], lambda b, i: (b, 0, 0))
    vspec = pl.BlockSpec((nb, HEAD_DIM, seq), lambda b, i: (b, 0, 0))
    assert NSA_WINDOW % tq == 0
    k_off = np.arange(tq)[:, None]
    q_off = np.tile(np.arange(tq), N_HEADS)[None, :]
    visible = np.stack([k_off <= q_off, np.ones((tq, rows), bool), k_off > q_off])
    wmask = jnp.asarray(np.where(visible, 0.0, NEG).astype(np.float32))
    kern = functools.partial(_nsa_kernel, tq=tq, top_n=min(TOP_N, n_slc))
    return pl.pallas_call(
        kern, grid=(bsz // nb, seq // tq),
        in_specs=[qspec, qspec, full(kc), full(kvct), full(ks), vspec, full(kw), vspec,
                  pl.BlockSpec((nb, tq, LANES), lambda b, i: (b, i, 0)),
                  pl.BlockSpec(c2st.shape, lambda b, i: (0, 0)),
                  pl.BlockSpec(wmask.shape, lambda b, i: (0, 0, 0), pipeline_mode=pl.Buffered(1))],
        out_specs=qspec,
        out_shape=jax.ShapeDtypeStruct(qt.shape, BF16),
        scratch_shapes=[pltpu.VMEM((nb, 1, rows), F32), pltpu.VMEM((nb, ACC_ROWS, rows), F32),
                        pltpu.VMEM((nb, tq, rows), F32), pltpu.VMEM((nb, tq, rows), F32)],
        compiler_params=_cparams(("parallel", "arbitrary")), name="nsa_attention",
    )(qt, qrt, kc, kvct, ks, vst, kw, vwt, smalls, c2st, wmask)


def _diff_kernel(qt_ref, k_ref, vt_ref, lam_ref, g_ref, o_ref, m_ref, acc_ref, sa_ref, sb_ref,
                 *, tq, lam_init):
    qi = pl.program_id(1)
    s0 = qi * tq
    rows = 4 * tq
    pairs = N_HEADS // 2
    pw = 2 * HEAD_DIM
    row = _iota((pw, 1), 0)
    lane_t = s0 + (_iota((1, rows), 1) & (tq - 1))

    def query_matrix(p):
        qt = qt_ref[0, p * pw:(p + 1) * pw, :]
        zero = jnp.zeros_like(qt)
        return jnp.concatenate(
            [jnp.where((row >= DIFF_HALF * c) & (row < DIFF_HALF * (c + 1)), qt, zero)
             for c in range(4)], axis=1)

    qmats = [query_matrix(p) for p in range(pairs)]

    def scores(j):
        k0 = pl.multiple_of(j * tq, tq)
        return [_dot(k_ref[0, pl.ds(k0, tq), p * pw:(p + 1) * pw], qmats[p]) for p in range(pairs)]

    def consume(tiles, j, diagonal):
        k0 = pl.multiple_of(j * tq, tq)
        for p, s in enumerate(tiles):
            if diagonal:
                s = jnp.where(k0 + _iota((tq, 1), 0) <= lane_t, s, NEG)
            vt = vt_ref[0, p * pw:(p + 1) * pw, pl.ds(k0, tq)]
            _flash_step(s, [_with_ones(vt[0:HEAD_DIM]), _with_ones(vt[HEAD_DIM:])],
                        m_ref.at[p], acc_ref.at[p])

    _flash_init(m_ref, acc_ref)
    _pipelined_tiles(0, qi, scores, consume, sa_ref, sb_ref)

    lv = lam_ref[...]
    lam = (jnp.exp(jnp.sum(lv[0:1] * lv[1:2], axis=-1, keepdims=True))
           - jnp.exp(jnp.sum(lv[2:3] * lv[3:4], axis=-1, keepdims=True)) + lam_init)
    for h in range(N_HEADS):
        a = acc_ref[h // 2, ACC_ROWS * (h % 2):ACC_ROWS * (h % 2 + 1), :]
        o_all = a[0:HEAD_DIM] * (1.0 / a[HEAD_DIM:HEAD_DIM + 1])
        o = o_all[:, :tq] - lam * o_all[:, tq:]
        y = o * lax.rsqrt(jnp.mean(o * o, axis=0, keepdims=True) + EPS)
        o_ref[0, HEAD_DIM * h:HEAD_DIM * (h + 1), :] = (
            (y * g_ref[...]) * (1.0 - lam_init)).astype(o_ref.dtype)


def _diff(qt, k, vt, lam_vecs, sub_g, lam_init, tq=ATTN_TILE):
    bsz, _, seq = qt.shape
    tq = min(tq, seq)
    pairs = N_HEADS // 2
    qspec = pl.BlockSpec((1, GROUP_WIDTH, tq), lambda b, i: (b, 0, i))
    kern = functools.partial(_diff_kernel, tq=tq, lam_init=lam_init)
    return pl.pallas_call(
        kern, grid=(bsz, seq // tq),
        in_specs=[qspec,
                  pl.BlockSpec((1, seq, GROUP_WIDTH), lambda b, i: (b, 0, 0)),
                  pl.BlockSpec((1, GROUP_WIDTH, seq), lambda b, i: (b, 0, 0)),
                  pl.BlockSpec(lam_vecs.shape, lambda b, i: (0, 0)),
                  pl.BlockSpec((HEAD_DIM, 1), lambda b, i: (0, 0))],
        out_specs=qspec,
        out_shape=jax.ShapeDtypeStruct(qt.shape, BF16),
        scratch_shapes=[pltpu.VMEM((pairs, 1, 4 * tq), F32),
                        pltpu.VMEM((pairs, 2 * ACC_ROWS, 2 * tq), F32),
                        pltpu.VMEM((pairs, tq, 4 * tq), F32), pltpu.VMEM((pairs, tq, 4 * tq), F32)],
        compiler_params=_cparams(("parallel", "arbitrary")), name="diff_attention",
    )(qt, k, vt, lam_vecs, sub_g.reshape(HEAD_DIM, 1))


def _dilated_bias(tq):
    max_back = max(w for w, _ in DILATED_PATTERNS) // tq
    classes = [0, 1, 2, 3, max_back]
    q = np.arange(tq)[None, :]
    k = np.arange(tq)[:, None]
    out = []
    for d in classes:
        delta = d * tq + q - k
        cnt = np.zeros((tq, tq), np.float64)
        for w, dil in DILATED_PATTERNS:
            cnt += (delta >= 0) & (delta <= w) & (delta % dil == 0)
        tab = np.where(cnt > 0, np.log2(np.maximum(cnt, 1.0)), NEG)
        out.append(np.concatenate([tab, tab], axis=1))
    return jnp.asarray(np.stack(out).astype(np.float32)), max_back


def _dilated_kernel(qt_ref, k_ref, vt_ref, bias_ref, o_ref, m_ref, acc_ref, sa_ref, sb_ref,
                    *, tq, max_back):
    qi = pl.program_id(1)
    nb = qt_ref.shape[0]
    pairs = N_HEADS // 2
    pw = 2 * HEAD_DIM
    row = _iota((pw, 1), 0)
    streams = [(b, p) for b in range(nb) for p in range(pairs)]

    def query_matrix(b, p):
        qt = qt_ref[b, p * pw:(p + 1) * pw, :]
        zero = jnp.zeros_like(qt)
        return jnp.concatenate([jnp.where(row < HEAD_DIM, qt, zero),
                                jnp.where(row >= HEAD_DIM, qt, zero)], axis=1)

    qmats = [query_matrix(b, p) for b, p in streams]
    _flash_init(m_ref, acc_ref)

    def scores(j):
        k0 = pl.multiple_of(j * tq, tq)
        return [_dot(k_ref[b, pl.ds(k0, tq), p * pw:(p + 1) * pw], qmats[g])
                for g, (b, p) in enumerate(streams)]

    def consume(tiles, j, diagonal):
        del diagonal
        k0 = pl.multiple_of(j * tq, tq)
        d = qi - j
        bias = bias_ref[jnp.where(d < 3, d, jnp.where(d == max_back, 4, 3))]
        for g, (b, p) in enumerate(streams):
            vt = vt_ref[b, p * pw:(p + 1) * pw, pl.ds(k0, tq)]
            _flash_step(tiles[g] + bias, [_with_ones(vt[0:HEAD_DIM]), _with_ones(vt[HEAD_DIM:])],
                        m_ref.at[g], acc_ref.at[g])

    _pipelined_tiles(jnp.maximum(qi - max_back, 0), qi, scores, consume, sa_ref, sb_ref)
    for g, (b, p) in enumerate(streams):
        for hh in range(2):
            a = acc_ref[g, ACC_ROWS * hh:ACC_ROWS * (hh + 1), :]
            h = 2 * p + hh
            o_ref[b, HEAD_DIM * h:HEAD_DIM * (h + 1), :] = (
                a[0:HEAD_DIM] * (1.0 / a[HEAD_DIM:HEAD_DIM + 1])).astype(o_ref.dtype)


def _dilated(qt, k, vt, tq=ATTN_TILE):
    bsz, _, seq = qt.shape
    tq = min(tq, seq)
    nb = 1
    streams = nb * (N_HEADS // 2)
    bias, max_back = _dilated_bias(tq)
    qspec = pl.BlockSpec((nb, GROUP_WIDTH, tq), lambda b, i: (b, 0, i))
    kern = functools.partial(_dilated_kernel, tq=tq, max_back=max_back)
    return pl.pallas_call(
        kern, grid=(bsz // nb, seq // tq),
        in_specs=[qspec,
                  pl.BlockSpec((nb, seq, GROUP_WIDTH), lambda b, i: (b, 0, 0)),
                  pl.BlockSpec((nb, GROUP_WIDTH, seq), lambda b, i: (b, 0, 0)),
                  pl.BlockSpec(bias.shape, lambda b, i: (0, 0, 0))],
        out_specs=qspec,
        out_shape=jax.ShapeDtypeStruct(qt.shape, BF16),
        scratch_shapes=[pltpu.VMEM((streams, 1, 2 * tq), F32),
                        pltpu.VMEM((streams, 2 * ACC_ROWS, tq), F32),
                        pltpu.VMEM((streams, tq, 2 * tq), F32),
                        pltpu.VMEM((streams, tq, 2 * tq), F32)],
        compiler_params=_cparams(("parallel", "arbitrary")), name="dilated_attention",
    )(qt, k, vt, bias)


def _mlstm_kernel(u_ref, up_ref, v_ref, sm_ref, o_ref, cw_ref, cb_ref, wqt_ref, wk_ref,
                  gb_ref, gcol_ref, hg_ref, out_ref, c_st, m_st):
    ci = pl.program_id(1)
    rows = u_ref.shape[1]

    @pl.when(ci == 0)
    def _():
        c_st[...] = jnp.zeros(c_st.shape, F32)
        m_st[...] = jnp.zeros(m_st.shape, F32)

    tail = jnp.where(ci > 0, up_ref[0], 0.0)
    ext = jnp.concatenate([tail, u_ref[0]], axis=0)
    cw = cw_ref[...]
    uc = cb_ref[...] + cw[MLSTM_CONV - 1:MLSTM_CONV] * ext[8:]
    for j in range(MLSTM_CONV - 1):
        shifted = pltpu.roll(ext, MLSTM_CONV - 1 - j, axis=0)[8:]
        uc = uc + cw[j:j + 1] * shifted
    uc = uc * jax.nn.sigmoid(uc)

    qt_all = _dot(wqt_ref[...], uc.T.astype(BF16))
    k_all = _dot(uc.astype(BF16), wk_ref[...]) * (HEAD_DIM ** -0.5)
    vt_all = v_ref[0].astype(F32).T.astype(BF16)
    ogt = jax.nn.sigmoid(o_ref[0]).T
    sm = sm_ref[0]
    smt = sm.T
    gb = gb_ref[...]
    gcol = gcol_ref[...]
    upper = _iota((rows, rows), 0) <= _iota((rows, rows), 1)
    lower = _iota((rows, rows), 1) <= _iota((rows, rows), 0)
    ig_rows = smt[8:16] + gcol[:, 0:1]
    lf_rows = jax.nn.log_sigmoid(smt[16:24] + gcol[:, 1:2])
    tri_upper = jnp.where(upper, 1.0, 0.0).astype(BF16)
    tri_lower = jnp.where(lower, 1.0, 0.0).astype(BF16)
    b_rows = sum(_dot(piece, tri_upper) for piece in _bf16_pieces(lf_rows))
    ig_cols = sm[:, SM_CI:SM_CI + N_HEADS] + gb[0:1]
    lf_cols = jax.nn.log_sigmoid(sm[:, SM_CF:SM_CF + N_HEADS] + gb[1:2])
    b_cols = sum(_dot(tri_lower, piece) for piece in _bf16_pieces(lf_cols))
    ones = jnp.ones((BF16_SUBLANES, rows), BF16)

    for h in range(N_HEADS):
        hs = slice(h * HEAD_DIM, (h + 1) * HEAD_DIM)
        b_row = b_rows[h:h + 1]
        src_row = ig_rows[N_HEADS + h:N_HEADS + h + 1] - b_row
        src_col = ig_cols[:, h:h + 1] - b_cols[:, h:h + 1]
        dmat = jnp.where(upper, b_row + src_col, NEG)
        a = b_row[:, rows - 1:rows]
        g_end = a + src_row
        m_loc = jnp.max(g_end, axis=-1, keepdims=True)
        w_end = jnp.exp(g_end - m_loc)

        state = c_st[h]
        m_in = m_st[h]
        inter = b_row + m_in
        m_t = jnp.maximum(inter, jnp.max(dmat, axis=0, keepdims=True))
        e_inter = jnp.exp(inter - m_t)
        qt = qt_all[hs].astype(BF16)
        kb = k_all[:, hs].astype(BF16)
        values = jnp.concatenate([vt_all[hs], ones], axis=0)
        p = (_dot(kb, qt) * jnp.exp(dmat - m_t)).astype(BF16)
        from_state = _dot(state.astype(BF16), qt)
        from_chunk = _dot(values, p)
        num = e_inter * from_state[0:HEAD_DIM] + from_chunk[0:HEAD_DIM]
        den = e_inter * from_state[HEAD_DIM:HEAD_DIM + 1] + from_chunk[HEAD_DIM:HEAD_DIM + 1]
        hh = num * (1.0 / jnp.maximum(jnp.abs(den), jnp.exp(-m_t)))
        hh = hh * lax.rsqrt(jnp.mean(hh * hh, axis=0, keepdims=True) + EPS) * hg_ref[hs, :]
        out_ref[0, hs, :] = (hh * ogt[hs]).astype(out_ref.dtype)

        m_new = jnp.maximum(a + m_in, m_loc)
        decay = jnp.exp(a + m_in - m_new)
        fresh = jnp.exp(m_loc - m_new)
        local = _dot((values.astype(F32) * w_end).astype(BF16), kb)
        c_st[h] = decay * state + fresh * local
        m_st[h] = m_new


def _mlstm(u, v, smalls, o_pre, conv_w, conv_b, wq, wk, gate_b, head_g, rows=MLSTM_ROWS):
    bsz, seq, _ = u.shape
    rows = min(rows, seq)
    row = pl.BlockSpec((1, rows, GROUP_WIDTH), lambda b, c: (b, c, 0))
    const2 = lambda a: pl.BlockSpec(a.shape, lambda b, c: (0,) * a.ndim)
    conv_b = conv_b.reshape(1, GROUP_WIDTH)
    head_g = head_g.reshape(GROUP_WIDTH, 1)
    eye = jnp.eye(N_HEADS, dtype=wq.dtype)
    wqt = jnp.einsum('hde,hg->hegd', wq, eye).reshape(GROUP_WIDTH, GROUP_WIDTH).astype(BF16)
    wkb = jnp.einsum('hde,hg->hdge', wk, eye).reshape(GROUP_WIDTH, GROUP_WIDTH).astype(BF16)
    zeros4 = jnp.zeros((N_HEADS,), gate_b.dtype)
    gcol = jnp.stack([jnp.concatenate([zeros4, gate_b[0]]), jnp.concatenate([gate_b[1], zeros4])],
                     axis=1)
    return pl.pallas_call(
        _mlstm_kernel, grid=(bsz, seq // rows),
        in_specs=[row,
                  pl.BlockSpec((1, 8, GROUP_WIDTH),
                               lambda b, c: (b, jnp.maximum(c * (rows // 8) - 1, 0), 0)),
                  row,
                  pl.BlockSpec((1, rows, LANES), lambda b, c: (b, c, 0)),
                  row, const2(conv_w), const2(conv_b), const2(wqt), const2(wkb), const2(gate_b),
                  const2(gcol), const2(head_g)],
        out_specs=pl.BlockSpec((1, GROUP_WIDTH, rows), lambda b, c: (b, 0, c)),
        out_shape=jax.ShapeDtypeStruct((bsz, GROUP_WIDTH, seq), BF16),
        scratch_shapes=[pltpu.VMEM((N_HEADS, ACC_ROWS, HEAD_DIM), F32),
                        pltpu.VMEM((N_HEADS, 1, 1), F32)],
        compiler_params=_cparams(("parallel", "arbitrary")), name="mlstm",
    )(u, u, v, smalls, o_pre, conv_w, conv_b, wqt, wkb, gate_b, gcol, head_g)


def _out_ffn_kernel(x_ref, oa_ref, ob_ref, oc_ref, od_ref, wo_ref, g_ref, wg_ref, wu_ref, wd_ref,
                    gf_ref, y_ref, act_ref, *, final, ff_chunk):
    mixed_t = jnp.concatenate([oa_ref[0], ob_ref[0], oc_ref[0], od_ref[0]], axis=0)
    x = x_ref[...] + lax.dot_general(mixed_t, wo_ref[...], (((0,), (0,)), ((), ())),
                                     preferred_element_type=F32)
    h = x * lax.rsqrt(jnp.mean(x * x, axis=-1, keepdims=True) + EPS)
    h = (h * g_ref[...]).astype(BF16)
    for c0 in range(0, D_FF, ff_chunk):
        gate = _dot(h, wg_ref[:, c0:c0 + ff_chunk])
        up = _dot(h, wu_ref[:, c0:c0 + ff_chunk])
        act_ref[:, c0:c0 + ff_chunk] = (gate * jax.nn.sigmoid(gate) * up).astype(BF16)
    y = x + _dot(act_ref[...], wd_ref[...])
    if final:
        y = y * lax.rsqrt(jnp.mean(y * y, axis=-1, keepdims=True) + EPS) * gf_ref[...]
    y_ref[...] = y


def _out_ffn(x2, o_a, o_b, o_c, o_d, w_out, gain, w_gate, w_up, w_down, gain_final, final, tm=ROW_TILE):
    n = x2.shape[0]
    seq = o_a.shape[2]
    tm = min(tm, seq)
    nblk_s = seq // tm
    row = lambda w: pl.BlockSpec((tm, w), lambda i: (i, 0))
    mixer = pl.BlockSpec((1, GROUP_WIDTH, tm), lambda i: (i // nblk_s, 0, i % nblk_s))
    const = lambda a: pl.BlockSpec(a.shape, lambda i: (0, 0), pipeline_mode=pl.Buffered(1))
    gain = gain.reshape(1, D_MODEL)
    gain_final = gain_final.reshape(1, D_MODEL)
    kern = functools.partial(_out_ffn_kernel, final=final, ff_chunk=256)
    return pl.pallas_call(
        kern, grid=(n // tm,),
        in_specs=[row(D_MODEL)] + [mixer] * 4
                 + [const(w_out), const(gain), const(w_gate), const(w_up), const(w_down),
                    const(gain_final)],
        out_specs=row(D_MODEL),
        out_shape=jax.ShapeDtypeStruct((n, D_MODEL), F32),
        scratch_shapes=[pltpu.VMEM((tm, D_FF), BF16)],
        compiler_params=_cparams(("parallel",)), name="out_ffn",
    )(x2, o_a, o_b, o_c, o_d, w_out, gain, w_gate, w_up, w_down, gain_final)


def kernel(x, norm_mix, w_in, nsa_cmp_pos, nsa_cmp_w, diff_lambda, diff_norm, mlstm_conv_w,
           mlstm_conv_b, mlstm_wq, mlstm_wk, mlstm_gate_b, mlstm_norm, w_out, norm_ffn, w_gate,
           w_up, w_down, norm_final):
    bsz, seq, _ = x.shape
    depth = w_in.shape[0]
    tables = _lane_tables(seq)
    x2 = x.reshape(bsz * seq, D_MODEL)
    r3 = lambda t: t.reshape(bsz, seq, t.shape[-1])

    for layer in range(depth):
        (a_q, a_qr, a_kvc, a_ks, a_vs, a_kw, a_vw, smalls, b_q, b_k, b_v, c_u, c_v, c_o,
         d_q, d_k, d_v) = _in_proj(x2, norm_mix[layer], w_in[layer], tables, seq,
                                   tm=min(ROW_TILE, seq))
        smalls3 = r3(smalls)

        kc, kvct = _compress(r3(a_kvc), nsa_cmp_w[layer], nsa_cmp_pos[layer])
        o_a = _nsa(a_q, a_qr, kc, kvct, r3(a_ks), a_vs, r3(a_kw), a_vw, smalls3)

        lam_init = 0.8 - 0.6 * math.exp(-0.3 * layer)
        o_b = _diff(b_q, r3(b_k), b_v, diff_lambda[layer], diff_norm[layer], lam_init)

        o_c = _mlstm(r3(c_u), r3(c_v), smalls3, r3(c_o),
                     mlstm_conv_w[layer], mlstm_conv_b[layer], mlstm_wq[layer], mlstm_wk[layer],
                     mlstm_gate_b[layer], mlstm_norm[layer])

        o_d = _dilated(d_q, r3(d_k), d_v)

        x2 = _out_ffn(x2, o_a, o_b, o_c, o_d, w_out[layer].astype(BF16), norm_ffn[layer],
                      w_gate[layer].astype(BF16), w_up[layer].astype(BF16),
                      w_down[layer].astype(BF16), norm_final, final=(layer == depth - 1))
    return x2.reshape(bsz, seq, D_MODEL)
```
